```python
import jax, jax.numpy as jnp
from jax import lax
import numpy as np

D_MODEL = 1024
BATCH = 8
SEQ = 2048
DEPTH = 1
DEC_BATCH = 128
DEC_SEQ = 1
PAST_LEN = 16384
PAGE_SIZE = 128

GDN_HEADS = 4
GDN_DK = 128
GDN_DV = 128
GDN_QK = GDN_HEADS * GDN_DK
GDN_WIDTH = GDN_HEADS * GDN_DV
GDN_QKV = 2 * GDN_QK + GDN_WIDTH
GDN_COLS = GDN_QKV + GDN_WIDTH + 2 * GDN_HEADS
CONV_W = 4
CHUNK = 64
RWKV_HEADS = 8
RWKV_N = 64
RWKV_WIDTH = RWKV_HEADS * RWKV_N
W_LORA = 64
A_LORA = 64
G_LORA = 128
RWKV_COLS = 3 * RWKV_WIDTH + W_LORA + A_LORA + G_LORA
IN_COLS = GDN_COLS + RWKV_COLS
MIX_WIDTH = GDN_WIDTH + RWKV_WIDTH
D_FF = 2816
NORM_EPS = 1e-6
GN_EPS = 64e-5
L2_EPS = 1e-6

kernel_name = 'hybrid_gdn_rwkv7_macaron_step'


def _rmsnorm(x, g):
    xf = x.astype(jnp.float32)
    y = xf * lax.rsqrt(jnp.mean(xf * xf, axis=-1, keepdims=True) + NORM_EPS)
    return (y * g.astype(jnp.float32)).astype(x.dtype)


def _l2norm(x):
    return x * lax.rsqrt(jnp.sum(x * x, axis=-1, keepdims=True) + L2_EPS)


def _swiglu(x, w_in, w_out):
    gate, up = jnp.split(x @ w_in, 2, axis=-1)
    return (jax.nn.silu(gate) * up) @ w_out


def _causal_conv(buf, u, w):
    T = u.shape[1]
    full = jnp.concatenate([buf.astype(u.dtype), u], axis=1)
    out = sum(w[j] * full[:, j:j + T] for j in range(CONV_W))
    return out, full[:, -(CONV_W - 1):]


def _gdn_chunked(q, k, v, log_alpha, beta, S0):
    B, T, H, _ = q.shape
    n = T // CHUNK

    def blocks(t):
        return jnp.moveaxis(t.reshape((B, n, CHUNK, H) + t.shape[3:]), 3, 1)

    q, k, v, beta = blocks(q), blocks(k), blocks(v), blocks(beta)
    g = jnp.cumsum(blocks(log_alpha), axis=-1)
    causal = jnp.tril(jnp.ones((CHUNK, CHUNK), dtype=bool))
    strict = jnp.tril(jnp.ones((CHUNK, CHUNK), dtype=bool), -1)
    decay = jnp.exp(jnp.where(causal, g[..., :, None] - g[..., None, :], -jnp.inf))
    k_beta = k * beta[..., None]
    Lm = jnp.where(strict, jnp.einsum('bhncd,bhnsd->bhncs', k_beta, k) * decay, 0.0)
    eye = jnp.eye(CHUNK, dtype=q.dtype)
    tinv = lax.linalg.triangular_solve(Lm + eye, jnp.broadcast_to(eye, Lm.shape),
                                       left_side=True, lower=True, unit_diagonal=True)
    u = jnp.einsum('bhncs,bhnsv->bhncv', tinv, v * beta[..., None])
    w = jnp.einsum('bhncs,bhnsk->bhnck', tinv, k_beta * jnp.exp(g)[..., None])
    attn = jnp.einsum('bhncd,bhnsd->bhncs', q, k) * decay
    q_dec = q * jnp.exp(g)[..., None]
    k_dec = k * jnp.exp(g[..., -1:] - g)[..., None]
    g_end = jnp.exp(g[..., -1])

    def step(S, xs):
        u_i, w_i, qd_i, at_i, kd_i, ge_i = xs
        v_new = u_i - jnp.einsum('bhck,bhkv->bhcv', w_i, S)
        o_i = jnp.einsum('bhck,bhkv->bhcv', qd_i, S) + jnp.einsum('bhcs,bhsv->bhcv', at_i, v_new)
        S = S * ge_i[..., None, None] + jnp.einsum('bhck,bhcv->bhkv', kd_i, v_new)
        return S, o_i

    xs = tuple(jnp.moveaxis(t, 2, 0) for t in (u, w, q_dec, attn, k_dec, g_end))
    S, o = lax.scan(step, S0, xs)
    o = jnp.moveaxis(jnp.moveaxis(o, 0, 2), 1, 3).reshape(B, T, H, -1)
    return o, S


def _gdn_recurrent(q, k, v, log_alpha, beta, S0):
    def step(S, xs):
        q_t, k_t, v_t, la_t, b_t = xs
        S = S * jnp.exp(la_t)[..., None, None]
        mem = jnp.einsum('bhk,bhkv->bhv', k_t, S)
        S = S + jnp.einsum('bhk,bhv->bhkv', k_t, (v_t - mem) * b_t[..., None])
        return S, jnp.einsum('bhk,bhkv->bhv', q_t, S)

    xs = (jnp.moveaxis(q, 1, 0), jnp.moveaxis(k, 1, 0), jnp.moveaxis(v, 1, 0),
          jnp.moveaxis(log_alpha, 1, 0), jnp.moveaxis(beta, 1, 0))
    S, o = lax.scan(step, S0, xs)
    return jnp.moveaxis(o, 0, 1), S


def _gdn_branch(p, conv_buf, S0, conv_w, a_log, dt_bias, norm_w, chunked):
    B, T, _ = p.shape
    qkv, new_buf = _causal_conv(conv_buf, p[..., :GDN_QKV], conv_w)
    qkv = jax.nn.silu(qkv).astype(jnp.float32)
    q = _l2norm(qkv[..., :GDN_QK].reshape(B, T, GDN_HEADS, GDN_DK)) * (GDN_DK ** -0.5)
    k = _l2norm(qkv[..., GDN_QK:2 * GDN_QK].reshape(B, T, GDN_HEADS, GDN_DK))
    v = qkv[..., 2 * GDN_QK:].reshape(B, T, GDN_HEADS, GDN_DV)
    off = GDN_QKV
    z = p[..., off:off + GDN_WIDTH].astype(jnp.float32)
    off += GDN_WIDTH
    a_raw = p[..., off:off + GDN_HEADS].astype(jnp.float32)
    off += GDN_HEADS
    b_raw = p[..., off:off + GDN_HEADS].astype(jnp.float32)
    beta = jax.nn.sigmoid(b_raw)
    log_alpha = -jnp.exp(a_log.astype(jnp.float32)) * jax.nn.softplus(a_raw + dt_bias.astype(jnp.float32))
    S0 = S0.astype(jnp.float32)
    if chunked:
        o, S = _gdn_chunked(q, k, v, log_alpha, beta, S0)
    else:
        o, S = _gdn_recurrent(q, k, v, log_alpha, beta, S0)
    o = o * lax.rsqrt(jnp.mean(o * o, axis=-1, keepdims=True) + NORM_EPS) * norm_w.astype(jnp.float32)
    o = o.reshape(B, T, GDN_WIDTH) * jax.nn.silu(z)
    return o.astype(p.dtype), new_buf, S


def _rwkv_branch(p, shift_prev, S0, mu, w0, w2, a0, a2, g2, k_k, k_a, r_k, ln_w, ln_b):
    B, T, _ = p.shape
    W = RWKV_WIDTH
    p_prev = jnp.concatenate([shift_prev[:, None].astype(p.dtype), p[:, :-1]], axis=1)
    xs = p + (p_prev - p) * mu
    new_shift = p[:, -1]
    r, k, v = xs[..., :W], xs[..., W:2 * W], xs[..., 2 * W:3 * W]
    off = 3 * W
    xw = xs[..., off:off + W_LORA]
    off += W_LORA
    xa = xs[..., off:off + A_LORA]
    off += A_LORA
    xg = xs[..., off:off + G_LORA]
    f32 = jnp.float32
    w_log = -jax.nn.softplus(-(w0 + jnp.tanh(xw) @ w2).astype(f32)) - 0.5
    a = jax.nn.sigmoid((a0 + xa @ a2).astype(f32))
    g = (jax.nn.sigmoid(xg) @ g2).astype(f32)
    r, k, v = r.astype(f32), k.astype(f32), v.astype(f32)
    hd = lambda t: t.reshape(B, T, RWKV_HEADS, RWKV_N)
    kk = _l2norm(hd(k * k_k.astype(f32)))
    k = k * (1.0 + (a - 1.0) * k_a.astype(f32))
    decay = jnp.exp(-jnp.exp(w_log))
    r_h, k_h, v_h, d_h, a_h = hd(r), hd(k), hd(v), hd(decay), hd(a)
    kka = kk * a_h

    def step(S, xs_t):
        r_t, k_t, v_t, d_t, kk_t, kka_t = xs_t
        sa = jnp.einsum('bhvk,bhk->bhv', S, -kk_t)
        S = S * d_t[:, :, None, :] + sa[..., None] * kka_t[:, :, None, :] + v_t[..., None] * k_t[:, :, None, :]
        return S, jnp.einsum('bhvk,bhk->bhv', S, r_t)

    S, y = lax.scan(step, S0.astype(f32), tuple(jnp.moveaxis(t, 1, 0) for t in (r_h, k_h, v_h, d_h, kk, kka)))
    y = jnp.moveaxis(y, 0, 1)
    mean = jnp.mean(y, axis=-1, keepdims=True)
    var = jnp.mean(jnp.square(y - mean), axis=-1, keepdims=True)
    y = ((y - mean) * lax.rsqrt(var + GN_EPS)).reshape(B, T, W) * ln_w.astype(f32) + ln_b.astype(f32)
    bonus = jnp.sum(r_h * k_h * r_k.astype(f32), axis=-1, keepdims=True) * v_h
    y = (y + bonus.reshape(B, T, W)) * g
    return y.astype(p.dtype), new_shift, S


def _layer(x, conv_buf, gdn_S, shift_prev, wkv_S, lw, chunked):
    (n_ffn1, w_f1_in, w_f1_out, n_mix, w_in, conv_w, a_log, dt_bias, gdn_nw, mu, w0, w2, a0, a2, g2,
     k_k, k_a, r_k, ln_w, ln_b, w_out, n_ffn2, w_f2_in, w_f2_out) = lw
    h = x + 0.5 * _swiglu(_rmsnorm(x, n_ffn1), w_f1_in, w_f1_out)
    p = _rmsnorm(h, n_mix) @ w_in
    o_a, new_conv, new_gdn = _gdn_branch(p[..., :GDN_COLS], conv_buf, gdn_S, conv_w, a_log, dt_bias, gdn_nw, chunked)
    o_b, new_shift, new_wkv = _rwkv_branch(p[..., GDN_COLS:], shift_prev, wkv_S, mu, w0, w2, a0, a2, g2,
                                           k_k, k_a, r_k, ln_w, ln_b)
    h = h + jnp.concatenate([o_a, o_b], axis=-1) @ w_out
    h = h + 0.5 * _swiglu(_rmsnorm(h, n_ffn2), w_f2_in, w_f2_out)
    return h, new_conv, new_gdn, new_shift, new_wkv


def setup_inputs(seed: int = 0) -> dict:
    key = jax.random.key(seed)
    ks = iter(jax.random.split(key, 48))
    f32 = jnp.float32
    nrm = lambda shape, s: jax.random.normal(next(ks), shape, f32) * s
    uni = lambda shape, lo, hi: jax.random.uniform(next(ks), shape, f32, minval=lo, maxval=hi)
    L = DEPTH
    dt = uni((L, GDN_HEADS), 0.001, 0.1)
    return {
        'x_prompt': nrm((BATCH, SEQ, D_MODEL), 1.0),
        'x_sample': nrm((DEC_BATCH, DEC_SEQ, D_MODEL), 1.0),
        'state_gdn_conv': nrm((L, DEC_BATCH, CONV_W - 1, GDN_QKV), 1.0),
        'state_gdn': nrm((L, DEC_BATCH, GDN_HEADS, GDN_DK, GDN_DV), 0.1),
        'state_rwkv_shift': nrm((L, DEC_BATCH, RWKV_COLS), 1.0),
        'state_rwkv': nrm((L, DEC_BATCH, RWKV_HEADS, RWKV_N, RWKV_N), 0.1),
        'norm_ffn1': 1.0 + nrm((L, D_MODEL), 0.02),
        'w_ffn1_in': nrm((L, D_MODEL, 2 * D_FF), D_MODEL ** -0.5),
        'w_ffn1_out': nrm((L, D_FF, D_MODEL), D_FF ** -0.5),
        'norm_mix': 1.0 + nrm((L, D_MODEL), 0.02),
        'w_in': nrm((L, D_MODEL, IN_COLS), D_MODEL ** -0.5),
        'gdn_conv_w': nrm((L, CONV_W, GDN_QKV), CONV_W ** -0.5),
        'gdn_a_log': jnp.log(uni((L, GDN_HEADS), 1.0, 16.0)),
        'gdn_dt_bias': dt + jnp.log(-jnp.expm1(-dt)),
        'gdn_norm_w': 1.0 + nrm((L, GDN_DV), 0.02),
        'rwkv_mu': uni((L, RWKV_COLS), 0.0, 1.0),
        'rwkv_w0': nrm((L, RWKV_WIDTH), 0.5),
        'rwkv_w2': nrm((L, W_LORA, RWKV_WIDTH), 0.1 * W_LORA ** -0.5),
        'rwkv_a0': nrm((L, RWKV_WIDTH), 0.1),
        'rwkv_a2': nrm((L, A_LORA, RWKV_WIDTH), 0.1 * A_LORA ** -0.5),
        'rwkv_g2': nrm((L, G_LORA, RWKV_WIDTH), G_LORA ** -0.5),
        'rwkv_k_k': 0.85 + nrm((L, RWKV_WIDTH), 0.02),
        'rwkv_k_a': 1.0 + nrm((L, RWKV_WIDTH), 0.02),
        'rwkv_r_k': nrm((L, RWKV_HEADS, RWKV_N), 0.1),
        'rwkv_ln_w': 1.0 + nrm((L, RWKV_WIDTH), 0.02),
        'rwkv_ln_b': nrm((L, RWKV_WIDTH), 0.01),
        'w_out': nrm((L, MIX_WIDTH, D_MODEL), MIX_WIDTH ** -0.5),
        'norm_ffn2': 1.0 + nrm((L, D_MODEL), 0.02),
        'w_ffn2_in': nrm((L, D_MODEL, 2 * D_FF), D_MODEL ** -0.5),
        'w_ffn2_out': nrm((L, D_FF, D_MODEL), D_FF ** -0.5),
        'norm_final': 1.0 + nrm((D_MODEL,), 0.02),
    }


def reference(x_prompt, x_sample, state_gdn_conv, state_gdn, state_rwkv_shift, state_rwkv,
              norm_ffn1, w_ffn1_in, w_ffn1_out, norm_mix, w_in, gdn_conv_w, gdn_a_log, gdn_dt_bias,
              gdn_norm_w, rwkv_mu, rwkv_w0, rwkv_w2, rwkv_a0, rwkv_a2, rwkv_g2, rwkv_k_k, rwkv_k_a,
              rwkv_r_k, rwkv_ln_w, rwkv_ln_b, w_out, norm_ffn2, w_ffn2_in, w_ffn2_out, norm_final):
    B = x_prompt.shape[0]
    dt = x_prompt.dtype
    yp, ys = x_prompt, x_sample
    pc, pg, psh, pw, sc, sg, ssh, sw = [], [], [], [], [], [], [], []
    for l in range(DEPTH):
        lw = (norm_ffn1[l], w_ffn1_in[l], w_ffn1_out[l], norm_mix[l], w_in[l], gdn_conv_w[l], gdn_a_log[l],
              gdn_dt_bias[l], gdn_norm_w[l], rwkv_mu[l], rwkv_w0[l], rwkv_w2[l], rwkv_a0[l], rwkv_a2[l],
              rwkv_g2[l], rwkv_k_k[l], rwkv_k_a[l], rwkv_r_k[l], rwkv_ln_w[l], rwkv_ln_b[l], w_out[l],
              norm_ffn2[l], w_ffn2_in[l], w_ffn2_out[l])
        yp, c, g, sh, w = _layer(yp, jnp.zeros((B, CONV_W - 1, GDN_QKV), dt),
                                 jnp.zeros((B, GDN_HEADS, GDN_DK, GDN_DV), jnp.float32),
                                 jnp.zeros((B, RWKV_COLS), dt),
                                 jnp.zeros((B, RWKV_HEADS, RWKV_N, RWKV_N), jnp.float32), lw, True)
        pc.append(c.astype(dt)); pg.append(g.astype(dt)); psh.append(sh.astype(dt)); pw.append(w.astype(dt))
        ys, c, g, sh, w = _layer(ys, state_gdn_conv[l], state_gdn[l], state_rwkv_shift[l], state_rwkv[l], lw, False)
        sc.append(c.astype(state_gdn_conv.dtype)); sg.append(g.astype(state_gdn.dtype))
        ssh.append(sh.astype(state_rwkv_shift.dtype)); sw.append(w.astype(state_rwkv.dtype))
    yp = _rmsnorm(yp, norm_final)
    ys = _rmsnorm(ys, norm_final)
    return (yp, ys, jnp.stack(pc), jnp.stack(pg), jnp.stack(psh), jnp.stack(pw),
            jnp.stack(sc), jnp.stack(sg), jnp.stack(ssh), jnp.stack(sw))
```

```python
import functools

import jax
import jax.numpy as jnp
from jax import lax
from jax.experimental import pallas as pl
from jax.experimental.pallas import tpu as pltpu

F32 = jnp.float32
BF16 = jnp.bfloat16

D_MODEL = 1024
D_FF = 2816
CONV_W = 4
GDN_HEADS = 4
GDN_DK = 128
GDN_QK = GDN_HEADS * GDN_DK
GDN_WIDTH = GDN_HEADS * GDN_DK
GDN_QKV = 3 * GDN_WIDTH
GDN_MAIN = GDN_QKV + GDN_WIDTH
RWKV_HEADS = 8
RWKV_N = 64
RWKV_WIDTH = RWKV_HEADS * RWKV_N
LORA_WA = 128
G_LORA = 128
RWKV_COLS = 3 * RWKV_WIDTH + LORA_WA + G_LORA
LANES = 128
SUBLANES = 8
AB_PAD = LANES
P_COLS = GDN_MAIN + RWKV_COLS + AB_PAD
NORM_EPS = 1e-6
GN_EPS = 64e-5
L2_EPS = 1e-6

GDN_CHUNK = 128
RWKV_CHUNK = 64
VMEM_LIMIT = 56 * 1024 * 1024


_NN = (((1,), (0,)), ((), ()))
_NT = (((1,), (1,)), ((), ()))


def _mm(a, b, dims=_NN):
    return lax.dot_general(a.astype(BF16), b.astype(BF16), dims, preferred_element_type=F32)


def _mm_exact(a, b):
    return lax.dot_general(a, b, _NN, precision=lax.Precision.HIGHEST, preferred_element_type=F32)


def _mm2(x, w_bf16):
    hi = x.astype(BF16)
    lo = (x - hi.astype(F32)).astype(BF16)
    return (jnp.dot(hi, w_bf16, preferred_element_type=F32)
            + jnp.dot(lo, w_bf16, preferred_element_type=F32))


def _sigmoid(x):
    return 1.0 / (1.0 + jnp.exp(-x))


def _silu(x):
    return x * _sigmoid(x)


def _softplus(x):
    return jnp.maximum(x, 0.0) + jnp.log(1.0 + jnp.exp(-jnp.abs(x)))


def _rms(x, g):
    return x * lax.rsqrt(jnp.mean(x * x, axis=-1, keepdims=True) + NORM_EPS) * g


def _iota2(shape, dim):
    return lax.broadcasted_iota(jnp.int32, shape, dim)


def _params(sem):
    return pltpu.CompilerParams(dimension_semantics=sem, vmem_limit_bytes=VMEM_LIMIT)


def _ffn_kernel(has_mix, final_norm, n_ff, *refs):
    it = iter(refs)
    x_ref = next(it)
    if has_mix:
        oa_ref, ob_ref, woa_ref, wob_ref = next(it), next(it), next(it), next(it)
    nw_ref, w1g_ref, w1u_ref, w2_ref = next(it), next(it), next(it), next(it)
    fw_ref = next(it) if final_norm else None
    out_ref = next(it)
    xn_scr, acc_scr = next(it), next(it)
    res_scr = next(it) if has_mix else None
    j = pl.program_id(1)

    @pl.when(j == 0)
    def _():
        x = x_ref[...]
        if has_mix:
            x = x + _mm(oa_ref[...], woa_ref[...]) + _mm(ob_ref[...], wob_ref[...])
            res_scr[...] = x
        xn_scr[...] = _rms(x, nw_ref[...]).astype(BF16)
        acc_scr[...] = jnp.zeros_like(acc_scr)

    xn = xn_scr[...]
    gate = jnp.dot(xn, w1g_ref[...], preferred_element_type=F32)
    up = jnp.dot(xn, w1u_ref[...], preferred_element_type=F32)
    act = (_silu(gate) * up).astype(BF16)
    acc_scr[...] += jnp.dot(act, w2_ref[...], preferred_element_type=F32)

    @pl.when(j == n_ff - 1)
    def _():
        base = res_scr[...] if has_mix else x_ref[...]
        h = base + 0.5 * acc_scr[...]
        if final_norm:
            h = _rms(h, fw_ref[...])
        out_ref[...] = h


def _ffn(x, norm_w, w1, w2, *, mix=None, final_w=None, tm, tf):
    m = x.shape[0]
    n_ff = D_FF // tf
    has_mix = mix is not None
    final_norm = final_w is not None
    row = lambda i, j: (i, 0)
    fixed = lambda i, j: (0, 0)
    in_specs = [pl.BlockSpec((tm, D_MODEL), row)]
    args = [x]
    if has_mix:
        oa, ob, woa, wob = mix
        in_specs += [pl.BlockSpec((tm, GDN_WIDTH), row), pl.BlockSpec((tm, RWKV_WIDTH), row),
                     pl.BlockSpec((GDN_WIDTH, D_MODEL), fixed), pl.BlockSpec((RWKV_WIDTH, D_MODEL), fixed)]
        args += [oa, ob, woa, wob]
    in_specs += [pl.BlockSpec((1, D_MODEL), fixed),
                 pl.BlockSpec((D_MODEL, tf), lambda i, j: (0, j)),
                 pl.BlockSpec((D_MODEL, tf), lambda i, j: (0, j + n_ff)),
                 pl.BlockSpec((tf, D_MODEL), lambda i, j: (j, 0))]
    args += [norm_w, w1, w1, w2]
    if final_norm:
        in_specs.append(pl.BlockSpec((1, D_MODEL), fixed))
        args.append(final_w)
    scratch = [pltpu.VMEM((tm, D_MODEL), BF16), pltpu.VMEM((tm, D_MODEL), F32)]
    if has_mix:
        scratch.append(pltpu.VMEM((tm, D_MODEL), F32))
    return pl.pallas_call(
        functools.partial(_ffn_kernel, has_mix, final_norm, n_ff),
        grid=(m // tm, n_ff),
        in_specs=in_specs,
        out_specs=pl.BlockSpec((tm, D_MODEL), row),
        out_shape=jax.ShapeDtypeStruct((m, D_MODEL), F32),
        scratch_shapes=scratch,
        compiler_params=_params(("parallel", "arbitrary")),
        name="ffn_mix" if has_mix else "ffn",
    )(*args)


def _proj_in_kernel(h_ref, nw_ref, w_ref, pg_ref, pr_ref, pab_ref):
    n = _rms(h_ref[...], nw_ref[...]).astype(BF16)
    pg_ref[...] = jnp.dot(n, w_ref[:, 0:GDN_MAIN], preferred_element_type=F32)
    pr_ref[...] = jnp.dot(n, w_ref[:, GDN_MAIN:GDN_MAIN + RWKV_COLS], preferred_element_type=F32)
    pab_ref[...] = jnp.dot(n, w_ref[:, GDN_MAIN + RWKV_COLS:P_COLS], preferred_element_type=F32)


def _proj_in(h, norm_w, w_perm, *, tm):
    m = h.shape[0]
    row = lambda i: (i, 0)
    fixed = lambda i: (0, 0)
    return pl.pallas_call(
        _proj_in_kernel,
        grid=(m // tm,),
        in_specs=[pl.BlockSpec((tm, D_MODEL), row), pl.BlockSpec((1, D_MODEL), fixed),
                  pl.BlockSpec((D_MODEL, P_COLS), fixed)],
        out_specs=[pl.BlockSpec((tm, GDN_MAIN), row), pl.BlockSpec((tm, RWKV_COLS), row),
                   pl.BlockSpec((tm, AB_PAD), row)],
        out_shape=[jax.ShapeDtypeStruct((m, GDN_MAIN), F32), jax.ShapeDtypeStruct((m, RWKV_COLS), F32),
                   jax.ShapeDtypeStruct((m, AB_PAD), F32)],
        compiler_params=_params(("parallel",)),
        name="proj_in",
    )(h, norm_w, w_perm)


def _inv_unit_lower(low, stack=None):
    n = low.shape[0]
    ri = _iota2(low.shape, 0)
    ci = _iota2(low.shape, 1) & (n - 1)
    t = jnp.where(ri == ci, 1.0, 0.0) - jnp.where((ri >> 1) == (ci >> 1), low, 0.0)
    s, lg = 2, 1
    while s < n:
        off = jnp.where(((ri >> (lg + 1)) == (ci >> (lg + 1))) & ((ri >> lg) != (ci >> lg)), low, 0.0)
        if stack is None:
            t = t - _mm(_mm(t, off), t)
        else:
            t = t - _mm(_mm(t, stack(off)), stack(t))
        s, lg = 2 * s, lg + 1
    return t


def _gdn_gates(ab, alog, dtb):
    log_alpha = -jnp.exp(alog) * _softplus(ab + dtb)
    return log_alpha, _sigmoid(ab)


def _gdn_out(o, z, norm_w):
    o = o * lax.rsqrt(jnp.mean(o * o, axis=-1, keepdims=True) + NORM_EPS) * norm_w
    return o * _silu(z)


def _gdn_prompt_kernel(tb, pg_ref, pab_ref, cw_ref, alog_ref, dtb_ref, nw_ref, o_ref, s_out_ref,
                       s_scr, ext_scr):
    t = pl.program_id(1)
    c = GDN_CHUNK

    @pl.when(t == 0)
    def _():
        s_scr[...] = jnp.zeros_like(s_scr)
        ext_scr[0:SUBLANES, :] = jnp.zeros((SUBLANES, GDN_QKV), F32)

    ext_scr[SUBLANES:SUBLANES + tb, :] = pg_ref[:, 0:GDN_QKV]
    acc = ext_scr[pl.ds(SUBLANES, tb), :] * cw_ref[CONV_W - 1:CONV_W, :]
    for j in range(1, CONV_W):
        acc = acc + ext_scr[pl.ds(SUBLANES - j, tb), :] * cw_ref[CONV_W - 1 - j:CONV_W - j, :]
    ext_scr[0:SUBLANES, :] = ext_scr[tb:tb + SUBLANES, :]
    qkv = _silu(acc)

    log_alpha, beta_all = _gdn_gates(pab_ref[...], alog_ref[...], dtb_ref[...])
    ri = _iota2((c, c), 0)
    ci = _iota2((c, c), 1)
    causal = ri >= ci
    strict = ri > ci
    tri = jnp.where(causal, 1.0, 0.0)

    for blk in range(tb // c):
        rows = slice(blk * c, (blk + 1) * c)
        g = _mm_exact(tri, log_alpha[rows, :])
        g_t = g.T
        for h in range(GDN_HEADS):
            lanes = slice(h * GDN_DK, (h + 1) * GDN_DK)
            q = qkv[rows, h * GDN_DK:(h + 1) * GDN_DK]
            k = qkv[rows, GDN_QK + h * GDN_DK:GDN_QK + (h + 1) * GDN_DK]
            v = qkv[rows, 2 * GDN_QK + h * GDN_DK:2 * GDN_QK + (h + 1) * GDN_DK]
            q = q * lax.rsqrt(jnp.sum(q * q, axis=-1, keepdims=True) + L2_EPS) * (GDN_DK ** -0.5)
            k = k * lax.rsqrt(jnp.sum(k * k, axis=-1, keepdims=True) + L2_EPS)
            beta = beta_all[rows, GDN_HEADS + h:GDN_HEADS + h + 1]
            g_col = g[:, h:h + 1]
            g_row = g_t[h:h + 1, :]
            decay = jnp.exp(jnp.where(causal, g_col - g_row, -jnp.inf))
            kb = k * beta
            lm = jnp.where(strict, _mm(kb, k, _NT) * decay, 0.0)
            attn = _mm(q, k, _NT) * decay
            tinv = _inv_unit_lower(lm)
            eg = jnp.exp(g_col)
            u = _mm(tinv, v * beta)
            w = _mm(tinv, kb * eg)
            g_last = g_col[c - 1:c, :]
            s = s_scr[h]
            v_new = u - _mm(w, s)
            o = _mm(q * eg, s) + _mm(attn, v_new)
            k_dec = k * jnp.exp(g_last - g_col)
            s_scr[h] = s * jnp.exp(g_last) + _mm(k_dec.T, v_new)
            z = pg_ref[rows, GDN_QKV + h * GDN_DK:GDN_QKV + (h + 1) * GDN_DK]
            o_ref[rows, lanes] = _gdn_out(o, z, nw_ref[...]).astype(o_ref.dtype)

    @pl.when(t == pl.num_programs(1) - 1)
    def _():
        s_out_ref[...] = s_scr[...]


def _gdn_prompt(pg, pab, conv_w, alog, dtb, norm_w, *, tb):
    b, t, _ = pg.shape
    fixed = lambda i, j: (0, 0)
    return pl.pallas_call(
        functools.partial(_gdn_prompt_kernel, tb),
        grid=(b, t // tb),
        in_specs=[pl.BlockSpec((None, tb, GDN_MAIN), lambda i, j: (i, j, 0)),
                  pl.BlockSpec((None, tb, AB_PAD), lambda i, j: (i, j, 0)),
                  pl.BlockSpec((CONV_W, GDN_QKV), fixed), pl.BlockSpec((1, AB_PAD), fixed),
                  pl.BlockSpec((1, AB_PAD), fixed), pl.BlockSpec((1, GDN_DK), fixed)],
        out_specs=[pl.BlockSpec((None, tb, GDN_WIDTH), lambda i, j: (i, j, 0)),
                   pl.BlockSpec((None, GDN_HEADS, GDN_DK, GDN_DK), lambda i, j: (i, 0, 0, 0))],
        out_shape=[jax.ShapeDtypeStruct((b, t, GDN_WIDTH), BF16),
                   jax.ShapeDtypeStruct((b, GDN_HEADS, GDN_DK, GDN_DK), F32)],
        scratch_shapes=[pltpu.VMEM((GDN_HEADS, GDN_DK, GDN_DK), F32),
                        pltpu.VMEM((tb + SUBLANES, GDN_QKV), F32)],
        compiler_params=_params(("parallel", "arbitrary")),
        name="gdn_prompt",
    )(pg, pab, conv_w, alog, dtb, norm_w)


def _gdn_step_kernel(bb, pg_ref, pab_ref, cs_ref, s_ref, cw_ref, alog_ref, dtb_ref, nw_ref,
                     o_ref, cs_out_ref, s_out_ref):
    u = pg_ref[:, 0:GDN_QKV]
    acc = u * cw_ref[CONV_W - 1:CONV_W, :]
    for j in range(CONV_W - 1):
        acc = acc + cs_ref[:, j * GDN_QKV:(j + 1) * GDN_QKV] * cw_ref[j:j + 1, :]
    cs_out_ref[:, 0:(CONV_W - 2) * GDN_QKV] = cs_ref[:, GDN_QKV:(CONV_W - 1) * GDN_QKV]
    cs_out_ref[:, (CONV_W - 2) * GDN_QKV:(CONV_W - 1) * GDN_QKV] = u
    qkv = _silu(acc)
    log_alpha, beta_all = _gdn_gates(pab_ref[...], alog_ref[...], dtb_ref[...])
    alpha_all = jnp.exp(log_alpha)
    eye = jnp.where(_iota2((GDN_DK, GDN_DK), 0) == _iota2((GDN_DK, GDN_DK), 1), 1.0, 0.0)

    def to_col(row):
        return jnp.sum(eye * row, axis=-1, keepdims=True)

    for h in range(GDN_HEADS):
        q = qkv[:, h * GDN_DK:(h + 1) * GDN_DK]
        k = qkv[:, GDN_QK + h * GDN_DK:GDN_QK + (h + 1) * GDN_DK]
        v = qkv[:, 2 * GDN_QK + h * GDN_DK:2 * GDN_QK + (h + 1) * GDN_DK]
        q = q * lax.rsqrt(jnp.sum(q * q, axis=-1, keepdims=True) + L2_EPS) * (GDN_DK ** -0.5)
        k = k * lax.rsqrt(jnp.sum(k * k, axis=-1, keepdims=True) + L2_EPS)
        z = pg_ref[:, GDN_QKV + h * GDN_DK:GDN_QKV + (h + 1) * GDN_DK]
        for i in range(bb):
            s = s_ref[i, h] * alpha_all[i:i + 1, h:h + 1]
            k_col = to_col(k[i:i + 1, :])
            mem = jnp.sum(k_col * s, axis=0, keepdims=True)
            delta = (v[i:i + 1, :] - mem) * beta_all[i:i + 1, GDN_HEADS + h:GDN_HEADS + h + 1]
            s = s + k_col * delta
            s_out_ref[i, h] = s
            o = jnp.sum(to_col(q[i:i + 1, :]) * s, axis=0, keepdims=True)
            o_ref[i:i + 1, h * GDN_DK:(h + 1) * GDN_DK] = _gdn_out(o, z[i:i + 1, :], nw_ref[...])


def _gdn_step(pg, pab, conv_state, s0, conv_w, alog, dtb, norm_w, *, bb):
    b = pg.shape[0]
    row = lambda i: (i, 0)
    fixed = lambda i: (0, 0)
    cs_cols = (CONV_W - 1) * GDN_QKV
    state_spec = pl.BlockSpec((bb, GDN_HEADS, GDN_DK, GDN_DK), lambda i: (i, 0, 0, 0))
    return pl.pallas_call(
        functools.partial(_gdn_step_kernel, bb),
        grid=(b // bb,),
        in_specs=[pl.BlockSpec((bb, GDN_MAIN), row), pl.BlockSpec((bb, AB_PAD), row),
                  pl.BlockSpec((bb, cs_cols), row), state_spec,
                  pl.BlockSpec((CONV_W, GDN_QKV), fixed), pl.BlockSpec((1, AB_PAD), fixed),
                  pl.BlockSpec((1, AB_PAD), fixed), pl.BlockSpec((1, GDN_DK), fixed)],
        out_specs=[pl.BlockSpec((bb, GDN_WIDTH), row), pl.BlockSpec((bb, cs_cols), row), state_spec],
        out_shape=[jax.ShapeDtypeStruct((b, GDN_WIDTH), F32), jax.ShapeDtypeStruct((b, cs_cols), F32),
                   jax.ShapeDtypeStruct(s0.shape, F32)],
        compiler_params=_params(("parallel",)),
        name="gdn_step",
    )(pg, pab, conv_state, s0, conv_w, alog, dtb, norm_w)


def _rwkv_prep(xs, w0, a0, wwa, g2, k_k, k_a, head_ones):
    w = RWKV_WIDTH
    r, k, v = xs[:, 0:w], xs[:, w:2 * w], xs[:, 2 * w:3 * w]
    wa_in = xs[:, 3 * w:3 * w + LORA_WA]
    lane = _iota2(wa_in.shape, 1)
    wa_in = jnp.where(lane < LORA_WA // 2, jnp.tanh(wa_in), wa_in)
    wa = _mm(wa_in, wwa)
    w_log = -_softplus(-(w0 + wa[:, 0:w])) - 0.5
    a = _sigmoid(a0 + wa[:, w:2 * w])
    g = _mm(_sigmoid(xs[:, 3 * w + LORA_WA:RWKV_COLS]), g2)
    kx = k * k_k
    kk = kx * lax.rsqrt(_mm2(kx * kx, head_ones) + L2_EPS)
    k = k * (1.0 + (a - 1.0) * k_a)
    return r, k, v, jnp.exp(w_log), kk, kk * a, g


def _rwkv_out(y, r, k, v, g, r_k, ln_w, ln_b, head_ones):
    inv_n = 1.0 / RWKV_N
    mean = _mm2(y, head_ones) * inv_n
    yc = y - mean
    var = _mm2(yc * yc, head_ones) * inv_n
    y = yc * lax.rsqrt(var + GN_EPS) * ln_w + ln_b
    bonus = _mm2(r * k * r_k, head_ones) * v
    return (y + bonus) * g


def _rwkv_prompt_kernel(tb, pr_ref, mu_ref, w0_ref, a0_ref, wwa_ref, g2_ref, kk_ref, ka_ref, rk_ref,
                        lnw_ref, lnb_ref, ones_ref, o_ref, s_out_ref, s_scr, ext_scr, y_scr):
    t = pl.program_id(1)
    c = RWKV_CHUNK
    n = RWKV_N
    pair = 2 * n
    lg_c = c.bit_length() - 1
    lg_n = n.bit_length() - 1

    @pl.when(t == 0)
    def _():
        s_scr[...] = jnp.zeros_like(s_scr)
        ext_scr[0:SUBLANES, :] = jnp.zeros((SUBLANES, RWKV_COLS), F32)

    p = pr_ref[...]
    ext_scr[SUBLANES:SUBLANES + tb, :] = p
    p_prev = ext_scr[pl.ds(SUBLANES - 1, tb), :]
    ext_scr[0:SUBLANES, :] = ext_scr[tb:tb + SUBLANES, :]
    xs = p + (p_prev - p) * mu_ref[...]
    head_ones = ones_ref[...]
    r, k, v, e, kk, kka, g = _rwkv_prep(xs, w0_ref[...], a0_ref[...], wwa_ref[...], g2_ref[...],
                                         kk_ref[...], ka_ref[...], head_ones)

    ri = _iota2((tb, tb), 0)
    ci = _iota2((tb, tb), 1)
    tri = jnp.where((ri >= ci) & ((ri >> lg_c) == (ci >> lg_c)), 1.0, 0.0)
    lg_inc = -_mm_exact(tri, e)
    lg_exc = lg_inc + e
    a_t = -kk * jnp.exp(lg_exc)
    r_t = r * jnp.exp(lg_inc)
    inv_g = jnp.exp(-lg_inc)
    b_t = kka * inv_g
    k_t = k * inv_g

    lane = _iota2((1, pair), 1)
    m0 = jnp.where(lane < n, 1.0, 0.0)
    m1 = 1.0 - m0

    def stack(z):
        return jnp.concatenate([z * m0, z * m1], axis=0)

    tt = _iota2((c, pair), 0)
    ss = _iota2((c, pair), 1) & (n - 1)
    strict = tt > ss
    incl = tt >= ss
    bd = jnp.where((_iota2((pair, pair), 0) >> lg_n) == (_iota2((pair, pair), 1) >> lg_n), 1.0, 0.0)

    for hp in range(RWKV_HEADS // 2):
        lanes = slice(hp * pair, (hp + 1) * pair)
        for blk in range(tb // c):
            rows = slice(blk * c, (blk + 1) * c)
            at, rt, bt, kt, vt = a_t[rows, lanes], r_t[rows, lanes], b_t[rows, lanes], k_t[rows, lanes], v[rows, lanes]
            lg_last = lg_inc[blk * c + c - 1:blk * c + c, lanes]
            to_end = jnp.exp(lg_last - lg_inc[rows, lanes])
            bh = kka[rows, lanes] * to_end
            kh = k[rows, lanes] * to_end
            sc = _mm(jnp.concatenate([at, rt], axis=0), jnp.concatenate([stack(bt), stack(kt)], axis=0), _NT)
            m_ab = jnp.where(strict, sc[0:c, 0:pair], 0.0)
            m_ak = jnp.where(strict, sc[0:c, pair:2 * pair], 0.0)
            p_rb = jnp.where(incl, sc[c:2 * c, 0:pair], 0.0)
            p_rk = jnp.where(incl, sc[c:2 * c, pair:2 * pair], 0.0)
            tinv = _inv_unit_lower(-m_ab, stack)
            w_t = _mm(tinv, stack(at))
            u0 = _mm(tinv, stack(_mm(m_ak, stack(vt))))
            s = s_scr[hp]
            u = u0 + _mm(w_t, s, _NT)
            y_scr[rows, lanes] = _mm(rt, s, _NT) + _mm(p_rb, stack(u)) + _mm(p_rk, stack(vt))
            upd = _mm(jnp.concatenate([u, vt], axis=0).T, jnp.concatenate([bh, kh], axis=0))
            s_scr[hp] = s * jnp.exp(lg_last) + bd * upd

    out = _rwkv_out(y_scr[...], r, k, v, g, rk_ref[...], lnw_ref[...], lnb_ref[...], head_ones)
    o_ref[...] = out.astype(o_ref.dtype)

    @pl.when(t == pl.num_programs(1) - 1)
    def _():
        for hp in range(RWKV_HEADS // 2):
            s = s_scr[hp]
            s_out_ref[2 * hp] = s[0:n, 0:n]
            s_out_ref[2 * hp + 1] = s[n:pair, n:pair]


def _rwkv_weight_specs(fixed):
    vec = pl.BlockSpec((1, RWKV_WIDTH), fixed)
    return [pl.BlockSpec((1, RWKV_COLS), fixed), vec, vec,
            pl.BlockSpec((LORA_WA, 2 * RWKV_WIDTH), fixed), pl.BlockSpec((G_LORA, RWKV_WIDTH), fixed),
            vec, vec, vec, vec, vec, pl.BlockSpec((RWKV_WIDTH, RWKV_WIDTH), fixed)]


def _rwkv_prompt(pr, weights, *, tb):
    b, t, _ = pr.shape
    fixed = lambda i, j: (0, 0)
    return pl.pallas_call(
        functools.partial(_rwkv_prompt_kernel, tb),
        grid=(b, t // tb),
        in_specs=[pl.BlockSpec((None, tb, RWKV_COLS), lambda i, j: (i, j, 0))] + _rwkv_weight_specs(fixed),
        out_specs=[pl.BlockSpec((None, tb, RWKV_WIDTH), lambda i, j: (i, j, 0)),
                   pl.BlockSpec((None, RWKV_HEADS, RWKV_N, RWKV_N), lambda i, j: (i, 0, 0, 0))],
        out_shape=[jax.ShapeDtypeStruct((b, t, RWKV_WIDTH), BF16),
                   jax.ShapeDtypeStruct((b, RWKV_HEADS, RWKV_N, RWKV_N), F32)],
        scratch_shapes=[pltpu.VMEM((RWKV_HEADS // 2, 2 * RWKV_N, 2 * RWKV_N), F32),
                        pltpu.VMEM((tb + SUBLANES, RWKV_COLS), F32),
                        pltpu.VMEM((tb, RWKV_WIDTH), F32)],
        compiler_params=_params(("parallel", "arbitrary")),
        name="rwkv_prompt",
    )(pr, *weights)


def _rwkv_step_kernel(bb, pr_ref, sh_ref, s_ref, mu_ref, w0_ref, a0_ref, wwa_ref, g2_ref, kk_ref, ka_ref,
                      rk_ref, lnw_ref, lnb_ref, ones_ref, o_ref, s_out_ref, y_scr):
    n = RWKV_N
    p = pr_ref[...]
    xs = p + (sh_ref[...] - p) * mu_ref[...]
    head_ones = ones_ref[...]
    r, k, v, e, kk, kka, g = _rwkv_prep(xs, w0_ref[...], a0_ref[...], wwa_ref[...], g2_ref[...],
                                         kk_ref[...], ka_ref[...], head_ones)
    decay = jnp.exp(-e)
    eye = jnp.where(_iota2((n, n), 0) == _iota2((n, n), 1), 1.0, 0.0)
    for h in range(RWKV_HEADS):
        lanes = slice(h * n, (h + 1) * n)
        r_h, k_h, v_h, d_h, kk_h, kka_h = r[:, lanes], k[:, lanes], v[:, lanes], decay[:, lanes], kk[:, lanes], kka[:, lanes]
        for i in range(bb):
            row = slice(i, i + 1)
            s = s_ref[i, h]
            sa = jnp.sum(s * (-kk_h[row, :]), axis=-1, keepdims=True)
            v_col = jnp.sum(eye * v_h[row, :], axis=-1, keepdims=True)
            s = s * d_h[row, :] + sa * kka_h[row, :] + v_col * k_h[row, :]
            s_out_ref[i, h] = s
            y_col = jnp.sum(s * r_h[row, :], axis=-1, keepdims=True)
            y_scr[row, lanes] = jnp.sum(eye * y_col, axis=0, keepdims=True)
    o_ref[...] = _rwkv_out(y_scr[...], r, k, v, g, rk_ref[...], lnw_ref[...], lnb_ref[...], head_ones)


def _rwkv_step(pr, shift, s0, weights, *, bb):
    b = pr.shape[0]
    row = lambda i: (i, 0)
    fixed = lambda i: (0, 0)
    state_spec = pl.BlockSpec((bb, RWKV_HEADS, RWKV_N, RWKV_N), lambda i: (i, 0, 0, 0))
    return pl.pallas_call(
        functools.partial(_rwkv_step_kernel, bb),
        grid=(b // bb,),
        in_specs=[pl.BlockSpec((bb, RWKV_COLS), row), pl.BlockSpec((bb, RWKV_COLS), row), state_spec]
        + _rwkv_weight_specs(fixed),
        out_specs=[pl.BlockSpec((bb, RWKV_WIDTH), row), state_spec],
        out_shape=[jax.ShapeDtypeStruct((b, RWKV_WIDTH), F32), jax.ShapeDtypeStruct(s0.shape, F32)],
        scratch_shapes=[pltpu.VMEM((bb, RWKV_WIDTH), F32)],
        compiler_params=_params(("parallel",)),
        name="rwkv_step",
    )(pr, shift, s0, *weights)


def _pad_lanes(x, width):
    return jnp.pad(x, ((0, 0), (0, width - x.shape[1])))


def kernel(x_prompt, x_sample, state_gdn_conv, state_gdn, state_rwkv_shift, state_rwkv, norm_ffn1, w_ffn1_in, w_ffn1_out, norm_mix, w_in, gdn_conv_w, gdn_a_log, gdn_dt_bias, gdn_norm_w, rwkv_mu, rwkv_w0, rwkv_w2, rwkv_a0, rwkv_a2, rwkv_g2, rwkv_k_k, rwkv_k_a, rwkv_r_k, rwkv_ln_w, rwkv_ln_b, w_out, norm_ffn2, w_ffn2_in, w_ffn2_out, norm_final):
    depth = norm_ffn1.shape[0]
    assert depth == 1, "the carried-state plumbing below is written for a single layer"
    b, t, _ = x_prompt.shape
    bs = x_sample.shape[0]
    assert x_sample.shape[1] == 1
    l = 0

    n_gdn_cols = GDN_MAIN + 2 * GDN_HEADS
    w_in_l = w_in[l]
    w_perm = jnp.concatenate(
        [w_in_l[:, :GDN_MAIN], w_in_l[:, n_gdn_cols:], w_in_l[:, GDN_MAIN:n_gdn_cols],
         jnp.zeros((D_MODEL, AB_PAD - 2 * GDN_HEADS), w_in.dtype)], axis=1).astype(BF16)
    w1a, w1b = w_ffn1_in[l].astype(BF16), w_ffn1_out[l].astype(BF16)
    w2a, w2b = w_ffn2_in[l].astype(BF16), w_ffn2_out[l].astype(BF16)
    woa, wob = w_out[l, :GDN_WIDTH].astype(BF16), w_out[l, GDN_WIDTH:].astype(BF16)
    half = LORA_WA // 2
    wwa = jnp.zeros((LORA_WA, 2 * RWKV_WIDTH), F32)
    wwa = wwa.at[:half, :RWKV_WIDTH].set(rwkv_w2[l]).at[half:, RWKV_WIDTH:].set(rwkv_a2[l]).astype(BF16)
    head_id = jnp.arange(RWKV_WIDTH) // RWKV_N
    head_ones = (head_id[:, None] == head_id[None, :]).astype(BF16)
    rwkv_w = (rwkv_mu[l][None], rwkv_w0[l][None], rwkv_a0[l][None], wwa, rwkv_g2[l].astype(BF16),
              rwkv_k_k[l][None], rwkv_k_a[l][None], rwkv_r_k[l].reshape(1, RWKV_WIDTH),
              rwkv_ln_w[l][None], rwkv_ln_b[l][None], head_ones)
    alog = _pad_lanes(gdn_a_log[l][None], AB_PAD)
    dtb = _pad_lanes(gdn_dt_bias[l][None], AB_PAD)
    gdn_w = (gdn_conv_w[l], alog, dtb, gdn_norm_w[l][None])
    nf1, nmix, nf2, nfin = norm_ffn1[l][None], norm_mix[l][None], norm_ffn2[l][None], norm_final[None]

    xp = x_prompt.reshape(b * t, D_MODEL)
    hp = _ffn(xp, nf1, w1a, w1b, tm=512, tf=1408)
    pg, pr, pab = _proj_in(hp, nmix, w_perm, tm=512)
    pg3, pr3, pab3 = pg.reshape(b, t, GDN_MAIN), pr.reshape(b, t, RWKV_COLS), pab.reshape(b, t, AB_PAD)
    oa, gdn_s = _gdn_prompt(pg3, pab3, *gdn_w, tb=256)
    ob, wkv_s = _rwkv_prompt(pr3, rwkv_w, tb=256)
    yp = _ffn(hp, nf2, w2a, w2b, mix=(oa.reshape(b * t, GDN_WIDTH), ob.reshape(b * t, RWKV_WIDTH), woa, wob),
              final_w=nfin, tm=512, tf=1408)

    xs = x_sample.reshape(bs, D_MODEL)
    hs = _ffn(xs, nf1, w1a, w1b, tm=bs, tf=1408)
    sg, sr, sab = _proj_in(hs, nmix, w_perm, tm=bs)
    conv_in = state_gdn_conv[l].reshape(bs, (CONV_W - 1) * GDN_QKV)
    oa_s, conv_s, gdn_ss = _gdn_step(sg, sab, conv_in, state_gdn[l], *gdn_w, bb=8)
    ob_s, wkv_ss = _rwkv_step(sr, state_rwkv_shift[l], state_rwkv[l], rwkv_w, bb=8)
    ys = _ffn(hs, nf2, w2a, w2b, mix=(oa_s, ob_s, woa, wob), final_w=nfin, tm=bs, tf=1408)

    return (yp.reshape(b, t, D_MODEL), ys.reshape(bs, 1, D_MODEL),
            pg3[:, t - (CONV_W - 1):, :GDN_QKV][None], gdn_s[None], pr3[:, t - 1, :][None], wkv_s[None],
            conv_s.reshape(1, bs, CONV_W - 1, GDN_QKV), gdn_ss[None], sr[None], wkv_ss[None])
```

```python
import functools

import jax
import jax.numpy as jnp
from jax import lax
from jax.experimental import pallas as pl
from jax.experimental.pallas import tpu as pltpu

F32 = jnp.float32
BF16 = jnp.bfloat16

D_MODEL = 1024
D_FF = 2816
CONV_W = 4
GDN_HEADS = 4
GDN_DK = 128
GDN_QK = GDN_HEADS * GDN_DK
GDN_WIDTH = GDN_HEADS * GDN_DK
GDN_QKV = 3 * GDN_WIDTH
GDN_MAIN = GDN_QKV + GDN_WIDTH
RWKV_HEADS = 8
RWKV_N = 64
RWKV_WIDTH = RWKV_HEADS * RWKV_N
LORA_WA = 128
G_LORA = 128
RWKV_COLS = 3 * RWKV_WIDTH + LORA_WA + G_LORA
LANES = 128
SUBLANES = 8
AB_PAD = LANES
P_COLS = GDN_MAIN + RWKV_COLS + AB_PAD
NORM_EPS = 1e-6
GN_EPS = 64e-5
L2_EPS = 1e-6

GDN_CHUNK = 128
RWKV_CHUNK = 64
VMEM_LIMIT = 56 * 1024 * 1024


_NN = (((1,), (0,)), ((), ()))
_NT = (((1,), (1,)), ((), ()))


def _mm(a, b, dims=_NN):
    return lax.dot_general(a.astype(BF16), b.astype(BF16), dims, preferred_element_type=F32)


def _mm_exact(a, b):
    return lax.dot_general(a, b, _NN, precision=lax.Precision.HIGHEST, preferred_element_type=F32)


def _mm2(x, w_bf16):
    hi = x.astype(BF16)
    lo = (x - hi.astype(F32)).astype(BF16)
    return (jnp.dot(hi, w_bf16, preferred_element_type=F32)
            + jnp.dot(lo, w_bf16, preferred_element_type=F32))


def _sigmoid(x):
    return 1.0 / (1.0 + jnp.exp(-x))


def _silu(x):
    return x * _sigmoid(x)


def _softplus(x):
    return jnp.maximum(x, 0.0) + jnp.log(1.0 + jnp.exp(-jnp.abs(x)))


def _rms(x, g):
    return x * lax.rsqrt(jnp.mean(x * x, axis=-1, keepdims=True) + NORM_EPS) * g


def _iota2(shape, dim):
    return lax.broadcasted_iota(jnp.int32, shape, dim)


def _params(sem):
    return pltpu.CompilerParams(dimension_semantics=sem, vmem_limit_bytes=VMEM_LIMIT)


def _ffn_kernel(has_mix, final_norm, n_ff, *refs):
    it = iter(refs)
    x_ref = next(it)
    if has_mix:
        oa_ref, ob_ref, woa_ref, wob_ref = next(it), next(it), next(it), next(it)
    nw_ref, w1g_ref, w1u_ref, w2_ref = next(it), next(it), next(it), next(it)
    fw_ref = next(it) if final_norm else None
    out_ref = next(it)
    xn_scr, acc_scr = next(it), next(it)
    res_scr = next(it) if has_mix else None
    j = pl.program_id(1)

    @pl.when(j == 0)
    def _():
        x = x_ref[...]
        if has_mix:
            x = x + _mm(oa_ref[...], woa_ref[...]) + _mm(ob_ref[...], wob_ref[...])
            res_scr[...] = x
        xn_scr[...] = _rms(x, nw_ref[...]).astype(BF16)
        acc_scr[...] = jnp.zeros_like(acc_scr)

    xn = xn_scr[...]
    gate = jnp.dot(xn, w1g_ref[...], preferred_element_type=F32)
    up = jnp.dot(xn, w1u_ref[...], preferred_element_type=F32)
    act = (_silu(gate) * up).astype(BF16)
    acc_scr[...] += jnp.dot(act, w2_ref[...], preferred_element_type=F32)

    @pl.when(j == n_ff - 1)
    def _():
        base = res_scr[...] if has_mix else x_ref[...]
        h = base + 0.5 * acc_scr[...]
        if final_norm:
            h = _rms(h, fw_ref[...])
        out_ref[...] = h


def _ffn(x, norm_w, w1, w2, *, mix=None, final_w=None, tm, tf):
    m = x.shape[0]
    n_ff = D_FF // tf
    has_mix = mix is not None
    final_norm = final_w is not None
    row = lambda i, j: (i, 0)
    fixed = lambda i, j: (0, 0)
    in_specs = [pl.BlockSpec((tm, D_MODEL), row)]
    args = [x]
    if has_mix:
        oa, ob, woa, wob = mix
        in_specs += [pl.BlockSpec((tm, GDN_WIDTH), row), pl.BlockSpec((tm, RWKV_WIDTH), row),
                     pl.BlockSpec((GDN_WIDTH, D_MODEL), fixed), pl.BlockSpec((RWKV_WIDTH, D_MODEL), fixed)]
        args += [oa, ob, woa, wob]
    in_specs += [pl.BlockSpec((1, D_MODEL), fixed),
                 pl.BlockSpec((D_MODEL, tf), lambda i, j: (0, j)),
                 pl.BlockSpec((D_MODEL, tf), lambda i, j: (0, j + n_ff)),
                 pl.BlockSpec((tf, D_MODEL), lambda i, j: (j, 0))]
    args += [norm_w, w1, w1, w2]
    if final_norm:
        in_specs.append(pl.BlockSpec((1, D_MODEL), fixed))
        args.append(final_w)
    scratch = [pltpu.VMEM((tm, D_MODEL), BF16), pltpu.VMEM((tm, D_MODEL), F32)]
    if has_mix:
        scratch.append(pltpu.VMEM((tm, D_MODEL), F32))
    return pl.pallas_call(
        functools.partial(_ffn_kernel, has_mix, final_norm, n_ff),
        grid=(m // tm, n_ff),
        in_specs=in_specs,
        out_specs=pl.BlockSpec((tm, D_MODEL), row),
        out_shape=jax.ShapeDtypeStruct((m, D_MODEL), F32),
        scratch_shapes=scratch,
        compiler_params=_params(("parallel", "arbitrary")),
        name="ffn_mix" if has_mix else "ffn",
    )(*args)


def _proj_in_kernel(h_ref, nw_ref, w_ref, pg_ref, pr_ref, pab_ref):
    n = _rms(h_ref[...], nw_ref[...]).astype(BF16)
    pg_ref[...] = jnp.dot(n, w_ref[:, 0:GDN_MAIN], preferred_element_type=F32)
    pr_ref[...] = jnp.dot(n, w_ref[:, GDN_MAIN:GDN_MAIN + RWKV_COLS], preferred_element_type=F32)
    pab_ref[...] = jnp.dot(n, w_ref[:, GDN_MAIN + RWKV_COLS:P_COLS], preferred_element_type=F32)


def _proj_in(h, norm_w, w_perm, *, tm):
    m = h.shape[0]
    row = lambda i: (i, 0)
    fixed = lambda i: (0, 0)
    return pl.pallas_call(
        _proj_in_kernel,
        grid=(m // tm,),
        in_specs=[pl.BlockSpec((tm, D_MODEL), row), pl.BlockSpec((1, D_MODEL), fixed),
                  pl.BlockSpec((D_MODEL, P_COLS), fixed)],
        out_specs=[pl.BlockSpec((tm, GDN_MAIN), row), pl.BlockSpec((tm, RWKV_COLS), row),
                   pl.BlockSpec((tm, AB_PAD), row)],
        out_shape=[jax.ShapeDtypeStruct((m, GDN_MAIN), F32), jax.ShapeDtypeStruct((m, RWKV_COLS), F32),
                   jax.ShapeDtypeStruct((m, AB_PAD), F32)],
        compiler_params=_params(("parallel",)),
        name="proj_in",
    )(h, norm_w, w_perm)


def _inv_unit_lower(lows, stack=None):
    shape = lows[0].shape
    n = shape[0]
    ri = _iota2(shape, 0)
    ci = _iota2(shape, 1) & (n - 1)
    eye = jnp.where(ri == ci, 1.0, 0.0)
    pair_blk = (ri >> 1) == (ci >> 1)
    ts = [eye - jnp.where(pair_blk, low, 0.0) for low in lows]
    s, lg = 2, 1
    while s < n:
        sel = ((ri >> (lg + 1)) == (ci >> (lg + 1))) & ((ri >> lg) != (ci >> lg))
        offs = [jnp.where(sel, low, 0.0) for low in lows]
        if stack is None:
            xs = [_mm(t, off) for t, off in zip(ts, offs)]
            ts = [t - _mm(x, t) for t, x in zip(ts, xs)]
        else:
            xs = [_mm(t, stack(off)) for t, off in zip(ts, offs)]
            ts = [t - _mm(x, stack(t)) for t, x in zip(ts, xs)]
        s, lg = 2 * s, lg + 1
    return ts


def _gdn_gates(ab, alog, dtb):
    log_alpha = -jnp.exp(alog) * _softplus(ab + dtb)
    return log_alpha, _sigmoid(ab)


def _gdn_out(o, z, norm_w):
    o = o * lax.rsqrt(jnp.mean(o * o, axis=-1, keepdims=True) + NORM_EPS) * norm_w
    return o * _silu(z)


def _gdn_prompt_kernel(tb, pg_ref, pab_ref, cw_ref, alog_ref, dtb_ref, nw_ref, o_ref, s_out_ref,
                       s_scr, ext_scr):
    t = pl.program_id(1)
    c = GDN_CHUNK

    @pl.when(t == 0)
    def _():
        s_scr[...] = jnp.zeros_like(s_scr)
        ext_scr[0:SUBLANES, :] = jnp.zeros((SUBLANES, GDN_QKV), F32)

    ext_scr[SUBLANES:SUBLANES + tb, :] = pg_ref[:, 0:GDN_QKV]
    acc = ext_scr[pl.ds(SUBLANES, tb), :] * cw_ref[CONV_W - 1:CONV_W, :]
    for j in range(1, CONV_W):
        acc = acc + ext_scr[pl.ds(SUBLANES - j, tb), :] * cw_ref[CONV_W - 1 - j:CONV_W - j, :]
    ext_scr[0:SUBLANES, :] = ext_scr[tb:tb + SUBLANES, :]
    qkv = _silu(acc)

    log_alpha, beta_all = _gdn_gates(pab_ref[...], alog_ref[...], dtb_ref[...])
    ri = _iota2((c, c), 0)
    ci = _iota2((c, c), 1)
    causal = ri >= ci
    strict = ri > ci
    tri = jnp.where(causal, 1.0, 0.0)

    items = [(blk, h) for blk in range(tb // c) for h in range(GDN_HEADS)]
    g_blk, gt_blk = [], []
    for blk in range(tb // c):
        g = _mm_exact(tri, log_alpha[blk * c:(blk + 1) * c, :])
        g_blk.append(g)
        gt_blk.append(g.T)
    qs, ks, vbs, kbs, decays, g_cols = [], [], [], [], [], []
    for blk, h in items:
        rows = slice(blk * c, (blk + 1) * c)
        q = qkv[rows, h * GDN_DK:(h + 1) * GDN_DK]
        k = qkv[rows, GDN_QK + h * GDN_DK:GDN_QK + (h + 1) * GDN_DK]
        v = qkv[rows, 2 * GDN_QK + h * GDN_DK:2 * GDN_QK + (h + 1) * GDN_DK]
        q = q * lax.rsqrt(jnp.sum(q * q, axis=-1, keepdims=True) + L2_EPS) * (GDN_DK ** -0.5)
        k = k * lax.rsqrt(jnp.sum(k * k, axis=-1, keepdims=True) + L2_EPS)
        beta = beta_all[rows, GDN_HEADS + h:GDN_HEADS + h + 1]
        g_col = g_blk[blk][:, h:h + 1]
        g_row = gt_blk[blk][h:h + 1, :]
        decays.append(jnp.exp(jnp.where(causal, g_col - g_row, -jnp.inf)))
        qs.append(q)
        ks.append(k)
        vbs.append(v * beta)
        kbs.append(k * beta)
        g_cols.append(g_col)
    kk = [_mm(kb, k, _NT) for kb, k in zip(kbs, ks)]
    qk = [_mm(q, k, _NT) for q, k in zip(qs, ks)]
    tinvs = _inv_unit_lower([jnp.where(strict, x * d, 0.0) for x, d in zip(kk, decays)])
    attns = [x * d for x, d in zip(qk, decays)]
    egs = [jnp.exp(g_col) for g_col in g_cols]
    us = [_mm(tinv, vb) for tinv, vb in zip(tinvs, vbs)]
    ws = [_mm(tinv, kb * eg) for tinv, kb, eg in zip(tinvs, kbs, egs)]

    states = [s_scr[h] for h in range(GDN_HEADS)]
    for blk in range(tb // c):
        rows = slice(blk * c, (blk + 1) * c)
        idx = [blk * GDN_HEADS + h for h in range(GDN_HEADS)]
        v_new = [us[i] - _mm(ws[i], states[h]) for h, i in enumerate(idx)]
        o_s = [_mm(qs[i] * egs[i], states[h]) for h, i in enumerate(idx)]
        o_v = [_mm(attns[i], v_new[h]) for h, i in enumerate(idx)]
        for h, i in enumerate(idx):
            g_last = g_cols[i][c - 1:c, :]
            k_dec = ks[i] * jnp.exp(g_last - g_cols[i])
            states[h] = states[h] * jnp.exp(g_last) + _mm(k_dec.T, v_new[h])
            z = pg_ref[rows, GDN_QKV + h * GDN_DK:GDN_QKV + (h + 1) * GDN_DK]
            o_ref[rows, h * GDN_DK:(h + 1) * GDN_DK] = _gdn_out(o_s[h] + o_v[h], z, nw_ref[...]).astype(o_ref.dtype)
    for h in range(GDN_HEADS):
        s_scr[h] = states[h]

    @pl.when(t == pl.num_programs(1) - 1)
    def _():
        s_out_ref[...] = s_scr[...]


def _gdn_prompt(pg, pab, conv_w, alog, dtb, norm_w, *, tb):
    b, t, _ = pg.shape
    fixed = lambda i, j: (0, 0)
    return pl.pallas_call(
        functools.partial(_gdn_prompt_kernel, tb),
        grid=(b, t // tb),
        in_specs=[pl.BlockSpec((None, tb, GDN_MAIN), lambda i, j: (i, j, 0)),
                  pl.BlockSpec((None, tb, AB_PAD), lambda i, j: (i, j, 0)),
                  pl.BlockSpec((CONV_W, GDN_QKV), fixed), pl.BlockSpec((1, AB_PAD), fixed),
                  pl.BlockSpec((1, AB_PAD), fixed), pl.BlockSpec((1, GDN_DK), fixed)],
        out_specs=[pl.BlockSpec((None, tb, GDN_WIDTH), lambda i, j: (i, j, 0)),
                   pl.BlockSpec((None, GDN_HEADS, GDN_DK, GDN_DK), lambda i, j: (i, 0, 0, 0))],
        out_shape=[jax.ShapeDtypeStruct((b, t, GDN_WIDTH), BF16),
                   jax.ShapeDtypeStruct((b, GDN_HEADS, GDN_DK, GDN_DK), F32)],
        scratch_shapes=[pltpu.VMEM((GDN_HEADS, GDN_DK, GDN_DK), F32),
                        pltpu.VMEM((tb + SUBLANES, GDN_QKV), F32)],
        compiler_params=_params(("parallel", "arbitrary")),
        name="gdn_prompt",
    )(pg, pab, conv_w, alog, dtb, norm_w)


def _gdn_step_kernel(bb, pg_ref, pab_ref, cs_ref, s_ref, cw_ref, alog_ref, dtb_ref, nw_ref,
                     o_ref, cs_out_ref, s_out_ref):
    u = pg_ref[:, 0:GDN_QKV]
    acc = u * cw_ref[CONV_W - 1:CONV_W, :]
    for j in range(CONV_W - 1):
        acc = acc + cs_ref[:, j * GDN_QKV:(j + 1) * GDN_QKV] * cw_ref[j:j + 1, :]
    cs_out_ref[:, 0:(CONV_W - 2) * GDN_QKV] = cs_ref[:, GDN_QKV:(CONV_W - 1) * GDN_QKV]
    cs_out_ref[:, (CONV_W - 2) * GDN_QKV:(CONV_W - 1) * GDN_QKV] = u
    qkv = _silu(acc)
    log_alpha, beta_all = _gdn_gates(pab_ref[...], alog_ref[...], dtb_ref[...])
    alpha_all = jnp.exp(log_alpha)
    eye = jnp.where(_iota2((GDN_DK, GDN_DK), 0) == _iota2((GDN_DK, GDN_DK), 1), 1.0, 0.0)

    def to_col(row):
        return jnp.sum(eye * row, axis=-1, keepdims=True)

    for h in range(GDN_HEADS):
        q = qkv[:, h * GDN_DK:(h + 1) * GDN_DK]
        k = qkv[:, GDN_QK + h * GDN_DK:GDN_QK + (h + 1) * GDN_DK]
        v = qkv[:, 2 * GDN_QK + h * GDN_DK:2 * GDN_QK + (h + 1) * GDN_DK]
        q = q * lax.rsqrt(jnp.sum(q * q, axis=-1, keepdims=True) + L2_EPS) * (GDN_DK ** -0.5)
        k = k * lax.rsqrt(jnp.sum(k * k, axis=-1, keepdims=True) + L2_EPS)
        z = pg_ref[:, GDN_QKV + h * GDN_DK:GDN_QKV + (h + 1) * GDN_DK]
        for i in range(bb):
            s = s_ref[i, h] * alpha_all[i:i + 1, h:h + 1]
            k_col = to_col(k[i:i + 1, :])
            mem = jnp.sum(k_col * s, axis=0, keepdims=True)
            delta = (v[i:i + 1, :] - mem) * beta_all[i:i + 1, GDN_HEADS + h:GDN_HEADS + h + 1]
            s = s + k_col * delta
            s_out_ref[i, h] = s
            o = jnp.sum(to_col(q[i:i + 1, :]) * s, axis=0, keepdims=True)
            o_ref[i:i + 1, h * GDN_DK:(h + 1) * GDN_DK] = _gdn_out(o, z[i:i + 1, :], nw_ref[...])


def _gdn_step(pg, pab, conv_state, s0, conv_w, alog, dtb, norm_w, *, bb):
    b = pg.shape[0]
    row = lambda i: (i, 0)
    fixed = lambda i: (0, 0)
    cs_cols = (CONV_W - 1) * GDN_QKV
    state_spec = pl.BlockSpec((bb, GDN_HEADS, GDN_DK, GDN_DK), lambda i: (i, 0, 0, 0))
    return pl.pallas_call(
        functools.partial(_gdn_step_kernel, bb),
        grid=(b // bb,),
        in_specs=[pl.BlockSpec((bb, GDN_MAIN), row), pl.BlockSpec((bb, AB_PAD), row),
                  pl.BlockSpec((bb, cs_cols), row), state_spec,
                  pl.BlockSpec((CONV_W, GDN_QKV), fixed), pl.BlockSpec((1, AB_PAD), fixed),
                  pl.BlockSpec((1, AB_PAD), fixed), pl.BlockSpec((1, GDN_DK), fixed)],
        out_specs=[pl.BlockSpec((bb, GDN_WIDTH), row), pl.BlockSpec((bb, cs_cols), row), state_spec],
        out_shape=[jax.ShapeDtypeStruct((b, GDN_WIDTH), F32), jax.ShapeDtypeStruct((b, cs_cols), F32),
                   jax.ShapeDtypeStruct(s0.shape, F32)],
        compiler_params=_params(("parallel",)),
        name="gdn_step",
    )(pg, pab, conv_state, s0, conv_w, alog, dtb, norm_w)


def _rwkv_prep(xs, w0, a0, wwa, g2, k_k, k_a, head_ones):
    w = RWKV_WIDTH
    r, k, v = xs[:, 0:w], xs[:, w:2 * w], xs[:, 2 * w:3 * w]
    wa_in = xs[:, 3 * w:3 * w + LORA_WA]
    lane = _iota2(wa_in.shape, 1)
    wa_in = jnp.where(lane < LORA_WA // 2, jnp.tanh(wa_in), wa_in)
    wa = _mm(wa_in, wwa)
    w_log = -_softplus(-(w0 + wa[:, 0:w])) - 0.5
    a = _sigmoid(a0 + wa[:, w:2 * w])
    g = _mm(_sigmoid(xs[:, 3 * w + LORA_WA:RWKV_COLS]), g2)
    kx = k * k_k
    kk = kx * lax.rsqrt(_mm2(kx * kx, head_ones) + L2_EPS)
    k = k * (1.0 + (a - 1.0) * k_a)
    return r, k, v, jnp.exp(w_log), kk, kk * a, g


def _rwkv_out(y, r, k, v, g, r_k, ln_w, ln_b, head_ones):
    inv_n = 1.0 / RWKV_N
    mean = _mm2(y, head_ones) * inv_n
    yc = y - mean
    var = _mm2(yc * yc, head_ones) * inv_n
    y = yc * lax.rsqrt(var + GN_EPS) * ln_w + ln_b
    bonus = _mm2(r * k * r_k, head_ones) * v
    return (y + bonus) * g


def _rwkv_prompt_kernel(tb, pr_ref, mu_ref, w0_ref, a0_ref, wwa_ref, g2_ref, kk_ref, ka_ref, rk_ref,
                        lnw_ref, lnb_ref, ones_ref, o_ref, s_out_ref, s_scr, ext_scr, y_scr):
    t = pl.program_id(1)
    c = RWKV_CHUNK
    n = RWKV_N
    pair = 2 * n
    lg_c = c.bit_length() - 1
    lg_n = n.bit_length() - 1

    @pl.when(t == 0)
    def _():
        s_scr[...] = jnp.zeros_like(s_scr)
        ext_scr[0:SUBLANES, :] = jnp.zeros((SUBLANES, RWKV_COLS), F32)

    p = pr_ref[...]
    ext_scr[SUBLANES:SUBLANES + tb, :] = p
    p_prev = ext_scr[pl.ds(SUBLANES - 1, tb), :]
    ext_scr[0:SUBLANES, :] = ext_scr[tb:tb + SUBLANES, :]
    xs = p + (p_prev - p) * mu_ref[...]
    head_ones = ones_ref[...]
    r, k, v, e, kk, kka, g = _rwkv_prep(xs, w0_ref[...], a0_ref[...], wwa_ref[...], g2_ref[...],
                                         kk_ref[...], ka_ref[...], head_ones)

    ri = _iota2((tb, tb), 0)
    ci = _iota2((tb, tb), 1)
    tri = jnp.where((ri >= ci) & ((ri >> lg_c) == (ci >> lg_c)), 1.0, 0.0)
    lg_inc = -_mm_exact(tri, e)
    lg_exc = lg_inc + e
    a_t = -kk * jnp.exp(lg_exc)
    r_t = r * jnp.exp(lg_inc)
    inv_g = jnp.exp(-lg_inc)
    b_t = kka * inv_g
    k_t = k * inv_g

    lane = _iota2((1, pair), 1)
    m0 = jnp.where(lane < n, 1.0, 0.0)
    m1 = 1.0 - m0

    def stack(z):
        return jnp.concatenate([z * m0, z * m1], axis=0)

    tt = _iota2((c, pair), 0)
    ss = _iota2((c, pair), 1) & (n - 1)
    strict = tt > ss
    incl = tt >= ss
    bd = jnp.where((_iota2((pair, pair), 0) >> lg_n) == (_iota2((pair, pair), 1) >> lg_n), 1.0, 0.0)

    n_pairs = RWKV_HEADS // 2
    items = [(blk, hp) for blk in range(tb // c) for hp in range(n_pairs)]

    def tile(x, blk, hp):
        return x[blk * c:(blk + 1) * c, hp * pair:(hp + 1) * pair]

    ats = [tile(a_t, *it) for it in items]
    rts = [tile(r_t, *it) for it in items]
    vts = [tile(v, *it) for it in items]
    svs = [stack(vt) for vt in vts]
    scs = [_mm(jnp.concatenate([at, rt], axis=0),
               jnp.concatenate([stack(tile(b_t, *it)), stack(tile(k_t, *it))], axis=0), _NT)
           for at, rt, it in zip(ats, rts, items)]
    tinvs = _inv_unit_lower([-jnp.where(strict, sc[0:c, 0:pair], 0.0) for sc in scs], stack)
    akv = [_mm(jnp.where(strict, sc[0:c, pair:2 * pair], 0.0), sv) for sc, sv in zip(scs, svs)]
    y_v = [_mm(jnp.where(incl, sc[c:2 * c, pair:2 * pair], 0.0), sv) for sc, sv in zip(scs, svs)]
    p_rbs = [jnp.where(incl, sc[c:2 * c, 0:pair], 0.0) for sc in scs]
    w_ts = [_mm(tinv, stack(at)) for tinv, at in zip(tinvs, ats)]
    u0s = [_mm(tinv, stack(x)) for tinv, x in zip(tinvs, akv)]

    states = [s_scr[hp] for hp in range(n_pairs)]
    for blk in range(tb // c):
        rows = slice(blk * c, (blk + 1) * c)
        idx = [blk * n_pairs + hp for hp in range(n_pairs)]
        us = [u0s[i] + _mm(w_ts[i], states[hp], _NT) for hp, i in enumerate(idx)]
        y_s = [_mm(rts[i], states[hp], _NT) for hp, i in enumerate(idx)]
        y_u = [_mm(p_rbs[i], stack(us[hp])) for hp, i in enumerate(idx)]
        for hp, i in enumerate(idx):
            lanes = slice(hp * pair, (hp + 1) * pair)
            lg_last = lg_inc[blk * c + c - 1:blk * c + c, lanes]
            to_end = jnp.exp(lg_last - lg_inc[rows, lanes])
            ends = jnp.concatenate([kka[rows, lanes] * to_end, k[rows, lanes] * to_end], axis=0)
            upd = _mm(jnp.concatenate([us[hp], vts[i]], axis=0).T, ends)
            states[hp] = states[hp] * jnp.exp(lg_last) + bd * upd
            y_scr[rows, lanes] = y_s[hp] + y_u[hp] + y_v[i]
    for hp in range(n_pairs):
        s_scr[hp] = states[hp]

    out = _rwkv_out(y_scr[...], r, k, v, g, rk_ref[...], lnw_ref[...], lnb_ref[...], head_ones)
    o_ref[...] = out.astype(o_ref.dtype)

    @pl.when(t == pl.num_programs(1) - 1)
    def _():
        for hp in range(RWKV_HEADS // 2):
            s = s_scr[hp]
            s_out_ref[2 * hp] = s[0:n, 0:n]
            s_out_ref[2 * hp + 1] = s[n:pair, n:pair]


def _rwkv_weight_specs(fixed):
    vec = pl.BlockSpec((1, RWKV_WIDTH), fixed)
    return [pl.BlockSpec((1, RWKV_COLS), fixed), vec, vec,
            pl.BlockSpec((LORA_WA, 2 * RWKV_WIDTH), fixed), pl.BlockSpec((G_LORA, RWKV_WIDTH), fixed),
            vec, vec, vec, vec, vec, pl.BlockSpec((RWKV_WIDTH, RWKV_WIDTH), fixed)]


def _rwkv_prompt(pr, weights, *, tb):
    b, t, _ = pr.shape
    fixed = lambda i, j: (0, 0)
    return pl.pallas_call(
        functools.partial(_rwkv_prompt_kernel, tb),
        grid=(b, t // tb),
        in_specs=[pl.BlockSpec((None, tb, RWKV_COLS), lambda i, j: (i, j, 0))] + _rwkv_weight_specs(fixed),
        out_specs=[pl.BlockSpec((None, tb, RWKV_WIDTH), lambda i, j: (i, j, 0)),
                   pl.BlockSpec((None, RWKV_HEADS, RWKV_N, RWKV_N), lambda i, j: (i, 0, 0, 0))],
        out_shape=[jax.ShapeDtypeStruct((b, t, RWKV_WIDTH), BF16),
                   jax.ShapeDtypeStruct((b, RWKV_HEADS, RWKV_N, RWKV_N), F32)],
        scratch_shapes=[pltpu.VMEM((RWKV_HEADS // 2, 2 * RWKV_N, 2 * RWKV_N), F32),
                        pltpu.VMEM((tb + SUBLANES, RWKV_COLS), F32),
                        pltpu.VMEM((tb, RWKV_WIDTH), F32)],
        compiler_params=_params(("parallel", "arbitrary")),
        name="rwkv_prompt",
    )(pr, *weights)


def _rwkv_step_kernel(bb, pr_ref, sh_ref, s_ref, mu_ref, w0_ref, a0_ref, wwa_ref, g2_ref, kk_ref, ka_ref,
                      rk_ref, lnw_ref, lnb_ref, ones_ref, o_ref, s_out_ref, y_scr):
    n = RWKV_N
    p = pr_ref[...]
    xs = p + (sh_ref[...] - p) * mu_ref[...]
    head_ones = ones_ref[...]
    r, k, v, e, kk, kka, g = _rwkv_prep(xs, w0_ref[...], a0_ref[...], wwa_ref[...], g2_ref[...],
                                         kk_ref[...], ka_ref[...], head_ones)
    decay = jnp.exp(-e)
    eye = jnp.where(_iota2((n, n), 0) == _iota2((n, n), 1), 1.0, 0.0)
    for h in range(RWKV_HEADS):
        lanes = slice(h * n, (h + 1) * n)
        r_h, k_h, v_h, d_h, kk_h, kka_h = r[:, lanes], k[:, lanes], v[:, lanes], decay[:, lanes], kk[:, lanes], kka[:, lanes]
        for i in range(bb):
            row = slice(i, i + 1)
            s = s_ref[i, h]
            sa = jnp.sum(s * (-kk_h[row, :]), axis=-1, keepdims=True)
            v_col = jnp.sum(eye * v_h[row, :], axis=-1, keepdims=True)
            s = s * d_h[row, :] + sa * kka_h[row, :] + v_col * k_h[row, :]
            s_out_ref[i, h] = s
            y_col = jnp.sum(s * r_h[row, :], axis=-1, keepdims=True)
            y_scr[row, lanes] = jnp.sum(eye * y_col, axis=0, keepdims=True)
    o_ref[...] = _rwkv_out(y_scr[...], r, k, v, g, rk_ref[...], lnw_ref[...], lnb_ref[...], head_ones)


def _rwkv_step(pr, shift, s0, weights, *, bb):
    b = pr.shape[0]
    row = lambda i: (i, 0)
    fixed = lambda i: (0, 0)
    state_spec = pl.BlockSpec((bb, RWKV_HEADS, RWKV_N, RWKV_N), lambda i: (i, 0, 0, 0))
    return pl.pallas_call(
        functools.partial(_rwkv_step_kernel, bb),
        grid=(b // bb,),
        in_specs=[pl.BlockSpec((bb, RWKV_COLS), row), pl.BlockSpec((bb, RWKV_COLS), row), state_spec]
        + _rwkv_weight_specs(fixed),
        out_specs=[pl.BlockSpec((bb, RWKV_WIDTH), row), state_spec],
        out_shape=[jax.ShapeDtypeStruct((b, RWKV_WIDTH), F32), jax.ShapeDtypeStruct(s0.shape, F32)],
        scratch_shapes=[pltpu.VMEM((bb, RWKV_WIDTH), F32)],
        compiler_params=_params(("parallel",)),
        name="rwkv_step",
    )(pr, shift, s0, *weights)


def _pad_lanes(x, width):
    return jnp.pad(x, ((0, 0), (0, width - x.shape[1])))


def kernel(x_prompt, x_sample, state_gdn_conv, state_gdn, state_rwkv_shift, state_rwkv, norm_ffn1, w_ffn1_in, w_ffn1_out, norm_mix, w_in, gdn_conv_w, gdn_a_log, gdn_dt_bias, gdn_norm_w, rwkv_mu, rwkv_w0, rwkv_w2, rwkv_a0, rwkv_a2, rwkv_g2, rwkv_k_k, rwkv_k_a, rwkv_r_k, rwkv_ln_w, rwkv_ln_b, w_out, norm_ffn2, w_ffn2_in, w_ffn2_out, norm_final):
    depth = norm_ffn1.shape[0]
    assert depth == 1, "the carried-state plumbing below is written for a single layer"
    b, t, _ = x_prompt.shape
    bs = x_sample.shape[0]
    assert x_sample.shape[1] == 1
    l = 0

    n_gdn_cols = GDN_MAIN + 2 * GDN_HEADS
    w_in_l = w_in[l]
    w_perm = jnp.concatenate(
        [w_in_l[:, :GDN_MAIN], w_in_l[:, n_gdn_cols:], w_in_l[:, GDN_MAIN:n_gdn_cols],
         jnp.zeros((D_MODEL, AB_PAD - 2 * GDN_HEADS), w_in.dtype)], axis=1).astype(BF16)
    w1a, w1b = w_ffn1_in[l].astype(BF16), w_ffn1_out[l].astype(BF16)
    w2a, w2b = w_ffn2_in[l].astype(BF16), w_ffn2_out[l].astype(BF16)
    woa, wob = w_out[l, :GDN_WIDTH].astype(BF16), w_out[l, GDN_WIDTH:].astype(BF16)
    half = LORA_WA // 2
    wwa = jnp.zeros((LORA_WA, 2 * RWKV_WIDTH), F32)
    wwa = wwa.at[:half, :RWKV_WIDTH].set(rwkv_w2[l]).at[half:, RWKV_WIDTH:].set(rwkv_a2[l]).astype(BF16)
    head_id = jnp.arange(RWKV_WIDTH) // RWKV_N
    head_ones = (head_id[:, None] == head_id[None, :]).astype(BF16)
    rwkv_w = (rwkv_mu[l][None], rwkv_w0[l][None], rwkv_a0[l][None], wwa, rwkv_g2[l].astype(BF16),
              rwkv_k_k[l][None], rwkv_k_a[l][None], rwkv_r_k[l].reshape(1, RWKV_WIDTH),
              rwkv_ln_w[l][None], rwkv_ln_b[l][None], head_ones)
    alog = _pad_lanes(gdn_a_log[l][None], AB_PAD)
    dtb = _pad_lanes(gdn_dt_bias[l][None], AB_PAD)
    gdn_w = (gdn_conv_w[l], alog, dtb, gdn_norm_w[l][None])
    nf1, nmix, nf2, nfin = norm_ffn1[l][None], norm_mix[l][None], norm_ffn2[l][None], norm_final[None]

    xp = x_prompt.reshape(b * t, D_MODEL)
    hp = _ffn(xp, nf1, w1a, w1b, tm=512, tf=1408)
    pg, pr, pab = _proj_in(hp, nmix, w_perm, tm=512)
    pg3, pr3, pab3 = pg.reshape(b, t, GDN_MAIN), pr.reshape(b, t, RWKV_COLS), pab.reshape(b, t, AB_PAD)
    oa, gdn_s = _gdn_prompt(pg3, pab3, *gdn_w, tb=256)
    ob, wkv_s = _rwkv_prompt(pr3, rwkv_w, tb=256)
    yp = _ffn(hp, nf2, w2a, w2b, mix=(oa.reshape(b * t, GDN_WIDTH), ob.reshape(b * t, RWKV_WIDTH), woa, wob),
              final_w=nfin, tm=512, tf=1408)

    xs = x_sample.reshape(bs, D_MODEL)
    hs = _ffn(xs, nf1, w1a, w1b, tm=bs, tf=1408)
    sg, sr, sab = _proj_in(hs, nmix, w_perm, tm=bs)
    conv_in = state_gdn_conv[l].reshape(bs, (CONV_W - 1) * GDN_QKV)
    oa_s, conv_s, gdn_ss = _gdn_step(sg, sab, conv_in, state_gdn[l], *gdn_w, bb=8)
    ob_s, wkv_ss = _rwkv_step(sr, state_rwkv_shift[l], state_rwkv[l], rwkv_w, bb=8)
    ys = _ffn(hs, nf2, w2a, w2b, mix=(oa_s, ob_s, woa, wob), final_w=nfin, tm=bs, tf=1408)

    return (yp.reshape(b, t, D_MODEL), ys.reshape(bs, 1, D_MODEL),
            pg3[:, t - (CONV_W - 1):, :GDN_QKV][None], gdn_s[None], pr3[:, t - 1, :][None], wkv_s[None],
            conv_s.reshape(1, bs, CONV_W - 1, GDN_QKV), gdn_ss[None], sr[None], wkv_ss[None])
```

```python
import functools

import jax
import jax.numpy as jnp
from jax import lax
from jax.experimental import pallas as pl
from jax.experimental.pallas import tpu as pltpu

F32 = jnp.float32
BF16 = jnp.bfloat16

D_MODEL = 1024
D_FF = 2816
CONV_W = 4
GDN_HEADS = 4
GDN_DK = 128
GDN_QK = GDN_HEADS * GDN_DK
GDN_WIDTH = GDN_HEADS * GDN_DK
GDN_QKV = 3 * GDN_WIDTH
GDN_MAIN = GDN_QKV + GDN_WIDTH
RWKV_HEADS = 8
RWKV_N = 64
RWKV_WIDTH = RWKV_HEADS * RWKV_N
LORA_WA = 128
G_LORA = 128
RWKV_COLS = 3 * RWKV_WIDTH + LORA_WA + G_LORA
LANES = 128
SUBLANES = 8
AB_PAD = LANES
P_COLS = GDN_MAIN + RWKV_COLS + AB_PAD
NORM_EPS = 1e-6
GN_EPS = 64e-5
L2_EPS = 1e-6

GDN_CHUNK = 128
RWKV_CHUNK = 64
VMEM_LIMIT = 56 * 1024 * 1024


_NN = (((1,), (0,)), ((), ()))
_NT = (((1,), (1,)), ((), ()))


def _mm(a, b, dims=_NN):
    return lax.dot_general(a.astype(BF16), b.astype(BF16), dims, preferred_element_type=F32)


def _mm_exact(a, b):
    return lax.dot_general(a, b, _NN, precision=lax.Precision.HIGHEST, preferred_element_type=F32)


def _mm2(x, w_bf16):
    hi = x.astype(BF16)
    lo = (x - hi.astype(F32)).astype(BF16)
    return (jnp.dot(hi, w_bf16, preferred_element_type=F32)
            + jnp.dot(lo, w_bf16, preferred_element_type=F32))


def _sigmoid(x):
    return 1.0 / (1.0 + jnp.exp(-x))


def _silu(x):
    return x * _sigmoid(x)


def _softplus(x):
    return jnp.maximum(x, 0.0) + jnp.log(1.0 + jnp.exp(-jnp.abs(x)))


def _rms(x, g):
    return x * lax.rsqrt(jnp.mean(x * x, axis=-1, keepdims=True) + NORM_EPS) * g


def _iota2(shape, dim):
    return lax.broadcasted_iota(jnp.int32, shape, dim)


def _params(sem):
    return pltpu.CompilerParams(dimension_semantics=sem, vmem_limit_bytes=VMEM_LIMIT)


def _ffn_kernel(has_mix, final_norm, sub, *refs):
    it = iter(refs)
    x_ref = next(it)
    if has_mix:
        oa_ref, ob_ref, woa_ref, wob_ref = next(it), next(it), next(it), next(it)
    nw_ref, w1_ref, w2_ref = next(it), next(it), next(it)
    fw_ref = next(it) if final_norm else None
    out_ref = next(it)
    for s in range(x_ref.shape[0] // sub):
        rows = slice(s * sub, (s + 1) * sub)
        x = x_ref[rows, :]
        if has_mix:
            x = x + _mm(oa_ref[rows, :], woa_ref[...]) + _mm(ob_ref[rows, :], wob_ref[...])
        xn = _rms(x, nw_ref[...]).astype(BF16)
        gate = jnp.dot(xn, w1_ref[:, 0:D_FF], preferred_element_type=F32)
        up = jnp.dot(xn, w1_ref[:, D_FF:2 * D_FF], preferred_element_type=F32)
        act = (_silu(gate) * up).astype(BF16)
        h = x + 0.5 * jnp.dot(act, w2_ref[...], preferred_element_type=F32)
        if final_norm:
            h = _rms(h, fw_ref[...])
        out_ref[rows, :] = h


def _resident(shape):
    return pl.BlockSpec(shape, lambda i: (0,) * len(shape), pipeline_mode=pl.Buffered(1))


def _ffn(x, norm_w, w1, w2, *, mix=None, final_w=None, tm, sub):
    m = x.shape[0]
    has_mix = mix is not None
    final_norm = final_w is not None
    row = lambda i: (i, 0)
    in_specs = [pl.BlockSpec((tm, D_MODEL), row)]
    args = [x]
    if has_mix:
        oa, ob, woa, wob = mix
        in_specs += [pl.BlockSpec((tm, GDN_WIDTH), row), pl.BlockSpec((tm, RWKV_WIDTH), row),
                     _resident((GDN_WIDTH, D_MODEL)), _resident((RWKV_WIDTH, D_MODEL))]
        args += [oa, ob, woa, wob]
    in_specs += [_resident((1, D_MODEL)), _resident((D_MODEL, 2 * D_FF)), _resident((D_FF, D_MODEL))]
    args += [norm_w, w1, w2]
    if final_norm:
        in_specs.append(_resident((1, D_MODEL)))
        args.append(final_w)
    return pl.pallas_call(
        functools.partial(_ffn_kernel, has_mix, final_norm, sub),
        grid=(m // tm,),
        in_specs=in_specs,
        out_specs=pl.BlockSpec((tm, D_MODEL), row),
        out_shape=jax.ShapeDtypeStruct((m, D_MODEL), F32),
        compiler_params=_params(("parallel",)),
        name="ffn_mix" if has_mix else "ffn",
    )(*args)


def _proj_in_kernel(h_ref, nw_ref, w_ref, pg_ref, pr_ref, pab_ref):
    n = _rms(h_ref[...], nw_ref[...]).astype(BF16)
    pg_ref[...] = jnp.dot(n, w_ref[:, 0:GDN_MAIN], preferred_element_type=F32)
    pr_ref[...] = jnp.dot(n, w_ref[:, GDN_MAIN:GDN_MAIN + RWKV_COLS], preferred_element_type=F32)
    pab_ref[...] = jnp.dot(n, w_ref[:, GDN_MAIN + RWKV_COLS:P_COLS], preferred_element_type=F32)


def _proj_in(h, norm_w, w_perm, *, tm):
    m = h.shape[0]
    row = lambda i: (i, 0)
    fixed = lambda i: (0, 0)
    return pl.pallas_call(
        _proj_in_kernel,
        grid=(m // tm,),
        in_specs=[pl.BlockSpec((tm, D_MODEL), row), pl.BlockSpec((1, D_MODEL), fixed),
                  pl.BlockSpec((D_MODEL, P_COLS), fixed)],
        out_specs=[pl.BlockSpec((tm, GDN_MAIN), row), pl.BlockSpec((tm, RWKV_COLS), row),
                   pl.BlockSpec((tm, AB_PAD), row)],
        out_shape=[jax.ShapeDtypeStruct((m, GDN_MAIN), F32), jax.ShapeDtypeStruct((m, RWKV_COLS), F32),
                   jax.ShapeDtypeStruct((m, AB_PAD), F32)],
        compiler_params=_params(("parallel",)),
        name="proj_in",
    )(h, norm_w, w_perm)


def _inv_unit_lower(lows, stack=None):
    shape = lows[0].shape
    n = shape[0]
    ri = _iota2(shape, 0)
    ci = _iota2(shape, 1) & (n - 1)
    eye = jnp.where(ri == ci, 1.0, 0.0)
    pair_blk = (ri >> 1) == (ci >> 1)
    ts = [eye - jnp.where(pair_blk, low, 0.0) for low in lows]
    s, lg = 2, 1
    while s < n:
        sel = ((ri >> (lg + 1)) == (ci >> (lg + 1))) & ((ri >> lg) != (ci >> lg))
        offs = [jnp.where(sel, low, 0.0) for low in lows]
        if stack is None:
            xs = [_mm(t, off) for t, off in zip(ts, offs)]
            ts = [t - _mm(x, t) for t, x in zip(ts, xs)]
        else:
            xs = [_mm(t, stack(off)) for t, off in zip(ts, offs)]
            ts = [t - _mm(x, stack(t)) for t, x in zip(ts, xs)]
        s, lg = 2 * s, lg + 1
    return ts


def _gdn_gates(ab, alog, dtb):
    log_alpha = -jnp.exp(alog) * _softplus(ab + dtb)
    return log_alpha, _sigmoid(ab)


def _gdn_out(o, z, norm_w):
    o = o * lax.rsqrt(jnp.mean(o * o, axis=-1, keepdims=True) + NORM_EPS) * norm_w
    return o * _silu(z)


def _gdn_prompt_kernel(tb, pg_ref, pab_ref, cw_ref, alog_ref, dtb_ref, nw_ref, o_ref, s_out_ref,
                       s_scr, ext_scr):
    t = pl.program_id(1)
    c = GDN_CHUNK

    @pl.when(t == 0)
    def _():
        s_scr[...] = jnp.zeros_like(s_scr)
        ext_scr[0:SUBLANES, :] = jnp.zeros((SUBLANES, GDN_QKV), F32)

    ext_scr[SUBLANES:SUBLANES + tb, :] = pg_ref[:, 0:GDN_QKV]
    acc = ext_scr[pl.ds(SUBLANES, tb), :] * cw_ref[CONV_W - 1:CONV_W, :]
    for j in range(1, CONV_W):
        acc = acc + ext_scr[pl.ds(SUBLANES - j, tb), :] * cw_ref[CONV_W - 1 - j:CONV_W - j, :]
    ext_scr[0:SUBLANES, :] = ext_scr[tb:tb + SUBLANES, :]
    qkv = _silu(acc)

    log_alpha, beta_all = _gdn_gates(pab_ref[...], alog_ref[...], dtb_ref[...])
    ri = _iota2((c, c), 0)
    ci = _iota2((c, c), 1)
    causal = ri >= ci
    strict = ri > ci
    tri = jnp.where(causal, 1.0, 0.0)

    items = [(blk, h) for blk in range(tb // c) for h in range(GDN_HEADS)]
    g_blk, gt_blk = [], []
    for blk in range(tb // c):
        g = _mm_exact(tri, log_alpha[blk * c:(blk + 1) * c, :])
        g_blk.append(g)
        gt_blk.append(g.T)
    qs, ks, vbs, kbs, decays, g_cols = [], [], [], [], [], []
    for blk, h in items:
        rows = slice(blk * c, (blk + 1) * c)
        q = qkv[rows, h * GDN_DK:(h + 1) * GDN_DK]
        k = qkv[rows, GDN_QK + h * GDN_DK:GDN_QK + (h + 1) * GDN_DK]
        v = qkv[rows, 2 * GDN_QK + h * GDN_DK:2 * GDN_QK + (h + 1) * GDN_DK]
        q = q * lax.rsqrt(jnp.sum(q * q, axis=-1, keepdims=True) + L2_EPS) * (GDN_DK ** -0.5)
        k = k * lax.rsqrt(jnp.sum(k * k, axis=-1, keepdims=True) + L2_EPS)
        beta = beta_all[rows, GDN_HEADS + h:GDN_HEADS + h + 1]
        g_col = g_blk[blk][:, h:h + 1]
        g_row = gt_blk[blk][h:h + 1, :]
        decays.append(jnp.exp(jnp.where(causal, g_col - g_row, -jnp.inf)))
        qs.append(q)
        ks.append(k)
        vbs.append(v * beta)
        kbs.append(k * beta)
        g_cols.append(g_col)
    kk = [_mm(kb, k, _NT) for kb, k in zip(kbs, ks)]
    qk = [_mm(q, k, _NT) for q, k in zip(qs, ks)]
    tinvs = _inv_unit_lower([jnp.where(strict, x * d, 0.0) for x, d in zip(kk, decays)])
    attns = [x * d for x, d in zip(qk, decays)]
    egs = [jnp.exp(g_col) for g_col in g_cols]
    us = [_mm(tinv, vb) for tinv, vb in zip(tinvs, vbs)]
    ws = [_mm(tinv, kb * eg) for tinv, kb, eg in zip(tinvs, kbs, egs)]

    states = [s_scr[h] for h in range(GDN_HEADS)]
    for blk in range(tb // c):
        rows = slice(blk * c, (blk + 1) * c)
        idx = [blk * GDN_HEADS + h for h in range(GDN_HEADS)]
        v_new = [us[i] - _mm(ws[i], states[h]) for h, i in enumerate(idx)]
        o_s = [_mm(qs[i] * egs[i], states[h]) for h, i in enumerate(idx)]
        o_v = [_mm(attns[i], v_new[h]) for h, i in enumerate(idx)]
        for h, i in enumerate(idx):
            g_last = g_cols[i][c - 1:c, :]
            k_dec = ks[i] * jnp.exp(g_last - g_cols[i])
            states[h] = states[h] * jnp.exp(g_last) + _mm(k_dec.T, v_new[h])
            z = pg_ref[rows, GDN_QKV + h * GDN_DK:GDN_QKV + (h + 1) * GDN_DK]
            o_ref[rows, h * GDN_DK:(h + 1) * GDN_DK] = _gdn_out(o_s[h] + o_v[h], z, nw_ref[...]).astype(o_ref.dtype)
    for h in range(GDN_HEADS):
        s_scr[h] = states[h]

    @pl.when(t == pl.num_programs(1) - 1)
    def _():
        s_out_ref[...] = s_scr[...]


def _gdn_prompt(pg, pab, conv_w, alog, dtb, norm_w, *, tb):
    b, t, _ = pg.shape
    fixed = lambda i, j: (0, 0)
    return pl.pallas_call(
        functools.partial(_gdn_prompt_kernel, tb),
        grid=(b, t // tb),
        in_specs=[pl.BlockSpec((None, tb, GDN_MAIN), lambda i, j: (i, j, 0)),
                  pl.BlockSpec((None, tb, AB_PAD), lambda i, j: (i, j, 0)),
                  pl.BlockSpec((CONV_W, GDN_QKV), fixed), pl.BlockSpec((1, AB_PAD), fixed),
                  pl.BlockSpec((1, AB_PAD), fixed), pl.BlockSpec((1, GDN_DK), fixed)],
        out_specs=[pl.BlockSpec((None, tb, GDN_WIDTH), lambda i, j: (i, j, 0)),
                   pl.BlockSpec((None, GDN_HEADS, GDN_DK, GDN_DK), lambda i, j: (i, 0, 0, 0))],
        out_shape=[jax.ShapeDtypeStruct((b, t, GDN_WIDTH), BF16),
                   jax.ShapeDtypeStruct((b, GDN_HEADS, GDN_DK, GDN_DK), F32)],
        scratch_shapes=[pltpu.VMEM((GDN_HEADS, GDN_DK, GDN_DK), F32),
                        pltpu.VMEM((tb + SUBLANES, GDN_QKV), F32)],
        compiler_params=_params(("parallel", "arbitrary")),
        name="gdn_prompt",
    )(pg, pab, conv_w, alog, dtb, norm_w)


def _gdn_step_kernel(bb, pg_ref, pab_ref, cs_ref, s_ref, cw_ref, alog_ref, dtb_ref, nw_ref,
                     o_ref, cs_out_ref, s_out_ref):
    u = pg_ref[:, 0:GDN_QKV]
    acc = u * cw_ref[CONV_W - 1:CONV_W, :]
    for j in range(CONV_W - 1):
        acc = acc + cs_ref[:, j * GDN_QKV:(j + 1) * GDN_QKV] * cw_ref[j:j + 1, :]
    cs_out_ref[:, 0:(CONV_W - 2) * GDN_QKV] = cs_ref[:, GDN_QKV:(CONV_W - 1) * GDN_QKV]
    cs_out_ref[:, (CONV_W - 2) * GDN_QKV:(CONV_W - 1) * GDN_QKV] = u
    qkv = _silu(acc)
    log_alpha, beta_all = _gdn_gates(pab_ref[...], alog_ref[...], dtb_ref[...])
    alpha_all = jnp.exp(log_alpha)
    eye = jnp.where(_iota2((GDN_DK, GDN_DK), 0) == _iota2((GDN_DK, GDN_DK), 1), 1.0, 0.0)

    def to_col(row):
        return jnp.sum(eye * row, axis=-1, keepdims=True)

    vs, k_cols, q_cols = [], [], []
    for h in range(GDN_HEADS):
        q = qkv[:, h * GDN_DK:(h + 1) * GDN_DK]
        k = qkv[:, GDN_QK + h * GDN_DK:GDN_QK + (h + 1) * GDN_DK]
        vs.append(qkv[:, 2 * GDN_QK + h * GDN_DK:2 * GDN_QK + (h + 1) * GDN_DK])
        q = q * lax.rsqrt(jnp.sum(q * q, axis=-1, keepdims=True) + L2_EPS) * (GDN_DK ** -0.5)
        k = k * lax.rsqrt(jnp.sum(k * k, axis=-1, keepdims=True) + L2_EPS)
        k_cols.append([to_col(k[i:i + 1, :]) for i in range(bb)])
        q_cols.append([to_col(q[i:i + 1, :]) for i in range(bb)])
    for h in range(GDN_HEADS):
        o_rows = []
        for i in range(bb):
            s = s_ref[i, h] * alpha_all[i:i + 1, h:h + 1]
            mem = jnp.sum(k_cols[h][i] * s, axis=0, keepdims=True)
            delta = (vs[h][i:i + 1, :] - mem) * beta_all[i:i + 1, GDN_HEADS + h:GDN_HEADS + h + 1]
            s = s + k_cols[h][i] * delta
            s_out_ref[i, h] = s
            o_rows.append(jnp.sum(q_cols[h][i] * s, axis=0, keepdims=True))
        o = jnp.concatenate(o_rows, axis=0)
        z = pg_ref[:, GDN_QKV + h * GDN_DK:GDN_QKV + (h + 1) * GDN_DK]
        o_ref[:, h * GDN_DK:(h + 1) * GDN_DK] = _gdn_out(o, z, nw_ref[...])


def _gdn_step(pg, pab, conv_state, s0, conv_w, alog, dtb, norm_w, *, bb):
    b = pg.shape[0]
    row = lambda i: (i, 0)
    fixed = lambda i: (0, 0)
    cs_cols = (CONV_W - 1) * GDN_QKV
    state_spec = pl.BlockSpec((bb, GDN_HEADS, GDN_DK, GDN_DK), lambda i: (i, 0, 0, 0))
    return pl.pallas_call(
        functools.partial(_gdn_step_kernel, bb),
        grid=(b // bb,),
        in_specs=[pl.BlockSpec((bb, GDN_MAIN), row), pl.BlockSpec((bb, AB_PAD), row),
                  pl.BlockSpec((bb, cs_cols), row), state_spec,
                  pl.BlockSpec((CONV_W, GDN_QKV), fixed), pl.BlockSpec((1, AB_PAD), fixed),
                  pl.BlockSpec((1, AB_PAD), fixed), pl.BlockSpec((1, GDN_DK), fixed)],
        out_specs=[pl.BlockSpec((bb, GDN_WIDTH), row), pl.BlockSpec((bb, cs_cols), row), state_spec],
        out_shape=[jax.ShapeDtypeStruct((b, GDN_WIDTH), F32), jax.ShapeDtypeStruct((b, cs_cols), F32),
                   jax.ShapeDtypeStruct(s0.shape, F32)],
        compiler_params=_params(("parallel",)),
        name="gdn_step",
    )(pg, pab, conv_state, s0, conv_w, alog, dtb, norm_w)


def _rwkv_prep(xs, w0, a0, wwa, g2, k_k, k_a, head_ones):
    w = RWKV_WIDTH
    r, k, v = xs[:, 0:w], xs[:, w:2 * w], xs[:, 2 * w:3 * w]
    wa_in = xs[:, 3 * w:3 * w + LORA_WA]
    lane = _iota2(wa_in.shape, 1)
    wa_in = jnp.where(lane < LORA_WA // 2, jnp.tanh(wa_in), wa_in)
    wa = _mm(wa_in, wwa)
    w_log = -_softplus(-(w0 + wa[:, 0:w])) - 0.5
    a = _sigmoid(a0 + wa[:, w:2 * w])
    g = _mm(_sigmoid(xs[:, 3 * w + LORA_WA:RWKV_COLS]), g2)
    kx = k * k_k
    kk = kx * lax.rsqrt(_mm2(kx * kx, head_ones) + L2_EPS)
    k = k * (1.0 + (a - 1.0) * k_a)
    return r, k, v, jnp.exp(w_log), kk, kk * a, g


def _rwkv_out(y, r, k, v, g, r_k, ln_w, ln_b, head_ones):
    inv_n = 1.0 / RWKV_N
    mean = _mm2(y, head_ones) * inv_n
    yc = y - mean
    var = _mm2(yc * yc, head_ones) * inv_n
    y = yc * lax.rsqrt(var + GN_EPS) * ln_w + ln_b
    bonus = _mm2(r * k * r_k, head_ones) * v
    return (y + bonus) * g


def _rwkv_prompt_kernel(tb, pr_ref, mu_ref, w0_ref, a0_ref, wwa_ref, g2_ref, kk_ref, ka_ref, rk_ref,
                        lnw_ref, lnb_ref, ones_ref, o_ref, s_out_ref, s_scr, ext_scr, y_scr):
    t = pl.program_id(1)
    c = RWKV_CHUNK
    n = RWKV_N
    pair = 2 * n
    lg_c = c.bit_length() - 1
    lg_n = n.bit_length() - 1

    @pl.when(t == 0)
    def _():
        s_scr[...] = jnp.zeros_like(s_scr)
        ext_scr[0:SUBLANES, :] = jnp.zeros((SUBLANES, RWKV_COLS), F32)

    p = pr_ref[...]
    ext_scr[SUBLANES:SUBLANES + tb, :] = p
    p_prev = ext_scr[pl.ds(SUBLANES - 1, tb), :]
    ext_scr[0:SUBLANES, :] = ext_scr[tb:tb + SUBLANES, :]
    xs = p + (p_prev - p) * mu_ref[...]
    head_ones = ones_ref[...]
    r, k, v, e, kk, kka, g = _rwkv_prep(xs, w0_ref[...], a0_ref[...], wwa_ref[...], g2_ref[...],
                                         kk_ref[...], ka_ref[...], head_ones)

    ri = _iota2((tb, tb), 0)
    ci = _iota2((tb, tb), 1)
    tri = jnp.where((ri >= ci) & ((ri >> lg_c) == (ci >> lg_c)), 1.0, 0.0)
    lg_inc = -_mm_exact(tri, e)
    lg_exc = lg_inc + e
    a_t = -kk * jnp.exp(lg_exc)
    r_t = r * jnp.exp(lg_inc)
    inv_g = jnp.exp(-lg_inc)
    b_t = kka * inv_g
    k_t = k * inv_g

    lane = _iota2((1, pair), 1)
    m0 = jnp.where(lane < n, 1.0, 0.0)
    m1 = 1.0 - m0

    def stack(z):
        return jnp.concatenate([z * m0, z * m1], axis=0)

    tt = _iota2((c, pair), 0)
    ss = _iota2((c, pair), 1) & (n - 1)
    strict = tt > ss
    incl = tt >= ss
    bd = jnp.where((_iota2((pair, pair), 0) >> lg_n) == (_iota2((pair, pair), 1) >> lg_n), 1.0, 0.0)

    n_pairs = RWKV_HEADS // 2
    items = [(blk, hp) for blk in range(tb // c) for hp in range(n_pairs)]

    def tile(x, blk, hp):
        return x[blk * c:(blk + 1) * c, hp * pair:(hp + 1) * pair]

    ats = [tile(a_t, *it) for it in items]
    rts = [tile(r_t, *it) for it in items]
    vts = [tile(v, *it) for it in items]
    svs = [stack(vt) for vt in vts]
    scs = [_mm(jnp.concatenate([at, rt], axis=0),
               jnp.concatenate([stack(tile(b_t, *it)), stack(tile(k_t, *it))], axis=0), _NT)
           for at, rt, it in zip(ats, rts, items)]
    tinvs = _inv_unit_lower([-jnp.where(strict, sc[0:c, 0:pair], 0.0) for sc in scs], stack)
    akv = [_mm(jnp.where(strict, sc[0:c, pair:2 * pair], 0.0), sv) for sc, sv in zip(scs, svs)]
    y_v = [_mm(jnp.where(incl, sc[c:2 * c, pair:2 * pair], 0.0), sv) for sc, sv in zip(scs, svs)]
    p_rbs = [jnp.where(incl, sc[c:2 * c, 0:pair], 0.0) for sc in scs]
    w_ts = [_mm(tinv, stack(at)) for tinv, at in zip(tinvs, ats)]
    u0s = [_mm(tinv, stack(x)) for tinv, x in zip(tinvs, akv)]

    states = [s_scr[hp] for hp in range(n_pairs)]
    for blk in range(tb // c):
        rows = slice(blk * c, (blk + 1) * c)
        idx = [blk * n_pairs + hp for hp in range(n_pairs)]
        us = [u0s[i] + _mm(w_ts[i], states[hp], _NT) for hp, i in enumerate(idx)]
        y_s = [_mm(rts[i], states[hp], _NT) for hp, i in enumerate(idx)]
        y_u = [_mm(p_rbs[i], stack(us[hp])) for hp, i in enumerate(idx)]
        for hp, i in enumerate(idx):
            lanes = slice(hp * pair, (hp + 1) * pair)
            lg_last = lg_inc[blk * c + c - 1:blk * c + c, lanes]
            to_end = jnp.exp(lg_last - lg_inc[rows, lanes])
            ends = jnp.concatenate([kka[rows, lanes] * to_end, k[rows, lanes] * to_end], axis=0)
            upd = _mm(jnp.concatenate([us[hp], vts[i]], axis=0).T, ends)
            states[hp] = states[hp] * jnp.exp(lg_last) + bd * upd
            y_scr[rows, lanes] = y_s[hp] + y_u[hp] + y_v[i]
    for hp in range(n_pairs):
        s_scr[hp] = states[hp]

    out = _rwkv_out(y_scr[...], r, k, v, g, rk_ref[...], lnw_ref[...], lnb_ref[...], head_ones)
    o_ref[...] = out.astype(o_ref.dtype)

    @pl.when(t == pl.num_programs(1) - 1)
    def _():
        for hp in range(RWKV_HEADS // 2):
            s = s_scr[hp]
            s_out_ref[2 * hp] = s[0:n, 0:n]
            s_out_ref[2 * hp + 1] = s[n:pair, n:pair]


def _rwkv_weight_specs(fixed):
    vec = pl.BlockSpec((1, RWKV_WIDTH), fixed)
    return [pl.BlockSpec((1, RWKV_COLS), fixed), vec, vec,
            pl.BlockSpec((LORA_WA, 2 * RWKV_WIDTH), fixed), pl.BlockSpec((G_LORA, RWKV_WIDTH), fixed),
            vec, vec, vec, vec, vec, pl.BlockSpec((RWKV_WIDTH, RWKV_WIDTH), fixed)]


def _rwkv_prompt(pr, weights, *, tb):
    b, t, _ = pr.shape
    fixed = lambda i, j: (0, 0)
    return pl.pallas_call(
        functools.partial(_rwkv_prompt_kernel, tb),
        grid=(b, t // tb),
        in_specs=[pl.BlockSpec((None, tb, RWKV_COLS), lambda i, j: (i, j, 0))] + _rwkv_weight_specs(fixed),
        out_specs=[pl.BlockSpec((None, tb, RWKV_WIDTH), lambda i, j: (i, j, 0)),
                   pl.BlockSpec((None, RWKV_HEADS, RWKV_N, RWKV_N), lambda i, j: (i, 0, 0, 0))],
        out_shape=[jax.ShapeDtypeStruct((b, t, RWKV_WIDTH), BF16),
                   jax.ShapeDtypeStruct((b, RWKV_HEADS, RWKV_N, RWKV_N), F32)],
        scratch_shapes=[pltpu.VMEM((RWKV_HEADS // 2, 2 * RWKV_N, 2 * RWKV_N), F32),
                        pltpu.VMEM((tb + SUBLANES, RWKV_COLS), F32),
                        pltpu.VMEM((tb, RWKV_WIDTH), F32)],
        compiler_params=_params(("parallel", "arbitrary")),
        name="rwkv_prompt",
    )(pr, *weights)


def _rwkv_step_kernel(bb, pr_ref, sh_ref, s_ref, mu_ref, w0_ref, a0_ref, wwa_ref, g2_ref, kk_ref, ka_ref,
                      rk_ref, lnw_ref, lnb_ref, ones_ref, o_ref, s_out_ref):
    n = RWKV_N
    p = pr_ref[...]
    xs = p + (sh_ref[...] - p) * mu_ref[...]
    head_ones = ones_ref[...]
    r, k, v, e, kk, kka, g = _rwkv_prep(xs, w0_ref[...], a0_ref[...], wwa_ref[...], g2_ref[...],
                                         kk_ref[...], ka_ref[...], head_ones)
    decay = jnp.exp(-e)
    eye = jnp.where(_iota2((n, n), 0) == _iota2((n, n), 1), 1.0, 0.0)
    items = [(h, i) for h in range(RWKV_HEADS) for i in range(bb)]
    n_tiles = RWKV_WIDTH // LANES

    def per_head(x):
        x_roll = pltpu.roll(x, n, axis=1)
        out = []
        for h in range(RWKV_HEADS):
            src, tile = (x, h // 2) if h % 2 == 0 else (x_roll, (h // 2 + 1) % n_tiles)
            out.append(src[:, tile * LANES:tile * LANES + n])
        return out

    v_h, kk_h, kka_h, k_h, d_h, r_h = (per_head(x) for x in (v, -kk, kka, k, decay, r))

    def vec(xs, h, i):
        return xs[h][i:i + 1, :]

    v_cols = [jnp.sum(eye * vec(v_h, h, i), axis=-1, keepdims=True) for h, i in items]
    sas = [jnp.sum(s_ref[i, h] * vec(kk_h, h, i), axis=-1, keepdims=True) for h, i in items]
    for (h, i), sa, v_col in zip(items, sas, v_cols):
        s_out_ref[i, h] = s_ref[i, h] * vec(d_h, h, i) + sa * vec(kka_h, h, i) + v_col * vec(k_h, h, i)
    y_cols = [jnp.sum(s_out_ref[i, h] * vec(r_h, h, i), axis=-1, keepdims=True) for h, i in items]
    y_heads = [jnp.concatenate([jnp.sum(eye * y_cols[h * bb + i], axis=0, keepdims=True) for i in range(bb)], axis=0)
               for h in range(RWKV_HEADS)]
    y = jnp.concatenate(y_heads, axis=1)
    o_ref[...] = _rwkv_out(y, r, k, v, g, rk_ref[...], lnw_ref[...], lnb_ref[...], head_ones)


def _rwkv_step(pr, shift, s0, weights, *, bb):
    b = pr.shape[0]
    row = lambda i: (i, 0)
    fixed = lambda i: (0, 0)
    state_spec = pl.BlockSpec((bb, RWKV_HEADS, RWKV_N, RWKV_N), lambda i: (i, 0, 0, 0))
    return pl.pallas_call(
        functools.partial(_rwkv_step_kernel, bb),
        grid=(b // bb,),
        in_specs=[pl.BlockSpec((bb, RWKV_COLS), row), pl.BlockSpec((bb, RWKV_COLS), row), state_spec]
        + _rwkv_weight_specs(fixed),
        out_specs=[pl.BlockSpec((bb, RWKV_WIDTH), row), state_spec],
        out_shape=[jax.ShapeDtypeStruct((b, RWKV_WIDTH), F32), jax.ShapeDtypeStruct(s0.shape, F32)],
        compiler_params=_params(("parallel",)),
        name="rwkv_step",
    )(pr, shift, s0, *weights)


def _pad_lanes(x, width):
    return jnp.pad(x, ((0, 0), (0, width - x.shape[1])))


def kernel(x_prompt, x_sample, state_gdn_conv, state_gdn, state_rwkv_shift, state_rwkv, norm_ffn1, w_ffn1_in, w_ffn1_out, norm_mix, w_in, gdn_conv_w, gdn_a_log, gdn_dt_bias, gdn_norm_w, rwkv_mu, rwkv_w0, rwkv_w2, rwkv_a0, rwkv_a2, rwkv_g2, rwkv_k_k, rwkv_k_a, rwkv_r_k, rwkv_ln_w, rwkv_ln_b, w_out, norm_ffn2, w_ffn2_in, w_ffn2_out, norm_final):
    depth = norm_ffn1.shape[0]
    assert depth == 1, "the carried-state plumbing below is written for a single layer"
    b, t, _ = x_prompt.shape
    bs = x_sample.shape[0]
    assert x_sample.shape[1] == 1
    l = 0

    n_gdn_cols = GDN_MAIN + 2 * GDN_HEADS
    w_in_l = w_in[l]
    w_perm = jnp.concatenate(
        [w_in_l[:, :GDN_MAIN], w_in_l[:, n_gdn_cols:], w_in_l[:, GDN_MAIN:n_gdn_cols],
         jnp.zeros((D_MODEL, AB_PAD - 2 * GDN_HEADS), w_in.dtype)], axis=1).astype(BF16)
    w1a, w1b = w_ffn1_in[l].astype(BF16), w_ffn1_out[l].astype(BF16)
    w2a, w2b = w_ffn2_in[l].astype(BF16), w_ffn2_out[l].astype(BF16)
    woa, wob = w_out[l, :GDN_WIDTH].astype(BF16), w_out[l, GDN_WIDTH:].astype(BF16)
    half = LORA_WA // 2
    wwa = jnp.zeros((LORA_WA, 2 * RWKV_WIDTH), F32)
    wwa = wwa.at[:half, :RWKV_WIDTH].set(rwkv_w2[l]).at[half:, RWKV_WIDTH:].set(rwkv_a2[l]).astype(BF16)
    head_id = jnp.arange(RWKV_WIDTH) // RWKV_N
    head_ones = (head_id[:, None] == head_id[None, :]).astype(BF16)
    rwkv_w = (rwkv_mu[l][None], rwkv_w0[l][None], rwkv_a0[l][None], wwa, rwkv_g2[l].astype(BF16),
              rwkv_k_k[l][None], rwkv_k_a[l][None], rwkv_r_k[l].reshape(1, RWKV_WIDTH),
              rwkv_ln_w[l][None], rwkv_ln_b[l][None], head_ones)
    alog = _pad_lanes(gdn_a_log[l][None], AB_PAD)
    dtb = _pad_lanes(gdn_dt_bias[l][None], AB_PAD)
    gdn_w = (gdn_conv_w[l], alog, dtb, gdn_norm_w[l][None])
    nf1, nmix, nf2, nfin = norm_ffn1[l][None], norm_mix[l][None], norm_ffn2[l][None], norm_final[None]

    xp = x_prompt.reshape(b * t, D_MODEL)
    hp = _ffn(xp, nf1, w1a, w1b, tm=512, sub=256)
    pg, pr, pab = _proj_in(hp, nmix, w_perm, tm=512)
    pg3, pr3, pab3 = pg.reshape(b, t, GDN_MAIN), pr.reshape(b, t, RWKV_COLS), pab.reshape(b, t, AB_PAD)
    oa, gdn_s = _gdn_prompt(pg3, pab3, *gdn_w, tb=256)
    ob, wkv_s = _rwkv_prompt(pr3, rwkv_w, tb=256)
    yp = _ffn(hp, nf2, w2a, w2b, mix=(oa.reshape(b * t, GDN_WIDTH), ob.reshape(b * t, RWKV_WIDTH), woa, wob),
              final_w=nfin, tm=512, sub=256)

    xs = x_sample.reshape(bs, D_MODEL)
    hs = _ffn(xs, nf1, w1a, w1b, tm=bs, sub=bs)
    sg, sr, sab = _proj_in(hs, nmix, w_perm, tm=bs)
    conv_in = state_gdn_conv[l].reshape(bs, (CONV_W - 1) * GDN_QKV)
    oa_s, conv_s, gdn_ss = _gdn_step(sg, sab, conv_in, state_gdn[l], *gdn_w, bb=8)
    ob_s, wkv_ss = _rwkv_step(sr, state_rwkv_shift[l], state_rwkv[l], rwkv_w, bb=8)
    ys = _ffn(hs, nf2, w2a, w2b, mix=(oa_s, ob_s, woa, wob), final_w=nfin, tm=bs, sub=bs)

    return (yp.reshape(b, t, D_MODEL), ys.reshape(bs, 1, D_MODEL),
            pg3[:, t - (CONV_W - 1):, :GDN_QKV][None], gdn_s[None], pr3[:, t - 1, :][None], wkv_s[None],
            conv_s.reshape(1, bs, CONV_W - 1, GDN_QKV), gdn_ss[None], sr[None], wkv_ss[None])
```

```python
import functools

import jax
import jax.numpy as jnp
from jax import lax
from jax.experimental import pallas as pl
from jax.experimental.pallas import tpu as pltpu

F32 = jnp.float32
BF16 = jnp.bfloat16

D_MODEL = 1024
D_FF = 2816
CONV_W = 4
GDN_HEADS = 4
GDN_DK = 128
GDN_QK = GDN_HEADS * GDN_DK
GDN_WIDTH = GDN_HEADS * GDN_DK
GDN_QKV = 3 * GDN_WIDTH
GDN_MAIN = GDN_QKV + GDN_WIDTH
RWKV_HEADS = 8
RWKV_N = 64
RWKV_WIDTH = RWKV_HEADS * RWKV_N
LORA_WA = 128
G_LORA = 128
RWKV_COLS = 3 * RWKV_WIDTH + LORA_WA + G_LORA
LANES = 128
SUBLANES = 8
AB_PAD = LANES
P_COLS = GDN_MAIN + RWKV_COLS + AB_PAD
NORM_EPS = 1e-6
GN_EPS = 64e-5
L2_EPS = 1e-6

GDN_CHUNK = 128
RWKV_CHUNK = 64
VMEM_LIMIT = 56 * 1024 * 1024


_NN = (((1,), (0,)), ((), ()))
_NT = (((1,), (1,)), ((), ()))


def _mm(a, b, dims=_NN):
    return lax.dot_general(a.astype(BF16), b.astype(BF16), dims, preferred_element_type=F32)


def _mm_split3(sel, x):
    sel = sel.astype(BF16)
    out = None
    for _ in range(3):
        piece = x.astype(BF16)
        part = jnp.dot(sel, piece, preferred_element_type=F32)
        out = part if out is None else out + part
        x = x - piece.astype(F32)
    return out


def _sigmoid(x):
    return 1.0 / (1.0 + jnp.exp(-x))


def _silu(x):
    return x * _sigmoid(x)


def _softplus(x):
    return jnp.maximum(x, 0.0) + jnp.log(1.0 + jnp.exp(-jnp.abs(x)))


def _rms(x, g):
    return x * lax.rsqrt(jnp.mean(x * x, axis=-1, keepdims=True) + NORM_EPS) * g


def _iota2(shape, dim):
    return lax.broadcasted_iota(jnp.int32, shape, dim)


def _params(sem):
    return pltpu.CompilerParams(dimension_semantics=sem, vmem_limit_bytes=VMEM_LIMIT)


def _ffn_kernel(has_mix, final_norm, sub, *refs):
    it = iter(refs)
    x_ref = next(it)
    if has_mix:
        oa_ref, ob_ref, woa_ref, wob_ref = next(it), next(it), next(it), next(it)
    nw_ref, w1_ref, w2_ref = next(it), next(it), next(it)
    fw_ref = next(it) if final_norm else None
    out_ref = next(it)
    for s in range(x_ref.shape[0] // sub):
        rows = slice(s * sub, (s + 1) * sub)
        x = x_ref[rows, :]
        if has_mix:
            x = x + _mm(oa_ref[rows, :], woa_ref[...]) + _mm(ob_ref[rows, :], wob_ref[...])
        xn = _rms(x, nw_ref[...]).astype(BF16)
        gate = jnp.dot(xn, w1_ref[:, 0:D_FF], preferred_element_type=F32)
        up = jnp.dot(xn, w1_ref[:, D_FF:2 * D_FF], preferred_element_type=F32)
        act = (_silu(gate) * up).astype(BF16)
        h = x + 0.5 * jnp.dot(act, w2_ref[...], preferred_element_type=F32)
        if final_norm:
            h = _rms(h, fw_ref[...])
        out_ref[rows, :] = h


def _resident(shape):
    return pl.BlockSpec(shape, lambda i: (0,) * len(shape), pipeline_mode=pl.Buffered(1))


def _ffn(x, norm_w, w1, w2, *, mix=None, final_w=None, tm, sub):
    m = x.shape[0]
    has_mix = mix is not None
    final_norm = final_w is not None
    row = lambda i: (i, 0)
    in_specs = [pl.BlockSpec((tm, D_MODEL), row)]
    args = [x]
    if has_mix:
        oa, ob, woa, wob = mix
        in_specs += [pl.BlockSpec((tm, GDN_WIDTH), row), pl.BlockSpec((tm, RWKV_WIDTH), row),
                     _resident((GDN_WIDTH, D_MODEL)), _resident((RWKV_WIDTH, D_MODEL))]
        args += [oa, ob, woa, wob]
    in_specs += [_resident((1, D_MODEL)), _resident((D_MODEL, 2 * D_FF)), _resident((D_FF, D_MODEL))]
    args += [norm_w, w1, w2]
    if final_norm:
        in_specs.append(_resident((1, D_MODEL)))
        args.append(final_w)
    return pl.pallas_call(
        functools.partial(_ffn_kernel, has_mix, final_norm, sub),
        grid=(m // tm,),
        in_specs=in_specs,
        out_specs=pl.BlockSpec((tm, D_MODEL), row),
        out_shape=jax.ShapeDtypeStruct((m, D_MODEL), F32),
        compiler_params=_params(("parallel",)),
        name="ffn_mix" if has_mix else "ffn",
    )(*args)


def _proj_in_kernel(h_ref, nw_ref, w_ref, pg_ref, pr_ref, pab_ref):
    n = _rms(h_ref[...], nw_ref[...]).astype(BF16)
    pg_ref[...] = jnp.dot(n, w_ref[:, 0:GDN_MAIN], preferred_element_type=F32)
    pr_ref[...] = jnp.dot(n, w_ref[:, GDN_MAIN:GDN_MAIN + RWKV_COLS], preferred_element_type=F32)
    pab_ref[...] = jnp.dot(n, w_ref[:, GDN_MAIN + RWKV_COLS:P_COLS], preferred_element_type=F32)


def _proj_in(h, norm_w, w_perm, *, tm):
    m = h.shape[0]
    row = lambda i: (i, 0)
    fixed = lambda i: (0, 0)
    return pl.pallas_call(
        _proj_in_kernel,
        grid=(m // tm,),
        in_specs=[pl.BlockSpec((tm, D_MODEL), row), pl.BlockSpec((1, D_MODEL), fixed),
                  pl.BlockSpec((D_MODEL, P_COLS), fixed)],
        out_specs=[pl.BlockSpec((tm, GDN_MAIN), row), pl.BlockSpec((tm, RWKV_COLS), row),
                   pl.BlockSpec((tm, AB_PAD), row)],
        out_shape=[jax.ShapeDtypeStruct((m, GDN_MAIN), F32), jax.ShapeDtypeStruct((m, RWKV_COLS), F32),
                   jax.ShapeDtypeStruct((m, AB_PAD), F32)],
        compiler_params=_params(("parallel",)),
        name="proj_in",
    )(h, norm_w, w_perm)


def _inv_unit_lower(lows, stack=None):
    shape = lows[0].shape
    n = shape[0]
    ri = _iota2(shape, 0)
    ci = _iota2(shape, 1) & (n - 1)
    eye = jnp.where(ri == ci, 1.0, 0.0)
    pair_blk = (ri >> 1) == (ci >> 1)
    ts = [eye - jnp.where(pair_blk, low, 0.0) for low in lows]
    s, lg = 2, 1
    while s < n:
        sel = ((ri >> (lg + 1)) == (ci >> (lg + 1))) & ((ri >> lg) != (ci >> lg))
        offs = [jnp.where(sel, low, 0.0) for low in lows]
        if stack is None:
            xs = [_mm(t, off) for t, off in zip(ts, offs)]
            ts = [t - _mm(x, t) for t, x in zip(ts, xs)]
        else:
            xs = [_mm(t, stack(off)) for t, off in zip(ts, offs)]
            ts = [t - _mm(x, stack(t)) for t, x in zip(ts, xs)]
        s, lg = 2 * s, lg + 1
    return ts


def _gdn_gates(ab, alog, dtb):
    log_alpha = -jnp.exp(alog) * _softplus(ab + dtb)
    return log_alpha, _sigmoid(ab)


def _gdn_out(o, z, norm_w):
    o = o * lax.rsqrt(jnp.mean(o * o, axis=-1, keepdims=True) + NORM_EPS) * norm_w
    return o * _silu(z)


def _gdn_prompt_kernel(nb, tb, pg_ref, pab_ref, cw_ref, alog_ref, dtb_ref, nw_ref, o_ref, s_out_ref,
                       s_scr, ext_scr):
    t = pl.program_id(1)
    c = GDN_CHUNK
    chunks_per_seq = tb // c

    @pl.when(t == 0)
    def _():
        s_scr[...] = jnp.zeros_like(s_scr)
        for q in range(nb):
            ext_scr[q, 0:SUBLANES, :] = jnp.zeros((SUBLANES, GDN_QKV), F32)

    accs = []
    for q in range(nb):
        ext_scr[q, SUBLANES:SUBLANES + tb, :] = pg_ref[q, :, 0:GDN_QKV]
        acc = ext_scr[q, pl.ds(SUBLANES, tb), :] * cw_ref[CONV_W - 1:CONV_W, :]
        for j in range(1, CONV_W):
            acc = acc + ext_scr[q, pl.ds(SUBLANES - j, tb), :] * cw_ref[CONV_W - 1 - j:CONV_W - j, :]
        ext_scr[q, 0:SUBLANES, :] = ext_scr[q, tb:tb + SUBLANES, :]
        accs.append(acc)
    qkv = _silu(jnp.concatenate(accs, axis=0))

    pab = jnp.concatenate([pab_ref[q] for q in range(nb)], axis=0)
    log_alpha, beta_all = _gdn_gates(pab, alog_ref[...], dtb_ref[...])
    ri = _iota2((c, c), 0)
    ci = _iota2((c, c), 1)
    causal = ri >= ci
    strict = ri > ci
    tri = jnp.where(causal, 1.0, 0.0)

    n_chunks = nb * chunks_per_seq
    items = [(ch, h) for ch in range(n_chunks) for h in range(GDN_HEADS)]
    g_blk, gt_blk = [], []
    for ch in range(n_chunks):
        g = _mm_split3(tri, log_alpha[ch * c:(ch + 1) * c, :])
        g_blk.append(g)
        gt_blk.append(g.T)
    qs, ks, vbs, kbs, decays, g_cols = [], [], [], [], [], []
    for ch, h in items:
        rows = slice(ch * c, (ch + 1) * c)
        q = qkv[rows, h * GDN_DK:(h + 1) * GDN_DK]
        k = qkv[rows, GDN_QK + h * GDN_DK:GDN_QK + (h + 1) * GDN_DK]
        v = qkv[rows, 2 * GDN_QK + h * GDN_DK:2 * GDN_QK + (h + 1) * GDN_DK]
        q = q * lax.rsqrt(jnp.sum(q * q, axis=-1, keepdims=True) + L2_EPS) * (GDN_DK ** -0.5)
        k = k * lax.rsqrt(jnp.sum(k * k, axis=-1, keepdims=True) + L2_EPS)
        beta = beta_all[rows, GDN_HEADS + h:GDN_HEADS + h + 1]
        g_col = g_blk[ch][:, h:h + 1]
        g_row = gt_blk[ch][h:h + 1, :]
        decays.append(jnp.exp(jnp.where(causal, g_col - g_row, -jnp.inf)))
        qs.append(q)
        ks.append(k)
        vbs.append(v * beta)
        kbs.append(k * beta)
        g_cols.append(g_col)
    kk = [_mm(kb, k, _NT) for kb, k in zip(kbs, ks)]
    qk = [_mm(q, k, _NT) for q, k in zip(qs, ks)]
    tinvs = _inv_unit_lower([jnp.where(strict, x * d, 0.0) for x, d in zip(kk, decays)])
    attns = [x * d for x, d in zip(qk, decays)]
    egs = [jnp.exp(g_col) for g_col in g_cols]
    us = [_mm(tinv, vb) for tinv, vb in zip(tinvs, vbs)]
    ws = [_mm(tinv, kb * eg) for tinv, kb, eg in zip(tinvs, kbs, egs)]
    g_lasts = [g_col[c - 1:c, :] for g_col in g_cols]
    kd_ts = [(k * jnp.exp(g_last - g_col)).T for k, g_last, g_col in zip(ks, g_lasts, g_cols)]
    q_hats = [q * eg - _mm(attn, w) for q, eg, attn, w in zip(qs, egs, attns, ws)]
    o0s = [_mm(attn, u) for attn, u in zip(attns, us)]
    s_mix = [_mm(kd_t, w) for kd_t, w in zip(kd_ts, ws)]
    s_add = [_mm(kd_t, u) for kd_t, u in zip(kd_ts, us)]

    chains = [(q, h) for q in range(nb) for h in range(GDN_HEADS)]
    states = [s_scr[q, h] for q, h in chains]
    for blk in range(chunks_per_seq):
        idx = [(q * chunks_per_seq + blk) * GDN_HEADS + h for q, h in chains]
        for (q, h), i, s in zip(chains, idx, states):
            o = o0s[i] + _mm(q_hats[i], s)
            rows = slice(blk * c, (blk + 1) * c)
            z = pg_ref[q, rows, GDN_QKV + h * GDN_DK:GDN_QKV + (h + 1) * GDN_DK]
            o_ref[q, rows, h * GDN_DK:(h + 1) * GDN_DK] = _gdn_out(o, z, nw_ref[...]).astype(o_ref.dtype)
        states = [s * jnp.exp(g_lasts[i]) - _mm(s_mix[i], s) + s_add[i] for i, s in zip(idx, states)]
    for (q, h), s in zip(chains, states):
        s_scr[q, h] = s

    @pl.when(t == pl.num_programs(1) - 1)
    def _():
        s_out_ref[...] = s_scr[...]


def _gdn_prompt(pg, pab, conv_w, alog, dtb, norm_w, *, nb, tb):
    b, t, _ = pg.shape
    fixed = lambda i, j: (0, 0)
    return pl.pallas_call(
        functools.partial(_gdn_prompt_kernel, nb, tb),
        grid=(b // nb, t // tb),
        in_specs=[pl.BlockSpec((nb, tb, GDN_MAIN), lambda i, j: (i, j, 0)),
                  pl.BlockSpec((nb, tb, AB_PAD), lambda i, j: (i, j, 0)),
                  pl.BlockSpec((CONV_W, GDN_QKV), fixed), pl.BlockSpec((1, AB_PAD), fixed),
                  pl.BlockSpec((1, AB_PAD), fixed), pl.BlockSpec((1, GDN_DK), fixed)],
        out_specs=[pl.BlockSpec((nb, tb, GDN_WIDTH), lambda i, j: (i, j, 0)),
                   pl.BlockSpec((nb, GDN_HEADS, GDN_DK, GDN_DK), lambda i, j: (i, 0, 0, 0))],
        out_shape=[jax.ShapeDtypeStruct((b, t, GDN_WIDTH), BF16),
                   jax.ShapeDtypeStruct((b, GDN_HEADS, GDN_DK, GDN_DK), F32)],
        scratch_shapes=[pltpu.VMEM((nb, GDN_HEADS, GDN_DK, GDN_DK), F32),
                        pltpu.VMEM((nb, tb + SUBLANES, GDN_QKV), F32)],
        compiler_params=_params(("parallel", "arbitrary")),
        name="gdn_prompt",
    )(pg, pab, conv_w, alog, dtb, norm_w)


def _gdn_step_kernel(bb, pg_ref, pab_ref, cs_ref, s_ref, cw_ref, alog_ref, dtb_ref, nw_ref,
                     o_ref, cs_out_ref, s_out_ref):
    u = pg_ref[:, 0:GDN_QKV]
    acc = u * cw_ref[CONV_W - 1:CONV_W, :]
    for j in range(CONV_W - 1):
        acc = acc + cs_ref[:, j * GDN_QKV:(j + 1) * GDN_QKV] * cw_ref[j:j + 1, :]
    cs_out_ref[:, 0:(CONV_W - 2) * GDN_QKV] = cs_ref[:, GDN_QKV:(CONV_W - 1) * GDN_QKV]
    cs_out_ref[:, (CONV_W - 2) * GDN_QKV:(CONV_W - 1) * GDN_QKV] = u
    qkv = _silu(acc)
    log_alpha, beta_all = _gdn_gates(pab_ref[...], alog_ref[...], dtb_ref[...])
    alpha_all = jnp.exp(log_alpha)
    eye = jnp.where(_iota2((GDN_DK, GDN_DK), 0) == _iota2((GDN_DK, GDN_DK), 1), 1.0, 0.0)

    def to_col(row):
        return jnp.sum(eye * row, axis=-1, keepdims=True)

    vs, k_cols, q_cols = [], [], []
    for h in range(GDN_HEADS):
        q = qkv[:, h * GDN_DK:(h + 1) * GDN_DK]
        k = qkv[:, GDN_QK + h * GDN_DK:GDN_QK + (h + 1) * GDN_DK]
        vs.append(qkv[:, 2 * GDN_QK + h * GDN_DK:2 * GDN_QK + (h + 1) * GDN_DK])
        q = q * lax.rsqrt(jnp.sum(q * q, axis=-1, keepdims=True) + L2_EPS) * (GDN_DK ** -0.5)
        k = k * lax.rsqrt(jnp.sum(k * k, axis=-1, keepdims=True) + L2_EPS)
        k_cols.append([to_col(k[i:i + 1, :]) for i in range(bb)])
        q_cols.append([to_col(q[i:i + 1, :]) for i in range(bb)])
    for h in range(GDN_HEADS):
        o_rows = []
        for i in range(bb):
            s = s_ref[i, h] * alpha_all[i:i + 1, h:h + 1]
            mem = jnp.sum(k_cols[h][i] * s, axis=0, keepdims=True)
            delta = (vs[h][i:i + 1, :] - mem) * beta_all[i:i + 1, GDN_HEADS + h:GDN_HEADS + h + 1]
            s = s + k_cols[h][i] * delta
            s_out_ref[i, h] = s
            o_rows.append(jnp.sum(q_cols[h][i] * s, axis=0, keepdims=True))
        o = jnp.concatenate(o_rows, axis=0)
        z = pg_ref[:, GDN_QKV + h * GDN_DK:GDN_QKV + (h + 1) * GDN_DK]
        o_ref[:, h * GDN_DK:(h + 1) * GDN_DK] = _gdn_out(o, z, nw_ref[...])


def _gdn_step(pg, pab, conv_state, s0, conv_w, alog, dtb, norm_w, *, bb):
    b = pg.shape[0]
    row = lambda i: (i, 0)
    fixed = lambda i: (0, 0)
    cs_cols = (CONV_W - 1) * GDN_QKV
    state_spec = pl.BlockSpec((bb, GDN_HEADS, GDN_DK, GDN_DK), lambda i: (i, 0, 0, 0))
    return pl.pallas_call(
        functools.partial(_gdn_step_kernel, bb),
        grid=(b // bb,),
        in_specs=[pl.BlockSpec((bb, GDN_MAIN), row), pl.BlockSpec((bb, AB_PAD), row),
                  pl.BlockSpec((bb, cs_cols), row), state_spec,
                  pl.BlockSpec((CONV_W, GDN_QKV), fixed), pl.BlockSpec((1, AB_PAD), fixed),
                  pl.BlockSpec((1, AB_PAD), fixed), pl.BlockSpec((1, GDN_DK), fixed)],
        out_specs=[pl.BlockSpec((bb, GDN_WIDTH), row), pl.BlockSpec((bb, cs_cols), row), state_spec],
        out_shape=[jax.ShapeDtypeStruct((b, GDN_WIDTH), F32), jax.ShapeDtypeStruct((b, cs_cols), F32),
                   jax.ShapeDtypeStruct(s0.shape, F32)],
        compiler_params=_params(("parallel",)),
        name="gdn_step",
    )(pg, pab, conv_state, s0, conv_w, alog, dtb, norm_w)


def _rwkv_prep(xs, w0, a0, wwa, g2, k_k, k_a, head_ones):
    w = RWKV_WIDTH
    r, k, v = xs[:, 0:w], xs[:, w:2 * w], xs[:, 2 * w:3 * w]
    wa_in = xs[:, 3 * w:3 * w + LORA_WA]
    lane = _iota2(wa_in.shape, 1)
    wa_in = jnp.where(lane < LORA_WA // 2, jnp.tanh(wa_in), wa_in)
    wa = _mm(wa_in, wwa)
    w_log = -_softplus(-(w0 + wa[:, 0:w])) - 0.5
    a = _sigmoid(a0 + wa[:, w:2 * w])
    g = _mm(_sigmoid(xs[:, 3 * w + LORA_WA:RWKV_COLS]), g2)
    kx = k * k_k
    kk = kx * lax.rsqrt(_mm(kx * kx, head_ones) + L2_EPS)
    k = k * (1.0 + (a - 1.0) * k_a)
    return r, k, v, jnp.exp(w_log), kk, kk * a, g


def _rwkv_out(y, r, k, v, g, r_k, ln_w, ln_b, head_ones):
    inv_n = 1.0 / RWKV_N
    mean = _mm(y, head_ones) * inv_n
    yc = y - mean
    var = _mm(yc * yc, head_ones) * inv_n
    y = yc * lax.rsqrt(var + GN_EPS) * ln_w + ln_b
    bonus = _mm(r * k * r_k, head_ones) * v
    return (y + bonus) * g


def _rwkv_prompt_kernel(nb, tb, pr_ref, mu_ref, w0_ref, a0_ref, wwa_ref, g2_ref, kk_ref, ka_ref, rk_ref,
                        lnw_ref, lnb_ref, ones_ref, o_ref, s_out_ref, s_scr, ext_scr, y_scr):
    t = pl.program_id(1)
    c = RWKV_CHUNK
    n = RWKV_N
    pair = 2 * n
    lg_c = c.bit_length() - 1
    lg_n = n.bit_length() - 1
    n_pairs = RWKV_HEADS // 2
    chunks_per_seq = tb // c

    @pl.when(t == 0)
    def _():
        s_scr[...] = jnp.zeros_like(s_scr)
        for q in range(nb):
            ext_scr[q, 0:SUBLANES, :] = jnp.zeros((SUBLANES, RWKV_COLS), F32)

    prevs = []
    for q in range(nb):
        ext_scr[q, SUBLANES:SUBLANES + tb, :] = pr_ref[q]
        prevs.append(ext_scr[q, pl.ds(SUBLANES - 1, tb), :])
        ext_scr[q, 0:SUBLANES, :] = ext_scr[q, tb:tb + SUBLANES, :]
    p = jnp.concatenate([pr_ref[q] for q in range(nb)], axis=0)
    xs = p + (jnp.concatenate(prevs, axis=0) - p) * mu_ref[...]
    head_ones = ones_ref[...]
    r, k, v, e, kk, kka, g = _rwkv_prep(xs, w0_ref[...], a0_ref[...], wwa_ref[...], g2_ref[...],
                                         kk_ref[...], ka_ref[...], head_ones)

    ri = _iota2((tb, tb), 0)
    ci = _iota2((tb, tb), 1)
    tri = jnp.where((ri >= ci) & ((ri >> lg_c) == (ci >> lg_c)), 1.0, 0.0)
    lg_inc = -jnp.concatenate([_mm_split3(tri, e[q * tb:(q + 1) * tb, :]) for q in range(nb)], axis=0)
    lg_exc = lg_inc + e
    a_t = -kk * jnp.exp(lg_exc)
    r_t = r * jnp.exp(lg_inc)
    inv_g = jnp.exp(-lg_inc)
    b_t = kka * inv_g
    k_t = k * inv_g

    first_head = _iota2((c, pair), 1) < n

    def stack(z):
        z = z.astype(BF16)
        zero = jnp.zeros_like(z)
        return jnp.concatenate([jnp.where(first_head, z, zero), jnp.where(first_head, zero, z)], axis=0)

    tt = _iota2((c, pair), 0)
    ss = _iota2((c, pair), 1) & (n - 1)
    strict = tt > ss
    incl = tt >= ss
    bd = jnp.where((_iota2((pair, pair), 0) >> lg_n) == (_iota2((pair, pair), 1) >> lg_n), 1.0, 0.0)

    items = [(ch, hp) for ch in range(nb * chunks_per_seq) for hp in range(n_pairs)]

    def tile(x, ch, hp):
        return x[ch * c:(ch + 1) * c, hp * pair:(hp + 1) * pair]

    ats = [tile(a_t, *it) for it in items]
    rts = [tile(r_t, *it) for it in items]
    vts = [tile(v, *it) for it in items]
    svs = [stack(vt) for vt in vts]
    scs = [_mm(jnp.concatenate([at, rt], axis=0),
               jnp.concatenate([stack(tile(b_t, *it)), stack(tile(k_t, *it))], axis=0), _NT)
           for at, rt, it in zip(ats, rts, items)]
    tinvs = _inv_unit_lower([-jnp.where(strict, sc[0:c, 0:pair], 0.0) for sc in scs], stack)
    akv = [_mm(jnp.where(strict, sc[0:c, pair:2 * pair], 0.0), sv) for sc, sv in zip(scs, svs)]
    y_v = [_mm(jnp.where(incl, sc[c:2 * c, pair:2 * pair], 0.0), sv) for sc, sv in zip(scs, svs)]
    p_rbs = [jnp.where(incl, sc[c:2 * c, 0:pair], 0.0) for sc in scs]
    w_ts = [_mm(tinv, stack(at)) for tinv, at in zip(tinvs, ats)]
    u0s = [_mm(tinv, stack(x)) for tinv, x in zip(tinvs, akv)]
    lg_lasts = [lg_inc[ch * c + c - 1:ch * c + c, hp * pair:(hp + 1) * pair] for ch, hp in items]
    to_ends = [jnp.exp(lg_last - tile(lg_inc, *it)) for lg_last, it in zip(lg_lasts, items)]
    bhs = [tile(kka, *it) * to_end for it, to_end in zip(items, to_ends)]
    khs = [tile(k, *it) * to_end for it, to_end in zip(items, to_ends)]
    r_hats = [rt + _mm(p_rb, stack(w_t)) for rt, p_rb, w_t in zip(rts, p_rbs, w_ts)]
    y0s = [_mm(p_rb, stack(u0)) + yv for p_rb, u0, yv in zip(p_rbs, u0s, y_v)]
    s_mix = [bd * _mm(w_t.T, bh) for w_t, bh in zip(w_ts, bhs)]
    s_add = [bd * _mm(jnp.concatenate([u0, vt], axis=0).T, jnp.concatenate([bh, kh], axis=0))
             for u0, vt, bh, kh in zip(u0s, vts, bhs, khs)]

    chains = [(q, hp) for q in range(nb) for hp in range(n_pairs)]
    states = [s_scr[q, hp] for q, hp in chains]
    for blk in range(chunks_per_seq):
        idx = [(q * chunks_per_seq + blk) * n_pairs + hp for q, hp in chains]
        for (q, hp), i, s in zip(chains, idx, states):
            row0 = (q * chunks_per_seq + blk) * c
            y_scr[row0:row0 + c, hp * pair:(hp + 1) * pair] = y0s[i] + _mm(r_hats[i], s, _NT)
        states = [s * jnp.exp(lg_lasts[i]) + _mm(s, s_mix[i]) + s_add[i] for i, s in zip(idx, states)]
    for (q, hp), s in zip(chains, states):
        s_scr[q, hp] = s

    out = _rwkv_out(y_scr[...], r, k, v, g, rk_ref[...], lnw_ref[...], lnb_ref[...], head_ones)
    for q in range(nb):
        o_ref[q] = out[q * tb:(q + 1) * tb, :].astype(o_ref.dtype)

    @pl.when(t == pl.num_programs(1) - 1)
    def _():
        for q, hp in chains:
            s = s_scr[q, hp]
            s_out_ref[q, 2 * hp] = s[0:n, 0:n]
            s_out_ref[q, 2 * hp + 1] = s[n:pair, n:pair]


def _rwkv_weight_specs(fixed):
    vec = pl.BlockSpec((1, RWKV_WIDTH), fixed)
    return [pl.BlockSpec((1, RWKV_COLS), fixed), vec, vec,
            pl.BlockSpec((LORA_WA, 2 * RWKV_WIDTH), fixed), pl.BlockSpec((G_LORA, RWKV_WIDTH), fixed),
            vec, vec, vec, vec, vec, pl.BlockSpec((RWKV_WIDTH, RWKV_WIDTH), fixed)]


def _rwkv_prompt(pr, weights, *, nb, tb):
    b, t, _ = pr.shape
    fixed = lambda i, j: (0, 0)
    return pl.pallas_call(
        functools.partial(_rwkv_prompt_kernel, nb, tb),
        grid=(b // nb, t // tb),
        in_specs=[pl.BlockSpec((nb, tb, RWKV_COLS), lambda i, j: (i, j, 0))] + _rwkv_weight_specs(fixed),
        out_specs=[pl.BlockSpec((nb, tb, RWKV_WIDTH), lambda i, j: (i, j, 0)),
                   pl.BlockSpec((nb, RWKV_HEADS, RWKV_N, RWKV_N), lambda i, j: (i, 0, 0, 0))],
        out_shape=[jax.ShapeDtypeStruct((b, t, RWKV_WIDTH), BF16),
                   jax.ShapeDtypeStruct((b, RWKV_HEADS, RWKV_N, RWKV_N), F32)],
        scratch_shapes=[pltpu.VMEM((nb, RWKV_HEADS // 2, 2 * RWKV_N, 2 * RWKV_N), F32),
                        pltpu.VMEM((nb, tb + SUBLANES, RWKV_COLS), F32),
                        pltpu.VMEM((nb * tb, RWKV_WIDTH), F32)],
        compiler_params=_params(("parallel", "arbitrary")),
        name="rwkv_prompt",
    )(pr, *weights)


def _rwkv_step_kernel(bb, pr_ref, sh_ref, s_ref, mu_ref, w0_ref, a0_ref, wwa_ref, g2_ref, kk_ref, ka_ref,
                      rk_ref, lnw_ref, lnb_ref, ones_ref, o_ref, s_out_ref):
    n = RWKV_N
    p = pr_ref[...]
    xs = p + (sh_ref[...] - p) * mu_ref[...]
    head_ones = ones_ref[...]
    r, k, v, e, kk, kka, g = _rwkv_prep(xs, w0_ref[...], a0_ref[...], wwa_ref[...], g2_ref[...],
                                         kk_ref[...], ka_ref[...], head_ones)
    decay = jnp.exp(-e)
    eye = jnp.where(_iota2((n, n), 0) == _iota2((n, n), 1), 1.0, 0.0)
    items = [(h, i) for h in range(RWKV_HEADS) for i in range(bb)]
    n_tiles = RWKV_WIDTH // LANES

    def per_head(x):
        x_roll = pltpu.roll(x, n, axis=1)
        out = []
        for h in range(RWKV_HEADS):
            src, tile = (x, h // 2) if h % 2 == 0 else (x_roll, (h // 2 + 1) % n_tiles)
            out.append(src[:, tile * LANES:tile * LANES + n])
        return out

    v_h, kk_h, kka_h, k_h, d_h, r_h = (per_head(x) for x in (v, -kk, kka, k, decay, r))

    def vec(xs, h, i):
        return xs[h][i:i + 1, :]

    v_cols = [jnp.sum(eye * vec(v_h, h, i), axis=-1, keepdims=True) for h, i in items]
    sas = [jnp.sum(s_ref[i, h] * vec(kk_h, h, i), axis=-1, keepdims=True) for h, i in items]
    for (h, i), sa, v_col in zip(items, sas, v_cols):
        s_out_ref[i, h] = s_ref[i, h] * vec(d_h, h, i) + sa * vec(kka_h, h, i) + v_col * vec(k_h, h, i)
    y_cols = [jnp.sum(s_out_ref[i, h] * vec(r_h, h, i), axis=-1, keepdims=True) for h, i in items]
    y_heads = [jnp.concatenate([jnp.sum(eye * y_cols[h * bb + i], axis=0, keepdims=True) for i in range(bb)], axis=0)
               for h in range(RWKV_HEADS)]
    y = jnp.concatenate(y_heads, axis=1)
    o_ref[...] = _rwkv_out(y, r, k, v, g, rk_ref[...], lnw_ref[...], lnb_ref[...], head_ones)


def _rwkv_step(pr, shift, s0, weights, *, bb):
    b = pr.shape[0]
    row = lambda i: (i, 0)
    fixed = lambda i: (0, 0)
    state_spec = pl.BlockSpec((bb, RWKV_HEADS, RWKV_N, RWKV_N), lambda i: (i, 0, 0, 0))
    return pl.pallas_call(
        functools.partial(_rwkv_step_kernel, bb),
        grid=(b // bb,),
        in_specs=[pl.BlockSpec((bb, RWKV_COLS), row), pl.BlockSpec((bb, RWKV_COLS), row), state_spec]
        + _rwkv_weight_specs(fixed),
        out_specs=[pl.BlockSpec((bb, RWKV_WIDTH), row), state_spec],
        out_shape=[jax.ShapeDtypeStruct((b, RWKV_WIDTH), F32), jax.ShapeDtypeStruct(s0.shape, F32)],
        compiler_params=_params(("parallel",)),
        name="rwkv_step",
    )(pr, shift, s0, *weights)


def _pad_lanes(x, width):
    return jnp.pad(x, ((0, 0), (0, width - x.shape[1])))


def kernel(x_prompt, x_sample, state_gdn_conv, state_gdn, state_rwkv_shift, state_rwkv, norm_ffn1, w_ffn1_in, w_ffn1_out, norm_mix, w_in, gdn_conv_w, gdn_a_log, gdn_dt_bias, gdn_norm_w, rwkv_mu, rwkv_w0, rwkv_w2, rwkv_a0, rwkv_a2, rwkv_g2, rwkv_k_k, rwkv_k_a, rwkv_r_k, rwkv_ln_w, rwkv_ln_b, w_out, norm_ffn2, w_ffn2_in, w_ffn2_out, norm_final):
    depth = norm_ffn1.shape[0]
    assert depth == 1, "the carried-state plumbing below is written for a single layer"
    b, t, _ = x_prompt.shape
    bs = x_sample.shape[0]
    assert x_sample.shape[1] == 1
    l = 0

    n_gdn_cols = GDN_MAIN + 2 * GDN_HEADS
    w_in_l = w_in[l]
    w_perm = jnp.concatenate(
        [w_in_l[:, :GDN_MAIN], w_in_l[:, n_gdn_cols:], w_in_l[:, GDN_MAIN:n_gdn_cols],
         jnp.zeros((D_MODEL, AB_PAD - 2 * GDN_HEADS), w_in.dtype)], axis=1).astype(BF16)
    w1a, w1b = w_ffn1_in[l].astype(BF16), w_ffn1_out[l].astype(BF16)
    w2a, w2b = w_ffn2_in[l].astype(BF16), w_ffn2_out[l].astype(BF16)
    woa, wob = w_out[l, :GDN_WIDTH].astype(BF16), w_out[l, GDN_WIDTH:].astype(BF16)
    half = LORA_WA // 2
    wwa = jnp.zeros((LORA_WA, 2 * RWKV_WIDTH), F32)
    wwa = wwa.at[:half, :RWKV_WIDTH].set(rwkv_w2[l]).at[half:, RWKV_WIDTH:].set(rwkv_a2[l]).astype(BF16)
    head_id = jnp.arange(RWKV_WIDTH) // RWKV_N
    head_ones = (head_id[:, None] == head_id[None, :]).astype(BF16)
    rwkv_w = (rwkv_mu[l][None], rwkv_w0[l][None], rwkv_a0[l][None], wwa, rwkv_g2[l].astype(BF16),
              rwkv_k_k[l][None], rwkv_k_a[l][None], rwkv_r_k[l].reshape(1, RWKV_WIDTH),
              rwkv_ln_w[l][None], rwkv_ln_b[l][None], head_ones)
    alog = _pad_lanes(gdn_a_log[l][None], AB_PAD)
    dtb = _pad_lanes(gdn_dt_bias[l][None], AB_PAD)
    gdn_w = (gdn_conv_w[l], alog, dtb, gdn_norm_w[l][None])
    nf1, nmix, nf2, nfin = norm_ffn1[l][None], norm_mix[l][None], norm_ffn2[l][None], norm_final[None]

    xp = x_prompt.reshape(b * t, D_MODEL)
    hp = _ffn(xp, nf1, w1a, w1b, tm=512, sub=256)
    pg, pr, pab = _proj_in(hp, nmix, w_perm, tm=512)
    pg3, pr3, pab3 = pg.reshape(b, t, GDN_MAIN), pr.reshape(b, t, RWKV_COLS), pab.reshape(b, t, AB_PAD)
    oa, gdn_s = _gdn_prompt(pg3, pab3, *gdn_w, nb=2, tb=256)
    ob, wkv_s = _rwkv_prompt(pr3, rwkv_w, nb=2, tb=256)
    yp = _ffn(hp, nf2, w2a, w2b, mix=(oa.reshape(b * t, GDN_WIDTH), ob.reshape(b * t, RWKV_WIDTH), woa, wob),
              final_w=nfin, tm=512, sub=256)

    xs = x_sample.reshape(bs, D_MODEL)
    hs = _ffn(xs, nf1, w1a, w1b, tm=bs, sub=bs)
    sg, sr, sab = _proj_in(hs, nmix, w_perm, tm=bs)
    conv_in = state_gdn_conv[l].reshape(bs, (CONV_W - 1) * GDN_QKV)
    oa_s, conv_s, gdn_ss = _gdn_step(sg, sab, conv_in, state_gdn[l], *gdn_w, bb=8)
    ob_s, wkv_ss = _rwkv_step(sr, state_rwkv_shift[l], state_rwkv[l], rwkv_w, bb=8)
    ys = _ffn(hs, nf2, w2a, w2b, mix=(oa_s, ob_s, woa, wob), final_w=nfin, tm=bs, sub=bs)

    return (yp.reshape(b, t, D_MODEL), ys.reshape(bs, 1, D_MODEL),
            pg3[:, t - (CONV_W - 1):, :GDN_QKV][None], gdn_s[None], pr3[:, t - 1, :][None], wkv_s[None],
            conv_s.reshape(1, bs, CONV_W - 1, GDN_QKV), gdn_ss[None], sr[None], wkv_ss[None])
```

```python
import functools

import jax
import jax.numpy as jnp
from jax import lax
from jax.experimental import pallas as pl
from jax.experimental.pallas import tpu as pltpu

F32 = jnp.float32
BF16 = jnp.bfloat16

D_MODEL = 1024
D_FF = 2816
CONV_W = 4
GDN_HEADS = 4
GDN_DK = 128
GDN_QK = GDN_HEADS * GDN_DK
GDN_WIDTH = GDN_HEADS * GDN_DK
GDN_QKV = 3 * GDN_WIDTH
GDN_MAIN = GDN_QKV + GDN_WIDTH
RWKV_HEADS = 8
RWKV_N = 64
RWKV_WIDTH = RWKV_HEADS * RWKV_N
LORA_WA = 128
G_LORA = 128
RWKV_COLS = 3 * RWKV_WIDTH + LORA_WA + G_LORA
LANES = 128
SUBLANES = 8
AB_PAD = LANES
NORM_EPS = 1e-6
GN_EPS = 64e-5
L2_EPS = 1e-6

GDN_CHUNK = 128
RWKV_CHUNK = 64
VMEM_LIMIT = 56 * 1024 * 1024


_NN = (((1,), (0,)), ((), ()))
_NT = (((1,), (1,)), ((), ()))


def _mm(a, b, dims=_NN):
    return lax.dot_general(a.astype(BF16), b.astype(BF16), dims, preferred_element_type=F32)


def _mm_split3(sel, x):
    sel = sel.astype(BF16)
    out = None
    for _ in range(3):
        piece = x.astype(BF16)
        part = jnp.dot(sel, piece, preferred_element_type=F32)
        out = part if out is None else out + part
        x = x - piece.astype(F32)
    return out


def _sigmoid(x):
    return 1.0 / (1.0 + jnp.exp(-x))


def _silu(x):
    return x * _sigmoid(x)


def _softplus(x):
    return jnp.maximum(x, 0.0) + jnp.log(1.0 + jnp.exp(-jnp.abs(x)))


def _rms(x, g):
    return x * lax.rsqrt(jnp.mean(x * x, axis=-1, keepdims=True) + NORM_EPS) * g


def _iota2(shape, dim):
    return lax.broadcasted_iota(jnp.int32, shape, dim)


def _params(sem):
    return pltpu.CompilerParams(dimension_semantics=sem, vmem_limit_bytes=VMEM_LIMIT)


def _resident(shape):
    return pl.BlockSpec(shape, lambda i: (0,) * len(shape), pipeline_mode=pl.Buffered(1))


def _ffn_kernel(has_mix, final_norm, sub, *refs):
    it = iter(refs)
    x_ref = next(it)
    if has_mix:
        oa_ref, ob_ref, woa_ref, wob_ref = next(it), next(it), next(it), next(it)
    nw_ref, w1_ref, w2_ref = next(it), next(it), next(it)
    fw_ref = next(it) if final_norm else None
    out_ref = next(it)
    for s in range(x_ref.shape[0] // sub):
        rows = slice(s * sub, (s + 1) * sub)
        x = x_ref[rows, :]
        if has_mix:
            x = x + _mm(oa_ref[rows, :], woa_ref[...]) + _mm(ob_ref[rows, :], wob_ref[...])
        xn = _rms(x, nw_ref[...]).astype(BF16)
        gate = jnp.dot(xn, w1_ref[:, 0:D_FF], preferred_element_type=F32)
        up = jnp.dot(xn, w1_ref[:, D_FF:2 * D_FF], preferred_element_type=F32)
        act = (_silu(gate) * up).astype(BF16)
        h = x + 0.5 * jnp.dot(act, w2_ref[...], preferred_element_type=F32)
        if final_norm:
            h = _rms(h, fw_ref[...])
        out_ref[rows, :] = h


def _ffn(x, norm_w, w1, w2, *, mix=None, final_w=None, tm, sub):
    m = x.shape[0]
    has_mix = mix is not None
    final_norm = final_w is not None
    row = lambda i: (i, 0)
    in_specs = [pl.BlockSpec((tm, D_MODEL), row)]
    args = [x]
    if has_mix:
        oa, ob, woa, wob = mix
        in_specs += [pl.BlockSpec((tm, GDN_WIDTH), row), pl.BlockSpec((tm, RWKV_WIDTH), row),
                     _resident((GDN_WIDTH, D_MODEL)), _resident((RWKV_WIDTH, D_MODEL))]
        args += [oa, ob, woa, wob]
    in_specs += [_resident((1, D_MODEL)), _resident((D_MODEL, 2 * D_FF)), _resident((D_FF, D_MODEL))]
    args += [norm_w, w1, w2]
    if final_norm:
        in_specs.append(_resident((1, D_MODEL)))
        args.append(final_w)
    return pl.pallas_call(
        functools.partial(_ffn_kernel, has_mix, final_norm, sub),
        grid=(m // tm,),
        in_specs=in_specs,
        out_specs=pl.BlockSpec((tm, D_MODEL), row),
        out_shape=jax.ShapeDtypeStruct((m, D_MODEL), F32),
        compiler_params=_params(("parallel",)),
        name="ffn_mix" if has_mix else "ffn",
    )(*args)


def _proj_in_kernel(h_ref, nw_ref, wg_ref, wr_ref, wab_ref, pg_ref, pr_ref, pab_ref):
    n = _rms(h_ref[...], nw_ref[...]).astype(BF16)
    pg_ref[...] = jnp.dot(n, wg_ref[...], preferred_element_type=F32)
    pr_ref[...] = jnp.dot(n, wr_ref[...], preferred_element_type=F32)
    pab_ref[...] = jnp.dot(n, wab_ref[...], preferred_element_type=F32)


def _proj_in(h, norm_w, w_slabs, *, tm):
    m = h.shape[0]
    row = lambda i: (i, 0)
    return pl.pallas_call(
        _proj_in_kernel,
        grid=(m // tm,),
        in_specs=[pl.BlockSpec((tm, D_MODEL), row), _resident((1, D_MODEL)), _resident((D_MODEL, GDN_MAIN)),
                  _resident((D_MODEL, RWKV_COLS)), _resident((D_MODEL, AB_PAD))],
        out_specs=[pl.BlockSpec((tm, GDN_MAIN), row), pl.BlockSpec((tm, RWKV_COLS), row),
                   pl.BlockSpec((tm, AB_PAD), row)],
        out_shape=[jax.ShapeDtypeStruct((m, GDN_MAIN), F32), jax.ShapeDtypeStruct((m, RWKV_COLS), F32),
                   jax.ShapeDtypeStruct((m, AB_PAD), F32)],
        compiler_params=_params(("parallel",)),
        name="proj_in",
    )(h, norm_w, *w_slabs)


def _inv_unit_lower(lows, stack=None):
    shape = lows[0].shape
    n = shape[0]
    ri = _iota2(shape, 0)
    ci = _iota2(shape, 1) & (n - 1)
    eye = jnp.where(ri == ci, 1.0, 0.0)
    pair_blk = (ri >> 1) == (ci >> 1)
    ts = [eye - jnp.where(pair_blk, low, 0.0) for low in lows]
    s, lg = 2, 1
    while s < n:
        sel = ((ri >> (lg + 1)) == (ci >> (lg + 1))) & ((ri >> lg) != (ci >> lg))
        offs = [jnp.where(sel, low, 0.0) for low in lows]
        if stack is None:
            xs = [_mm(t, off) for t, off in zip(ts, offs)]
            ts = [t - _mm(x, t) for t, x in zip(ts, xs)]
        else:
            xs = [_mm(t, stack(off)) for t, off in zip(ts, offs)]
            ts = [t - _mm(x, stack(t)) for t, x in zip(ts, xs)]
        s, lg = 2 * s, lg + 1
    return ts


def _gdn_gates(ab, alog, dtb):
    log_alpha = -jnp.exp(alog) * _softplus(ab + dtb)
    return log_alpha, _sigmoid(ab)


def _gdn_out(o, z, norm_w):
    o = o * lax.rsqrt(jnp.mean(o * o, axis=-1, keepdims=True) + NORM_EPS) * norm_w
    return o * _silu(z)


def _gdn_prompt_kernel(nb, tb, pg_ref, pab_ref, cw_ref, alog_ref, dtb_ref, nw_ref, o_ref, s_out_ref,
                       s_scr, ext_scr):
    t = pl.program_id(1)
    c = GDN_CHUNK
    chunks_per_seq = tb // c

    @pl.when(t == 0)
    def _():
        s_scr[...] = jnp.zeros_like(s_scr)
        for q in range(nb):
            ext_scr[q, 0:SUBLANES, :] = jnp.zeros((SUBLANES, GDN_QKV), F32)

    accs = []
    for q in range(nb):
        ext_scr[q, SUBLANES:SUBLANES + tb, :] = pg_ref[q, :, 0:GDN_QKV]
        acc = ext_scr[q, pl.ds(SUBLANES, tb), :] * cw_ref[CONV_W - 1:CONV_W, :]
        for j in range(1, CONV_W):
            acc = acc + ext_scr[q, pl.ds(SUBLANES - j, tb), :] * cw_ref[CONV_W - 1 - j:CONV_W - j, :]
        ext_scr[q, 0:SUBLANES, :] = ext_scr[q, tb:tb + SUBLANES, :]
        accs.append(acc)
    qkv = _silu(jnp.concatenate(accs, axis=0))

    pab = jnp.concatenate([pab_ref[q] for q in range(nb)], axis=0)
    log_alpha, beta_all = _gdn_gates(pab, alog_ref[...], dtb_ref[...])
    ri = _iota2((c, c), 0)
    ci = _iota2((c, c), 1)
    causal = ri >= ci
    strict = ri > ci
    tri = jnp.where(causal, 1.0, 0.0)

    n_chunks = nb * chunks_per_seq
    items = [(ch, h) for ch in range(n_chunks) for h in range(GDN_HEADS)]
    g_blk, gt_blk = [], []
    for ch in range(n_chunks):
        g = _mm_split3(tri, log_alpha[ch * c:(ch + 1) * c, :])
        g_blk.append(g)
        gt_blk.append(g.T)
    qs, ks, vbs, kbs, decays, g_cols = [], [], [], [], [], []
    for ch, h in items:
        rows = slice(ch * c, (ch + 1) * c)
        q = qkv[rows, h * GDN_DK:(h + 1) * GDN_DK]
        k = qkv[rows, GDN_QK + h * GDN_DK:GDN_QK + (h + 1) * GDN_DK]
        v = qkv[rows, 2 * GDN_QK + h * GDN_DK:2 * GDN_QK + (h + 1) * GDN_DK]
        q = q * lax.rsqrt(jnp.sum(q * q, axis=-1, keepdims=True) + L2_EPS) * (GDN_DK ** -0.5)
        k = k * lax.rsqrt(jnp.sum(k * k, axis=-1, keepdims=True) + L2_EPS)
        beta = beta_all[rows, GDN_HEADS + h:GDN_HEADS + h + 1]
        g_col = g_blk[ch][:, h:h + 1]
        g_row = gt_blk[ch][h:h + 1, :]
        decays.append(jnp.exp(jnp.where(causal, g_col - g_row, -jnp.inf)))
        qs.append(q)
        ks.append(k)
        vbs.append(v * beta)
        kbs.append(k * beta)
        g_cols.append(g_col)
    kk = [_mm(kb, k, _NT) for kb, k in zip(kbs, ks)]
    qk = [_mm(q, k, _NT) for q, k in zip(qs, ks)]
    tinvs = _inv_unit_lower([jnp.where(strict, x * d, 0.0) for x, d in zip(kk, decays)])
    attns = [x * d for x, d in zip(qk, decays)]
    egs = [jnp.exp(g_col) for g_col in g_cols]
    us = [_mm(tinv, vb) for tinv, vb in zip(tinvs, vbs)]
    ws = [_mm(tinv, kb * eg) for tinv, kb, eg in zip(tinvs, kbs, egs)]
    g_lasts = [g_col[c - 1:c, :] for g_col in g_cols]
    kd_ts = [(k * jnp.exp(g_last - g_col)).T for k, g_last, g_col in zip(ks, g_lasts, g_cols)]
    q_hats = [q * eg - _mm(attn, w) for q, eg, attn, w in zip(qs, egs, attns, ws)]
    o0s = [_mm(attn, u) for attn, u in zip(attns, us)]
    s_mix = [_mm(kd_t, w) for kd_t, w in zip(kd_ts, ws)]
    s_add = [_mm(kd_t, u) for kd_t, u in zip(kd_ts, us)]

    chains = [(q, h) for q in range(nb) for h in range(GDN_HEADS)]
    states = [s_scr[q, h] for q, h in chains]
    for blk in range(chunks_per_seq):
        idx = [(q * chunks_per_seq + blk) * GDN_HEADS + h for q, h in chains]
        for (q, h), i, s in zip(chains, idx, states):
            o = o0s[i] + _mm(q_hats[i], s)
            rows = slice(blk * c, (blk + 1) * c)
            z = pg_ref[q, rows, GDN_QKV + h * GDN_DK:GDN_QKV + (h + 1) * GDN_DK]
            o_ref[q, rows, h * GDN_DK:(h + 1) * GDN_DK] = _gdn_out(o, z, nw_ref[...]).astype(o_ref.dtype)
        states = [s * jnp.exp(g_lasts[i]) - _mm(s_mix[i], s) + s_add[i] for i, s in zip(idx, states)]
    for (q, h), s in zip(chains, states):
        s_scr[q, h] = s

    @pl.when(t == pl.num_programs(1) - 1)
    def _():
        s_out_ref[...] = s_scr[...]


def _gdn_prompt(pg, pab, conv_w, alog, dtb, norm_w, *, nb, tb):
    b, t, _ = pg.shape
    fixed = lambda i, j: (0, 0)
    return pl.pallas_call(
        functools.partial(_gdn_prompt_kernel, nb, tb),
        grid=(b // nb, t // tb),
        in_specs=[pl.BlockSpec((nb, tb, GDN_MAIN), lambda i, j: (i, j, 0)),
                  pl.BlockSpec((nb, tb, AB_PAD), lambda i, j: (i, j, 0)),
                  pl.BlockSpec((CONV_W, GDN_QKV), fixed), pl.BlockSpec((1, AB_PAD), fixed),
                  pl.BlockSpec((1, AB_PAD), fixed), pl.BlockSpec((1, GDN_DK), fixed)],
        out_specs=[pl.BlockSpec((nb, tb, GDN_WIDTH), lambda i, j: (i, j, 0)),
                   pl.BlockSpec((nb, GDN_HEADS, GDN_DK, GDN_DK), lambda i, j: (i, 0, 0, 0))],
        out_shape=[jax.ShapeDtypeStruct((b, t, GDN_WIDTH), BF16),
                   jax.ShapeDtypeStruct((b, GDN_HEADS, GDN_DK, GDN_DK), F32)],
        scratch_shapes=[pltpu.VMEM((nb, GDN_HEADS, GDN_DK, GDN_DK), F32),
                        pltpu.VMEM((nb, tb + SUBLANES, GDN_QKV), F32)],
        compiler_params=_params(("parallel", "arbitrary")),
        name="gdn_prompt",
    )(pg, pab, conv_w, alog, dtb, norm_w)


def _gdn_step_kernel(bb, pg_ref, pab_ref, cs_ref, s_ref, cw_ref, alog_ref, dtb_ref, nw_ref,
                     o_ref, cs_out_ref, s_out_ref):
    u = pg_ref[:, 0:GDN_QKV]
    acc = u * cw_ref[CONV_W - 1:CONV_W, :]
    for j in range(CONV_W - 1):
        acc = acc + cs_ref[:, j * GDN_QKV:(j + 1) * GDN_QKV] * cw_ref[j:j + 1, :]
    cs_out_ref[:, 0:(CONV_W - 2) * GDN_QKV] = cs_ref[:, GDN_QKV:(CONV_W - 1) * GDN_QKV]
    cs_out_ref[:, (CONV_W - 2) * GDN_QKV:(CONV_W - 1) * GDN_QKV] = u
    qkv = _silu(acc)
    log_alpha, beta_all = _gdn_gates(pab_ref[...], alog_ref[...], dtb_ref[...])
    alpha_all = jnp.exp(log_alpha)
    eye = jnp.where(_iota2((GDN_DK, GDN_DK), 0) == _iota2((GDN_DK, GDN_DK), 1), 1.0, 0.0)

    def to_col(row):
        return jnp.sum(eye * row, axis=-1, keepdims=True)

    vs, k_cols, q_cols = [], [], []
    for h in range(GDN_HEADS):
        q = qkv[:, h * GDN_DK:(h + 1) * GDN_DK]
        k = qkv[:, GDN_QK + h * GDN_DK:GDN_QK + (h + 1) * GDN_DK]
        vs.append(qkv[:, 2 * GDN_QK + h * GDN_DK:2 * GDN_QK + (h + 1) * GDN_DK])
        q = q * lax.rsqrt(jnp.sum(q * q, axis=-1, keepdims=True) + L2_EPS) * (GDN_DK ** -0.5)
        k = k * lax.rsqrt(jnp.sum(k * k, axis=-1, keepdims=True) + L2_EPS)
        k_cols.append([to_col(k[i:i + 1, :]) for i in range(bb)])
        q_cols.append([to_col(q[i:i + 1, :]) for i in range(bb)])
    for h in range(GDN_HEADS):
        o_rows = []
        for i in range(bb):
            s = s_ref[i, h] * alpha_all[i:i + 1, h:h + 1]
            mem = jnp.sum(k_cols[h][i] * s, axis=0, keepdims=True)
            delta = (vs[h][i:i + 1, :] - mem) * beta_all[i:i + 1, GDN_HEADS + h:GDN_HEADS + h + 1]
            s = s + k_cols[h][i] * delta
            s_out_ref[i, h] = s
            o_rows.append(jnp.sum(q_cols[h][i] * s, axis=0, keepdims=True))
        o = jnp.concatenate(o_rows, axis=0)
        z = pg_ref[:, GDN_QKV + h * GDN_DK:GDN_QKV + (h + 1) * GDN_DK]
        o_ref[:, h * GDN_DK:(h + 1) * GDN_DK] = _gdn_out(o, z, nw_ref[...])


def _gdn_step(pg, pab, conv_state, s0, conv_w, alog, dtb, norm_w, *, bb):
    b = pg.shape[0]
    row = lambda i: (i, 0)
    fixed = lambda i: (0, 0)
    cs_cols = (CONV_W - 1) * GDN_QKV
    state_spec = pl.BlockSpec((bb, GDN_HEADS, GDN_DK, GDN_DK), lambda i: (i, 0, 0, 0))
    return pl.pallas_call(
        functools.partial(_gdn_step_kernel, bb),
        grid=(b // bb,),
        in_specs=[pl.BlockSpec((bb, GDN_MAIN), row), pl.BlockSpec((bb, AB_PAD), row),
                  pl.BlockSpec((bb, cs_cols), row), state_spec,
                  pl.BlockSpec((CONV_W, GDN_QKV), fixed), pl.BlockSpec((1, AB_PAD), fixed),
                  pl.BlockSpec((1, AB_PAD), fixed), pl.BlockSpec((1, GDN_DK), fixed)],
        out_specs=[pl.BlockSpec((bb, GDN_WIDTH), row), pl.BlockSpec((bb, cs_cols), row), state_spec],
        out_shape=[jax.ShapeDtypeStruct((b, GDN_WIDTH), F32), jax.ShapeDtypeStruct((b, cs_cols), F32),
                   jax.ShapeDtypeStruct(s0.shape, F32)],
        compiler_params=_params(("parallel",)),
        name="gdn_step",
    )(pg, pab, conv_state, s0, conv_w, alog, dtb, norm_w)


def _rwkv_prep(xs, w0, a0, wwa, g2, k_k, k_a, head_ones):
    w = RWKV_WIDTH
    r, k, v = xs[:, 0:w], xs[:, w:2 * w], xs[:, 2 * w:3 * w]
    wa_in = xs[:, 3 * w:3 * w + LORA_WA]
    lane = _iota2(wa_in.shape, 1)
    wa_in = jnp.where(lane < LORA_WA // 2, jnp.tanh(wa_in), wa_in)
    wa = _mm(wa_in, wwa)
    w_log = -_softplus(-(w0 + wa[:, 0:w])) - 0.5
    a = _sigmoid(a0 + wa[:, w:2 * w])
    g = _mm(_sigmoid(xs[:, 3 * w + LORA_WA:RWKV_COLS]), g2)
    kx = k * k_k
    kk = kx * lax.rsqrt(_mm(kx * kx, head_ones) + L2_EPS)
    k = k * (1.0 + (a - 1.0) * k_a)
    return r, k, v, jnp.exp(w_log), kk, kk * a, g


def _rwkv_out(y, r, k, v, g, r_k, ln_w, ln_b, head_ones):
    inv_n = 1.0 / RWKV_N
    mean = _mm(y, head_ones) * inv_n
    yc = y - mean
    var = _mm(yc * yc, head_ones) * inv_n
    y = yc * lax.rsqrt(var + GN_EPS) * ln_w + ln_b
    bonus = _mm(r * k * r_k, head_ones) * v
    return (y + bonus) * g


def _rwkv_prompt_kernel(nb, tb, pr_ref, mu_ref, w0_ref, a0_ref, wwa_ref, g2_ref, kk_ref, ka_ref, rk_ref,
                        lnw_ref, lnb_ref, ones_ref, o_ref, s_out_ref, s_scr, ext_scr, y_scr):
    t = pl.program_id(1)
    c = RWKV_CHUNK
    n = RWKV_N
    pair = 2 * n
    lg_c = c.bit_length() - 1
    lg_n = n.bit_length() - 1
    n_pairs = RWKV_HEADS // 2
    chunks_per_seq = tb // c

    @pl.when(t == 0)
    def _():
        s_scr[...] = jnp.zeros_like(s_scr)
        for q in range(nb):
            ext_scr[q, 0:SUBLANES, :] = jnp.zeros((SUBLANES, RWKV_COLS), F32)

    prevs = []
    for q in range(nb):
        ext_scr[q, SUBLANES:SUBLANES + tb, :] = pr_ref[q]
        prevs.append(ext_scr[q, pl.ds(SUBLANES - 1, tb), :])
        ext_scr[q, 0:SUBLANES, :] = ext_scr[q, tb:tb + SUBLANES, :]
    p = jnp.concatenate([pr_ref[q] for q in range(nb)], axis=0)
    xs = p + (jnp.concatenate(prevs, axis=0) - p) * mu_ref[...]
    head_ones = ones_ref[...]
    r, k, v, e, kk, kka, g = _rwkv_prep(xs, w0_ref[...], a0_ref[...], wwa_ref[...], g2_ref[...],
                                         kk_ref[...], ka_ref[...], head_ones)

    ri = _iota2((tb, tb), 0)
    ci = _iota2((tb, tb), 1)
    tri = jnp.where((ri >= ci) & ((ri >> lg_c) == (ci >> lg_c)), 1.0, 0.0)
    lg_inc = -jnp.concatenate([_mm_split3(tri, e[q * tb:(q + 1) * tb, :]) for q in range(nb)], axis=0)
    lg_exc = lg_inc + e
    a_t = -kk * jnp.exp(lg_exc)
    r_t = r * jnp.exp(lg_inc)
    inv_g = jnp.exp(-lg_inc)
    b_t = kka * inv_g
    k_t = k * inv_g

    first_head = _iota2((c, pair), 1) < n

    def stack(z):
        z = z.astype(BF16)
        zero = jnp.zeros_like(z)
        return jnp.concatenate([jnp.where(first_head, z, zero), jnp.where(first_head, zero, z)], axis=0)

    tt = _iota2((c, pair), 0)
    ss = _iota2((c, pair), 1) & (n - 1)
    strict = tt > ss
    incl = tt >= ss
    bd = jnp.where((_iota2((pair, pair), 0) >> lg_n) == (_iota2((pair, pair), 1) >> lg_n), 1.0, 0.0)

    items = [(ch, hp) for ch in range(nb * chunks_per_seq) for hp in range(n_pairs)]

    def tile(x, ch, hp):
        return x[ch * c:(ch + 1) * c, hp * pair:(hp + 1) * pair]

    ats = [tile(a_t, *it) for it in items]
    rts = [tile(r_t, *it) for it in items]
    vts = [tile(v, *it) for it in items]
    svs = [stack(vt) for vt in vts]
    scs = [_mm(jnp.concatenate([at, rt], axis=0),
               jnp.concatenate([stack(tile(b_t, *it)), stack(tile(k_t, *it))], axis=0), _NT)
           for at, rt, it in zip(ats, rts, items)]
    tinvs = _inv_unit_lower([-jnp.where(strict, sc[0:c, 0:pair], 0.0) for sc in scs], stack)
    akv = [_mm(jnp.where(strict, sc[0:c, pair:2 * pair], 0.0), sv) for sc, sv in zip(scs, svs)]
    y_v = [_mm(jnp.where(incl, sc[c:2 * c, pair:2 * pair], 0.0), sv) for sc, sv in zip(scs, svs)]
    p_rbs = [jnp.where(incl, sc[c:2 * c, 0:pair], 0.0) for sc in scs]
    w_ts = [_mm(tinv, stack(at)) for tinv, at in zip(tinvs, ats)]
    u0s = [_mm(tinv, stack(x)) for tinv, x in zip(tinvs, akv)]
    lg_lasts = [lg_inc[ch * c + c - 1:ch * c + c, hp * pair:(hp + 1) * pair] for ch, hp in items]
    to_ends = [jnp.exp(lg_last - tile(lg_inc, *it)) for lg_last, it in zip(lg_lasts, items)]
    bhs = [tile(kka, *it) * to_end for it, to_end in zip(items, to_ends)]
    khs = [tile(k, *it) * to_end for it, to_end in zip(items, to_ends)]
    r_hats = [rt + _mm(p_rb, stack(w_t)) for rt, p_rb, w_t in zip(rts, p_rbs, w_ts)]
    y0s = [_mm(p_rb, stack(u0)) + yv for p_rb, u0, yv in zip(p_rbs, u0s, y_v)]
    s_mix = [bd * _mm(w_t.T, bh) for w_t, bh in zip(w_ts, bhs)]
    s_add = [bd * _mm(jnp.concatenate([u0, vt], axis=0).T, jnp.concatenate([bh, kh], axis=0))
             for u0, vt, bh, kh in zip(u0s, vts, bhs, khs)]

    chains = [(q, hp) for q in range(nb) for hp in range(n_pairs)]
    states = [s_scr[q, hp] for q, hp in chains]
    for blk in range(chunks_per_seq):
        idx = [(q * chunks_per_seq + blk) * n_pairs + hp for q, hp in chains]
        for (q, hp), i, s in zip(chains, idx, states):
            row0 = (q * chunks_per_seq + blk) * c
            y_scr[row0:row0 + c, hp * pair:(hp + 1) * pair] = y0s[i] + _mm(r_hats[i], s, _NT)
        states = [s * jnp.exp(lg_lasts[i]) + _mm(s, s_mix[i]) + s_add[i] for i, s in zip(idx, states)]
    for (q, hp), s in zip(chains, states):
        s_scr[q, hp] = s

    out = _rwkv_out(y_scr[...], r, k, v, g, rk_ref[...], lnw_ref[...], lnb_ref[...], head_ones)
    for q in range(nb):
        o_ref[q] = out[q * tb:(q + 1) * tb, :].astype(o_ref.dtype)

    @pl.when(t == pl.num_programs(1) - 1)
    def _():
        for q, hp in chains:
            s = s_scr[q, hp]
            s_out_ref[q, 2 * hp] = s[0:n, 0:n]
            s_out_ref[q, 2 * hp + 1] = s[n:pair, n:pair]


def _rwkv_weight_specs(fixed):
    vec = pl.BlockSpec((1, RWKV_WIDTH), fixed)
    return [pl.BlockSpec((1, RWKV_COLS), fixed), vec, vec,
            pl.BlockSpec((LORA_WA, 2 * RWKV_WIDTH), fixed), pl.BlockSpec((G_LORA, RWKV_WIDTH), fixed),
            vec, vec, vec, vec, vec, pl.BlockSpec((RWKV_WIDTH, RWKV_WIDTH), fixed)]


def _rwkv_prompt(pr, weights, *, nb, tb):
    b, t, _ = pr.shape
    fixed = lambda i, j: (0, 0)
    return pl.pallas_call(
        functools.partial(_rwkv_prompt_kernel, nb, tb),
        grid=(b // nb, t // tb),
        in_specs=[pl.BlockSpec((nb, tb, RWKV_COLS), lambda i, j: (i, j, 0))] + _rwkv_weight_specs(fixed),
        out_specs=[pl.BlockSpec((nb, tb, RWKV_WIDTH), lambda i, j: (i, j, 0)),
                   pl.BlockSpec((nb, RWKV_HEADS, RWKV_N, RWKV_N), lambda i, j: (i, 0, 0, 0))],
        out_shape=[jax.ShapeDtypeStruct((b, t, RWKV_WIDTH), BF16),
                   jax.ShapeDtypeStruct((b, RWKV_HEADS, RWKV_N, RWKV_N), F32)],
        scratch_shapes=[pltpu.VMEM((nb, RWKV_HEADS // 2, 2 * RWKV_N, 2 * RWKV_N), F32),
                        pltpu.VMEM((nb, tb + SUBLANES, RWKV_COLS), F32),
                        pltpu.VMEM((nb * tb, RWKV_WIDTH), F32)],
        compiler_params=_params(("parallel", "arbitrary")),
        name="rwkv_prompt",
    )(pr, *weights)


def _rwkv_step_kernel(pr_ref, sh_ref, s_ref, mu_ref, w0_ref, a0_ref, wwa_ref, g2_ref, kk_ref, ka_ref,
                      rk_ref, lnw_ref, lnb_ref, ones_ref, o_ref, s_out_ref, vec_scr, row_scr, y_scr):
    h = pl.program_id(0)
    n = RWKV_N
    batch = pr_ref.shape[0]

    @pl.when(h == 0)
    def _():
        p = pr_ref[...]
        xs = p + (sh_ref[...] - p) * mu_ref[...]
        r, k, v, e, kk, kka, g = _rwkv_prep(xs, w0_ref[...], a0_ref[...], wwa_ref[...], g2_ref[...],
                                             kk_ref[...], ka_ref[...], ones_ref[...])
        for j, x in enumerate((-kk, kka, k, jnp.exp(-e), r, v)):
            vec_scr[j] = x.T.reshape(RWKV_HEADS, n, batch)
        for j, x in enumerate((r, k, v, g)):
            row_scr[j] = x

    nkk, kka, k, decay, r = (vec_scr[j, h] for j in range(5))
    for vi in range(n):
        s = s_ref[vi]
        sa = jnp.sum(s * nkk, axis=0, keepdims=True)
        s = s * decay + sa * kka + vec_scr[5, h, vi:vi + 1, :] * k
        s_out_ref[vi] = s
        y_scr[h, vi:vi + 1, :] = jnp.sum(s * r, axis=0, keepdims=True)

    @pl.when(h == RWKV_HEADS - 1)
    def _():
        y = y_scr[...].reshape(RWKV_WIDTH, batch).T
        o_ref[...] = _rwkv_out(y, row_scr[0], row_scr[1], row_scr[2], row_scr[3], rk_ref[...], lnw_ref[...],
                               lnb_ref[...], ones_ref[...])


def _rwkv_step(pr, shift, s0_t, weights):
    b = pr.shape[0]
    fixed = lambda i: (0, 0)
    state_spec = pl.BlockSpec((None, RWKV_N, RWKV_N, b), lambda i: (i, 0, 0, 0))
    return pl.pallas_call(
        _rwkv_step_kernel,
        grid=(RWKV_HEADS,),
        in_specs=[pl.BlockSpec((b, RWKV_COLS), fixed), pl.BlockSpec((b, RWKV_COLS), fixed), state_spec]
        + _rwkv_weight_specs(fixed),
        out_specs=[pl.BlockSpec((b, RWKV_WIDTH), fixed), state_spec],
        out_shape=[jax.ShapeDtypeStruct((b, RWKV_WIDTH), F32), jax.ShapeDtypeStruct(s0_t.shape, F32)],
        scratch_shapes=[pltpu.VMEM((6, RWKV_HEADS, RWKV_N, b), F32), pltpu.VMEM((4, b, RWKV_WIDTH), F32),
                        pltpu.VMEM((RWKV_HEADS, RWKV_N, b), F32)],
        compiler_params=_params(("arbitrary",)),
        name="rwkv_step",
    )(pr, shift, s0_t, *weights)


def _pad_lanes(x, width):
    return jnp.pad(x, ((0, 0), (0, width - x.shape[1])))


def kernel(x_prompt, x_sample, state_gdn_conv, state_gdn, state_rwkv_shift, state_rwkv, norm_ffn1, w_ffn1_in, w_ffn1_out, norm_mix, w_in, gdn_conv_w, gdn_a_log, gdn_dt_bias, gdn_norm_w, rwkv_mu, rwkv_w0, rwkv_w2, rwkv_a0, rwkv_a2, rwkv_g2, rwkv_k_k, rwkv_k_a, rwkv_r_k, rwkv_ln_w, rwkv_ln_b, w_out, norm_ffn2, w_ffn2_in, w_ffn2_out, norm_final):
    depth = norm_ffn1.shape[0]
    assert depth == 1, "the carried-state plumbing below is written for a single layer"
    b, t, _ = x_prompt.shape
    bs = x_sample.shape[0]
    assert x_sample.shape[1] == 1
    l = 0

    n_gdn_cols = GDN_MAIN + 2 * GDN_HEADS
    w_in_l = w_in[l]
    w_slabs = (w_in_l[:, :GDN_MAIN].astype(BF16), w_in_l[:, n_gdn_cols:].astype(BF16),
               _pad_lanes(w_in_l[:, GDN_MAIN:n_gdn_cols], AB_PAD).astype(BF16))
    w1a, w1b = w_ffn1_in[l].astype(BF16), w_ffn1_out[l].astype(BF16)
    w2a, w2b = w_ffn2_in[l].astype(BF16), w_ffn2_out[l].astype(BF16)
    woa, wob = w_out[l, :GDN_WIDTH].astype(BF16), w_out[l, GDN_WIDTH:].astype(BF16)
    half = LORA_WA // 2
    wwa = jnp.zeros((LORA_WA, 2 * RWKV_WIDTH), F32)
    wwa = wwa.at[:half, :RWKV_WIDTH].set(rwkv_w2[l]).at[half:, RWKV_WIDTH:].set(rwkv_a2[l]).astype(BF16)
    head_id = jnp.arange(RWKV_WIDTH) // RWKV_N
    head_ones = (head_id[:, None] == head_id[None, :]).astype(BF16)
    rwkv_w = (rwkv_mu[l][None], rwkv_w0[l][None], rwkv_a0[l][None], wwa, rwkv_g2[l].astype(BF16),
              rwkv_k_k[l][None], rwkv_k_a[l][None], rwkv_r_k[l].reshape(1, RWKV_WIDTH),
              rwkv_ln_w[l][None], rwkv_ln_b[l][None], head_ones)
    alog = _pad_lanes(gdn_a_log[l][None], AB_PAD)
    dtb = _pad_lanes(gdn_dt_bias[l][None], AB_PAD)
    gdn_w = (gdn_conv_w[l], alog, dtb, gdn_norm_w[l][None])
    nf1, nmix, nf2, nfin = norm_ffn1[l][None], norm_mix[l][None], norm_ffn2[l][None], norm_final[None]

    xp = x_prompt.reshape(b * t, D_MODEL)
    hp = _ffn(xp, nf1, w1a, w1b, tm=512, sub=256)
    pg, pr, pab = _proj_in(hp, nmix, w_slabs, tm=512)
    pg3, pr3, pab3 = pg.reshape(b, t, GDN_MAIN), pr.reshape(b, t, RWKV_COLS), pab.reshape(b, t, AB_PAD)
    oa, gdn_s = _gdn_prompt(pg3, pab3, *gdn_w, nb=2, tb=256)
    ob, wkv_s = _rwkv_prompt(pr3, rwkv_w, nb=2, tb=256)
    yp = _ffn(hp, nf2, w2a, w2b, mix=(oa.reshape(b * t, GDN_WIDTH), ob.reshape(b * t, RWKV_WIDTH), woa, wob),
              final_w=nfin, tm=512, sub=256)

    xs = x_sample.reshape(bs, D_MODEL)
    hs = _ffn(xs, nf1, w1a, w1b, tm=bs, sub=bs)
    sg, sr, sab = _proj_in(hs, nmix, w_slabs, tm=bs)
    conv_in = state_gdn_conv[l].reshape(bs, (CONV_W - 1) * GDN_QKV)
    oa_s, conv_s, gdn_ss = _gdn_step(sg, sab, conv_in, state_gdn[l], *gdn_w, bb=8)
    ob_s, wkv_t = _rwkv_step(sr, state_rwkv_shift[l], jnp.transpose(state_rwkv[l], (1, 2, 3, 0)), rwkv_w)
    wkv_ss = jnp.transpose(wkv_t, (3, 0, 1, 2))
    ys = _ffn(hs, nf2, w2a, w2b, mix=(oa_s, ob_s, woa, wob), final_w=nfin, tm=bs, sub=bs)

    return (yp.reshape(b, t, D_MODEL), ys.reshape(bs, 1, D_MODEL),
            pg3[:, t - (CONV_W - 1):, :GDN_QKV][None], gdn_s[None], pr3[:, t - 1, :][None], wkv_s[None],
            conv_s.reshape(1, bs, CONV_W - 1, GDN_QKV), gdn_ss[None], sr[None], wkv_ss[None])
```

```python
import functools

import jax
import jax.numpy as jnp
from jax import lax
from jax.experimental import pallas as pl
from jax.experimental.pallas import tpu as pltpu

F32 = jnp.float32
BF16 = jnp.bfloat16

D_MODEL = 1024
D_FF = 2816
CONV_W = 4
GDN_HEADS = 4
GDN_DK = 128
GDN_QK = GDN_HEADS * GDN_DK
GDN_WIDTH = GDN_HEADS * GDN_DK
GDN_QKV = 3 * GDN_WIDTH
GDN_MAIN = GDN_QKV + GDN_WIDTH
RWKV_HEADS = 8
RWKV_N = 64
RWKV_WIDTH = RWKV_HEADS * RWKV_N
LORA_WA = 128
G_LORA = 128
RWKV_COLS = 3 * RWKV_WIDTH + LORA_WA + G_LORA
LANES = 128
SUBLANES = 8
AB_PAD = LANES
NORM_EPS = 1e-6
GN_EPS = 64e-5
L2_EPS = 1e-6

GDN_CHUNK = 128
RWKV_CHUNK = 64
VMEM_LIMIT = 56 * 1024 * 1024


_NN = (((1,), (0,)), ((), ()))
_NT = (((1,), (1,)), ((), ()))


def _mm(a, b, dims=_NN):
    return lax.dot_general(a.astype(BF16), b.astype(BF16), dims, preferred_element_type=F32)


def _mm_split3(sel, x):
    sel = sel.astype(BF16)
    out = None
    for _ in range(3):
        piece = x.astype(BF16)
        part = jnp.dot(sel, piece, preferred_element_type=F32)
        out = part if out is None else out + part
        x = x - piece.astype(F32)
    return out


def _sigmoid(x):
    return 1.0 / (1.0 + jnp.exp(-x))


def _silu(x):
    return x * _sigmoid(x)


def _softplus(x):
    return jnp.maximum(x, 0.0) + jnp.log(1.0 + jnp.exp(-jnp.abs(x)))


def _rms(x, g):
    return x * lax.rsqrt(jnp.mean(x * x, axis=-1, keepdims=True) + NORM_EPS) * g


def _iota2(shape, dim):
    return lax.broadcasted_iota(jnp.int32, shape, dim)


def _shift_rows(prev_tail, x, j):
    rolled = pltpu.roll(jnp.concatenate([prev_tail, x], axis=0), j, axis=0)
    return rolled[SUBLANES:, :]


def _params(sem):
    return pltpu.CompilerParams(dimension_semantics=sem, vmem_limit_bytes=VMEM_LIMIT)


def _resident(shape):
    return pl.BlockSpec(shape, lambda i: (0,) * len(shape), pipeline_mode=pl.Buffered(1))


def _ffn_kernel(has_mix, final_norm, sub, *refs):
    it = iter(refs)
    x_ref = next(it)
    if has_mix:
        oa_ref, ob_ref, woa_ref, wob_ref = next(it), next(it), next(it), next(it)
    nw_ref, w1_ref, w2_ref = next(it), next(it), next(it)
    fw_ref = next(it) if final_norm else None
    out_ref = next(it)
    for s in range(x_ref.shape[0] // sub):
        rows = slice(s * sub, (s + 1) * sub)
        x = x_ref[rows, :]
        if has_mix:
            x = x + _mm(oa_ref[rows, :], woa_ref[...]) + _mm(ob_ref[rows, :], wob_ref[...])
        xn = _rms(x, nw_ref[...]).astype(BF16)
        gate = jnp.dot(xn, w1_ref[:, 0:D_FF], preferred_element_type=F32)
        up = jnp.dot(xn, w1_ref[:, D_FF:2 * D_FF], preferred_element_type=F32)
        act = (_silu(gate) * up).astype(BF16)
        h = x + 0.5 * jnp.dot(act, w2_ref[...], preferred_element_type=F32)
        if final_norm:
            h = _rms(h, fw_ref[...])
        out_ref[rows, :] = h


def _ffn(x, norm_w, w1, w2, *, mix=None, final_w=None, tm, sub):
    m = x.shape[0]
    has_mix = mix is not None
    final_norm = final_w is not None
    row = lambda i: (i, 0)
    in_specs = [pl.BlockSpec((tm, D_MODEL), row)]
    args = [x]
    if has_mix:
        oa, ob, woa, wob = mix
        in_specs += [pl.BlockSpec((tm, GDN_WIDTH), row), pl.BlockSpec((tm, RWKV_WIDTH), row),
                     _resident((GDN_WIDTH, D_MODEL)), _resident((RWKV_WIDTH, D_MODEL))]
        args += [oa, ob, woa, wob]
    in_specs += [_resident((1, D_MODEL)), _resident((D_MODEL, 2 * D_FF)), _resident((D_FF, D_MODEL))]
    args += [norm_w, w1, w2]
    if final_norm:
        in_specs.append(_resident((1, D_MODEL)))
        args.append(final_w)
    return pl.pallas_call(
        functools.partial(_ffn_kernel, has_mix, final_norm, sub),
        grid=(m // tm,),
        in_specs=in_specs,
        out_specs=pl.BlockSpec((tm, D_MODEL), row),
        out_shape=jax.ShapeDtypeStruct((m, D_MODEL), F32),
        compiler_params=_params(("parallel",)),
        name="ffn_mix" if has_mix else "ffn",
    )(*args)


def _l2norm(x):
    return x * lax.rsqrt(jnp.sum(x * x, axis=-1, keepdims=True) + L2_EPS)


def _gdn_qkv(acc, col0=0):
    act = _silu(acc)
    parts = []
    for lo in range(0, acc.shape[1], GDN_DK):
        head = act[:, lo:lo + GDN_DK]
        if col0 + lo < GDN_QK:
            head = _l2norm(head) * (GDN_DK ** -0.5)
        elif col0 + lo < 2 * GDN_QK:
            head = _l2norm(head)
        parts.append(head)
    return jnp.concatenate(parts, axis=1)


PROJ_ROWS = 64


def _proj_in_kernel(seq_tiles, sub, h_ref, nw_ref, wg_ref, wr_ref, wab_ref, *refs):
    if seq_tiles is None:
        pg_ref, pr_ref, pab_ref = refs
    else:
        cw_ref, pg_ref, pr_ref, pab_ref, tail_ref, tail_scr, next_scr = refs

        @pl.when(pl.program_id(0) % seq_tiles == 0)
        def _():
            tail_scr[...] = jnp.zeros_like(tail_scr)

    for s in range(h_ref.shape[0] // sub):
        rows = slice(s * sub, (s + 1) * sub)
        n = _rms(h_ref[rows, :], nw_ref[...]).astype(BF16)
        pg_ref[rows, :] = jnp.dot(n, wg_ref[...], preferred_element_type=F32)
        pr_ref[rows, :] = jnp.dot(n, wr_ref[...], preferred_element_type=F32)
        pab_ref[rows, :] = jnp.dot(n, wab_ref[...], preferred_element_type=F32)
        if seq_tiles is None:
            continue
        base = s * sub
        next_scr[...] = pg_ref[base + sub - SUBLANES:base + sub, 0:GDN_QKV]
        for r0 in reversed(range(0, sub, PROJ_ROWS)):
            tile_rows = slice(base + r0, base + r0 + PROJ_ROWS)
            for lo in range(0, GDN_QKV, GDN_DK):
                cols = slice(lo, lo + GDN_DK)
                above = tail_scr[:, cols] if r0 == 0 else pg_ref[base + r0 - SUBLANES:base + r0, cols]
                x = pg_ref[tile_rows, cols]
                acc = x * cw_ref[CONV_W - 1:CONV_W, cols]
                for j in range(1, CONV_W):
                    acc = acc + _shift_rows(above, x, j) * cw_ref[CONV_W - 1 - j:CONV_W - j, cols]
                pg_ref[tile_rows, cols] = _gdn_qkv(acc, lo)
        tail_scr[...] = next_scr[...]
    if seq_tiles is not None:
        tail_ref[...] = tail_scr[...]


def _proj_in(h, norm_w, w_slabs, conv_w=None, *, tm, sub, seq_rows=None):
    m = h.shape[0]
    row = lambda i: (i, 0)
    in_specs = [pl.BlockSpec((tm, D_MODEL), row), _resident((1, D_MODEL)), _resident((D_MODEL, GDN_MAIN)),
                _resident((D_MODEL, RWKV_COLS)), _resident((D_MODEL, AB_PAD))]
    out_specs = [pl.BlockSpec((tm, GDN_MAIN), row), pl.BlockSpec((tm, RWKV_COLS), row),
                 pl.BlockSpec((tm, AB_PAD), row)]
    out_shape = [jax.ShapeDtypeStruct((m, GDN_MAIN), F32), jax.ShapeDtypeStruct((m, RWKV_COLS), F32),
                 jax.ShapeDtypeStruct((m, AB_PAD), F32)]
    args = [h, norm_w, *w_slabs]
    scratch = []
    seq_tiles = None
    if conv_w is not None:
        seq_tiles = seq_rows // tm
        in_specs.append(_resident((CONV_W, GDN_QKV)))
        args.append(conv_w)
        out_specs.append(pl.BlockSpec((None, SUBLANES, GDN_QKV), lambda i: (i // seq_tiles, 0, 0)))
        out_shape.append(jax.ShapeDtypeStruct((m // seq_rows, SUBLANES, GDN_QKV), F32))
        scratch += [pltpu.VMEM((SUBLANES, GDN_QKV), F32), pltpu.VMEM((SUBLANES, GDN_QKV), F32)]
    return pl.pallas_call(
        functools.partial(_proj_in_kernel, seq_tiles, sub),
        grid=(m // tm,),
        in_specs=in_specs,
        out_specs=out_specs,
        out_shape=out_shape,
        scratch_shapes=scratch,
        compiler_params=_params(("arbitrary",)),
        name="proj_in",
    )(*args)


def _inv_unit_lower(lows, stack=None):
    shape = lows[0].shape
    n = shape[0]
    ri = _iota2(shape, 0)
    ci = _iota2(shape, 1) & (n - 1)
    eye = jnp.where(ri == ci, 1.0, 0.0)
    pair_blk = (ri >> 1) == (ci >> 1)
    ts = [eye - jnp.where(pair_blk, low, 0.0) for low in lows]
    s, lg = 2, 1
    while s < n:
        sel = ((ri >> (lg + 1)) == (ci >> (lg + 1))) & ((ri >> lg) != (ci >> lg))
        offs = [jnp.where(sel, low, 0.0) for low in lows]
        if stack is None:
            xs = [_mm(t, off) for t, off in zip(ts, offs)]
            ts = [t - _mm(x, t) for t, x in zip(ts, xs)]
        else:
            xs = [_mm(t, stack(off)) for t, off in zip(ts, offs)]
            ts = [t - _mm(x, stack(t)) for t, x in zip(ts, xs)]
        s, lg = 2 * s, lg + 1
    return ts


def _pair_stack(m):
    first = _iota2((m, 2 * m), 1) < m

    def stack(z):
        z = z.astype(BF16)
        zero = jnp.zeros_like(z)
        return jnp.concatenate([jnp.where(first, z, zero), jnp.where(first, zero, z)], axis=0)

    return stack


def _inv_unit_lower_halves(lows):
    n = lows[0].shape[0]
    m = n // 2
    first = _iota2((m, n), 1) < m
    stack = _pair_stack(m)
    t_diag = _inv_unit_lower([jnp.where(first, low[0:m, :], low[m:n, :]) for low in lows], stack)
    lower_left = (_iota2((n, n), 0) >= m) & (_iota2((n, n), 1) < m)
    xs = [_mm(t, jnp.where(lower_left, low, 0.0)) for t, low in zip(t_diag, lows)]
    t_ll = [_mm(x, stack(t)) for x, t in zip(xs, t_diag)]
    return [jnp.concatenate([jnp.where(first, t, 0.0), jnp.where(first, 0.0, t) - ll], axis=0)
            for t, ll in zip(t_diag, t_ll)]


def _gdn_gates(ab, alog, dtb):
    log_alpha = -jnp.exp(alog) * _softplus(ab + dtb)
    return log_alpha, _sigmoid(ab)


def _gdn_out(o, z, norm_w):
    o = o * lax.rsqrt(jnp.mean(o * o, axis=-1, keepdims=True) + NORM_EPS) * norm_w
    return o * _silu(z)


def _gdn_prompt_kernel(nb, tb, pg_ref, pab_ref, alog_ref, dtb_ref, nw_ref, o_ref, s_out_ref, s_scr):
    t = pl.program_id(1)
    c = GDN_CHUNK
    chunks_per_seq = tb // c

    @pl.when(t == 0)
    def _():
        s_scr[...] = jnp.zeros_like(s_scr)

    pab = jnp.concatenate([pab_ref[q] for q in range(nb)], axis=0)
    log_alpha, beta_all = _gdn_gates(pab, alog_ref[...], dtb_ref[...])
    ri = _iota2((c, c), 0)
    ci = _iota2((c, c), 1)
    causal = ri >= ci
    strict = ri > ci
    tri = jnp.where(causal, 1.0, 0.0)

    n_chunks = nb * chunks_per_seq
    items = [(ch, h) for ch in range(n_chunks) for h in range(GDN_HEADS)]
    g_blk, gt_blk = [], []
    for ch in range(n_chunks):
        g = _mm_split3(tri, log_alpha[ch * c:(ch + 1) * c, :])
        g_blk.append(g)
        gt_blk.append(g.T)
    qs, ks, vbs, kbs, decays, g_cols = [], [], [], [], [], []
    for ch, h in items:
        rows = slice(ch * c, (ch + 1) * c)
        seq, seq_rows = ch // chunks_per_seq, slice((ch % chunks_per_seq) * c, (ch % chunks_per_seq + 1) * c)
        q = pg_ref[seq, seq_rows, h * GDN_DK:(h + 1) * GDN_DK]
        k = pg_ref[seq, seq_rows, GDN_QK + h * GDN_DK:GDN_QK + (h + 1) * GDN_DK]
        v = pg_ref[seq, seq_rows, 2 * GDN_QK + h * GDN_DK:2 * GDN_QK + (h + 1) * GDN_DK]
        beta = beta_all[rows, GDN_HEADS + h:GDN_HEADS + h + 1]
        g_col = g_blk[ch][:, h:h + 1]
        g_row = gt_blk[ch][h:h + 1, :]
        decays.append(jnp.exp(jnp.where(causal, g_col - g_row, -jnp.inf)))
        qs.append(q)
        ks.append(k)
        vbs.append(v * beta)
        kbs.append(k * beta)
        g_cols.append(g_col)
    kk = [_mm(kb, k, _NT) for kb, k in zip(kbs, ks)]
    qk = [_mm(q, k, _NT) for q, k in zip(qs, ks)]
    tinvs = _inv_unit_lower_halves([jnp.where(strict, x * d, 0.0) for x, d in zip(kk, decays)])
    attns = [x * d for x, d in zip(qk, decays)]
    egs = [jnp.exp(g_col) for g_col in g_cols]
    us = [_mm(tinv, vb) for tinv, vb in zip(tinvs, vbs)]
    ws = [_mm(tinv, kb * eg) for tinv, kb, eg in zip(tinvs, kbs, egs)]
    g_lasts = [g_col[c - 1:c, :] for g_col in g_cols]
    kd_ts = [(k * jnp.exp(g_last - g_col)).T for k, g_last, g_col in zip(ks, g_lasts, g_cols)]
    q_hats = [q * eg - _mm(attn, w) for q, eg, attn, w in zip(qs, egs, attns, ws)]
    o0s = [_mm(attn, u) for attn, u in zip(attns, us)]
    s_mix = [_mm(kd_t, w) for kd_t, w in zip(kd_ts, ws)]
    s_add = [_mm(kd_t, u) for kd_t, u in zip(kd_ts, us)]

    chains = [(q, h) for q in range(nb) for h in range(GDN_HEADS)]
    states = [s_scr[q, h] for q, h in chains]
    for blk in range(chunks_per_seq):
        idx = [(q * chunks_per_seq + blk) * GDN_HEADS + h for q, h in chains]
        for (q, h), i, s in zip(chains, idx, states):
            o = o0s[i] + _mm(q_hats[i], s)
            rows = slice(blk * c, (blk + 1) * c)
            z = pg_ref[q, rows, GDN_QKV + h * GDN_DK:GDN_QKV + (h + 1) * GDN_DK]
            o_ref[q, rows, h * GDN_DK:(h + 1) * GDN_DK] = _gdn_out(o, z, nw_ref[...]).astype(o_ref.dtype)
        states = [s * jnp.exp(g_lasts[i]) - _mm(s_mix[i], s) + s_add[i] for i, s in zip(idx, states)]
    for (q, h), s in zip(chains, states):
        s_scr[q, h] = s

    @pl.when(t == pl.num_programs(1) - 1)
    def _():
        s_out_ref[...] = s_scr[...]


def _gdn_prompt(pg, pab, alog, dtb, norm_w, *, nb, tb):
    b, t, _ = pg.shape
    fixed = lambda i, j: (0, 0)
    return pl.pallas_call(
        functools.partial(_gdn_prompt_kernel, nb, tb),
        grid=(b // nb, t // tb),
        in_specs=[pl.BlockSpec((nb, tb, GDN_MAIN), lambda i, j: (i, j, 0)),
                  pl.BlockSpec((nb, tb, AB_PAD), lambda i, j: (i, j, 0)),
                  pl.BlockSpec((1, AB_PAD), fixed), pl.BlockSpec((1, AB_PAD), fixed),
                  pl.BlockSpec((1, GDN_DK), fixed)],
        out_specs=[pl.BlockSpec((nb, tb, GDN_WIDTH), lambda i, j: (i, j, 0)),
                   pl.BlockSpec((nb, GDN_HEADS, GDN_DK, GDN_DK), lambda i, j: (i, 0, 0, 0))],
        out_shape=[jax.ShapeDtypeStruct((b, t, GDN_WIDTH), BF16),
                   jax.ShapeDtypeStruct((b, GDN_HEADS, GDN_DK, GDN_DK), F32)],
        scratch_shapes=[pltpu.VMEM((nb, GDN_HEADS, GDN_DK, GDN_DK), F32)],
        compiler_params=_params(("parallel", "arbitrary")),
        name="gdn_prompt",
    )(pg, pab, alog, dtb, norm_w)


def _gdn_step_kernel(bb, pg_ref, pab_ref, cs_ref, s_ref, cw_ref, alog_ref, dtb_ref, nw_ref,
                     o_ref, cs_out_ref, s_out_ref):
    u = pg_ref[:, 0:GDN_QKV]
    acc = u * cw_ref[CONV_W - 1:CONV_W, :]
    for j in range(CONV_W - 1):
        acc = acc + cs_ref[:, j * GDN_QKV:(j + 1) * GDN_QKV] * cw_ref[j:j + 1, :]
    cs_out_ref[:, 0:(CONV_W - 2) * GDN_QKV] = cs_ref[:, GDN_QKV:(CONV_W - 1) * GDN_QKV]
    cs_out_ref[:, (CONV_W - 2) * GDN_QKV:(CONV_W - 1) * GDN_QKV] = u
    qkv = _silu(acc)
    log_alpha, beta_all = _gdn_gates(pab_ref[...], alog_ref[...], dtb_ref[...])
    alpha_all = jnp.exp(log_alpha)
    eye = jnp.where(_iota2((GDN_DK, GDN_DK), 0) == _iota2((GDN_DK, GDN_DK), 1), 1.0, 0.0)

    def to_col(row):
        return jnp.sum(eye * row, axis=-1, keepdims=True)

    vs, k_cols, q_cols = [], [], []
    for h in range(GDN_HEADS):
        q = qkv[:, h * GDN_DK:(h + 1) * GDN_DK]
        k = qkv[:, GDN_QK + h * GDN_DK:GDN_QK + (h + 1) * GDN_DK]
        vs.append(qkv[:, 2 * GDN_QK + h * GDN_DK:2 * GDN_QK + (h + 1) * GDN_DK])
        q = _l2norm(q) * (GDN_DK ** -0.5)
        k = _l2norm(k)
        k_cols.append([to_col(k[i:i + 1, :]) for i in range(bb)])
        q_cols.append([to_col(q[i:i + 1, :]) for i in range(bb)])
    for h in range(GDN_HEADS):
        o_rows = []
        for i in range(bb):
            s = s_ref[i, h] * alpha_all[i:i + 1, h:h + 1]
            mem = jnp.sum(k_cols[h][i] * s, axis=0, keepdims=True)
            delta = (vs[h][i:i + 1, :] - mem) * beta_all[i:i + 1, GDN_HEADS + h:GDN_HEADS + h + 1]
            s = s + k_cols[h][i] * delta
            s_out_ref[i, h] = s
            o_rows.append(jnp.sum(q_cols[h][i] * s, axis=0, keepdims=True))
        o = jnp.concatenate(o_rows, axis=0)
        z = pg_ref[:, GDN_QKV + h * GDN_DK:GDN_QKV + (h + 1) * GDN_DK]
        o_ref[:, h * GDN_DK:(h + 1) * GDN_DK] = _gdn_out(o, z, nw_ref[...])


def _gdn_step(pg, pab, conv_state, s0, conv_w, alog, dtb, norm_w, *, bb):
    b = pg.shape[0]
    row = lambda i: (i, 0)
    fixed = lambda i: (0, 0)
    cs_cols = (CONV_W - 1) * GDN_QKV
    state_spec = pl.BlockSpec((bb, GDN_HEADS, GDN_DK, GDN_DK), lambda i: (i, 0, 0, 0))
    return pl.pallas_call(
        functools.partial(_gdn_step_kernel, bb),
        grid=(b // bb,),
        in_specs=[pl.BlockSpec((bb, GDN_MAIN), row), pl.BlockSpec((bb, AB_PAD), row),
                  pl.BlockSpec((bb, cs_cols), row), state_spec,
                  pl.BlockSpec((CONV_W, GDN_QKV), fixed), pl.BlockSpec((1, AB_PAD), fixed),
                  pl.BlockSpec((1, AB_PAD), fixed), pl.BlockSpec((1, GDN_DK), fixed)],
        out_specs=[pl.BlockSpec((bb, GDN_WIDTH), row), pl.BlockSpec((bb, cs_cols), row), state_spec],
        out_shape=[jax.ShapeDtypeStruct((b, GDN_WIDTH), F32), jax.ShapeDtypeStruct((b, cs_cols), F32),
                   jax.ShapeDtypeStruct(s0.shape, F32)],
        compiler_params=_params(("parallel",)),
        name="gdn_step",
    )(pg, pab, conv_state, s0, conv_w, alog, dtb, norm_w)


def _rwkv_prep(xs, w0, a0, wwa, g2, k_k, k_a, head_ones):
    w = RWKV_WIDTH
    r, k, v = xs[:, 0:w], xs[:, w:2 * w], xs[:, 2 * w:3 * w]
    wa_in = xs[:, 3 * w:3 * w + LORA_WA]
    lane = _iota2(wa_in.shape, 1)
    wa_in = jnp.where(lane < LORA_WA // 2, jnp.tanh(wa_in), wa_in)
    wa = _mm(wa_in, wwa)
    w_log = -_softplus(-(w0 + wa[:, 0:w])) - 0.5
    a = _sigmoid(a0 + wa[:, w:2 * w])
    g = _mm(_sigmoid(xs[:, 3 * w + LORA_WA:RWKV_COLS]), g2)
    kx = k * k_k
    kk = kx * lax.rsqrt(_mm(kx * kx, head_ones) + L2_EPS)
    k = k * (1.0 + (a - 1.0) * k_a)
    return r, k, v, jnp.exp(w_log), kk, kk * a, g


def _rwkv_out(y, r, k, v, g, r_k, ln_w, ln_b, head_ones):
    inv_n = 1.0 / RWKV_N
    mean = _mm(y, head_ones) * inv_n
    yc = y - mean
    var = _mm(yc * yc, head_ones) * inv_n
    y = yc * lax.rsqrt(var + GN_EPS) * ln_w + ln_b
    bonus = _mm(r * k * r_k, head_ones) * v
    return (y + bonus) * g


def _rwkv_prompt_kernel(nb, tb, pr_ref, mu_ref, w0_ref, a0_ref, wwa_ref, g2_ref, kk_ref, ka_ref, rk_ref,
                        lnw_ref, lnb_ref, ones_ref, o_ref, s_out_ref, s_scr, ext_scr, y_scr):
    t = pl.program_id(1)
    c = RWKV_CHUNK
    n = RWKV_N
    pair = 2 * n
    lg_c = c.bit_length() - 1
    lg_n = n.bit_length() - 1
    n_pairs = RWKV_HEADS // 2
    chunks_per_seq = tb // c

    @pl.when(t == 0)
    def _():
        s_scr[...] = jnp.zeros_like(s_scr)
        ext_scr[...] = jnp.zeros_like(ext_scr)

    prevs = []
    for q in range(nb):
        prevs.append(_shift_rows(ext_scr[q], pr_ref[q], 1))
        ext_scr[q] = pr_ref[q, tb - SUBLANES:tb, :]
    p = jnp.concatenate([pr_ref[q] for q in range(nb)], axis=0)
    xs = p + (jnp.concatenate(prevs, axis=0) - p) * mu_ref[...]
    head_ones = ones_ref[...]
    r, k, v, e, kk, kka, g = _rwkv_prep(xs, w0_ref[...], a0_ref[...], wwa_ref[...], g2_ref[...],
                                         kk_ref[...], ka_ref[...], head_ones)

    ri = _iota2((tb, tb), 0)
    ci = _iota2((tb, tb), 1)
    tri = jnp.where((ri >= ci) & ((ri >> lg_c) == (ci >> lg_c)), 1.0, 0.0)
    lg_inc = -jnp.concatenate([_mm_split3(tri, e[q * tb:(q + 1) * tb, :]) for q in range(nb)], axis=0)
    lg_exc = lg_inc + e
    a_t = -kk * jnp.exp(lg_exc)
    r_t = r * jnp.exp(lg_inc)
    inv_g = jnp.exp(-lg_inc)
    b_t = kka * inv_g
    k_t = k * inv_g

    assert c == n, "two heads side by side fill one (chunk, 2 * chunk) score tile"
    stack = _pair_stack(n)

    tt = _iota2((c, pair), 0)
    ss = _iota2((c, pair), 1) & (n - 1)
    strict = tt > ss
    incl = tt >= ss
    bd = jnp.where((_iota2((pair, pair), 0) >> lg_n) == (_iota2((pair, pair), 1) >> lg_n), 1.0, 0.0)

    items = [(ch, hp) for ch in range(nb * chunks_per_seq) for hp in range(n_pairs)]

    def tile(x, ch, hp):
        return x[ch * c:(ch + 1) * c, hp * pair:(hp + 1) * pair]

    ats = [tile(a_t, *it) for it in items]
    rts = [tile(r_t, *it) for it in items]
    vts = [tile(v, *it) for it in items]
    svs = [stack(vt) for vt in vts]
    scs = [_mm(jnp.concatenate([at, rt], axis=0),
               jnp.concatenate([stack(tile(b_t, *it)), stack(tile(k_t, *it))], axis=0), _NT)
           for at, rt, it in zip(ats, rts, items)]
    tinvs = _inv_unit_lower([-jnp.where(strict, sc[0:c, 0:pair], 0.0) for sc in scs], stack)
    akv = [_mm(jnp.where(strict, sc[0:c, pair:2 * pair], 0.0), sv) for sc, sv in zip(scs, svs)]
    y_v = [_mm(jnp.where(incl, sc[c:2 * c, pair:2 * pair], 0.0), sv) for sc, sv in zip(scs, svs)]
    p_rbs = [jnp.where(incl, sc[c:2 * c, 0:pair], 0.0) for sc in scs]
    w_ts = [_mm(tinv, stack(at)) for tinv, at in zip(tinvs, ats)]
    u0s = [_mm(tinv, stack(x)) for tinv, x in zip(tinvs, akv)]
    lg_lasts = [lg_inc[ch * c + c - 1:ch * c + c, hp * pair:(hp + 1) * pair] for ch, hp in items]
    to_ends = [jnp.exp(lg_last - tile(lg_inc, *it)) for lg_last, it in zip(lg_lasts, items)]
    bhs = [tile(kka, *it) * to_end for it, to_end in zip(items, to_ends)]
    khs = [tile(k, *it) * to_end for it, to_end in zip(items, to_ends)]
    r_hats = [rt + _mm(p_rb, stack(w_t)) for rt, p_rb, w_t in zip(rts, p_rbs, w_ts)]
    y0s = [_mm(p_rb, stack(u0)) + yv for p_rb, u0, yv in zip(p_rbs, u0s, y_v)]
    s_mix = [bd * _mm(w_t.T, bh) for w_t, bh in zip(w_ts, bhs)]
    s_add = [bd * _mm(jnp.concatenate([u0, vt], axis=0).T, jnp.concatenate([bh, kh], axis=0))
             for u0, vt, bh, kh in zip(u0s, vts, bhs, khs)]

    chains = [(q, hp) for q in range(nb) for hp in range(n_pairs)]
    states = [s_scr[q, hp] for q, hp in chains]
    for blk in range(chunks_per_seq):
        idx = [(q * chunks_per_seq + blk) * n_pairs + hp for q, hp in chains]
        for (q, hp), i, s in zip(chains, idx, states):
            row0 = (q * chunks_per_seq + blk) * c
            y_scr[row0:row0 + c, hp * pair:(hp + 1) * pair] = y0s[i] + _mm(r_hats[i], s, _NT)
        states = [s * jnp.exp(lg_lasts[i]) + _mm(s, s_mix[i]) + s_add[i] for i, s in zip(idx, states)]
    for (q, hp), s in zip(chains, states):
        s_scr[q, hp] = s

    out = _rwkv_out(y_scr[...], r, k, v, g, rk_ref[...], lnw_ref[...], lnb_ref[...], head_ones)
    for q in range(nb):
        o_ref[q] = out[q * tb:(q + 1) * tb, :].astype(o_ref.dtype)

    @pl.when(t == pl.num_programs(1) - 1)
    def _():
        for q, hp in chains:
            s = s_scr[q, hp]
            s_out_ref[q, 2 * hp] = s[0:n, 0:n]
            s_out_ref[q, 2 * hp + 1] = s[n:pair, n:pair]


def _rwkv_weight_specs(fixed):
    vec = pl.BlockSpec((1, RWKV_WIDTH), fixed)
    return [pl.BlockSpec((1, RWKV_COLS), fixed), vec, vec,
            pl.BlockSpec((LORA_WA, 2 * RWKV_WIDTH), fixed), pl.BlockSpec((G_LORA, RWKV_WIDTH), fixed),
            vec, vec, vec, vec, vec, pl.BlockSpec((RWKV_WIDTH, RWKV_WIDTH), fixed)]


def _rwkv_prompt(pr, weights, *, nb, tb):
    b, t, _ = pr.shape
    fixed = lambda i, j: (0, 0)
    return pl.pallas_call(
        functools.partial(_rwkv_prompt_kernel, nb, tb),
        grid=(b // nb, t // tb),
        in_specs=[pl.BlockSpec((nb, tb, RWKV_COLS), lambda i, j: (i, j, 0))] + _rwkv_weight_specs(fixed),
        out_specs=[pl.BlockSpec((nb, tb, RWKV_WIDTH), lambda i, j: (i, j, 0)),
                   pl.BlockSpec((nb, RWKV_HEADS, RWKV_N, RWKV_N), lambda i, j: (i, 0, 0, 0))],
        out_shape=[jax.ShapeDtypeStruct((b, t, RWKV_WIDTH), BF16),
                   jax.ShapeDtypeStruct((b, RWKV_HEADS, RWKV_N, RWKV_N), F32)],
        scratch_shapes=[pltpu.VMEM((nb, RWKV_HEADS // 2, 2 * RWKV_N, 2 * RWKV_N), F32),
                        pltpu.VMEM((nb, SUBLANES, RWKV_COLS), F32),
                        pltpu.VMEM((nb * tb, RWKV_WIDTH), F32)],
        compiler_params=_params(("parallel", "arbitrary")),
        name="rwkv_prompt",
    )(pr, *weights)


def _rwkv_step_kernel(pr_ref, sh_ref, s_ref, mu_ref, w0_ref, a0_ref, wwa_ref, g2_ref, kk_ref, ka_ref,
                      rk_ref, lnw_ref, lnb_ref, ones_ref, o_ref, s_out_ref, vec_scr, row_scr, y_scr):
    h = pl.program_id(0)
    n = RWKV_N
    batch = pr_ref.shape[0]

    @pl.when(h == 0)
    def _():
        p = pr_ref[...]
        xs = p + (sh_ref[...] - p) * mu_ref[...]
        r, k, v, e, kk, kka, g = _rwkv_prep(xs, w0_ref[...], a0_ref[...], wwa_ref[...], g2_ref[...],
                                             kk_ref[...], ka_ref[...], ones_ref[...])
        for j, x in enumerate((-kk, kka, k, jnp.exp(-e), r, v)):
            vec_scr[j] = x.T.reshape(RWKV_HEADS, n, batch)
        for j, x in enumerate((r, k, v, g)):
            row_scr[j] = x

    nkk, kka, k, decay, r = (vec_scr[j, h] for j in range(5))
    for vi in range(n):
        s = s_ref[vi]
        sa = jnp.sum(s * nkk, axis=0, keepdims=True)
        s = s * decay + sa * kka + vec_scr[5, h, vi:vi + 1, :] * k
        s_out_ref[vi] = s
        y_scr[h, vi:vi + 1, :] = jnp.sum(s * r, axis=0, keepdims=True)

    @pl.when(h == RWKV_HEADS - 1)
    def _():
        y = y_scr[...].reshape(RWKV_WIDTH, batch).T
        o_ref[...] = _rwkv_out(y, row_scr[0], row_scr[1], row_scr[2], row_scr[3], rk_ref[...], lnw_ref[...],
                               lnb_ref[...], ones_ref[...])


def _rwkv_step(pr, shift, s0_t, weights):
    b = pr.shape[0]
    fixed = lambda i: (0, 0)
    state_spec = pl.BlockSpec((None, RWKV_N, RWKV_N, b), lambda i: (i, 0, 0, 0))
    return pl.pallas_call(
        _rwkv_step_kernel,
        grid=(RWKV_HEADS,),
        in_specs=[pl.BlockSpec((b, RWKV_COLS), fixed), pl.BlockSpec((b, RWKV_COLS), fixed), state_spec]
        + _rwkv_weight_specs(fixed),
        out_specs=[pl.BlockSpec((b, RWKV_WIDTH), fixed), state_spec],
        out_shape=[jax.ShapeDtypeStruct((b, RWKV_WIDTH), F32), jax.ShapeDtypeStruct(s0_t.shape, F32)],
        scratch_shapes=[pltpu.VMEM((6, RWKV_HEADS, RWKV_N, b), F32), pltpu.VMEM((4, b, RWKV_WIDTH), F32),
                        pltpu.VMEM((RWKV_HEADS, RWKV_N, b), F32)],
        compiler_params=_params(("arbitrary",)),
        name="rwkv_step",
    )(pr, shift, s0_t, *weights)


def _pad_lanes(x, width):
    return jnp.pad(x, ((0, 0), (0, width - x.shape[1])))


def kernel(x_prompt, x_sample, state_gdn_conv, state_gdn, state_rwkv_shift, state_rwkv, norm_ffn1, w_ffn1_in, w_ffn1_out, norm_mix, w_in, gdn_conv_w, gdn_a_log, gdn_dt_bias, gdn_norm_w, rwkv_mu, rwkv_w0, rwkv_w2, rwkv_a0, rwkv_a2, rwkv_g2, rwkv_k_k, rwkv_k_a, rwkv_r_k, rwkv_ln_w, rwkv_ln_b, w_out, norm_ffn2, w_ffn2_in, w_ffn2_out, norm_final):
    depth = norm_ffn1.shape[0]
    assert depth == 1, "the carried-state plumbing below is written for a single layer"
    b, t, _ = x_prompt.shape
    bs = x_sample.shape[0]
    assert x_sample.shape[1] == 1
    l = 0

    n_gdn_cols = GDN_MAIN + 2 * GDN_HEADS
    w_in_l = w_in[l]
    w_slabs = (w_in_l[:, :GDN_MAIN].astype(BF16), w_in_l[:, n_gdn_cols:].astype(BF16),
               _pad_lanes(w_in_l[:, GDN_MAIN:n_gdn_cols], AB_PAD).astype(BF16))
    w1a, w1b = w_ffn1_in[l].astype(BF16), w_ffn1_out[l].astype(BF16)
    w2a, w2b = w_ffn2_in[l].astype(BF16), w_ffn2_out[l].astype(BF16)
    woa, wob = w_out[l, :GDN_WIDTH].astype(BF16), w_out[l, GDN_WIDTH:].astype(BF16)
    half = LORA_WA // 2
    wwa = jnp.zeros((LORA_WA, 2 * RWKV_WIDTH), F32)
    wwa = wwa.at[:half, :RWKV_WIDTH].set(rwkv_w2[l]).at[half:, RWKV_WIDTH:].set(rwkv_a2[l]).astype(BF16)
    head_id = jnp.arange(RWKV_WIDTH) // RWKV_N
    head_ones = (head_id[:, None] == head_id[None, :]).astype(BF16)
    rwkv_w = (rwkv_mu[l][None], rwkv_w0[l][None], rwkv_a0[l][None], wwa, rwkv_g2[l].astype(BF16),
              rwkv_k_k[l][None], rwkv_k_a[l][None], rwkv_r_k[l].reshape(1, RWKV_WIDTH),
              rwkv_ln_w[l][None], rwkv_ln_b[l][None], head_ones)
    alog = _pad_lanes(gdn_a_log[l][None], AB_PAD)
    dtb = _pad_lanes(gdn_dt_bias[l][None], AB_PAD)
    gdn_w = (gdn_conv_w[l], alog, dtb, gdn_norm_w[l][None])
    nf1, nmix, nf2, nfin = norm_ffn1[l][None], norm_mix[l][None], norm_ffn2[l][None], norm_final[None]

    xp = x_prompt.reshape(b * t, D_MODEL)
    hp = _ffn(xp, nf1, w1a, w1b, tm=512, sub=256)
    pg, pr, pab, conv_tail = _proj_in(hp, nmix, w_slabs, gdn_conv_w[l], tm=512, sub=256, seq_rows=t)
    pg3, pr3, pab3 = pg.reshape(b, t, GDN_MAIN), pr.reshape(b, t, RWKV_COLS), pab.reshape(b, t, AB_PAD)
    oa, gdn_s = _gdn_prompt(pg3, pab3, *gdn_w[1:], nb=2, tb=256)
    ob, wkv_s = _rwkv_prompt(pr3, rwkv_w, nb=2, tb=256)
    yp = _ffn(hp, nf2, w2a, w2b, mix=(oa.reshape(b * t, GDN_WIDTH), ob.reshape(b * t, RWKV_WIDTH), woa, wob),
              final_w=nfin, tm=512, sub=256)

    xs = x_sample.reshape(bs, D_MODEL)
    hs = _ffn(xs, nf1, w1a, w1b, tm=bs, sub=bs)
    sg, sr, sab = _proj_in(hs, nmix, w_slabs, tm=bs, sub=bs)
    conv_in = state_gdn_conv[l].reshape(bs, (CONV_W - 1) * GDN_QKV)
    oa_s, conv_s, gdn_ss = _gdn_step(sg, sab, conv_in, state_gdn[l], *gdn_w, bb=8)
    ob_s, wkv_t = _rwkv_step(sr, state_rwkv_shift[l], jnp.transpose(state_rwkv[l], (1, 2, 3, 0)), rwkv_w)
    wkv_ss = jnp.transpose(wkv_t, (3, 0, 1, 2))
    ys = _ffn(hs, nf2, w2a, w2b, mix=(oa_s, ob_s, woa, wob), final_w=nfin, tm=bs, sub=bs)

    return (yp.reshape(b, t, D_MODEL), ys.reshape(bs, 1, D_MODEL),
            conv_tail[:, SUBLANES - (CONV_W - 1):, :][None], gdn_s[None], pr3[:, t - 1, :][None], wkv_s[None],
            conv_s.reshape(1, bs, CONV_W - 1, GDN_QKV), gdn_ss[None], sr[None], wkv_ss[None])
```

```python
import functools

import jax
import jax.numpy as jnp
from jax import lax
from jax.experimental import pallas as pl
from jax.experimental.pallas import tpu as pltpu

F32 = jnp.float32
BF16 = jnp.bfloat16

D_MODEL = 1024
D_FF = 2816
CONV_W = 4
GDN_HEADS = 4
GDN_DK = 128
GDN_QK = GDN_HEADS * GDN_DK
GDN_WIDTH = GDN_HEADS * GDN_DK
GDN_QKV = 3 * GDN_WIDTH
GDN_MAIN = GDN_QKV + GDN_WIDTH
RWKV_HEADS = 8
RWKV_N = 64
RWKV_WIDTH = RWKV_HEADS * RWKV_N
LORA_WA = 128
G_LORA = 128
RWKV_COLS = 3 * RWKV_WIDTH + LORA_WA + G_LORA
LANES = 128
SUBLANES = 8
AB_PAD = LANES
NORM_EPS = 1e-6
GN_EPS = 64e-5
L2_EPS = 1e-6

GDN_CHUNK = 128
RWKV_CHUNK = 64
VMEM_LIMIT = 56 * 1024 * 1024


_NN = (((1,), (0,)), ((), ()))
_NT = (((1,), (1,)), ((), ()))


def _mm(a, b, dims=_NN):
    return lax.dot_general(a.astype(BF16), b.astype(BF16), dims, preferred_element_type=F32)


def _mm_split3(sel, x):
    sel = sel.astype(BF16)
    out = None
    for _ in range(3):
        piece = x.astype(BF16)
        part = jnp.dot(sel, piece, preferred_element_type=F32)
        out = part if out is None else out + part
        x = x - piece.astype(F32)
    return out


def _sigmoid(x):
    return 1.0 / (1.0 + jnp.exp(-x))


def _silu(x):
    return x * _sigmoid(x)


def _softplus(x):
    return jnp.maximum(x, 0.0) + jnp.log(1.0 + jnp.exp(-jnp.abs(x)))


def _rms(x, g):
    return x * lax.rsqrt(jnp.mean(x * x, axis=-1, keepdims=True) + NORM_EPS) * g


def _iota2(shape, dim):
    return lax.broadcasted_iota(jnp.int32, shape, dim)


def _shift_rows(prev_tail, x, j):
    rolled = pltpu.roll(jnp.concatenate([prev_tail, x], axis=0), j, axis=0)
    return rolled[SUBLANES:, :]


def _params(sem):
    return pltpu.CompilerParams(dimension_semantics=sem, vmem_limit_bytes=VMEM_LIMIT)


def _resident(shape):
    return pl.BlockSpec(shape, lambda i: (0,) * len(shape), pipeline_mode=pl.Buffered(1))


def _ffn_kernel(has_mix, final_norm, sub, *refs):
    it = iter(refs)
    x_ref = next(it)
    if has_mix:
        oa_ref, ob_ref, woa_ref, wob_ref = next(it), next(it), next(it), next(it)
    nw_ref, w1_ref, w2_ref = next(it), next(it), next(it)
    fw_ref = next(it) if final_norm else None
    out_ref = next(it)
    for s in range(x_ref.shape[0] // sub):
        rows = slice(s * sub, (s + 1) * sub)
        x = x_ref[rows, :]
        if has_mix:
            x = x + _mm(oa_ref[rows, :], woa_ref[...]) + _mm(ob_ref[rows, :], wob_ref[...])
        xn = _rms(x, nw_ref[...]).astype(BF16)
        gate = jnp.dot(xn, w1_ref[:, 0:D_FF], preferred_element_type=F32)
        up = jnp.dot(xn, w1_ref[:, D_FF:2 * D_FF], preferred_element_type=F32)
        act = (_silu(gate) * up).astype(BF16)
        h = x + 0.5 * jnp.dot(act, w2_ref[...], preferred_element_type=F32)
        if final_norm:
            h = _rms(h, fw_ref[...])
        out_ref[rows, :] = h


def _ffn(x, norm_w, w1, w2, *, mix=None, final_w=None, tm, sub):
    m = x.shape[0]
    has_mix = mix is not None
    final_norm = final_w is not None
    row = lambda i: (i, 0)
    in_specs = [pl.BlockSpec((tm, D_MODEL), row)]
    args = [x]
    if has_mix:
        oa, ob, woa, wob = mix
        in_specs += [pl.BlockSpec((tm, GDN_WIDTH), row), pl.BlockSpec((tm, RWKV_WIDTH), row),
                     _resident((GDN_WIDTH, D_MODEL)), _resident((RWKV_WIDTH, D_MODEL))]
        args += [oa, ob, woa, wob]
    in_specs += [_resident((1, D_MODEL)), _resident((D_MODEL, 2 * D_FF)), _resident((D_FF, D_MODEL))]
    args += [norm_w, w1, w2]
    if final_norm:
        in_specs.append(_resident((1, D_MODEL)))
        args.append(final_w)
    return pl.pallas_call(
        functools.partial(_ffn_kernel, has_mix, final_norm, sub),
        grid=(m // tm,),
        in_specs=in_specs,
        out_specs=pl.BlockSpec((tm, D_MODEL), row),
        out_shape=jax.ShapeDtypeStruct((m, D_MODEL), F32),
        compiler_params=_params(("parallel",)),
        name="ffn_mix" if has_mix else "ffn",
    )(*args)


def _l2norm(x):
    return x * lax.rsqrt(jnp.sum(x * x, axis=-1, keepdims=True) + L2_EPS)


def _proj_in_kernel(h_ref, nw_ref, wg_ref, wr_ref, wab_ref, pg_ref, pr_ref, pab_ref):
    n = _rms(h_ref[...], nw_ref[...]).astype(BF16)
    pg_ref[...] = jnp.dot(n, wg_ref[...], preferred_element_type=F32)
    pr_ref[...] = jnp.dot(n, wr_ref[...], preferred_element_type=F32)
    pab_ref[...] = jnp.dot(n, wab_ref[...], preferred_element_type=F32)


def _proj_in(h, norm_w, w_slabs, *, tm):
    m = h.shape[0]
    row = lambda i: (i, 0)
    return pl.pallas_call(
        _proj_in_kernel,
        grid=(m // tm,),
        in_specs=[pl.BlockSpec((tm, D_MODEL), row), _resident((1, D_MODEL)), _resident((D_MODEL, GDN_MAIN)),
                  _resident((D_MODEL, RWKV_COLS)), _resident((D_MODEL, AB_PAD))],
        out_specs=[pl.BlockSpec((tm, GDN_MAIN), row), pl.BlockSpec((tm, RWKV_COLS), row),
                   pl.BlockSpec((tm, AB_PAD), row)],
        out_shape=[jax.ShapeDtypeStruct((m, GDN_MAIN), F32), jax.ShapeDtypeStruct((m, RWKV_COLS), F32),
                   jax.ShapeDtypeStruct((m, AB_PAD), F32)],
        compiler_params=_params(("parallel",)),
        name="proj_in",
    )(h, norm_w, *w_slabs)


def _inv_unit_lower(lows, stack=None):
    shape = lows[0].shape
    n = shape[0]
    ri = _iota2(shape, 0)
    ci = _iota2(shape, 1) & (n - 1)
    eye = jnp.where(ri == ci, 1.0, 0.0)
    pair_blk = (ri >> 1) == (ci >> 1)
    ts = [eye - jnp.where(pair_blk, low, 0.0) for low in lows]
    s, lg = 2, 1
    while s < n:
        sel = ((ri >> (lg + 1)) == (ci >> (lg + 1))) & ((ri >> lg) != (ci >> lg))
        offs = [jnp.where(sel, low, 0.0) for low in lows]
        if stack is None:
            xs = [_mm(t, off) for t, off in zip(ts, offs)]
            ts = [t - _mm(x, t) for t, x in zip(ts, xs)]
        else:
            xs = [_mm(t, stack(off)) for t, off in zip(ts, offs)]
            ts = [t - _mm(x, stack(t)) for t, x in zip(ts, xs)]
        s, lg = 2 * s, lg + 1
    return ts


def _pair_stack(m):
    first = _iota2((m, 2 * m), 1) < m

    def stack(z):
        z = z.astype(BF16)
        zero = jnp.zeros_like(z)
        return jnp.concatenate([jnp.where(first, z, zero), jnp.where(first, zero, z)], axis=0)

    return stack


def _inv_unit_lower_halves(lows):
    n = lows[0].shape[0]
    m = n // 2
    first = _iota2((m, n), 1) < m
    stack = _pair_stack(m)
    t_diag = _inv_unit_lower([jnp.where(first, low[0:m, :], low[m:n, :]) for low in lows], stack)
    lower_left = (_iota2((n, n), 0) >= m) & (_iota2((n, n), 1) < m)
    xs = [_mm(t, jnp.where(lower_left, low, 0.0)) for t, low in zip(t_diag, lows)]
    t_ll = [_mm(x, stack(t)) for x, t in zip(xs, t_diag)]
    return [jnp.concatenate([jnp.where(first, t, 0.0), jnp.where(first, 0.0, t) - ll], axis=0)
            for t, ll in zip(t_diag, t_ll)]


def _gdn_gates(ab, alog, dtb):
    log_alpha = -jnp.exp(alog) * _softplus(ab + dtb)
    return log_alpha, _sigmoid(ab)


def _gdn_out(o, z, norm_w):
    o = o * lax.rsqrt(jnp.mean(o * o, axis=-1, keepdims=True) + NORM_EPS) * norm_w
    return o * _silu(z)


def _gdn_prompt_kernel(nb, tb, pg_ref, pab_ref, cw_ref, alog_ref, dtb_ref, nw_ref, o_ref, s_out_ref,
                       s_scr, ext_scr):
    t = pl.program_id(1)
    c = GDN_CHUNK
    chunks_per_seq = tb // c

    @pl.when(t == 0)
    def _():
        s_scr[...] = jnp.zeros_like(s_scr)
        ext_scr[...] = jnp.zeros_like(ext_scr)

    accs = []
    for q in range(nb):
        u = pg_ref[q, :, 0:GDN_QKV]
        acc = u * cw_ref[CONV_W - 1:CONV_W, :]
        for j in range(1, CONV_W):
            acc = acc + _shift_rows(ext_scr[q], u, j) * cw_ref[CONV_W - 1 - j:CONV_W - j, :]
        ext_scr[q] = u[tb - SUBLANES:tb, :]
        accs.append(acc)
    qkv = _silu(jnp.concatenate(accs, axis=0))

    pab = jnp.concatenate([pab_ref[q] for q in range(nb)], axis=0)
    log_alpha, beta_all = _gdn_gates(pab, alog_ref[...], dtb_ref[...])
    ri = _iota2((c, c), 0)
    ci = _iota2((c, c), 1)
    causal = ri >= ci
    strict = ri > ci
    tri = jnp.where(causal, 1.0, 0.0)

    n_chunks = nb * chunks_per_seq
    items = [(ch, h) for ch in range(n_chunks) for h in range(GDN_HEADS)]
    g_blk, gt_blk = [], []
    for ch in range(n_chunks):
        g = _mm_split3(tri, log_alpha[ch * c:(ch + 1) * c, :])
        g_blk.append(g)
        gt_blk.append(g.T)
    qs, ks, vbs, kbs, decays, g_cols = [], [], [], [], [], []
    for ch, h in items:
        rows = slice(ch * c, (ch + 1) * c)
        q = _l2norm(qkv[rows, h * GDN_DK:(h + 1) * GDN_DK]) * (GDN_DK ** -0.5)
        k = _l2norm(qkv[rows, GDN_QK + h * GDN_DK:GDN_QK + (h + 1) * GDN_DK])
        v = qkv[rows, 2 * GDN_QK + h * GDN_DK:2 * GDN_QK + (h + 1) * GDN_DK]
        beta = beta_all[rows, GDN_HEADS + h:GDN_HEADS + h + 1]
        g_col = g_blk[ch][:, h:h + 1]
        g_row = gt_blk[ch][h:h + 1, :]
        decays.append(jnp.exp(jnp.where(causal, g_col - g_row, -jnp.inf)))
        qs.append(q)
        ks.append(k)
        vbs.append(v * beta)
        kbs.append(k * beta)
        g_cols.append(g_col)
    kk = [_mm(kb, k, _NT) for kb, k in zip(kbs, ks)]
    qk = [_mm(q, k, _NT) for q, k in zip(qs, ks)]
    tinvs = _inv_unit_lower_halves([jnp.where(strict, x * d, 0.0) for x, d in zip(kk, decays)])
    attns = [x * d for x, d in zip(qk, decays)]
    egs = [jnp.exp(g_col) for g_col in g_cols]
    us = [_mm(tinv, vb) for tinv, vb in zip(tinvs, vbs)]
    ws = [_mm(tinv, kb * eg) for tinv, kb, eg in zip(tinvs, kbs, egs)]
    g_lasts = [g_col[c - 1:c, :] for g_col in g_cols]
    kd_ts = [(k * jnp.exp(g_last - g_col)).T for k, g_last, g_col in zip(ks, g_lasts, g_cols)]
    q_hats = [q * eg - _mm(attn, w) for q, eg, attn, w in zip(qs, egs, attns, ws)]
    o0s = [_mm(attn, u) for attn, u in zip(attns, us)]
    s_mix = [_mm(kd_t, w) for kd_t, w in zip(kd_ts, ws)]
    s_add = [_mm(kd_t, u) for kd_t, u in zip(kd_ts, us)]

    chains = [(q, h) for q in range(nb) for h in range(GDN_HEADS)]
    states = [s_scr[q, h] for q, h in chains]
    for blk in range(chunks_per_seq):
        idx = [(q * chunks_per_seq + blk) * GDN_HEADS + h for q, h in chains]
        for (q, h), i, s in zip(chains, idx, states):
            o = o0s[i] + _mm(q_hats[i], s)
            rows = slice(blk * c, (blk + 1) * c)
            z = pg_ref[q, rows, GDN_QKV + h * GDN_DK:GDN_QKV + (h + 1) * GDN_DK]
            o_ref[q, rows, h * GDN_DK:(h + 1) * GDN_DK] = _gdn_out(o, z, nw_ref[...]).astype(o_ref.dtype)
        states = [s * jnp.exp(g_lasts[i]) - _mm(s_mix[i], s) + s_add[i] for i, s in zip(idx, states)]
    for (q, h), s in zip(chains, states):
        s_scr[q, h] = s

    @pl.when(t == pl.num_programs(1) - 1)
    def _():
        s_out_ref[...] = s_scr[...]


def _gdn_prompt(pg, pab, conv_w, alog, dtb, norm_w, *, nb, tb):
    b, t, _ = pg.shape
    fixed = lambda i, j: (0, 0)
    return pl.pallas_call(
        functools.partial(_gdn_prompt_kernel, nb, tb),
        grid=(b // nb, t // tb),
        in_specs=[pl.BlockSpec((nb, tb, GDN_MAIN), lambda i, j: (i, j, 0)),
                  pl.BlockSpec((nb, tb, AB_PAD), lambda i, j: (i, j, 0)),
                  pl.BlockSpec((CONV_W, GDN_QKV), fixed), pl.BlockSpec((1, AB_PAD), fixed),
                  pl.BlockSpec((1, AB_PAD), fixed), pl.BlockSpec((1, GDN_DK), fixed)],
        out_specs=[pl.BlockSpec((nb, tb, GDN_WIDTH), lambda i, j: (i, j, 0)),
                   pl.BlockSpec((nb, GDN_HEADS, GDN_DK, GDN_DK), lambda i, j: (i, 0, 0, 0))],
        out_shape=[jax.ShapeDtypeStruct((b, t, GDN_WIDTH), BF16),
                   jax.ShapeDtypeStruct((b, GDN_HEADS, GDN_DK, GDN_DK), F32)],
        scratch_shapes=[pltpu.VMEM((nb, GDN_HEADS, GDN_DK, GDN_DK), F32),
                        pltpu.VMEM((nb, SUBLANES, GDN_QKV), F32)],
        compiler_params=_params(("parallel", "arbitrary")),
        name="gdn_prompt",
    )(pg, pab, conv_w, alog, dtb, norm_w)


def _gdn_step_kernel(bb, pg_ref, pab_ref, cs_ref, s_ref, cw_ref, alog_ref, dtb_ref, nw_ref,
                     o_ref, cs_out_ref, s_out_ref):
    u = pg_ref[:, 0:GDN_QKV]
    acc = u * cw_ref[CONV_W - 1:CONV_W, :]
    for j in range(CONV_W - 1):
        acc = acc + cs_ref[:, j * GDN_QKV:(j + 1) * GDN_QKV] * cw_ref[j:j + 1, :]
    cs_out_ref[:, 0:(CONV_W - 2) * GDN_QKV] = cs_ref[:, GDN_QKV:(CONV_W - 1) * GDN_QKV]
    cs_out_ref[:, (CONV_W - 2) * GDN_QKV:(CONV_W - 1) * GDN_QKV] = u
    qkv = _silu(acc)
    log_alpha, beta_all = _gdn_gates(pab_ref[...], alog_ref[...], dtb_ref[...])
    alpha_all = jnp.exp(log_alpha)
    eye = jnp.where(_iota2((GDN_DK, GDN_DK), 0) == _iota2((GDN_DK, GDN_DK), 1), 1.0, 0.0)

    def to_col(row):
        return jnp.sum(eye * row, axis=-1, keepdims=True)

    vs, k_cols, q_cols = [], [], []
    for h in range(GDN_HEADS):
        q = qkv[:, h * GDN_DK:(h + 1) * GDN_DK]
        k = qkv[:, GDN_QK + h * GDN_DK:GDN_QK + (h + 1) * GDN_DK]
        vs.append(qkv[:, 2 * GDN_QK + h * GDN_DK:2 * GDN_QK + (h + 1) * GDN_DK])
        q = _l2norm(q) * (GDN_DK ** -0.5)
        k = _l2norm(k)
        k_cols.append([to_col(k[i:i + 1, :]) for i in range(bb)])
        q_cols.append([to_col(q[i:i + 1, :]) for i in range(bb)])
    for h in range(GDN_HEADS):
        o_rows = []
        for i in range(bb):
            s = s_ref[i, h] * alpha_all[i:i + 1, h:h + 1]
            mem = jnp.sum(k_cols[h][i] * s, axis=0, keepdims=True)
            delta = (vs[h][i:i + 1, :] - mem) * beta_all[i:i + 1, GDN_HEADS + h:GDN_HEADS + h + 1]
            s = s + k_cols[h][i] * delta
            s_out_ref[i, h] = s
            o_rows.append(jnp.sum(q_cols[h][i] * s, axis=0, keepdims=True))
        o = jnp.concatenate(o_rows, axis=0)
        z = pg_ref[:, GDN_QKV + h * GDN_DK:GDN_QKV + (h + 1) * GDN_DK]
        o_ref[:, h * GDN_DK:(h + 1) * GDN_DK] = _gdn_out(o, z, nw_ref[...])


def _gdn_step(pg, pab, conv_state, s0, conv_w, alog, dtb, norm_w, *, bb):
    b = pg.shape[0]
    row = lambda i: (i, 0)
    fixed = lambda i: (0, 0)
    cs_cols = (CONV_W - 1) * GDN_QKV
    state_spec = pl.BlockSpec((bb, GDN_HEADS, GDN_DK, GDN_DK), lambda i: (i, 0, 0, 0))
    return pl.pallas_call(
        functools.partial(_gdn_step_kernel, bb),
        grid=(b // bb,),
        in_specs=[pl.BlockSpec((bb, GDN_MAIN), row), pl.BlockSpec((bb, AB_PAD), row),
                  pl.BlockSpec((bb, cs_cols), row), state_spec,
                  pl.BlockSpec((CONV_W, GDN_QKV), fixed), pl.BlockSpec((1, AB_PAD), fixed),
                  pl.BlockSpec((1, AB_PAD), fixed), pl.BlockSpec((1, GDN_DK), fixed)],
        out_specs=[pl.BlockSpec((bb, GDN_WIDTH), row), pl.BlockSpec((bb, cs_cols), row), state_spec],
        out_shape=[jax.ShapeDtypeStruct((b, GDN_WIDTH), F32), jax.ShapeDtypeStruct((b, cs_cols), F32),
                   jax.ShapeDtypeStruct(s0.shape, F32)],
        compiler_params=_params(("parallel",)),
        name="gdn_step",
    )(pg, pab, conv_state, s0, conv_w, alog, dtb, norm_w)


def _rwkv_prep(xs, w0, a0, wwa, g2, k_k, k_a, head_ones):
    w = RWKV_WIDTH
    r, k, v = xs[:, 0:w], xs[:, w:2 * w], xs[:, 2 * w:3 * w]
    wa_in = xs[:, 3 * w:3 * w + LORA_WA]
    lane = _iota2(wa_in.shape, 1)
    wa_in = jnp.where(lane < LORA_WA // 2, jnp.tanh(wa_in), wa_in)
    wa = _mm(wa_in, wwa)
    w_log = -_softplus(-(w0 + wa[:, 0:w])) - 0.5
    a = _sigmoid(a0 + wa[:, w:2 * w])
    g = _mm(_sigmoid(xs[:, 3 * w + LORA_WA:RWKV_COLS]), g2)
    kx = k * k_k
    kk = kx * lax.rsqrt(_mm(kx * kx, head_ones) + L2_EPS)
    k = k * (1.0 + (a - 1.0) * k_a)
    return r, k, v, jnp.exp(w_log), kk, kk * a, g


def _rwkv_out(y, r, k, v, g, r_k, ln_w, ln_b, head_ones):
    inv_n = 1.0 / RWKV_N
    mean = _mm(y, head_ones) * inv_n
    yc = y - mean
    var = _mm(yc * yc, head_ones) * inv_n
    y = yc * lax.rsqrt(var + GN_EPS) * ln_w + ln_b
    bonus = _mm(r * k * r_k, head_ones) * v
    return (y + bonus) * g


def _rwkv_prompt_kernel(nb, tb, pr_ref, mu_ref, w0_ref, a0_ref, wwa_ref, g2_ref, kk_ref, ka_ref, rk_ref,
                        lnw_ref, lnb_ref, ones_ref, o_ref, s_out_ref, s_scr, ext_scr, y_scr):
    t = pl.program_id(1)
    c = RWKV_CHUNK
    n = RWKV_N
    pair = 2 * n
    lg_c = c.bit_length() - 1
    lg_n = n.bit_length() - 1
    n_pairs = RWKV_HEADS // 2
    chunks_per_seq = tb // c

    @pl.when(t == 0)
    def _():
        s_scr[...] = jnp.zeros_like(s_scr)
        ext_scr[...] = jnp.zeros_like(ext_scr)

    prevs = []
    for q in range(nb):
        prevs.append(_shift_rows(ext_scr[q], pr_ref[q], 1))
        ext_scr[q] = pr_ref[q, tb - SUBLANES:tb, :]
    p = jnp.concatenate([pr_ref[q] for q in range(nb)], axis=0)
    xs = p + (jnp.concatenate(prevs, axis=0) - p) * mu_ref[...]
    head_ones = ones_ref[...]
    r, k, v, e, kk, kka, g = _rwkv_prep(xs, w0_ref[...], a0_ref[...], wwa_ref[...], g2_ref[...],
                                         kk_ref[...], ka_ref[...], head_ones)

    ri = _iota2((tb, tb), 0)
    ci = _iota2((tb, tb), 1)
    tri = jnp.where((ri >= ci) & ((ri >> lg_c) == (ci >> lg_c)), 1.0, 0.0)
    lg_inc = -jnp.concatenate([_mm_split3(tri, e[q * tb:(q + 1) * tb, :]) for q in range(nb)], axis=0)
    lg_exc = lg_inc + e
    a_t = -kk * jnp.exp(lg_exc)
    r_t = r * jnp.exp(lg_inc)
    inv_g = jnp.exp(-lg_inc)
    b_t = kka * inv_g
    k_t = k * inv_g

    assert c == n, "two heads side by side fill one (chunk, 2 * chunk) score tile"
    stack = _pair_stack(n)

    tt = _iota2((c, pair), 0)
    ss = _iota2((c, pair), 1) & (n - 1)
    strict = tt > ss
    incl = tt >= ss
    bd = jnp.where((_iota2((pair, pair), 0) >> lg_n) == (_iota2((pair, pair), 1) >> lg_n), 1.0, 0.0)

    items = [(ch, hp) for ch in range(nb * chunks_per_seq) for hp in range(n_pairs)]

    def tile(x, ch, hp):
        return x[ch * c:(ch + 1) * c, hp * pair:(hp + 1) * pair]

    ats = [tile(a_t, *it) for it in items]
    rts = [tile(r_t, *it) for it in items]
    vts = [tile(v, *it) for it in items]
    svs = [stack(vt) for vt in vts]
    scs = [_mm(jnp.concatenate([at, rt], axis=0),
               jnp.concatenate([stack(tile(b_t, *it)), stack(tile(k_t, *it))], axis=0), _NT)
           for at, rt, it in zip(ats, rts, items)]
    tinvs = _inv_unit_lower([-jnp.where(strict, sc[0:c, 0:pair], 0.0) for sc in scs], stack)
    akv = [_mm(jnp.where(strict, sc[0:c, pair:2 * pair], 0.0), sv) for sc, sv in zip(scs, svs)]
    y_v = [_mm(jnp.where(incl, sc[c:2 * c, pair:2 * pair], 0.0), sv) for sc, sv in zip(scs, svs)]
    p_rbs = [jnp.where(incl, sc[c:2 * c, 0:pair], 0.0) for sc in scs]
    w_ts = [_mm(tinv, stack(at)) for tinv, at in zip(tinvs, ats)]
    u0s = [_mm(tinv, stack(x)) for tinv, x in zip(tinvs, akv)]
    lg_lasts = [lg_inc[ch * c + c - 1:ch * c + c, hp * pair:(hp + 1) * pair] for ch, hp in items]
    to_ends = [jnp.exp(lg_last - tile(lg_inc, *it)) for lg_last, it in zip(lg_lasts, items)]
    bhs = [tile(kka, *it) * to_end for it, to_end in zip(items, to_ends)]
    khs = [tile(k, *it) * to_end for it, to_end in zip(items, to_ends)]
    r_hats = [rt + _mm(p_rb, stack(w_t)) for rt, p_rb, w_t in zip(rts, p_rbs, w_ts)]
    y0s = [_mm(p_rb, stack(u0)) + yv for p_rb, u0, yv in zip(p_rbs, u0s, y_v)]
    s_mix = [bd * _mm(w_t.T, bh) for w_t, bh in zip(w_ts, bhs)]
    s_add = [bd * _mm(jnp.concatenate([u0, vt], axis=0).T, jnp.concatenate([bh, kh], axis=0))
             for u0, vt, bh, kh in zip(u0s, vts, bhs, khs)]

    chains = [(q, hp) for q in range(nb) for hp in range(n_pairs)]
    states = [s_scr[q, hp] for q, hp in chains]
    for blk in range(chunks_per_seq):
        idx = [(q * chunks_per_seq + blk) * n_pairs + hp for q, hp in chains]
        for (q, hp), i, s in zip(chains, idx, states):
            row0 = (q * chunks_per_seq + blk) * c
            y_scr[row0:row0 + c, hp * pair:(hp + 1) * pair] = y0s[i] + _mm(r_hats[i], s, _NT)
        states = [s * jnp.exp(lg_lasts[i]) + _mm(s, s_mix[i]) + s_add[i] for i, s in zip(idx, states)]
    for (q, hp), s in zip(chains, states):
        s_scr[q, hp] = s

    out = _rwkv_out(y_scr[...], r, k, v, g, rk_ref[...], lnw_ref[...], lnb_ref[...], head_ones)
    for q in range(nb):
        o_ref[q] = out[q * tb:(q + 1) * tb, :].astype(o_ref.dtype)

    @pl.when(t == pl.num_programs(1) - 1)
    def _():
        for q, hp in chains:
            s = s_scr[q, hp]
            s_out_ref[q, 2 * hp] = s[0:n, 0:n]
            s_out_ref[q, 2 * hp + 1] = s[n:pair, n:pair]


def _rwkv_weight_specs(fixed):
    vec = pl.BlockSpec((1, RWKV_WIDTH), fixed)
    return [pl.BlockSpec((1, RWKV_COLS), fixed), vec, vec,
            pl.BlockSpec((LORA_WA, 2 * RWKV_WIDTH), fixed), pl.BlockSpec((G_LORA, RWKV_WIDTH), fixed),
            vec, vec, vec, vec, vec, pl.BlockSpec((RWKV_WIDTH, RWKV_WIDTH), fixed)]


def _rwkv_prompt(pr, weights, *, nb, tb):
    b, t, _ = pr.shape
    fixed = lambda i, j: (0, 0)
    return pl.pallas_call(
        functools.partial(_rwkv_prompt_kernel, nb, tb),
        grid=(b // nb, t // tb),
        in_specs=[pl.BlockSpec((nb, tb, RWKV_COLS), lambda i, j: (i, j, 0))] + _rwkv_weight_specs(fixed),
        out_specs=[pl.BlockSpec((nb, tb, RWKV_WIDTH), lambda i, j: (i, j, 0)),
                   pl.BlockSpec((nb, RWKV_HEADS, RWKV_N, RWKV_N), lambda i, j: (i, 0, 0, 0))],
        out_shape=[jax.ShapeDtypeStruct((b, t, RWKV_WIDTH), BF16),
                   jax.ShapeDtypeStruct((b, RWKV_HEADS, RWKV_N, RWKV_N), F32)],
        scratch_shapes=[pltpu.VMEM((nb, RWKV_HEADS // 2, 2 * RWKV_N, 2 * RWKV_N), F32),
                        pltpu.VMEM((nb, SUBLANES, RWKV_COLS), F32),
                        pltpu.VMEM((nb * tb, RWKV_WIDTH), F32)],
        compiler_params=_params(("parallel", "arbitrary")),
        name="rwkv_prompt",
    )(pr, *weights)


def _rwkv_step_kernel(pr_ref, sh_ref, s_ref, mu_ref, w0_ref, a0_ref, wwa_ref, g2_ref, kk_ref, ka_ref,
                      rk_ref, lnw_ref, lnb_ref, ones_ref, o_ref, s_out_ref, vec_scr, row_scr, y_scr):
    h = pl.program_id(0)
    n = RWKV_N
    batch = pr_ref.shape[0]

    @pl.when(h == 0)
    def _():
        p = pr_ref[...]
        xs = p + (sh_ref[...] - p) * mu_ref[...]
        r, k, v, e, kk, kka, g = _rwkv_prep(xs, w0_ref[...], a0_ref[...], wwa_ref[...], g2_ref[...],
                                             kk_ref[...], ka_ref[...], ones_ref[...])
        for j, x in enumerate((-kk, kka, k, jnp.exp(-e), r, v)):
            vec_scr[j] = x.T.reshape(RWKV_HEADS, n, batch)
        for j, x in enumerate((r, k, v, g)):
            row_scr[j] = x

    nkk, kka, k, decay, r = (vec_scr[j, h] for j in range(5))
    for vi in range(n):
        s = s_ref[vi]
        sa = jnp.sum(s * nkk, axis=0, keepdims=True)
        s = s * decay + sa * kka + vec_scr[5, h, vi:vi + 1, :] * k
        s_out_ref[vi] = s
        y_scr[h, vi:vi + 1, :] = jnp.sum(s * r, axis=0, keepdims=True)

    @pl.when(h == RWKV_HEADS - 1)
    def _():
        y = y_scr[...].reshape(RWKV_WIDTH, batch).T
        o_ref[...] = _rwkv_out(y, row_scr[0], row_scr[1], row_scr[2], row_scr[3], rk_ref[...], lnw_ref[...],
                               lnb_ref[...], ones_ref[...])


def _rwkv_step(pr, shift, s0_t, weights):
    b = pr.shape[0]
    fixed = lambda i: (0, 0)
    state_spec = pl.BlockSpec((None, RWKV_N, RWKV_N, b), lambda i: (i, 0, 0, 0))
    return pl.pallas_call(
        _rwkv_step_kernel,
        grid=(RWKV_HEADS,),
        in_specs=[pl.BlockSpec((b, RWKV_COLS), fixed), pl.BlockSpec((b, RWKV_COLS), fixed), state_spec]
        + _rwkv_weight_specs(fixed),
        out_specs=[pl.BlockSpec((b, RWKV_WIDTH), fixed), state_spec],
        out_shape=[jax.ShapeDtypeStruct((b, RWKV_WIDTH), F32), jax.ShapeDtypeStruct(s0_t.shape, F32)],
        scratch_shapes=[pltpu.VMEM((6, RWKV_HEADS, RWKV_N, b), F32), pltpu.VMEM((4, b, RWKV_WIDTH), F32),
                        pltpu.VMEM((RWKV_HEADS, RWKV_N, b), F32)],
        compiler_params=_params(("arbitrary",)),
        name="rwkv_step",
    )(pr, shift, s0_t, *weights)


def _pad_lanes(x, width):
    return jnp.pad(x, ((0, 0), (0, width - x.shape[1])))


def kernel(x_prompt, x_sample, state_gdn_conv, state_gdn, state_rwkv_shift, state_rwkv, norm_ffn1, w_ffn1_in, w_ffn1_out, norm_mix, w_in, gdn_conv_w, gdn_a_log, gdn_dt_bias, gdn_norm_w, rwkv_mu, rwkv_w0, rwkv_w2, rwkv_a0, rwkv_a2, rwkv_g2, rwkv_k_k, rwkv_k_a, rwkv_r_k, rwkv_ln_w, rwkv_ln_b, w_out, norm_ffn2, w_ffn2_in, w_ffn2_out, norm_final):
    depth = norm_ffn1.shape[0]
    assert depth == 1, "the carried-state plumbing below is written for a single layer"
    b, t, _ = x_prompt.shape
    bs = x_sample.shape[0]
    assert x_sample.shape[1] == 1
    l = 0

    n_gdn_cols = GDN_MAIN + 2 * GDN_HEADS
    w_in_l = w_in[l]
    w_slabs = (w_in_l[:, :GDN_MAIN].astype(BF16), w_in_l[:, n_gdn_cols:].astype(BF16),
               _pad_lanes(w_in_l[:, GDN_MAIN:n_gdn_cols], AB_PAD).astype(BF16))
    w1a, w1b = w_ffn1_in[l].astype(BF16), w_ffn1_out[l].astype(BF16)
    w2a, w2b = w_ffn2_in[l].astype(BF16), w_ffn2_out[l].astype(BF16)
    woa, wob = w_out[l, :GDN_WIDTH].astype(BF16), w_out[l, GDN_WIDTH:].astype(BF16)
    half = LORA_WA // 2
    wwa = jnp.zeros((LORA_WA, 2 * RWKV_WIDTH), F32)
    wwa = wwa.at[:half, :RWKV_WIDTH].set(rwkv_w2[l]).at[half:, RWKV_WIDTH:].set(rwkv_a2[l]).astype(BF16)
    head_id = jnp.arange(RWKV_WIDTH) // RWKV_N
    head_ones = (head_id[:, None] == head_id[None, :]).astype(BF16)
    rwkv_w = (rwkv_mu[l][None], rwkv_w0[l][None], rwkv_a0[l][None], wwa, rwkv_g2[l].astype(BF16),
              rwkv_k_k[l][None], rwkv_k_a[l][None], rwkv_r_k[l].reshape(1, RWKV_WIDTH),
              rwkv_ln_w[l][None], rwkv_ln_b[l][None], head_ones)
    alog = _pad_lanes(gdn_a_log[l][None], AB_PAD)
    dtb = _pad_lanes(gdn_dt_bias[l][None], AB_PAD)
    gdn_w = (gdn_conv_w[l], alog, dtb, gdn_norm_w[l][None])
    nf1, nmix, nf2, nfin = norm_ffn1[l][None], norm_mix[l][None], norm_ffn2[l][None], norm_final[None]

    xp = x_prompt.reshape(b * t, D_MODEL)
    hp = _ffn(xp, nf1, w1a, w1b, tm=512, sub=256)
    pg, pr, pab = _proj_in(hp, nmix, w_slabs, tm=512)
    pg3, pr3, pab3 = pg.reshape(b, t, GDN_MAIN), pr.reshape(b, t, RWKV_COLS), pab.reshape(b, t, AB_PAD)
    oa, gdn_s = _gdn_prompt(pg3, pab3, *gdn_w, nb=2, tb=256)
    ob, wkv_s = _rwkv_prompt(pr3, rwkv_w, nb=2, tb=256)
    yp = _ffn(hp, nf2, w2a, w2b, mix=(oa.reshape(b * t, GDN_WIDTH), ob.reshape(b * t, RWKV_WIDTH), woa, wob),
              final_w=nfin, tm=512, sub=256)

    xs = x_sample.reshape(bs, D_MODEL)
    hs = _ffn(xs, nf1, w1a, w1b, tm=bs, sub=bs)
    sg, sr, sab = _proj_in(hs, nmix, w_slabs, tm=bs)
    conv_in = state_gdn_conv[l].reshape(bs, (CONV_W - 1) * GDN_QKV)
    oa_s, conv_s, gdn_ss = _gdn_step(sg, sab, conv_in, state_gdn[l], *gdn_w, bb=8)
    ob_s, wkv_t = _rwkv_step(sr, state_rwkv_shift[l], jnp.transpose(state_rwkv[l], (1, 2, 3, 0)), rwkv_w)
    wkv_ss = jnp.transpose(wkv_t, (3, 0, 1, 2))
    ys = _ffn(hs, nf2, w2a, w2b, mix=(oa_s, ob_s, woa, wob), final_w=nfin, tm=bs, sub=bs)

    return (yp.reshape(b, t, D_MODEL), ys.reshape(bs, 1, D_MODEL),
            pg3[:, t - (CONV_W - 1):, :GDN_QKV][None], gdn_s[None], pr3[:, t - 1, :][None], wkv_s[None],
            conv_s.reshape(1, bs, CONV_W - 1, GDN_QKV), gdn_ss[None], sr[None], wkv_ss[None])
```

```python
import functools

import jax
import jax.numpy as jnp
from jax import lax
from jax.experimental import pallas as pl
from jax.experimental.pallas import tpu as pltpu

F32 = jnp.float32
BF16 = jnp.bfloat16

D_MODEL = 1024
D_FF = 2816
CONV_W = 4
GDN_HEADS = 4
GDN_DK = 128
GDN_QK = GDN_HEADS * GDN_DK
GDN_WIDTH = GDN_HEADS * GDN_DK
GDN_QKV = 3 * GDN_WIDTH
GDN_MAIN = GDN_QKV + GDN_WIDTH
RWKV_HEADS = 8
RWKV_N = 64
RWKV_WIDTH = RWKV_HEADS * RWKV_N
LORA_WA = 128
G_LORA = 128
RWKV_COLS = 3 * RWKV_WIDTH + LORA_WA + G_LORA
LANES = 128
SUBLANES = 8
AB_PAD = LANES
NORM_EPS = 1e-6
GN_EPS = 64e-5
L2_EPS = 1e-6

GDN_CHUNK = 128
RWKV_CHUNK = 64
VMEM_LIMIT = 56 * 1024 * 1024


_NN = (((1,), (0,)), ((), ()))
_NT = (((1,), (1,)), ((), ()))


def _mm(a, b, dims=_NN):
    return lax.dot_general(a.astype(BF16), b.astype(BF16), dims, preferred_element_type=F32)


def _mm_split3(sel, x):
    sel = sel.astype(BF16)
    out = None
    for _ in range(3):
        piece = x.astype(BF16)
        part = jnp.dot(sel, piece, preferred_element_type=F32)
        out = part if out is None else out + part
        x = x - piece.astype(F32)
    return out


def _sigmoid(x):
    return 1.0 / (1.0 + jnp.exp(-x))


def _silu(x):
    return x * _sigmoid(x)


def _softplus(x):
    return jnp.maximum(x, 0.0) + jnp.log(1.0 + jnp.exp(-jnp.abs(x)))


def _rms(x, g):
    return x * lax.rsqrt(jnp.mean(x * x, axis=-1, keepdims=True) + NORM_EPS) * g


def _iota2(shape, dim):
    return lax.broadcasted_iota(jnp.int32, shape, dim)


def _shift_rows(prev_tail, x, j):
    rolled = pltpu.roll(jnp.concatenate([prev_tail, x], axis=0), j, axis=0)
    return rolled[SUBLANES:, :]


def _params(sem):
    return pltpu.CompilerParams(dimension_semantics=sem, vmem_limit_bytes=VMEM_LIMIT)


def _resident(shape):
    return pl.BlockSpec(shape, lambda i: (0,) * len(shape), pipeline_mode=pl.Buffered(1))


def _ffn_rows(x_ref, mix_refs, w_refs, out_ref, sub):
    woa_ref, wob_ref, nw_ref, w1_ref, w2_ref, fw_ref = w_refs
    for s in range(x_ref.shape[0] // sub):
        rows = slice(s * sub, (s + 1) * sub)
        x = x_ref[rows, :]
        if mix_refs is not None:
            oa_ref, ob_ref = mix_refs
            x = x + _mm(oa_ref[rows, :], woa_ref[...]) + _mm(ob_ref[rows, :], wob_ref[...])
        xn = _rms(x, nw_ref[...]).astype(BF16)
        gate = jnp.dot(xn, w1_ref[:, 0:D_FF], preferred_element_type=F32)
        up = jnp.dot(xn, w1_ref[:, D_FF:2 * D_FF], preferred_element_type=F32)
        act = (_silu(gate) * up).astype(BF16)
        h = x + 0.5 * jnp.dot(act, w2_ref[...], preferred_element_type=F32)
        if fw_ref is not None:
            h = _rms(h, fw_ref[...])
        out_ref[rows, :] = h


def _ffn_kernel(has_mix, final_norm, sub, *refs):
    it = iter(refs)
    x_ref, xt_ref = next(it), next(it)
    mix_refs = tail_mix_refs = None
    woa_ref = wob_ref = None
    if has_mix:
        mix_refs, tail_mix_refs = (next(it), next(it)), (next(it), next(it))
        woa_ref, wob_ref = next(it), next(it)
    nw_ref, w1_ref, w2_ref = next(it), next(it), next(it)
    fw_ref = next(it) if final_norm else None
    out_ref, out_tail_ref = next(it), next(it)
    w_refs = (woa_ref, wob_ref, nw_ref, w1_ref, w2_ref, fw_ref)
    i = pl.program_id(0)
    last = pl.num_programs(0) - 1

    @pl.when(i < last)
    def _():
        _ffn_rows(x_ref, mix_refs, w_refs, out_ref, sub)

    @pl.when(i == last)
    def _():
        _ffn_rows(xt_ref, tail_mix_refs, w_refs, out_tail_ref, xt_ref.shape[0])


def _ffn(x, x_tail, norm_w, w1, w2, *, mix=None, final_w=None, tm, sub):
    m, mt = x.shape[0], x_tail.shape[0]
    n_tiles = m // tm
    has_mix = mix is not None
    final_norm = final_w is not None
    row = lambda i: (jnp.minimum(i, n_tiles - 1), 0)
    tail = lambda i: (0, 0)
    in_specs = [pl.BlockSpec((tm, D_MODEL), row), pl.BlockSpec((mt, D_MODEL), tail)]
    args = [x, x_tail]
    if has_mix:
        oa, ob, oa_tail, ob_tail, woa, wob = mix
        in_specs += [pl.BlockSpec((tm, GDN_WIDTH), row), pl.BlockSpec((tm, RWKV_WIDTH), row),
                     pl.BlockSpec((mt, GDN_WIDTH), tail), pl.BlockSpec((mt, RWKV_WIDTH), tail),
                     _resident((GDN_WIDTH, D_MODEL)), _resident((RWKV_WIDTH, D_MODEL))]
        args += [oa, ob, oa_tail, ob_tail, woa, wob]
    in_specs += [_resident((1, D_MODEL)), _resident((D_MODEL, 2 * D_FF)), _resident((D_FF, D_MODEL))]
    args += [norm_w, w1, w2]
    if final_norm:
        in_specs.append(_resident((1, D_MODEL)))
        args.append(final_w)
    return pl.pallas_call(
        functools.partial(_ffn_kernel, has_mix, final_norm, sub),
        grid=(n_tiles + 1,),
        in_specs=in_specs,
        out_specs=[pl.BlockSpec((tm, D_MODEL), row), pl.BlockSpec((mt, D_MODEL), tail)],
        out_shape=[jax.ShapeDtypeStruct((m, D_MODEL), F32), jax.ShapeDtypeStruct((mt, D_MODEL), F32)],
        compiler_params=_params(("arbitrary",)),
        name="ffn_mix" if has_mix else "ffn",
    )(*args)


def _l2norm(x):
    return x * lax.rsqrt(jnp.sum(x * x, axis=-1, keepdims=True) + L2_EPS)


def _proj_in_rows(h_ref, nw_ref, w_refs, out_refs):
    n = _rms(h_ref[...], nw_ref[...]).astype(BF16)
    for w_ref, out_ref in zip(w_refs, out_refs):
        out_ref[...] = jnp.dot(n, w_ref[...], preferred_element_type=F32)


def _proj_in_kernel(h_ref, ht_ref, nw_ref, wg_ref, wr_ref, wab_ref, pg_ref, pr_ref, pab_ref,
                    pgt_ref, prt_ref, pabt_ref):
    i = pl.program_id(0)
    last = pl.num_programs(0) - 1
    w_refs = (wg_ref, wr_ref, wab_ref)

    @pl.when(i < last)
    def _():
        _proj_in_rows(h_ref, nw_ref, w_refs, (pg_ref, pr_ref, pab_ref))

    @pl.when(i == last)
    def _():
        _proj_in_rows(ht_ref, nw_ref, w_refs, (pgt_ref, prt_ref, pabt_ref))


def _proj_in(h, h_tail, norm_w, w_slabs, *, tm):
    m, mt = h.shape[0], h_tail.shape[0]
    n_tiles = m // tm
    row = lambda i: (jnp.minimum(i, n_tiles - 1), 0)
    tail = lambda i: (0, 0)
    widths = (GDN_MAIN, RWKV_COLS, AB_PAD)
    return pl.pallas_call(
        _proj_in_kernel,
        grid=(n_tiles + 1,),
        in_specs=[pl.BlockSpec((tm, D_MODEL), row), pl.BlockSpec((mt, D_MODEL), tail), _resident((1, D_MODEL))]
        + [_resident((D_MODEL, w)) for w in widths],
        out_specs=[pl.BlockSpec((tm, w), row) for w in widths] + [pl.BlockSpec((mt, w), tail) for w in widths],
        out_shape=[jax.ShapeDtypeStruct((m, w), F32) for w in widths]
        + [jax.ShapeDtypeStruct((mt, w), F32) for w in widths],
        compiler_params=_params(("arbitrary",)),
        name="proj_in",
    )(h, h_tail, norm_w, *w_slabs)


def _inv_unit_lower(lows, stack=None):
    shape = lows[0].shape
    n = shape[0]
    ri = _iota2(shape, 0)
    ci = _iota2(shape, 1) & (n - 1)
    eye = jnp.where(ri == ci, 1.0, 0.0)
    pair_blk = (ri >> 1) == (ci >> 1)
    ts = [eye - jnp.where(pair_blk, low, 0.0) for low in lows]
    s, lg = 2, 1
    while s < n:
        sel = ((ri >> (lg + 1)) == (ci >> (lg + 1))) & ((ri >> lg) != (ci >> lg))
        offs = [jnp.where(sel, low, 0.0) for low in lows]
        if stack is None:
            xs = [_mm(t, off) for t, off in zip(ts, offs)]
            ts = [t - _mm(x, t) for t, x in zip(ts, xs)]
        else:
            xs = [_mm(t, stack(off)) for t, off in zip(ts, offs)]
            ts = [t - _mm(x, stack(t)) for t, x in zip(ts, xs)]
        s, lg = 2 * s, lg + 1
    return ts


def _pair_stack(m):
    first = _iota2((m, 2 * m), 1) < m

    def stack(z):
        z = z.astype(BF16)
        zero = jnp.zeros_like(z)
        return jnp.concatenate([jnp.where(first, z, zero), jnp.where(first, zero, z)], axis=0)

    return stack


def _inv_unit_lower_halves(lows):
    n = lows[0].shape[0]
    m = n // 2
    first = _iota2((m, n), 1) < m
    stack = _pair_stack(m)
    t_diag = _inv_unit_lower([jnp.where(first, low[0:m, :], low[m:n, :]) for low in lows], stack)
    lower_left = (_iota2((n, n), 0) >= m) & (_iota2((n, n), 1) < m)
    xs = [_mm(t, jnp.where(lower_left, low, 0.0)) for t, low in zip(t_diag, lows)]
    t_ll = [_mm(x, stack(t)) for x, t in zip(xs, t_diag)]
    return [jnp.concatenate([jnp.where(first, t, 0.0), jnp.where(first, 0.0, t) - ll], axis=0)
            for t, ll in zip(t_diag, t_ll)]


def _gdn_gates(ab, alog, dtb):
    log_alpha = -jnp.exp(alog) * _softplus(ab + dtb)
    return log_alpha, _sigmoid(ab)


def _gdn_out(o, z, norm_w):
    o = o * lax.rsqrt(jnp.mean(o * o, axis=-1, keepdims=True) + NORM_EPS) * norm_w
    return o * _silu(z)


def _gdn_prompt_kernel(nb, tb, pg_ref, pab_ref, cw_ref, alog_ref, dtb_ref, nw_ref, o_ref, s_out_ref,
                       s_scr, ext_scr):
    t = pl.program_id(1)
    c = GDN_CHUNK
    chunks_per_seq = tb // c

    @pl.when(t == 0)
    def _():
        s_scr[...] = jnp.zeros_like(s_scr)
        ext_scr[...] = jnp.zeros_like(ext_scr)

    accs = []
    for q in range(nb):
        u = pg_ref[q, :, 0:GDN_QKV]
        acc = u * cw_ref[CONV_W - 1:CONV_W, :]
        for j in range(1, CONV_W):
            acc = acc + _shift_rows(ext_scr[q], u, j) * cw_ref[CONV_W - 1 - j:CONV_W - j, :]
        ext_scr[q] = u[tb - SUBLANES:tb, :]
        accs.append(acc)
    qkv = _silu(jnp.concatenate(accs, axis=0))

    pab = jnp.concatenate([pab_ref[q] for q in range(nb)], axis=0)
    log_alpha, beta_all = _gdn_gates(pab, alog_ref[...], dtb_ref[...])
    ri = _iota2((c, c), 0)
    ci = _iota2((c, c), 1)
    causal = ri >= ci
    strict = ri > ci
    tri = jnp.where(causal, 1.0, 0.0)

    n_chunks = nb * chunks_per_seq
    items = [(ch, h) for ch in range(n_chunks) for h in range(GDN_HEADS)]
    g_blk, gt_blk = [], []
    for ch in range(n_chunks):
        g = _mm_split3(tri, log_alpha[ch * c:(ch + 1) * c, :])
        g_blk.append(g)
        gt_blk.append(g.T)
    qs, ks, vbs, kbs, decays, g_cols = [], [], [], [], [], []
    for ch, h in items:
        rows = slice(ch * c, (ch + 1) * c)
        q = _l2norm(qkv[rows, h * GDN_DK:(h + 1) * GDN_DK]) * (GDN_DK ** -0.5)
        k = _l2norm(qkv[rows, GDN_QK + h * GDN_DK:GDN_QK + (h + 1) * GDN_DK])
        v = qkv[rows, 2 * GDN_QK + h * GDN_DK:2 * GDN_QK + (h + 1) * GDN_DK]
        beta = beta_all[rows, GDN_HEADS + h:GDN_HEADS + h + 1]
        g_col = g_blk[ch][:, h:h + 1]
        g_row = gt_blk[ch][h:h + 1, :]
        decays.append(jnp.exp(jnp.where(causal, g_col - g_row, -jnp.inf)))
        qs.append(q)
        ks.append(k)
        vbs.append(v * beta)
        kbs.append(k * beta)
        g_cols.append(g_col)
    kk = [_mm(kb, k, _NT) for kb, k in zip(kbs, ks)]
    qk = [_mm(q, k, _NT) for q, k in zip(qs, ks)]
    tinvs = _inv_unit_lower_halves([jnp.where(strict, x * d, 0.0) for x, d in zip(kk, decays)])
    attns = [x * d for x, d in zip(qk, decays)]
    egs = [jnp.exp(g_col) for g_col in g_cols]
    us = [_mm(tinv, vb) for tinv, vb in zip(tinvs, vbs)]
    ws = [_mm(tinv, kb * eg) for tinv, kb, eg in zip(tinvs, kbs, egs)]
    g_lasts = [g_col[c - 1:c, :] for g_col in g_cols]
    kd_ts = [(k * jnp.exp(g_last - g_col)).T for k, g_last, g_col in zip(ks, g_lasts, g_cols)]
    q_hats = [q * eg - _mm(attn, w) for q, eg, attn, w in zip(qs, egs, attns, ws)]
    o0s = [_mm(attn, u) for attn, u in zip(attns, us)]
    s_mix = [_mm(kd_t, w) for kd_t, w in zip(kd_ts, ws)]
    s_add = [_mm(kd_t, u) for kd_t, u in zip(kd_ts, us)]

    chains = [(q, h) for q in range(nb) for h in range(GDN_HEADS)]
    states = [s_scr[q, h] for q, h in chains]
    for blk in range(chunks_per_seq):
        idx = [(q * chunks_per_seq + blk) * GDN_HEADS + h for q, h in chains]
        for (q, h), i, s in zip(chains, idx, states):
            o = o0s[i] + _mm(q_hats[i], s)
            rows = slice(blk * c, (blk + 1) * c)
            z = pg_ref[q, rows, GDN_QKV + h * GDN_DK:GDN_QKV + (h + 1) * GDN_DK]
            o_ref[q, rows, h * GDN_DK:(h + 1) * GDN_DK] = _gdn_out(o, z, nw_ref[...]).astype(o_ref.dtype)
        states = [s * jnp.exp(g_lasts[i]) - _mm(s_mix[i], s) + s_add[i] for i, s in zip(idx, states)]
    for (q, h), s in zip(chains, states):
        s_scr[q, h] = s

    @pl.when(t == pl.num_programs(1) - 1)
    def _():
        s_out_ref[...] = s_scr[...]


def _gdn_prompt(pg, pab, conv_w, alog, dtb, norm_w, *, nb, tb):
    b, t, _ = pg.shape
    fixed = lambda i, j: (0, 0)
    return pl.pallas_call(
        functools.partial(_gdn_prompt_kernel, nb, tb),
        grid=(b // nb, t // tb),
        in_specs=[pl.BlockSpec((nb, tb, GDN_MAIN), lambda i, j: (i, j, 0)),
                  pl.BlockSpec((nb, tb, AB_PAD), lambda i, j: (i, j, 0)),
                  pl.BlockSpec((CONV_W, GDN_QKV), fixed), pl.BlockSpec((1, AB_PAD), fixed),
                  pl.BlockSpec((1, AB_PAD), fixed), pl.BlockSpec((1, GDN_DK), fixed)],
        out_specs=[pl.BlockSpec((nb, tb, GDN_WIDTH), lambda i, j: (i, j, 0)),
                   pl.BlockSpec((nb, GDN_HEADS, GDN_DK, GDN_DK), lambda i, j: (i, 0, 0, 0))],
        out_shape=[jax.ShapeDtypeStruct((b, t, GDN_WIDTH), BF16),
                   jax.ShapeDtypeStruct((b, GDN_HEADS, GDN_DK, GDN_DK), F32)],
        scratch_shapes=[pltpu.VMEM((nb, GDN_HEADS, GDN_DK, GDN_DK), F32),
                        pltpu.VMEM((nb, SUBLANES, GDN_QKV), F32)],
        compiler_params=_params(("parallel", "arbitrary")),
        name="gdn_prompt",
    )(pg, pab, conv_w, alog, dtb, norm_w)


def _gdn_step_kernel(bb, pg_ref, pab_ref, cs_ref, s_ref, cw_ref, alog_ref, dtb_ref, nw_ref,
                     o_ref, cs_out_ref, s_out_ref):
    u = pg_ref[:, 0:GDN_QKV]
    acc = u * cw_ref[CONV_W - 1:CONV_W, :]
    for j in range(CONV_W - 1):
        acc = acc + cs_ref[:, j * GDN_QKV:(j + 1) * GDN_QKV] * cw_ref[j:j + 1, :]
    cs_out_ref[:, 0:(CONV_W - 2) * GDN_QKV] = cs_ref[:, GDN_QKV:(CONV_W - 1) * GDN_QKV]
    cs_out_ref[:, (CONV_W - 2) * GDN_QKV:(CONV_W - 1) * GDN_QKV] = u
    qkv = _silu(acc)
    log_alpha, beta_all = _gdn_gates(pab_ref[...], alog_ref[...], dtb_ref[...])
    alpha_all = jnp.exp(log_alpha)
    eye = jnp.where(_iota2((GDN_DK, GDN_DK), 0) == _iota2((GDN_DK, GDN_DK), 1), 1.0, 0.0)

    def to_col(row):
        return jnp.sum(eye * row, axis=-1, keepdims=True)

    vs, k_cols, q_cols = [], [], []
    for h in range(GDN_HEADS):
        q = qkv[:, h * GDN_DK:(h + 1) * GDN_DK]
        k = qkv[:, GDN_QK + h * GDN_DK:GDN_QK + (h + 1) * GDN_DK]
        vs.append(qkv[:, 2 * GDN_QK + h * GDN_DK:2 * GDN_QK + (h + 1) * GDN_DK])
        q = _l2norm(q) * (GDN_DK ** -0.5)
        k = _l2norm(k)
        k_cols.append([to_col(k[i:i + 1, :]) for i in range(bb)])
        q_cols.append([to_col(q[i:i + 1, :]) for i in range(bb)])
    for h in range(GDN_HEADS):
        o_rows = []
        for i in range(bb):
            s = s_ref[i, h] * alpha_all[i:i + 1, h:h + 1]
            mem = jnp.sum(k_cols[h][i] * s, axis=0, keepdims=True)
            delta = (vs[h][i:i + 1, :] - mem) * beta_all[i:i + 1, GDN_HEADS + h:GDN_HEADS + h + 1]
            s = s + k_cols[h][i] * delta
            s_out_ref[i, h] = s
            o_rows.append(jnp.sum(q_cols[h][i] * s, axis=0, keepdims=True))
        o = jnp.concatenate(o_rows, axis=0)
        z = pg_ref[:, GDN_QKV + h * GDN_DK:GDN_QKV + (h + 1) * GDN_DK]
        o_ref[:, h * GDN_DK:(h + 1) * GDN_DK] = _gdn_out(o, z, nw_ref[...])


def _gdn_step(pg, pab, conv_state, s0, conv_w, alog, dtb, norm_w, *, bb):
    b = pg.shape[0]
    row = lambda i: (i, 0)
    fixed = lambda i: (0, 0)
    cs_cols = (CONV_W - 1) * GDN_QKV
    state_spec = pl.BlockSpec((bb, GDN_HEADS, GDN_DK, GDN_DK), lambda i: (i, 0, 0, 0))
    return pl.pallas_call(
        functools.partial(_gdn_step_kernel, bb),
        grid=(b // bb,),
        in_specs=[pl.BlockSpec((bb, GDN_MAIN), row), pl.BlockSpec((bb, AB_PAD), row),
                  pl.BlockSpec((bb, cs_cols), row), state_spec,
                  pl.BlockSpec((CONV_W, GDN_QKV), fixed), pl.BlockSpec((1, AB_PAD), fixed),
                  pl.BlockSpec((1, AB_PAD), fixed), pl.BlockSpec((1, GDN_DK), fixed)],
        out_specs=[pl.BlockSpec((bb, GDN_WIDTH), row), pl.BlockSpec((bb, cs_cols), row), state_spec],
        out_shape=[jax.ShapeDtypeStruct((b, GDN_WIDTH), F32), jax.ShapeDtypeStruct((b, cs_cols), F32),
                   jax.ShapeDtypeStruct(s0.shape, F32)],
        compiler_params=_params(("parallel",)),
        name="gdn_step",
    )(pg, pab, conv_state, s0, conv_w, alog, dtb, norm_w)


def _rwkv_prep(xs, w0, a0, wwa, g2, k_k, k_a, head_ones):
    w = RWKV_WIDTH
    r, k, v = xs[:, 0:w], xs[:, w:2 * w], xs[:, 2 * w:3 * w]
    wa_in = xs[:, 3 * w:3 * w + LORA_WA]
    lane = _iota2(wa_in.shape, 1)
    wa_in = jnp.where(lane < LORA_WA // 2, jnp.tanh(wa_in), wa_in)
    wa = _mm(wa_in, wwa)
    w_log = -_softplus(-(w0 + wa[:, 0:w])) - 0.5
    a = _sigmoid(a0 + wa[:, w:2 * w])
    g = _mm(_sigmoid(xs[:, 3 * w + LORA_WA:RWKV_COLS]), g2)
    kx = k * k_k
    kk = kx * lax.rsqrt(_mm(kx * kx, head_ones) + L2_EPS)
    k = k * (1.0 + (a - 1.0) * k_a)
    return r, k, v, jnp.exp(w_log), kk, kk * a, g


def _rwkv_out(y, r, k, v, g, r_k, ln_w, ln_b, head_ones):
    inv_n = 1.0 / RWKV_N
    mean = _mm(y, head_ones) * inv_n
    yc = y - mean
    var = _mm(yc * yc, head_ones) * inv_n
    y = yc * lax.rsqrt(var + GN_EPS) * ln_w + ln_b
    bonus = _mm(r * k * r_k, head_ones) * v
    return (y + bonus) * g


def _rwkv_prompt_kernel(nb, tb, n_cast, pr_ref, mu_ref, w0_ref, a0_ref, wwa_ref, g2_ref, kk_ref, ka_ref, rk_ref,
                        lnw_ref, lnb_ref, ones_ref, *refs):
    cast_in, (o_ref, s_out_ref), cast_out = refs[:n_cast], refs[n_cast:n_cast + 2], refs[n_cast + 2:2 * n_cast + 2]
    s_scr, ext_scr, y_scr = refs[2 * n_cast + 2:]
    for src, dst in zip(cast_in, cast_out):
        dst[...] = src[...].astype(BF16)
    t = pl.program_id(1)
    c = RWKV_CHUNK
    n = RWKV_N
    pair = 2 * n
    lg_c = c.bit_length() - 1
    lg_n = n.bit_length() - 1
    n_pairs = RWKV_HEADS // 2
    chunks_per_seq = tb // c

    @pl.when(t == 0)
    def _():
        s_scr[...] = jnp.zeros_like(s_scr)
        ext_scr[...] = jnp.zeros_like(ext_scr)

    prevs = []
    for q in range(nb):
        prevs.append(_shift_rows(ext_scr[q], pr_ref[q], 1))
        ext_scr[q] = pr_ref[q, tb - SUBLANES:tb, :]
    p = jnp.concatenate([pr_ref[q] for q in range(nb)], axis=0)
    xs = p + (jnp.concatenate(prevs, axis=0) - p) * mu_ref[...]
    head_ones = ones_ref[...]
    r, k, v, e, kk, kka, g = _rwkv_prep(xs, w0_ref[...], a0_ref[...], wwa_ref[...], g2_ref[...],
                                         kk_ref[...], ka_ref[...], head_ones)

    ri = _iota2((tb, tb), 0)
    ci = _iota2((tb, tb), 1)
    tri = jnp.where((ri >= ci) & ((ri >> lg_c) == (ci >> lg_c)), 1.0, 0.0)
    lg_inc = -jnp.concatenate([_mm_split3(tri, e[q * tb:(q + 1) * tb, :]) for q in range(nb)], axis=0)
    lg_exc = lg_inc + e
    a_t = -kk * jnp.exp(lg_exc)
    r_t = r * jnp.exp(lg_inc)
    inv_g = jnp.exp(-lg_inc)
    b_t = kka * inv_g
    k_t = k * inv_g

    assert c == n, "two heads side by side fill one (chunk, 2 * chunk) score tile"
    stack = _pair_stack(n)

    tt = _iota2((c, pair), 0)
    ss = _iota2((c, pair), 1) & (n - 1)
    strict = tt > ss
    incl = tt >= ss
    bd = jnp.where((_iota2((pair, pair), 0) >> lg_n) == (_iota2((pair, pair), 1) >> lg_n), 1.0, 0.0)

    items = [(ch, hp) for ch in range(nb * chunks_per_seq) for hp in range(n_pairs)]

    def tile(x, ch, hp):
        return x[ch * c:(ch + 1) * c, hp * pair:(hp + 1) * pair]

    ats = [tile(a_t, *it) for it in items]
    rts = [tile(r_t, *it) for it in items]
    vts = [tile(v, *it) for it in items]
    svs = [stack(vt) for vt in vts]
    scs = [_mm(jnp.concatenate([at, rt], axis=0),
               jnp.concatenate([stack(tile(b_t, *it)), stack(tile(k_t, *it))], axis=0), _NT)
           for at, rt, it in zip(ats, rts, items)]
    tinvs = _inv_unit_lower([-jnp.where(strict, sc[0:c, 0:pair], 0.0) for sc in scs], stack)
    akv = [_mm(jnp.where(strict, sc[0:c, pair:2 * pair], 0.0), sv) for sc, sv in zip(scs, svs)]
    y_v = [_mm(jnp.where(incl, sc[c:2 * c, pair:2 * pair], 0.0), sv) for sc, sv in zip(scs, svs)]
    p_rbs = [jnp.where(incl, sc[c:2 * c, 0:pair], 0.0) for sc in scs]
    w_ts = [_mm(tinv, stack(at)) for tinv, at in zip(tinvs, ats)]
    u0s = [_mm(tinv, stack(x)) for tinv, x in zip(tinvs, akv)]
    lg_lasts = [lg_inc[ch * c + c - 1:ch * c + c, hp * pair:(hp + 1) * pair] for ch, hp in items]
    to_ends = [jnp.exp(lg_last - tile(lg_inc, *it)) for lg_last, it in zip(lg_lasts, items)]
    bhs = [tile(kka, *it) * to_end for it, to_end in zip(items, to_ends)]
    khs = [tile(k, *it) * to_end for it, to_end in zip(items, to_ends)]
    r_hats = [rt + _mm(p_rb, stack(w_t)) for rt, p_rb, w_t in zip(rts, p_rbs, w_ts)]
    y0s = [_mm(p_rb, stack(u0)) + yv for p_rb, u0, yv in zip(p_rbs, u0s, y_v)]
    s_mix = [bd * _mm(w_t.T, bh) for w_t, bh in zip(w_ts, bhs)]
    s_add = [bd * _mm(jnp.concatenate([u0, vt], axis=0).T, jnp.concatenate([bh, kh], axis=0))
             for u0, vt, bh, kh in zip(u0s, vts, bhs, khs)]

    chains = [(q, hp) for q in range(nb) for hp in range(n_pairs)]
    states = [s_scr[q, hp] for q, hp in chains]
    for blk in range(chunks_per_seq):
        idx = [(q * chunks_per_seq + blk) * n_pairs + hp for q, hp in chains]
        for (q, hp), i, s in zip(chains, idx, states):
            row0 = (q * chunks_per_seq + blk) * c
            y_scr[row0:row0 + c, hp * pair:(hp + 1) * pair] = y0s[i] + _mm(r_hats[i], s, _NT)
        states = [s * jnp.exp(lg_lasts[i]) + _mm(s, s_mix[i]) + s_add[i] for i, s in zip(idx, states)]
    for (q, hp), s in zip(chains, states):
        s_scr[q, hp] = s

    out = _rwkv_out(y_scr[...], r, k, v, g, rk_ref[...], lnw_ref[...], lnb_ref[...], head_ones)
    for q in range(nb):
        o_ref[q] = out[q * tb:(q + 1) * tb, :].astype(o_ref.dtype)

    @pl.when(t == pl.num_programs(1) - 1)
    def _():
        for q, hp in chains:
            s = s_scr[q, hp]
            s_out_ref[q, 2 * hp] = s[0:n, 0:n]
            s_out_ref[q, 2 * hp + 1] = s[n:pair, n:pair]


def _rwkv_weight_specs(fixed):
    vec = pl.BlockSpec((1, RWKV_WIDTH), fixed)
    return [pl.BlockSpec((1, RWKV_COLS), fixed), vec, vec,
            pl.BlockSpec((LORA_WA, 2 * RWKV_WIDTH), fixed), pl.BlockSpec((G_LORA, RWKV_WIDTH), fixed),
            vec, vec, vec, vec, vec, pl.BlockSpec((RWKV_WIDTH, RWKV_WIDTH), fixed)]


def _cast_chunk_spec(rows, cols, steps, t_blocks):
    chunks = max(c for c in range(1, steps + 1) if rows % c == 0 and (rows // c) % (2 * SUBLANES) == 0)
    return pl.BlockSpec((rows // chunks, cols), lambda i, j: (jnp.minimum(i * t_blocks + j, chunks - 1), 0))


def _rwkv_prompt(pr, weights, to_cast=(), *, nb, tb):
    b, t, _ = pr.shape
    fixed = lambda i, j: (0, 0)
    steps = (b // nb) * (t // tb)
    cast_specs = [_cast_chunk_spec(*w.shape, steps, t // tb) for w in to_cast]
    return pl.pallas_call(
        functools.partial(_rwkv_prompt_kernel, nb, tb, len(to_cast)),
        grid=(b // nb, t // tb),
        in_specs=[pl.BlockSpec((nb, tb, RWKV_COLS), lambda i, j: (i, j, 0))] + _rwkv_weight_specs(fixed)
        + cast_specs,
        out_specs=[pl.BlockSpec((nb, tb, RWKV_WIDTH), lambda i, j: (i, j, 0)),
                   pl.BlockSpec((nb, RWKV_HEADS, RWKV_N, RWKV_N), lambda i, j: (i, 0, 0, 0))] + cast_specs,
        out_shape=[jax.ShapeDtypeStruct((b, t, RWKV_WIDTH), BF16),
                   jax.ShapeDtypeStruct((b, RWKV_HEADS, RWKV_N, RWKV_N), F32)]
        + [jax.ShapeDtypeStruct(w.shape, BF16) for w in to_cast],
        scratch_shapes=[pltpu.VMEM((nb, RWKV_HEADS // 2, 2 * RWKV_N, 2 * RWKV_N), F32),
                        pltpu.VMEM((nb, SUBLANES, RWKV_COLS), F32),
                        pltpu.VMEM((nb * tb, RWKV_WIDTH), F32)],
        compiler_params=_params(("arbitrary", "arbitrary")),
        name="rwkv_prompt",
    )(pr, *weights, *to_cast)


def _rwkv_step_kernel(pr_ref, sh_ref, s_ref, mu_ref, w0_ref, a0_ref, wwa_ref, g2_ref, kk_ref, ka_ref,
                      rk_ref, lnw_ref, lnb_ref, ones_ref, o_ref, s_out_ref, vec_scr, row_scr, y_scr):
    h = pl.program_id(0)
    n = RWKV_N
    batch = pr_ref.shape[0]

    @pl.when(h == 0)
    def _():
        p = pr_ref[...]
        xs = p + (sh_ref[...] - p) * mu_ref[...]
        r, k, v, e, kk, kka, g = _rwkv_prep(xs, w0_ref[...], a0_ref[...], wwa_ref[...], g2_ref[...],
                                             kk_ref[...], ka_ref[...], ones_ref[...])
        for j, x in enumerate((-kk, kka, k, jnp.exp(-e), r, v)):
            vec_scr[j] = x.T.reshape(RWKV_HEADS, n, batch)
        for j, x in enumerate((r, k, v, g)):
            row_scr[j] = x

    nkk, kka, k, decay, r = (vec_scr[j, h] for j in range(5))
    for vi in range(n):
        s = s_ref[vi]
        sa = jnp.sum(s * nkk, axis=0, keepdims=True)
        s = s * decay + sa * kka + vec_scr[5, h, vi:vi + 1, :] * k
        s_out_ref[vi] = s
        y_scr[h, vi:vi + 1, :] = jnp.sum(s * r, axis=0, keepdims=True)

    @pl.when(h == RWKV_HEADS - 1)
    def _():
        y = y_scr[...].reshape(RWKV_WIDTH, batch).T
        o_ref[...] = _rwkv_out(y, row_scr[0], row_scr[1], row_scr[2], row_scr[3], rk_ref[...], lnw_ref[...],
                               lnb_ref[...], ones_ref[...])


def _rwkv_step(pr, shift, s0_t, weights):
    b = pr.shape[0]
    fixed = lambda i: (0, 0)
    state_spec = pl.BlockSpec((None, RWKV_N, RWKV_N, b), lambda i: (i, 0, 0, 0))
    return pl.pallas_call(
        _rwkv_step_kernel,
        grid=(RWKV_HEADS,),
        in_specs=[pl.BlockSpec((b, RWKV_COLS), fixed), pl.BlockSpec((b, RWKV_COLS), fixed), state_spec]
        + _rwkv_weight_specs(fixed),
        out_specs=[pl.BlockSpec((b, RWKV_WIDTH), fixed), state_spec],
        out_shape=[jax.ShapeDtypeStruct((b, RWKV_WIDTH), F32), jax.ShapeDtypeStruct(s0_t.shape, F32)],
        scratch_shapes=[pltpu.VMEM((6, RWKV_HEADS, RWKV_N, b), F32), pltpu.VMEM((4, b, RWKV_WIDTH), F32),
                        pltpu.VMEM((RWKV_HEADS, RWKV_N, b), F32)],
        compiler_params=_params(("arbitrary",)),
        name="rwkv_step",
    )(pr, shift, s0_t, *weights)


def _pad_lanes(x, width):
    return jnp.pad(x, ((0, 0), (0, width - x.shape[1])))


def kernel(x_prompt, x_sample, state_gdn_conv, state_gdn, state_rwkv_shift, state_rwkv, norm_ffn1, w_ffn1_in, w_ffn1_out, norm_mix, w_in, gdn_conv_w, gdn_a_log, gdn_dt_bias, gdn_norm_w, rwkv_mu, rwkv_w0, rwkv_w2, rwkv_a0, rwkv_a2, rwkv_g2, rwkv_k_k, rwkv_k_a, rwkv_r_k, rwkv_ln_w, rwkv_ln_b, w_out, norm_ffn2, w_ffn2_in, w_ffn2_out, norm_final):
    depth = norm_ffn1.shape[0]
    assert depth == 1, "the carried-state plumbing below is written for a single layer"
    b, t, _ = x_prompt.shape
    bs = x_sample.shape[0]
    assert x_sample.shape[1] == 1
    l = 0

    n_gdn_cols = GDN_MAIN + 2 * GDN_HEADS
    w_in_l = w_in[l]
    w_slabs = (w_in_l[:, :GDN_MAIN].astype(BF16), w_in_l[:, n_gdn_cols:].astype(BF16),
               _pad_lanes(w_in_l[:, GDN_MAIN:n_gdn_cols], AB_PAD).astype(BF16))
    w1a, w1b = w_ffn1_in[l].astype(BF16), w_ffn1_out[l].astype(BF16)
    woa, wob = w_out[l, :GDN_WIDTH].astype(BF16), w_out[l, GDN_WIDTH:].astype(BF16)
    half = LORA_WA // 2
    wwa = jnp.zeros((LORA_WA, 2 * RWKV_WIDTH), F32)
    wwa = wwa.at[:half, :RWKV_WIDTH].set(rwkv_w2[l]).at[half:, RWKV_WIDTH:].set(rwkv_a2[l]).astype(BF16)
    head_id = jnp.arange(RWKV_WIDTH) // RWKV_N
    head_ones = (head_id[:, None] == head_id[None, :]).astype(BF16)
    rwkv_w = (rwkv_mu[l][None], rwkv_w0[l][None], rwkv_a0[l][None], wwa, rwkv_g2[l].astype(BF16),
              rwkv_k_k[l][None], rwkv_k_a[l][None], rwkv_r_k[l].reshape(1, RWKV_WIDTH),
              rwkv_ln_w[l][None], rwkv_ln_b[l][None], head_ones)
    alog = _pad_lanes(gdn_a_log[l][None], AB_PAD)
    dtb = _pad_lanes(gdn_dt_bias[l][None], AB_PAD)
    gdn_w = (gdn_conv_w[l], alog, dtb, gdn_norm_w[l][None])
    nf1, nmix, nf2, nfin = norm_ffn1[l][None], norm_mix[l][None], norm_ffn2[l][None], norm_final[None]

    xp, xs = x_prompt.reshape(b * t, D_MODEL), x_sample.reshape(bs, D_MODEL)
    hp, hs = _ffn(xp, xs, nf1, w1a, w1b, tm=512, sub=256)
    pg, pr, pab, sg, sr, sab = _proj_in(hp, hs, nmix, w_slabs, tm=512)

    pg3, pr3, pab3 = pg.reshape(b, t, GDN_MAIN), pr.reshape(b, t, RWKV_COLS), pab.reshape(b, t, AB_PAD)
    oa, gdn_s = _gdn_prompt(pg3, pab3, *gdn_w, nb=2, tb=256)
    ob, wkv_s, w2a, w2b = _rwkv_prompt(pr3, rwkv_w, (w_ffn2_in[l], w_ffn2_out[l]), nb=2, tb=256)

    conv_in = state_gdn_conv[l].reshape(bs, (CONV_W - 1) * GDN_QKV)
    oa_s, conv_s, gdn_ss = _gdn_step(sg, sab, conv_in, state_gdn[l], *gdn_w, bb=8)
    ob_s, wkv_t = _rwkv_step(sr, state_rwkv_shift[l], jnp.transpose(state_rwkv[l], (1, 2, 3, 0)), rwkv_w)
    wkv_ss = jnp.transpose(wkv_t, (3, 0, 1, 2))

    yp, ys = _ffn(hp, hs, nf2, w2a, w2b, final_w=nfin, tm=512, sub=256,
                  mix=(oa.reshape(b * t, GDN_WIDTH), ob.reshape(b * t, RWKV_WIDTH), oa_s, ob_s, woa, wob))

    return (yp.reshape(b, t, D_MODEL), ys.reshape(bs, 1, D_MODEL),
            pg3[:, t - (CONV_W - 1):, :GDN_QKV][None], gdn_s[None], pr3[:, t - 1, :][None], wkv_s[None],
            conv_s.reshape(1, bs, CONV_W - 1, GDN_QKV), gdn_ss[None], sr[None], wkv_ss[None])
```

```python
import functools

import jax
import jax.numpy as jnp
from jax import lax
from jax.experimental import pallas as pl
from jax.experimental.pallas import tpu as pltpu

F32 = jnp.float32
BF16 = jnp.bfloat16

D_MODEL = 1024
D_FF = 2816
CONV_W = 4
GDN_HEADS = 4
GDN_DK = 128
GDN_QK = GDN_HEADS * GDN_DK
GDN_WIDTH = GDN_HEADS * GDN_DK
GDN_QKV = 3 * GDN_WIDTH
GDN_MAIN = GDN_QKV + GDN_WIDTH
RWKV_HEADS = 8
RWKV_N = 64
RWKV_WIDTH = RWKV_HEADS * RWKV_N
LORA_WA = 128
G_LORA = 128
RWKV_COLS = 3 * RWKV_WIDTH + LORA_WA + G_LORA
LANES = 128
SUBLANES = 8
AB_PAD = LANES
NORM_EPS = 1e-6
GN_EPS = 64e-5
L2_EPS = 1e-6

GDN_CHUNK = 128
RWKV_CHUNK = 64
VMEM_LIMIT = 56 * 1024 * 1024


_NN = (((1,), (0,)), ((), ()))
_NT = (((1,), (1,)), ((), ()))


def _mm(a, b, dims=_NN):
    return lax.dot_general(a.astype(BF16), b.astype(BF16), dims, preferred_element_type=F32)


def _mm_split3(sel, x):
    sel = sel.astype(BF16)
    out = None
    for _ in range(3):
        piece = x.astype(BF16)
        part = jnp.dot(sel, piece, preferred_element_type=F32)
        out = part if out is None else out + part
        x = x - piece.astype(F32)
    return out


def _sigmoid(x):
    return 1.0 / (1.0 + jnp.exp(-x))


def _silu(x):
    return x * _sigmoid(x)


def _softplus(x):
    return jnp.maximum(x, 0.0) + jnp.log(1.0 + jnp.exp(-jnp.abs(x)))


def _rms(x, g):
    return x * lax.rsqrt(jnp.mean(x * x, axis=-1, keepdims=True) + NORM_EPS) * g


def _iota2(shape, dim):
    return lax.broadcasted_iota(jnp.int32, shape, dim)


def _shift_rows(prev_tail, x, j):
    rolled = pltpu.roll(jnp.concatenate([prev_tail, x], axis=0), j, axis=0)
    return rolled[SUBLANES:, :]


def _params(sem):
    return pltpu.CompilerParams(dimension_semantics=sem, vmem_limit_bytes=VMEM_LIMIT)


def _resident(shape):
    return pl.BlockSpec(shape, lambda i: (0,) * len(shape), pipeline_mode=pl.Buffered(1))


def _ffn_rows(x_ref, mix_refs, w_refs, out_ref, sub):
    woa_ref, wob_ref, nw_ref, w1_ref, w2_ref, fw_ref = w_refs
    for s in range(x_ref.shape[0] // sub):
        rows = slice(s * sub, (s + 1) * sub)
        x = x_ref[rows, :]
        if mix_refs is not None:
            oa_ref, ob_ref = mix_refs
            x = x + _mm(oa_ref[rows, :], woa_ref[...]) + _mm(ob_ref[rows, :], wob_ref[...])
        xn = _rms(x, nw_ref[...]).astype(BF16)
        gate = jnp.dot(xn, w1_ref[:, 0:D_FF], preferred_element_type=F32)
        up = jnp.dot(xn, w1_ref[:, D_FF:2 * D_FF], preferred_element_type=F32)
        act = (_silu(gate) * up).astype(BF16)
        h = x + 0.5 * jnp.dot(act, w2_ref[...], preferred_element_type=F32)
        if fw_ref is not None:
            h = _rms(h, fw_ref[...])
        out_ref[rows, :] = h


def _ffn_kernel(has_mix, final_norm, sub, *refs):
    it = iter(refs)
    x_ref, xt_ref = next(it), next(it)
    mix_refs = tail_mix_refs = None
    woa_ref = wob_ref = None
    if has_mix:
        mix_refs, tail_mix_refs = (next(it), next(it)), (next(it), next(it))
        woa_ref, wob_ref = next(it), next(it)
    nw_ref, w1_ref, w2_ref = next(it), next(it), next(it)
    fw_ref = next(it) if final_norm else None
    out_ref, out_tail_ref = next(it), next(it)
    w_refs = (woa_ref, wob_ref, nw_ref, w1_ref, w2_ref, fw_ref)
    i = pl.program_id(0)
    last = pl.num_programs(0) - 1

    @pl.when(i < last)
    def _():
        _ffn_rows(x_ref, mix_refs, w_refs, out_ref, sub)

    @pl.when(i == last)
    def _():
        _ffn_rows(xt_ref, tail_mix_refs, w_refs, out_tail_ref, xt_ref.shape[0])


def _ffn(x, x_tail, norm_w, w1, w2, *, mix=None, final_w=None, tm, sub):
    m, mt = x.shape[0], x_tail.shape[0]
    n_tiles = m // tm
    has_mix = mix is not None
    final_norm = final_w is not None
    row = lambda i: (jnp.minimum(i, n_tiles - 1), 0)
    tail = lambda i: (0, 0)
    in_specs = [pl.BlockSpec((tm, D_MODEL), row), pl.BlockSpec((mt, D_MODEL), tail)]
    args = [x, x_tail]
    if has_mix:
        oa, ob, oa_tail, ob_tail, woa, wob = mix
        in_specs += [pl.BlockSpec((tm, GDN_WIDTH), row), pl.BlockSpec((tm, RWKV_WIDTH), row),
                     pl.BlockSpec((mt, GDN_WIDTH), tail), pl.BlockSpec((mt, RWKV_WIDTH), tail),
                     _resident((GDN_WIDTH, D_MODEL)), _resident((RWKV_WIDTH, D_MODEL))]
        args += [oa, ob, oa_tail, ob_tail, woa, wob]
    in_specs += [_resident((1, D_MODEL)), _resident((D_MODEL, 2 * D_FF)), _resident((D_FF, D_MODEL))]
    args += [norm_w, w1, w2]
    if final_norm:
        in_specs.append(_resident((1, D_MODEL)))
        args.append(final_w)
    return pl.pallas_call(
        functools.partial(_ffn_kernel, has_mix, final_norm, sub),
        grid=(n_tiles + 1,),
        in_specs=in_specs,
        out_specs=[pl.BlockSpec((tm, D_MODEL), row), pl.BlockSpec((mt, D_MODEL), tail)],
        out_shape=[jax.ShapeDtypeStruct((m, D_MODEL), F32), jax.ShapeDtypeStruct((mt, D_MODEL), F32)],
        compiler_params=_params(("arbitrary",)),
        name="ffn_mix" if has_mix else "ffn",
    )(*args)


def _l2norm(x):
    return x * lax.rsqrt(jnp.sum(x * x, axis=-1, keepdims=True) + L2_EPS)


def _proj_in_rows(h_ref, nw_ref, w_refs, out_refs):
    n = _rms(h_ref[...], nw_ref[...]).astype(BF16)
    for w_ref, out_ref in zip(w_refs, out_refs):
        out_ref[...] = jnp.dot(n, w_ref[...], preferred_element_type=F32)


def _proj_in_kernel(h_ref, ht_ref, nw_ref, wg_ref, wr_ref, wab_ref, pg_ref, pr_ref, pab_ref,
                    pgt_ref, prt_ref, pabt_ref):
    i = pl.program_id(0)
    last = pl.num_programs(0) - 1
    w_refs = (wg_ref, wr_ref, wab_ref)

    @pl.when(i < last)
    def _():
        _proj_in_rows(h_ref, nw_ref, w_refs, (pg_ref, pr_ref, pab_ref))

    @pl.when(i == last)
    def _():
        _proj_in_rows(ht_ref, nw_ref, w_refs, (pgt_ref, prt_ref, pabt_ref))


def _proj_in(h, h_tail, norm_w, w_slabs, *, tm):
    m, mt = h.shape[0], h_tail.shape[0]
    n_tiles = m // tm
    row = lambda i: (jnp.minimum(i, n_tiles - 1), 0)
    tail = lambda i: (0, 0)
    widths = (GDN_MAIN, RWKV_COLS, AB_PAD)
    return pl.pallas_call(
        _proj_in_kernel,
        grid=(n_tiles + 1,),
        in_specs=[pl.BlockSpec((tm, D_MODEL), row), pl.BlockSpec((mt, D_MODEL), tail), _resident((1, D_MODEL))]
        + [_resident((D_MODEL, w)) for w in widths],
        out_specs=[pl.BlockSpec((tm, w), row) for w in widths] + [pl.BlockSpec((mt, w), tail) for w in widths],
        out_shape=[jax.ShapeDtypeStruct((m, w), F32) for w in widths]
        + [jax.ShapeDtypeStruct((mt, w), F32) for w in widths],
        compiler_params=_params(("arbitrary",)),
        name="proj_in",
    )(h, h_tail, norm_w, *w_slabs)


def _inv_unit_lower(lows, stack=None):
    shape = lows[0].shape
    n = shape[0]
    ri = _iota2(shape, 0)
    ci = _iota2(shape, 1) & (n - 1)
    eye = jnp.where(ri == ci, 1.0, 0.0)
    pair_blk = (ri >> 1) == (ci >> 1)
    ts = [eye - jnp.where(pair_blk, low, 0.0) for low in lows]
    s, lg = 2, 1
    while s < n:
        sel = ((ri >> (lg + 1)) == (ci >> (lg + 1))) & ((ri >> lg) != (ci >> lg))
        offs = [jnp.where(sel, low, 0.0) for low in lows]
        if stack is None:
            xs = [_mm(t, off) for t, off in zip(ts, offs)]
            ts = [t - _mm(x, t) for t, x in zip(ts, xs)]
        else:
            xs = [_mm(t, stack(off)) for t, off in zip(ts, offs)]
            ts = [t - _mm(x, stack(t)) for t, x in zip(ts, xs)]
        s, lg = 2 * s, lg + 1
    return ts


def _pair_stack(m):
    first = _iota2((m, 2 * m), 1) < m

    def stack(z):
        z = z.astype(BF16)
        zero = jnp.zeros_like(z)
        return jnp.concatenate([jnp.where(first, z, zero), jnp.where(first, zero, z)], axis=0)

    return stack


def _inv_unit_lower_halves(lows):
    n = lows[0].shape[0]
    m = n // 2
    first = _iota2((m, n), 1) < m
    stack = _pair_stack(m)
    t_diag = _inv_unit_lower([jnp.where(first, low[0:m, :], low[m:n, :]) for low in lows], stack)
    lower_left = (_iota2((n, n), 0) >= m) & (_iota2((n, n), 1) < m)
    xs = [_mm(t, jnp.where(lower_left, low, 0.0)) for t, low in zip(t_diag, lows)]
    t_ll = [_mm(x, stack(t)) for x, t in zip(xs, t_diag)]
    return [jnp.concatenate([jnp.where(first, t, 0.0), jnp.where(first, 0.0, t) - ll], axis=0)
            for t, ll in zip(t_diag, t_ll)]


def _gdn_gates(ab, alog, dtb):
    log_alpha = -jnp.exp(alog) * _softplus(ab + dtb)
    return log_alpha, _sigmoid(ab)


def _gdn_out(o, z, norm_w):
    o = o * lax.rsqrt(jnp.mean(o * o, axis=-1, keepdims=True) + NORM_EPS) * norm_w
    return o * _silu(z)


def _gdn_prompt_kernel(nb, tb, pg_ref, pab_ref, cw_ref, alog_ref, dtb_ref, nw_ref, o_ref, s_out_ref,
                       s_scr, ext_scr):
    t = pl.program_id(1)
    c = GDN_CHUNK
    chunks_per_seq = tb // c

    @pl.when(t == 0)
    def _():
        s_scr[...] = jnp.zeros_like(s_scr)
        ext_scr[...] = jnp.zeros_like(ext_scr)

    accs = []
    for q in range(nb):
        u = pg_ref[q, :, 0:GDN_QKV]
        acc = u * cw_ref[CONV_W - 1:CONV_W, :]
        for j in range(1, CONV_W):
            acc = acc + _shift_rows(ext_scr[q], u, j) * cw_ref[CONV_W - 1 - j:CONV_W - j, :]
        ext_scr[q] = u[tb - SUBLANES:tb, :]
        accs.append(acc)
    qkv = _silu(jnp.concatenate(accs, axis=0))

    pab = jnp.concatenate([pab_ref[q] for q in range(nb)], axis=0)
    log_alpha, beta_all = _gdn_gates(pab, alog_ref[...], dtb_ref[...])
    ri = _iota2((c, c), 0)
    ci = _iota2((c, c), 1)
    causal = ri >= ci
    strict = ri > ci
    tri = jnp.where(causal, 1.0, 0.0)

    n_chunks = nb * chunks_per_seq
    items = [(ch, h) for ch in range(n_chunks) for h in range(GDN_HEADS)]
    g_blk, gt_blk = [], []
    for ch in range(n_chunks):
        g = _mm_split3(tri, log_alpha[ch * c:(ch + 1) * c, :])
        g_blk.append(g)
        gt_blk.append(g.T)
    qs, ks, vbs, kbs, decays, g_cols = [], [], [], [], [], []
    for ch, h in items:
        rows = slice(ch * c, (ch + 1) * c)
        q = _l2norm(qkv[rows, h * GDN_DK:(h + 1) * GDN_DK]) * (GDN_DK ** -0.5)
        k = _l2norm(qkv[rows, GDN_QK + h * GDN_DK:GDN_QK + (h + 1) * GDN_DK])
        v = qkv[rows, 2 * GDN_QK + h * GDN_DK:2 * GDN_QK + (h + 1) * GDN_DK]
        beta = beta_all[rows, GDN_HEADS + h:GDN_HEADS + h + 1]
        g_col = g_blk[ch][:, h:h + 1]
        g_row = gt_blk[ch][h:h + 1, :]
        decays.append(jnp.exp(jnp.where(causal, g_col - g_row, -jnp.inf)))
        qs.append(q)
        ks.append(k)
        vbs.append(v * beta)
        kbs.append(k * beta)
        g_cols.append(g_col)
    kk = [_mm(kb, k, _NT) for kb, k in zip(kbs, ks)]
    qk = [_mm(q, k, _NT) for q, k in zip(qs, ks)]
    tinvs = _inv_unit_lower_halves([jnp.where(strict, x * d, 0.0) for x, d in zip(kk, decays)])
    attns = [x * d for x, d in zip(qk, decays)]
    egs = [jnp.exp(g_col) for g_col in g_cols]
    us = [_mm(tinv, vb) for tinv, vb in zip(tinvs, vbs)]
    ws = [_mm(tinv, kb * eg) for tinv, kb, eg in zip(tinvs, kbs, egs)]
    g_lasts = [g_col[c - 1:c, :] for g_col in g_cols]
    kd_ts = [(k * jnp.exp(g_last - g_col)).T for k, g_last, g_col in zip(ks, g_lasts, g_cols)]
    q_hats = [q * eg - _mm(attn, w) for q, eg, attn, w in zip(qs, egs, attns, ws)]
    o0s = [_mm(attn, u) for attn, u in zip(attns, us)]
    s_mix = [_mm(kd_t, w) for kd_t, w in zip(kd_ts, ws)]
    s_add = [_mm(kd_t, u) for kd_t, u in zip(kd_ts, us)]

    chains = [(q, h) for q in range(nb) for h in range(GDN_HEADS)]
    states = [s_scr[q, h] for q, h in chains]
    for blk in range(chunks_per_seq):
        idx = [(q * chunks_per_seq + blk) * GDN_HEADS + h for q, h in chains]
        for (q, h), i, s in zip(chains, idx, states):
            o = o0s[i] + _mm(q_hats[i], s)
            rows = slice(blk * c, (blk + 1) * c)
            z = pg_ref[q, rows, GDN_QKV + h * GDN_DK:GDN_QKV + (h + 1) * GDN_DK]
            o_ref[q, rows, h * GDN_DK:(h + 1) * GDN_DK] = _gdn_out(o, z, nw_ref[...]).astype(o_ref.dtype)
        states = [s * jnp.exp(g_lasts[i]) - _mm(s_mix[i], s) + s_add[i] for i, s in zip(idx, states)]
    for (q, h), s in zip(chains, states):
        s_scr[q, h] = s

    @pl.when(t == pl.num_programs(1) - 1)
    def _():
        s_out_ref[...] = s_scr[...]


def _gdn_prompt(pg, pab, conv_w, alog, dtb, norm_w, *, nb, tb):
    b, t, _ = pg.shape
    fixed = lambda i, j: (0, 0)
    return pl.pallas_call(
        functools.partial(_gdn_prompt_kernel, nb, tb),
        grid=(b // nb, t // tb),
        in_specs=[pl.BlockSpec((nb, tb, GDN_MAIN), lambda i, j: (i, j, 0)),
                  pl.BlockSpec((nb, tb, AB_PAD), lambda i, j: (i, j, 0)),
                  pl.BlockSpec((CONV_W, GDN_QKV), fixed), pl.BlockSpec((1, AB_PAD), fixed),
                  pl.BlockSpec((1, AB_PAD), fixed), pl.BlockSpec((1, GDN_DK), fixed)],
        out_specs=[pl.BlockSpec((nb, tb, GDN_WIDTH), lambda i, j: (i, j, 0)),
                   pl.BlockSpec((nb, GDN_HEADS, GDN_DK, GDN_DK), lambda i, j: (i, 0, 0, 0))],
        out_shape=[jax.ShapeDtypeStruct((b, t, GDN_WIDTH), BF16),
                   jax.ShapeDtypeStruct((b, GDN_HEADS, GDN_DK, GDN_DK), F32)],
        scratch_shapes=[pltpu.VMEM((nb, GDN_HEADS, GDN_DK, GDN_DK), F32),
                        pltpu.VMEM((nb, SUBLANES, GDN_QKV), F32)],
        compiler_params=_params(("parallel", "arbitrary")),
        name="gdn_prompt",
    )(pg, pab, conv_w, alog, dtb, norm_w)


def _gdn_step_kernel(bb, pg_ref, pab_ref, cs_ref, s_ref, cw_ref, alog_ref, dtb_ref, nw_ref,
                     o_ref, cs_out_ref, s_out_ref):
    u = pg_ref[:, 0:GDN_QKV]
    acc = u * cw_ref[CONV_W - 1:CONV_W, :]
    for j in range(CONV_W - 1):
        acc = acc + cs_ref[:, j * GDN_QKV:(j + 1) * GDN_QKV] * cw_ref[j:j + 1, :]
    cs_out_ref[:, 0:(CONV_W - 2) * GDN_QKV] = cs_ref[:, GDN_QKV:(CONV_W - 1) * GDN_QKV]
    cs_out_ref[:, (CONV_W - 2) * GDN_QKV:(CONV_W - 1) * GDN_QKV] = u
    qkv = _silu(acc)
    log_alpha, beta_all = _gdn_gates(pab_ref[...], alog_ref[...], dtb_ref[...])
    alpha_all = jnp.exp(log_alpha)
    eye = jnp.where(_iota2((GDN_DK, GDN_DK), 0) == _iota2((GDN_DK, GDN_DK), 1), 1.0, 0.0)

    def to_col(row):
        return jnp.sum(eye * row, axis=-1, keepdims=True)

    vs, k_cols, q_cols = [], [], []
    for h in range(GDN_HEADS):
        q = qkv[:, h * GDN_DK:(h + 1) * GDN_DK]
        k = qkv[:, GDN_QK + h * GDN_DK:GDN_QK + (h + 1) * GDN_DK]
        vs.append(qkv[:, 2 * GDN_QK + h * GDN_DK:2 * GDN_QK + (h + 1) * GDN_DK])
        q = _l2norm(q) * (GDN_DK ** -0.5)
        k = _l2norm(k)
        k_cols.append([to_col(k[i:i + 1, :]) for i in range(bb)])
        q_cols.append([to_col(q[i:i + 1, :]) for i in range(bb)])
    for h in range(GDN_HEADS):
        o_rows = []
        for i in range(bb):
            s = s_ref[i, h] * alpha_all[i:i + 1, h:h + 1]
            mem = jnp.sum(k_cols[h][i] * s, axis=0, keepdims=True)
            delta = (vs[h][i:i + 1, :] - mem) * beta_all[i:i + 1, GDN_HEADS + h:GDN_HEADS + h + 1]
            s = s + k_cols[h][i] * delta
            s_out_ref[i, h] = s
            o_rows.append(jnp.sum(q_cols[h][i] * s, axis=0, keepdims=True))
        o = jnp.concatenate(o_rows, axis=0)
        z = pg_ref[:, GDN_QKV + h * GDN_DK:GDN_QKV + (h + 1) * GDN_DK]
        o_ref[:, h * GDN_DK:(h + 1) * GDN_DK] = _gdn_out(o, z, nw_ref[...])


def _gdn_step(pg, pab, conv_state, s0, conv_w, alog, dtb, norm_w, *, bb):
    b = pg.shape[0]
    row = lambda i: (i, 0)
    fixed = lambda i: (0, 0)
    cs_cols = (CONV_W - 1) * GDN_QKV
    state_spec = pl.BlockSpec((bb, GDN_HEADS, GDN_DK, GDN_DK), lambda i: (i, 0, 0, 0))
    return pl.pallas_call(
        functools.partial(_gdn_step_kernel, bb),
        grid=(b // bb,),
        in_specs=[pl.BlockSpec((bb, GDN_MAIN), row), pl.BlockSpec((bb, AB_PAD), row),
                  pl.BlockSpec((bb, cs_cols), row), state_spec,
                  pl.BlockSpec((CONV_W, GDN_QKV), fixed), pl.BlockSpec((1, AB_PAD), fixed),
                  pl.BlockSpec((1, AB_PAD), fixed), pl.BlockSpec((1, GDN_DK), fixed)],
        out_specs=[pl.BlockSpec((bb, GDN_WIDTH), row), pl.BlockSpec((bb, cs_cols), row), state_spec],
        out_shape=[jax.ShapeDtypeStruct((b, GDN_WIDTH), F32), jax.ShapeDtypeStruct((b, cs_cols), F32),
                   jax.ShapeDtypeStruct(s0.shape, F32)],
        compiler_params=_params(("parallel",)),
        name="gdn_step",
    )(pg, pab, conv_state, s0, conv_w, alog, dtb, norm_w)


def _rwkv_prep(xs, w0, a0, wwa, g2, k_k, k_a, head_ones):
    w = RWKV_WIDTH
    r, k, v = xs[:, 0:w], xs[:, w:2 * w], xs[:, 2 * w:3 * w]
    wa_in = xs[:, 3 * w:3 * w + LORA_WA]
    lane = _iota2(wa_in.shape, 1)
    wa_in = jnp.where(lane < LORA_WA // 2, jnp.tanh(wa_in), wa_in)
    wa = _mm(wa_in, wwa)
    w_log = -_softplus(-(w0 + wa[:, 0:w])) - 0.5
    a = _sigmoid(a0 + wa[:, w:2 * w])
    g = _mm(_sigmoid(xs[:, 3 * w + LORA_WA:RWKV_COLS]), g2)
    kx = k * k_k
    kk = kx * lax.rsqrt(_mm(kx * kx, head_ones) + L2_EPS)
    k = k * (1.0 + (a - 1.0) * k_a)
    return r, k, v, jnp.exp(w_log), kk, kk * a, g


def _rwkv_out(y, r, k, v, g, r_k, ln_w, ln_b, head_ones):
    inv_n = 1.0 / RWKV_N
    mean = _mm(y, head_ones) * inv_n
    yc = y - mean
    var = _mm(yc * yc, head_ones) * inv_n
    y = yc * lax.rsqrt(var + GN_EPS) * ln_w + ln_b
    bonus = _mm(r * k * r_k, head_ones) * v
    return (y + bonus) * g


def _rwkv_prompt_kernel(nb, tb, n_cast, pr_ref, mu_ref, w0_ref, a0_ref, wwa_ref, g2_ref, kk_ref, ka_ref, rk_ref,
                        lnw_ref, lnb_ref, ones_ref, *refs):
    cast_in, (o_ref, s_out_ref), cast_out = refs[:n_cast], refs[n_cast:n_cast + 2], refs[n_cast + 2:2 * n_cast + 2]
    s_scr, ext_scr, y_scr = refs[2 * n_cast + 2:]
    for src, dst in zip(cast_in, cast_out):
        dst[...] = src[...].astype(BF16)
    t = pl.program_id(1)
    c = RWKV_CHUNK
    n = RWKV_N
    pair = 2 * n
    lg_c = c.bit_length() - 1
    lg_n = n.bit_length() - 1
    n_pairs = RWKV_HEADS // 2
    chunks_per_seq = tb // c

    @pl.when(t == 0)
    def _():
        s_scr[...] = jnp.zeros_like(s_scr)
        ext_scr[...] = jnp.zeros_like(ext_scr)

    prevs = []
    for q in range(nb):
        prevs.append(_shift_rows(ext_scr[q], pr_ref[q], 1))
        ext_scr[q] = pr_ref[q, tb - SUBLANES:tb, :]
    p = jnp.concatenate([pr_ref[q] for q in range(nb)], axis=0)
    xs = p + (jnp.concatenate(prevs, axis=0) - p) * mu_ref[...]
    head_ones = ones_ref[...]
    r, k, v, e, kk, kka, g = _rwkv_prep(xs, w0_ref[...], a0_ref[...], wwa_ref[...], g2_ref[...],
                                         kk_ref[...], ka_ref[...], head_ones)

    ri = _iota2((tb, tb), 0)
    ci = _iota2((tb, tb), 1)
    tri = jnp.where((ri >= ci) & ((ri >> lg_c) == (ci >> lg_c)), 1.0, 0.0)
    lg_inc = -jnp.concatenate([_mm_split3(tri, e[q * tb:(q + 1) * tb, :]) for q in range(nb)], axis=0)
    lg_exc = lg_inc + e
    a_t = -kk * jnp.exp(lg_exc)
    r_t = r * jnp.exp(lg_inc)
    inv_g = jnp.exp(-lg_inc)
    b_t = kka * inv_g
    k_t = k * inv_g

    assert c == n, "two heads side by side fill one (chunk, 2 * chunk) score tile"
    stack = _pair_stack(n)

    tt = _iota2((c, pair), 0)
    ss = _iota2((c, pair), 1) & (n - 1)
    strict = tt > ss
    incl = tt >= ss
    bd = jnp.where((_iota2((pair, pair), 0) >> lg_n) == (_iota2((pair, pair), 1) >> lg_n), 1.0, 0.0)

    items = [(ch, hp) for ch in range(nb * chunks_per_seq) for hp in range(n_pairs)]

    def tile(x, ch, hp):
        return x[ch * c:(ch + 1) * c, hp * pair:(hp + 1) * pair]

    ats = [tile(a_t, *it) for it in items]
    rts = [tile(r_t, *it) for it in items]
    vts = [tile(v, *it) for it in items]
    svs = [stack(vt) for vt in vts]
    scs = [_mm(jnp.concatenate([at, rt], axis=0),
               jnp.concatenate([stack(tile(b_t, *it)), stack(tile(k_t, *it))], axis=0), _NT)
           for at, rt, it in zip(ats, rts, items)]
    tinvs = _inv_unit_lower([-jnp.where(strict, sc[0:c, 0:pair], 0.0) for sc in scs], stack)
    akv = [_mm(jnp.where(strict, sc[0:c, pair:2 * pair], 0.0), sv) for sc, sv in zip(scs, svs)]
    y_v = [_mm(jnp.where(incl, sc[c:2 * c, pair:2 * pair], 0.0), sv) for sc, sv in zip(scs, svs)]
    p_rbs = [jnp.where(incl, sc[c:2 * c, 0:pair], 0.0) for sc in scs]
    w_ts = [_mm(tinv, stack(at)) for tinv, at in zip(tinvs, ats)]
    u0s = [_mm(tinv, stack(x)) for tinv, x in zip(tinvs, akv)]
    lg_lasts = [lg_inc[ch * c + c - 1:ch * c + c, hp * pair:(hp + 1) * pair] for ch, hp in items]
    to_ends = [jnp.exp(lg_last - tile(lg_inc, *it)) for lg_last, it in zip(lg_lasts, items)]
    bhs = [tile(kka, *it) * to_end for it, to_end in zip(items, to_ends)]
    khs = [tile(k, *it) * to_end for it, to_end in zip(items, to_ends)]
    r_hats = [rt + _mm(p_rb, stack(w_t)) for rt, p_rb, w_t in zip(rts, p_rbs, w_ts)]
    y0s = [_mm(p_rb, stack(u0)) + yv for p_rb, u0, yv in zip(p_rbs, u0s, y_v)]
    s_mix = [bd * _mm(w_t.T, bh) for w_t, bh in zip(w_ts, bhs)]
    s_add = [bd * _mm(jnp.concatenate([u0, vt], axis=0).T, jnp.concatenate([bh, kh], axis=0))
             for u0, vt, bh, kh in zip(u0s, vts, bhs, khs)]

    chains = [(q, hp) for q in range(nb) for hp in range(n_pairs)]
    states = [s_scr[q, hp] for q, hp in chains]
    for blk in range(chunks_per_seq):
        idx = [(q * chunks_per_seq + blk) * n_pairs + hp for q, hp in chains]
        for (q, hp), i, s in zip(chains, idx, states):
            row0 = (q * chunks_per_seq + blk) * c
            y_scr[row0:row0 + c, hp * pair:(hp + 1) * pair] = y0s[i] + _mm(r_hats[i], s, _NT)
        states = [s * jnp.exp(lg_lasts[i]) + _mm(s, s_mix[i]) + s_add[i] for i, s in zip(idx, states)]
    for (q, hp), s in zip(chains, states):
        s_scr[q, hp] = s

    out = _rwkv_out(y_scr[...], r, k, v, g, rk_ref[...], lnw_ref[...], lnb_ref[...], head_ones)
    for q in range(nb):
        o_ref[q] = out[q * tb:(q + 1) * tb, :].astype(o_ref.dtype)

    @pl.when(t == pl.num_programs(1) - 1)
    def _():
        for q, hp in chains:
            s = s_scr[q, hp]
            s_out_ref[q, 2 * hp] = s[0:n, 0:n]
            s_out_ref[q, 2 * hp + 1] = s[n:pair, n:pair]


def _rwkv_weight_specs(fixed):
    vec = pl.BlockSpec((1, RWKV_WIDTH), fixed)
    return [pl.BlockSpec((1, RWKV_COLS), fixed), vec, vec,
            pl.BlockSpec((LORA_WA, 2 * RWKV_WIDTH), fixed), pl.BlockSpec((G_LORA, RWKV_WIDTH), fixed),
            vec, vec, vec, vec, vec, pl.BlockSpec((RWKV_WIDTH, RWKV_WIDTH), fixed)]


def _cast_chunk_spec(rows, cols, steps, t_blocks):
    chunks = max(c for c in range(1, steps + 1) if rows % c == 0 and (rows // c) % (2 * SUBLANES) == 0)
    return pl.BlockSpec((rows // chunks, cols), lambda i, j: (jnp.minimum(i * t_blocks + j, chunks - 1), 0))


def _rwkv_prompt(pr, weights, to_cast=(), *, nb, tb):
    b, t, _ = pr.shape
    fixed = lambda i, j: (0, 0)
    steps = (b // nb) * (t // tb)
    cast_specs = [_cast_chunk_spec(*w.shape, steps, t // tb) for w in to_cast]
    return pl.pallas_call(
        functools.partial(_rwkv_prompt_kernel, nb, tb, len(to_cast)),
        grid=(b // nb, t // tb),
        in_specs=[pl.BlockSpec((nb, tb, RWKV_COLS), lambda i, j: (i, j, 0))] + _rwkv_weight_specs(fixed)
        + cast_specs,
        out_specs=[pl.BlockSpec((nb, tb, RWKV_WIDTH), lambda i, j: (i, j, 0)),
                   pl.BlockSpec((nb, RWKV_HEADS, RWKV_N, RWKV_N), lambda i, j: (i, 0, 0, 0))] + cast_specs,
        out_shape=[jax.ShapeDtypeStruct((b, t, RWKV_WIDTH), BF16),
                   jax.ShapeDtypeStruct((b, RWKV_HEADS, RWKV_N, RWKV_N), F32)]
        + [jax.ShapeDtypeStruct(w.shape, BF16) for w in to_cast],
        scratch_shapes=[pltpu.VMEM((nb, RWKV_HEADS // 2, 2 * RWKV_N, 2 * RWKV_N), F32),
                        pltpu.VMEM((nb, SUBLANES, RWKV_COLS), F32),
                        pltpu.VMEM((nb * tb, RWKV_WIDTH), F32)],
        compiler_params=_params(("arbitrary", "arbitrary")),
        name="rwkv_prompt",
    )(pr, *weights, *to_cast)


def _rwkv_step_kernel(pr_ref, sh_ref, s_ref, mu_ref, w0_ref, a0_ref, wwa_ref, g2_ref, kk_ref, ka_ref,
                      rk_ref, lnw_ref, lnb_ref, ones_ref, o_ref, s_out_ref, vec_scr, row_scr, y_scr):
    h = pl.program_id(0)
    n = RWKV_N
    batch = pr_ref.shape[0]

    @pl.when(h == 0)
    def _():
        p = pr_ref[...]
        xs = p + (sh_ref[...] - p) * mu_ref[...]
        r, k, v, e, kk, kka, g = _rwkv_prep(xs, w0_ref[...], a0_ref[...], wwa_ref[...], g2_ref[...],
                                             kk_ref[...], ka_ref[...], ones_ref[...])
        for j, x in enumerate((-kk, kka, k, jnp.exp(-e), r, v)):
            vec_scr[j] = x.T.reshape(RWKV_HEADS, n, batch)
        for j, x in enumerate((r, k, v, g)):
            row_scr[j] = x

    nkk, kka, k, decay, r = (vec_scr[j, h] for j in range(5))
    for vi in range(n):
        s = s_ref[vi]
        sa = jnp.sum(s * nkk, axis=0, keepdims=True)
        s = s * decay + sa * kka + vec_scr[5, h, vi:vi + 1, :] * k
        s_out_ref[vi] = s
        y_scr[h, vi:vi + 1, :] = jnp.sum(s * r, axis=0, keepdims=True)

    @pl.when(h == RWKV_HEADS - 1)
    def _():
        y = y_scr[...].reshape(RWKV_WIDTH, batch).T
        o_ref[...] = _rwkv_out(y, row_scr[0], row_scr[1], row_scr[2], row_scr[3], rk_ref[...], lnw_ref[...],
                               lnb_ref[...], ones_ref[...])


def _rwkv_step(pr, shift, s0_t, weights):
    b = pr.shape[0]
    fixed = lambda i: (0, 0)
    state_spec = pl.BlockSpec((None, RWKV_N, RWKV_N, b), lambda i: (i, 0, 0, 0))
    return pl.pallas_call(
        _rwkv_step_kernel,
        grid=(RWKV_HEADS,),
        in_specs=[pl.BlockSpec((b, RWKV_COLS), fixed), pl.BlockSpec((b, RWKV_COLS), fixed), state_spec]
        + _rwkv_weight_specs(fixed),
        out_specs=[pl.BlockSpec((b, RWKV_WIDTH), fixed), state_spec],
        out_shape=[jax.ShapeDtypeStruct((b, RWKV_WIDTH), F32), jax.ShapeDtypeStruct(s0_t.shape, F32)],
        scratch_shapes=[pltpu.VMEM((6, RWKV_HEADS, RWKV_N, b), F32), pltpu.VMEM((4, b, RWKV_WIDTH), F32),
                        pltpu.VMEM((RWKV_HEADS, RWKV_N, b), F32)],
        compiler_params=_params(("arbitrary",)),
        name="rwkv_step",
    )(pr, shift, s0_t, *weights)


def _pad_lanes(x, width):
    return jnp.pad(x, ((0, 0), (0, width - x.shape[1])))


def kernel(x_prompt, x_sample, state_gdn_conv, state_gdn, state_rwkv_shift, state_rwkv, norm_ffn1, w_ffn1_in, w_ffn1_out, norm_mix, w_in, gdn_conv_w, gdn_a_log, gdn_dt_bias, gdn_norm_w, rwkv_mu, rwkv_w0, rwkv_w2, rwkv_a0, rwkv_a2, rwkv_g2, rwkv_k_k, rwkv_k_a, rwkv_r_k, rwkv_ln_w, rwkv_ln_b, w_out, norm_ffn2, w_ffn2_in, w_ffn2_out, norm_final):
    depth = norm_ffn1.shape[0]
    assert depth == 1, "the carried-state plumbing below is written for a single layer"
    b, t, _ = x_prompt.shape
    bs = x_sample.shape[0]
    assert x_sample.shape[1] == 1
    l = 0

    n_gdn_cols = GDN_MAIN + 2 * GDN_HEADS
    w_in_l = w_in[l]
    w_slabs = (w_in_l[:, :GDN_MAIN].astype(BF16), w_in_l[:, n_gdn_cols:].astype(BF16),
               _pad_lanes(w_in_l[:, GDN_MAIN:n_gdn_cols], AB_PAD).astype(BF16))
    w1a, w1b = w_ffn1_in[l].astype(BF16), w_ffn1_out[l].astype(BF16)
    woa, wob = w_out[l, :GDN_WIDTH].astype(BF16), w_out[l, GDN_WIDTH:].astype(BF16)
    half = LORA_WA // 2
    wwa = jnp.zeros((LORA_WA, 2 * RWKV_WIDTH), F32)
    wwa = wwa.at[:half, :RWKV_WIDTH].set(rwkv_w2[l]).at[half:, RWKV_WIDTH:].set(rwkv_a2[l]).astype(BF16)
    head_id = jnp.arange(RWKV_WIDTH) // RWKV_N
    head_ones = (head_id[:, None] == head_id[None, :]).astype(BF16)
    rwkv_w = (rwkv_mu[l][None], rwkv_w0[l][None], rwkv_a0[l][None], wwa, rwkv_g2[l].astype(BF16),
              rwkv_k_k[l][None], rwkv_k_a[l][None], rwkv_r_k[l].reshape(1, RWKV_WIDTH),
              rwkv_ln_w[l][None], rwkv_ln_b[l][None], head_ones)
    alog = _pad_lanes(gdn_a_log[l][None], AB_PAD)
    dtb = _pad_lanes(gdn_dt_bias[l][None], AB_PAD)
    gdn_w = (gdn_conv_w[l], alog, dtb, gdn_norm_w[l][None])
    nf1, nmix, nf2, nfin = norm_ffn1[l][None], norm_mix[l][None], norm_ffn2[l][None], norm_final[None]

    xp, xs = x_prompt.reshape(b * t, D_MODEL), x_sample.reshape(bs, D_MODEL)
    hp, hs = _ffn(xp, xs, nf1, w1a, w1b, tm=1024, sub=256)
    pg, pr, pab, sg, sr, sab = _proj_in(hp, hs, nmix, w_slabs, tm=1024)

    pg3, pr3, pab3 = pg.reshape(b, t, GDN_MAIN), pr.reshape(b, t, RWKV_COLS), pab.reshape(b, t, AB_PAD)
    oa, gdn_s = _gdn_prompt(pg3, pab3, *gdn_w, nb=2, tb=256)
    ob, wkv_s, w2a, w2b = _rwkv_prompt(pr3, rwkv_w, (w_ffn2_in[l], w_ffn2_out[l]), nb=2, tb=256)

    conv_in = state_gdn_conv[l].reshape(bs, (CONV_W - 1) * GDN_QKV)
    oa_s, conv_s, gdn_ss = _gdn_step(sg, sab, conv_in, state_gdn[l], *gdn_w, bb=8)
    ob_s, wkv_t = _rwkv_step(sr, state_rwkv_shift[l], jnp.transpose(state_rwkv[l], (1, 2, 3, 0)), rwkv_w)
    wkv_ss = jnp.transpose(wkv_t, (3, 0, 1, 2))

    yp, ys = _ffn(hp, hs, nf2, w2a, w2b, final_w=nfin, tm=1024, sub=256,
                  mix=(oa.reshape(b * t, GDN_WIDTH), ob.reshape(b * t, RWKV_WIDTH), oa_s, ob_s, woa, wob))

    return (yp.reshape(b, t, D_MODEL), ys.reshape(bs, 1, D_MODEL),
            pg3[:, t - (CONV_W - 1):, :GDN_QKV][None], gdn_s[None], pr3[:, t - 1, :][None], wkv_s[None],
            conv_s.reshape(1, bs, CONV_W - 1, GDN_QKV), gdn_ss[None], sr[None], wkv_ss[None])
```

```python
import functools
import math

import jax
import jax.numpy as jnp
from jax import lax
from jax.experimental import pallas as pl
from jax.experimental.pallas import tpu as pltpu

F32 = jnp.float32
BF16 = jnp.bfloat16

D_MODEL = 1024
D_FF = 2816
CONV_W = 4
GDN_HEADS = 4
GDN_DK = 128
GDN_QK = GDN_HEADS * GDN_DK
GDN_WIDTH = GDN_HEADS * GDN_DK
GDN_QKV = 3 * GDN_WIDTH
GDN_MAIN = GDN_QKV + GDN_WIDTH
RWKV_HEADS = 8
RWKV_N = 64
RWKV_WIDTH = RWKV_HEADS * RWKV_N
LORA_WA = 128
G_LORA = 128
RWKV_COLS = 3 * RWKV_WIDTH + LORA_WA + G_LORA
LANES = 128
SUBLANES = 8
AB_PAD = LANES
NORM_EPS = 1e-6
GN_EPS = 64e-5
L2_EPS = 1e-6

GDN_CHUNK = 128
RWKV_CHUNK = 64
VMEM_LIMIT = 56 * 1024 * 1024
DENSE_TILE_ROWS = 1024
DENSE_SUB_ROWS = 256
MIXER_SEQS = 2
MIXER_BLOCK_ROWS = 256
GDN_STEP_ROWS = 8


_NN = (((1,), (0,)), ((), ()))
_NT = (((1,), (1,)), ((), ()))


def _mm(a, b, dims=_NN):
    return lax.dot_general(a.astype(BF16), b.astype(BF16), dims, preferred_element_type=F32)


def _mm_split3(sel, x):
    sel = sel.astype(BF16)
    out = None
    for _ in range(3):
        piece = x.astype(BF16)
        part = jnp.dot(sel, piece, preferred_element_type=F32)
        out = part if out is None else out + part
        x = x - piece.astype(F32)
    return out


def _sigmoid(x):
    return 1.0 / (1.0 + jnp.exp(-x))


def _silu(x):
    return x * _sigmoid(x)


def _softplus(x):
    return jnp.maximum(x, 0.0) + jnp.log(1.0 + jnp.exp(-jnp.abs(x)))


def _rms(x, g):
    return x * lax.rsqrt(jnp.mean(x * x, axis=-1, keepdims=True) + NORM_EPS) * g


def _iota2(shape, dim):
    return lax.broadcasted_iota(jnp.int32, shape, dim)


def _shift_rows(prev_tail, x, j):
    rolled = pltpu.roll(jnp.concatenate([prev_tail, x], axis=0), j, axis=0)
    return rolled[SUBLANES:, :]


def _params(sem):
    return pltpu.CompilerParams(dimension_semantics=sem, vmem_limit_bytes=VMEM_LIMIT)


def _resident(shape):
    return pl.BlockSpec(shape, lambda i: (0,) * len(shape), pipeline_mode=pl.Buffered(1))


def _ffn_rows(x_ref, mix_refs, w_refs, out_ref, sub):
    woa_ref, wob_ref, nw_ref, w1_ref, w2_ref, fw_ref = w_refs
    for s in range(x_ref.shape[0] // sub):
        rows = slice(s * sub, (s + 1) * sub)
        x = x_ref[rows, :]
        if mix_refs is not None:
            oa_ref, ob_ref = mix_refs
            x = x + _mm(oa_ref[rows, :], woa_ref[...]) + _mm(ob_ref[rows, :], wob_ref[...])
        xn = _rms(x, nw_ref[...]).astype(BF16)
        gate = jnp.dot(xn, w1_ref[:, 0:D_FF], preferred_element_type=F32)
        up = jnp.dot(xn, w1_ref[:, D_FF:2 * D_FF], preferred_element_type=F32)
        act = (_silu(gate) * up).astype(BF16)
        h = x + 0.5 * jnp.dot(act, w2_ref[...], preferred_element_type=F32)
        if fw_ref is not None:
            h = _rms(h, fw_ref[...])
        out_ref[rows, :] = h


def _ffn_kernel(has_mix, final_norm, sub, *refs):
    it = iter(refs)
    x_ref, xt_ref = next(it), next(it)
    mix_refs = tail_mix_refs = None
    woa_ref = wob_ref = None
    if has_mix:
        mix_refs, tail_mix_refs = (next(it), next(it)), (next(it), next(it))
        woa_ref, wob_ref = next(it), next(it)
    nw_ref, w1_ref, w2_ref = next(it), next(it), next(it)
    fw_ref = next(it) if final_norm else None
    out_ref, out_tail_ref = next(it), next(it)
    w_refs = (woa_ref, wob_ref, nw_ref, w1_ref, w2_ref, fw_ref)
    i = pl.program_id(0)
    last = pl.num_programs(0) - 1

    @pl.when(i < last)
    def _():
        _ffn_rows(x_ref, mix_refs, w_refs, out_ref, sub)

    @pl.when(i == last)
    def _():
        _ffn_rows(xt_ref, tail_mix_refs, w_refs, out_tail_ref, xt_ref.shape[0])


def _ffn(x, x_tail, norm_w, w1, w2, *, mix=None, final_w=None, tm, sub):
    m, mt = x.shape[0], x_tail.shape[0]
    n_tiles = m // tm
    has_mix = mix is not None
    final_norm = final_w is not None
    row = lambda i: (jnp.minimum(i, n_tiles - 1), 0)
    tail = lambda i: (0, 0)
    in_specs = [pl.BlockSpec((tm, D_MODEL), row), pl.BlockSpec((mt, D_MODEL), tail)]
    args = [x, x_tail]
    if has_mix:
        oa, ob, oa_tail, ob_tail, woa, wob = mix
        in_specs += [pl.BlockSpec((tm, GDN_WIDTH), row), pl.BlockSpec((tm, RWKV_WIDTH), row),
                     pl.BlockSpec((mt, GDN_WIDTH), tail), pl.BlockSpec((mt, RWKV_WIDTH), tail),
                     _resident((GDN_WIDTH, D_MODEL)), _resident((RWKV_WIDTH, D_MODEL))]
        args += [oa, ob, oa_tail, ob_tail, woa, wob]
    in_specs += [_resident((1, D_MODEL)), _resident((D_MODEL, 2 * D_FF)), _resident((D_FF, D_MODEL))]
    args += [norm_w, w1, w2]
    if final_norm:
        in_specs.append(_resident((1, D_MODEL)))
        args.append(final_w)
    return pl.pallas_call(
        functools.partial(_ffn_kernel, has_mix, final_norm, sub),
        grid=(n_tiles + 1,),
        in_specs=in_specs,
        out_specs=[pl.BlockSpec((tm, D_MODEL), row), pl.BlockSpec((mt, D_MODEL), tail)],
        out_shape=[jax.ShapeDtypeStruct((m, D_MODEL), F32), jax.ShapeDtypeStruct((mt, D_MODEL), F32)],
        compiler_params=_params(("arbitrary",)),
        name="ffn_mix" if has_mix else "ffn",
    )(*args)


def _l2norm(x):
    return x * lax.rsqrt(jnp.sum(x * x, axis=-1, keepdims=True) + L2_EPS)


def _proj_in_rows(h_ref, nw_ref, w_refs, out_refs):
    n = _rms(h_ref[...], nw_ref[...]).astype(BF16)
    for w_ref, out_ref in zip(w_refs, out_refs):
        out_ref[...] = jnp.dot(n, w_ref[...], preferred_element_type=F32)


def _proj_in_kernel(h_ref, ht_ref, nw_ref, wg_ref, wr_ref, wab_ref, pg_ref, pr_ref, pab_ref,
                    pgt_ref, prt_ref, pabt_ref):
    i = pl.program_id(0)
    last = pl.num_programs(0) - 1
    w_refs = (wg_ref, wr_ref, wab_ref)

    @pl.when(i < last)
    def _():
        _proj_in_rows(h_ref, nw_ref, w_refs, (pg_ref, pr_ref, pab_ref))

    @pl.when(i == last)
    def _():
        _proj_in_rows(ht_ref, nw_ref, w_refs, (pgt_ref, prt_ref, pabt_ref))


def _proj_in(h, h_tail, norm_w, w_slabs, *, tm):
    m, mt = h.shape[0], h_tail.shape[0]
    n_tiles = m // tm
    row = lambda i: (jnp.minimum(i, n_tiles - 1), 0)
    tail = lambda i: (0, 0)
    widths = (GDN_MAIN, RWKV_COLS, AB_PAD)
    return pl.pallas_call(
        _proj_in_kernel,
        grid=(n_tiles + 1,),
        in_specs=[pl.BlockSpec((tm, D_MODEL), row), pl.BlockSpec((mt, D_MODEL), tail), _resident((1, D_MODEL))]
        + [_resident((D_MODEL, w)) for w in widths],
        out_specs=[pl.BlockSpec((tm, w), row) for w in widths] + [pl.BlockSpec((mt, w), tail) for w in widths],
        out_shape=[jax.ShapeDtypeStruct((m, w), F32) for w in widths]
        + [jax.ShapeDtypeStruct((mt, w), F32) for w in widths],
        compiler_params=_params(("arbitrary",)),
        name="proj_in",
    )(h, h_tail, norm_w, *w_slabs)


def _inv_unit_lower(lows, stack=None):
    shape = lows[0].shape
    n = shape[0]
    ri = _iota2(shape, 0)
    ci = _iota2(shape, 1) & (n - 1)
    eye = jnp.where(ri == ci, 1.0, 0.0)
    pair_blk = (ri >> 1) == (ci >> 1)
    ts = [eye - jnp.where(pair_blk, low, 0.0) for low in lows]
    s, lg = 2, 1
    while s < n:
        sel = ((ri >> (lg + 1)) == (ci >> (lg + 1))) & ((ri >> lg) != (ci >> lg))
        offs = [jnp.where(sel, low, 0.0) for low in lows]
        if stack is None:
            xs = [_mm(t, off) for t, off in zip(ts, offs)]
            ts = [t - _mm(x, t) for t, x in zip(ts, xs)]
        else:
            xs = [_mm(t, stack(off)) for t, off in zip(ts, offs)]
            ts = [t - _mm(x, stack(t)) for t, x in zip(ts, xs)]
        s, lg = 2 * s, lg + 1
    return ts


def _pair_stack(m):
    first = _iota2((m, 2 * m), 1) < m

    def stack(z):
        z = z.astype(BF16)
        zero = jnp.zeros_like(z)
        return jnp.concatenate([jnp.where(first, z, zero), jnp.where(first, zero, z)], axis=0)

    return stack


def _inv_unit_lower_halves(lows):
    n = lows[0].shape[0]
    m = n // 2
    first = _iota2((m, n), 1) < m
    stack = _pair_stack(m)
    t_diag = _inv_unit_lower([jnp.where(first, low[0:m, :], low[m:n, :]) for low in lows], stack)
    lower_left = (_iota2((n, n), 0) >= m) & (_iota2((n, n), 1) < m)
    xs = [_mm(t, jnp.where(lower_left, low, 0.0)) for t, low in zip(t_diag, lows)]
    t_ll = [_mm(x, stack(t)) for x, t in zip(xs, t_diag)]
    return [jnp.concatenate([jnp.where(first, t, 0.0), jnp.where(first, 0.0, t) - ll], axis=0)
            for t, ll in zip(t_diag, t_ll)]


def _gdn_gates(ab, alog, dtb):
    log_alpha = -jnp.exp(alog) * _softplus(ab + dtb)
    return log_alpha, _sigmoid(ab)


def _gdn_out(o, z, norm_w):
    o = o * lax.rsqrt(jnp.mean(o * o, axis=-1, keepdims=True) + NORM_EPS) * norm_w
    return o * _silu(z)


def _gdn_prompt_kernel(nb, tb, pg_ref, pab_ref, cw_ref, alog_ref, dtb_ref, nw_ref, o_ref, s_out_ref,
                       s_scr, ext_scr):
    t = pl.program_id(1)
    c = GDN_CHUNK
    chunks_per_seq = tb // c

    @pl.when(t == 0)
    def _():
        s_scr[...] = jnp.zeros_like(s_scr)
        ext_scr[...] = jnp.zeros_like(ext_scr)

    accs = []
    for q in range(nb):
        u = pg_ref[q, :, 0:GDN_QKV]
        acc = u * cw_ref[CONV_W - 1:CONV_W, :]
        for j in range(1, CONV_W):
            acc = acc + _shift_rows(ext_scr[q], u, j) * cw_ref[CONV_W - 1 - j:CONV_W - j, :]
        ext_scr[q] = u[tb - SUBLANES:tb, :]
        accs.append(acc)
    qkv = _silu(jnp.concatenate(accs, axis=0))

    pab = jnp.concatenate([pab_ref[q] for q in range(nb)], axis=0)
    log_alpha, beta_all = _gdn_gates(pab, alog_ref[...], dtb_ref[...])
    ri = _iota2((c, c), 0)
    ci = _iota2((c, c), 1)
    causal = ri >= ci
    strict = ri > ci
    tri = jnp.where(causal, 1.0, 0.0)

    n_chunks = nb * chunks_per_seq
    items = [(ch, h) for ch in range(n_chunks) for h in range(GDN_HEADS)]
    g_blk, gt_blk = [], []
    for ch in range(n_chunks):
        g = _mm_split3(tri, log_alpha[ch * c:(ch + 1) * c, :])
        g_blk.append(g)
        gt_blk.append(g.T)
    qs, ks, vbs, kbs, decays, g_cols = [], [], [], [], [], []
    for ch, h in items:
        rows = slice(ch * c, (ch + 1) * c)
        q = _l2norm(qkv[rows, h * GDN_DK:(h + 1) * GDN_DK]) * (GDN_DK ** -0.5)
        k = _l2norm(qkv[rows, GDN_QK + h * GDN_DK:GDN_QK + (h + 1) * GDN_DK])
        v = qkv[rows, 2 * GDN_QK + h * GDN_DK:2 * GDN_QK + (h + 1) * GDN_DK]
        beta = beta_all[rows, GDN_HEADS + h:GDN_HEADS + h + 1]
        g_col = g_blk[ch][:, h:h + 1]
        g_row = gt_blk[ch][h:h + 1, :]
        decays.append(jnp.exp(jnp.where(causal, g_col - g_row, -jnp.inf)))
        qs.append(q)
        ks.append(k)
        vbs.append(v * beta)
        kbs.append(k * beta)
        g_cols.append(g_col)
    kk = [_mm(kb, k, _NT) for kb, k in zip(kbs, ks)]
    qk = [_mm(q, k, _NT) for q, k in zip(qs, ks)]
    tinvs = _inv_unit_lower_halves([jnp.where(strict, x * d, 0.0) for x, d in zip(kk, decays)])
    attns = [x * d for x, d in zip(qk, decays)]
    egs = [jnp.exp(g_col) for g_col in g_cols]
    us = [_mm(tinv, vb) for tinv, vb in zip(tinvs, vbs)]
    ws = [_mm(tinv, kb * eg) for tinv, kb, eg in zip(tinvs, kbs, egs)]
    g_lasts = [g_col[c - 1:c, :] for g_col in g_cols]
    kd_ts = [(k * jnp.exp(g_last - g_col)).T for k, g_last, g_col in zip(ks, g_lasts, g_cols)]
    q_hats = [q * eg - _mm(attn, w) for q, eg, attn, w in zip(qs, egs, attns, ws)]
    o0s = [_mm(attn, u) for attn, u in zip(attns, us)]
    s_mix = [_mm(kd_t, w) for kd_t, w in zip(kd_ts, ws)]
    s_add = [_mm(kd_t, u) for kd_t, u in zip(kd_ts, us)]

    chains = [(q, h) for q in range(nb) for h in range(GDN_HEADS)]
    states = [s_scr[q, h] for q, h in chains]
    for blk in range(chunks_per_seq):
        idx = [(q * chunks_per_seq + blk) * GDN_HEADS + h for q, h in chains]
        for (q, h), i, s in zip(chains, idx, states):
            o = o0s[i] + _mm(q_hats[i], s)
            rows = slice(blk * c, (blk + 1) * c)
            z = pg_ref[q, rows, GDN_QKV + h * GDN_DK:GDN_QKV + (h + 1) * GDN_DK]
            o_ref[q, rows, h * GDN_DK:(h + 1) * GDN_DK] = _gdn_out(o, z, nw_ref[...]).astype(o_ref.dtype)
        states = [s * jnp.exp(g_lasts[i]) - _mm(s_mix[i], s) + s_add[i] for i, s in zip(idx, states)]
    for (q, h), s in zip(chains, states):
        s_scr[q, h] = s

    @pl.when(t == pl.num_programs(1) - 1)
    def _():
        s_out_ref[...] = s_scr[...]


def _gdn_prompt(pg, pab, conv_w, alog, dtb, norm_w, *, nb, tb):
    b, t, _ = pg.shape
    fixed = lambda i, j: (0, 0)
    return pl.pallas_call(
        functools.partial(_gdn_prompt_kernel, nb, tb),
        grid=(b // nb, t // tb),
        in_specs=[pl.BlockSpec((nb, tb, GDN_MAIN), lambda i, j: (i, j, 0)),
                  pl.BlockSpec((nb, tb, AB_PAD), lambda i, j: (i, j, 0)),
                  pl.BlockSpec((CONV_W, GDN_QKV), fixed), pl.BlockSpec((1, AB_PAD), fixed),
                  pl.BlockSpec((1, AB_PAD), fixed), pl.BlockSpec((1, GDN_DK), fixed)],
        out_specs=[pl.BlockSpec((nb, tb, GDN_WIDTH), lambda i, j: (i, j, 0)),
                   pl.BlockSpec((nb, GDN_HEADS, GDN_DK, GDN_DK), lambda i, j: (i, 0, 0, 0))],
        out_shape=[jax.ShapeDtypeStruct((b, t, GDN_WIDTH), BF16),
                   jax.ShapeDtypeStruct((b, GDN_HEADS, GDN_DK, GDN_DK), F32)],
        scratch_shapes=[pltpu.VMEM((nb, GDN_HEADS, GDN_DK, GDN_DK), F32),
                        pltpu.VMEM((nb, SUBLANES, GDN_QKV), F32)],
        compiler_params=_params(("parallel", "arbitrary")),
        name="gdn_prompt",
    )(pg, pab, conv_w, alog, dtb, norm_w)


def _gdn_step_kernel(bb, pg_ref, pab_ref, cs_ref, s_ref, cw_ref, alog_ref, dtb_ref, nw_ref,
                     o_ref, cs_out_ref, s_out_ref):
    u = pg_ref[:, 0:GDN_QKV]
    acc = u * cw_ref[CONV_W - 1:CONV_W, :]
    for j in range(CONV_W - 1):
        acc = acc + cs_ref[:, j * GDN_QKV:(j + 1) * GDN_QKV] * cw_ref[j:j + 1, :]
    cs_out_ref[:, 0:(CONV_W - 2) * GDN_QKV] = cs_ref[:, GDN_QKV:(CONV_W - 1) * GDN_QKV]
    cs_out_ref[:, (CONV_W - 2) * GDN_QKV:(CONV_W - 1) * GDN_QKV] = u
    qkv = _silu(acc)
    log_alpha, beta_all = _gdn_gates(pab_ref[...], alog_ref[...], dtb_ref[...])
    alpha_all = jnp.exp(log_alpha)
    eye = jnp.where(_iota2((GDN_DK, GDN_DK), 0) == _iota2((GDN_DK, GDN_DK), 1), 1.0, 0.0)

    def to_col(row):
        return jnp.sum(eye * row, axis=-1, keepdims=True)

    vs, k_cols, q_cols = [], [], []
    for h in range(GDN_HEADS):
        q = qkv[:, h * GDN_DK:(h + 1) * GDN_DK]
        k = qkv[:, GDN_QK + h * GDN_DK:GDN_QK + (h + 1) * GDN_DK]
        vs.append(qkv[:, 2 * GDN_QK + h * GDN_DK:2 * GDN_QK + (h + 1) * GDN_DK])
        q = _l2norm(q) * (GDN_DK ** -0.5)
        k = _l2norm(k)
        k_cols.append([to_col(k[i:i + 1, :]) for i in range(bb)])
        q_cols.append([to_col(q[i:i + 1, :]) for i in range(bb)])
    for h in range(GDN_HEADS):
        o_rows = []
        for i in range(bb):
            s = s_ref[i, h] * alpha_all[i:i + 1, h:h + 1]
            mem = jnp.sum(k_cols[h][i] * s, axis=0, keepdims=True)
            delta = (vs[h][i:i + 1, :] - mem) * beta_all[i:i + 1, GDN_HEADS + h:GDN_HEADS + h + 1]
            s = s + k_cols[h][i] * delta
            s_out_ref[i, h] = s
            o_rows.append(jnp.sum(q_cols[h][i] * s, axis=0, keepdims=True))
        o = jnp.concatenate(o_rows, axis=0)
        z = pg_ref[:, GDN_QKV + h * GDN_DK:GDN_QKV + (h + 1) * GDN_DK]
        o_ref[:, h * GDN_DK:(h + 1) * GDN_DK] = _gdn_out(o, z, nw_ref[...])


def _gdn_step(pg, pab, conv_state, s0, conv_w, alog, dtb, norm_w, *, bb):
    b = pg.shape[0]
    row = lambda i: (i, 0)
    fixed = lambda i: (0, 0)
    cs_cols = (CONV_W - 1) * GDN_QKV
    state_spec = pl.BlockSpec((bb, GDN_HEADS, GDN_DK, GDN_DK), lambda i: (i, 0, 0, 0))
    return pl.pallas_call(
        functools.partial(_gdn_step_kernel, bb),
        grid=(b // bb,),
        in_specs=[pl.BlockSpec((bb, GDN_MAIN), row), pl.BlockSpec((bb, AB_PAD), row),
                  pl.BlockSpec((bb, cs_cols), row), state_spec,
                  pl.BlockSpec((CONV_W, GDN_QKV), fixed), pl.BlockSpec((1, AB_PAD), fixed),
                  pl.BlockSpec((1, AB_PAD), fixed), pl.BlockSpec((1, GDN_DK), fixed)],
        out_specs=[pl.BlockSpec((bb, GDN_WIDTH), row), pl.BlockSpec((bb, cs_cols), row), state_spec],
        out_shape=[jax.ShapeDtypeStruct((b, GDN_WIDTH), F32), jax.ShapeDtypeStruct((b, cs_cols), F32),
                   jax.ShapeDtypeStruct(s0.shape, F32)],
        compiler_params=_params(("parallel",)),
        name="gdn_step",
    )(pg, pab, conv_state, s0, conv_w, alog, dtb, norm_w)


def _rwkv_prep(xs, w0, a0, wwa, g2, k_k, k_a, head_ones):
    w = RWKV_WIDTH
    r, k, v = xs[:, 0:w], xs[:, w:2 * w], xs[:, 2 * w:3 * w]
    wa_in = xs[:, 3 * w:3 * w + LORA_WA]
    lane = _iota2(wa_in.shape, 1)
    wa_in = jnp.where(lane < LORA_WA // 2, jnp.tanh(wa_in), wa_in)
    wa = _mm(wa_in, wwa)
    e = math.exp(-0.5) * _sigmoid(w0 + wa[:, 0:w])
    a = _sigmoid(a0 + wa[:, w:2 * w])
    g = _mm(_sigmoid(xs[:, 3 * w + LORA_WA:RWKV_COLS]), g2)
    kx = k * k_k
    kk = kx * lax.rsqrt(_mm(kx * kx, head_ones) + L2_EPS)
    k = k * (1.0 + (a - 1.0) * k_a)
    return r, k, v, e, kk, kk * a, g


def _rwkv_out(y, r, k, v, g, r_k, ln_w, ln_b, head_ones):
    inv_n = 1.0 / RWKV_N
    mean = _mm(y, head_ones) * inv_n
    yc = y - mean
    var = _mm(yc * yc, head_ones) * inv_n
    y = yc * lax.rsqrt(var + GN_EPS) * ln_w + ln_b
    bonus = _mm(r * k * r_k, head_ones) * v
    return (y + bonus) * g


def _rwkv_prompt_kernel(nb, tb, n_cast, pr_ref, mu_ref, w0_ref, a0_ref, wwa_ref, g2_ref, kk_ref, ka_ref, rk_ref,
                        lnw_ref, lnb_ref, ones_ref, *refs):
    cast_in, (o_ref, s_out_ref), cast_out = refs[:n_cast], refs[n_cast:n_cast + 2], refs[n_cast + 2:2 * n_cast + 2]
    s_scr, ext_scr, y_scr = refs[2 * n_cast + 2:]
    for src, dst in zip(cast_in, cast_out):
        dst[...] = src[...].astype(BF16)
    t = pl.program_id(1)
    c = RWKV_CHUNK
    n = RWKV_N
    pair = 2 * n
    lg_c = c.bit_length() - 1
    lg_n = n.bit_length() - 1
    n_pairs = RWKV_HEADS // 2
    chunks_per_seq = tb // c

    @pl.when(t == 0)
    def _():
        s_scr[...] = jnp.zeros_like(s_scr)
        ext_scr[...] = jnp.zeros_like(ext_scr)

    prevs = []
    for q in range(nb):
        prevs.append(_shift_rows(ext_scr[q], pr_ref[q], 1))
        ext_scr[q] = pr_ref[q, tb - SUBLANES:tb, :]
    p = jnp.concatenate([pr_ref[q] for q in range(nb)], axis=0)
    xs = p + (jnp.concatenate(prevs, axis=0) - p) * mu_ref[...]
    head_ones = ones_ref[...]
    r, k, v, e, kk, kka, g = _rwkv_prep(xs, w0_ref[...], a0_ref[...], wwa_ref[...], g2_ref[...],
                                         kk_ref[...], ka_ref[...], head_ones)

    ri = _iota2((tb, tb), 0)
    ci = _iota2((tb, tb), 1)
    tri = jnp.where((ri >= ci) & ((ri >> lg_c) == (ci >> lg_c)), 1.0, 0.0)
    lg_inc = -jnp.concatenate([_mm_split3(tri, e[q * tb:(q + 1) * tb, :]) for q in range(nb)], axis=0)
    lg_exc = lg_inc + e
    a_t = -kk * jnp.exp(lg_exc)
    r_t = r * jnp.exp(lg_inc)
    inv_g = jnp.exp(-lg_inc)
    b_t = kka * inv_g
    k_t = k * inv_g

    assert c == n, "two heads side by side fill one (chunk, 2 * chunk) score tile"
    stack = _pair_stack(n)

    tt = _iota2((c, pair), 0)
    ss = _iota2((c, pair), 1) & (n - 1)
    strict = tt > ss
    incl = tt >= ss
    bd = jnp.where((_iota2((pair, pair), 0) >> lg_n) == (_iota2((pair, pair), 1) >> lg_n), 1.0, 0.0)

    items = [(ch, hp) for ch in range(nb * chunks_per_seq) for hp in range(n_pairs)]

    def tile(x, ch, hp):
        return x[ch * c:(ch + 1) * c, hp * pair:(hp + 1) * pair]

    ats = [tile(a_t, *it) for it in items]
    rts = [tile(r_t, *it) for it in items]
    vts = [tile(v, *it) for it in items]
    svs = [stack(vt) for vt in vts]
    scs = [_mm(jnp.concatenate([at, rt], axis=0),
               jnp.concatenate([stack(tile(b_t, *it)), stack(tile(k_t, *it))], axis=0), _NT)
           for at, rt, it in zip(ats, rts, items)]
    tinvs = _inv_unit_lower([-jnp.where(strict, sc[0:c, 0:pair], 0.0) for sc in scs], stack)
    akv = [_mm(jnp.where(strict, sc[0:c, pair:2 * pair], 0.0), sv) for sc, sv in zip(scs, svs)]
    y_v = [_mm(jnp.where(incl, sc[c:2 * c, pair:2 * pair], 0.0), sv) for sc, sv in zip(scs, svs)]
    p_rbs = [jnp.where(incl, sc[c:2 * c, 0:pair], 0.0) for sc in scs]
    w_ts = [_mm(tinv, stack(at)) for tinv, at in zip(tinvs, ats)]
    u0s = [_mm(tinv, stack(x)) for tinv, x in zip(tinvs, akv)]
    lg_lasts = [lg_inc[ch * c + c - 1:ch * c + c, hp * pair:(hp + 1) * pair] for ch, hp in items]
    to_ends = [jnp.exp(lg_last - tile(lg_inc, *it)) for lg_last, it in zip(lg_lasts, items)]
    bhs = [tile(kka, *it) * to_end for it, to_end in zip(items, to_ends)]
    khs = [tile(k, *it) * to_end for it, to_end in zip(items, to_ends)]
    r_hats = [rt + _mm(p_rb, stack(w_t)) for rt, p_rb, w_t in zip(rts, p_rbs, w_ts)]
    y0s = [_mm(p_rb, stack(u0)) + yv for p_rb, u0, yv in zip(p_rbs, u0s, y_v)]
    s_mix = [bd * _mm(w_t.T, bh) for w_t, bh in zip(w_ts, bhs)]
    s_add = [bd * _mm(jnp.concatenate([u0, vt], axis=0).T, jnp.concatenate([bh, kh], axis=0))
             for u0, vt, bh, kh in zip(u0s, vts, bhs, khs)]

    chains = [(q, hp) for q in range(nb) for hp in range(n_pairs)]
    states = [s_scr[q, hp] for q, hp in chains]
    for blk in range(chunks_per_seq):
        idx = [(q * chunks_per_seq + blk) * n_pairs + hp for q, hp in chains]
        for (q, hp), i, s in zip(chains, idx, states):
            row0 = (q * chunks_per_seq + blk) * c
            y_scr[row0:row0 + c, hp * pair:(hp + 1) * pair] = y0s[i] + _mm(r_hats[i], s, _NT)
        states = [s * jnp.exp(lg_lasts[i]) + _mm(s, s_mix[i]) + s_add[i] for i, s in zip(idx, states)]
    for (q, hp), s in zip(chains, states):
        s_scr[q, hp] = s

    out = _rwkv_out(y_scr[...], r, k, v, g, rk_ref[...], lnw_ref[...], lnb_ref[...], head_ones)
    for q in range(nb):
        o_ref[q] = out[q * tb:(q + 1) * tb, :].astype(o_ref.dtype)

    @pl.when(t == pl.num_programs(1) - 1)
    def _():
        for q, hp in chains:
            s = s_scr[q, hp]
            s_out_ref[q, 2 * hp] = s[0:n, 0:n]
            s_out_ref[q, 2 * hp + 1] = s[n:pair, n:pair]


def _rwkv_weight_specs(fixed):
    vec = pl.BlockSpec((1, RWKV_WIDTH), fixed)
    return [pl.BlockSpec((1, RWKV_COLS), fixed), vec, vec,
            pl.BlockSpec((LORA_WA, 2 * RWKV_WIDTH), fixed), pl.BlockSpec((G_LORA, RWKV_WIDTH), fixed),
            vec, vec, vec, vec, vec, pl.BlockSpec((RWKV_WIDTH, RWKV_WIDTH), fixed)]


def _cast_chunk_spec(rows, cols, steps, t_blocks):
    chunks = max(c for c in range(1, steps + 1) if rows % c == 0 and (rows // c) % (2 * SUBLANES) == 0)
    return pl.BlockSpec((rows // chunks, cols), lambda i, j: (jnp.minimum(i * t_blocks + j, chunks - 1), 0))


def _rwkv_prompt(pr, weights, to_cast=(), *, nb, tb):
    b, t, _ = pr.shape
    fixed = lambda i, j: (0, 0)
    steps = (b // nb) * (t // tb)
    cast_specs = [_cast_chunk_spec(*w.shape, steps, t // tb) for w in to_cast]
    return pl.pallas_call(
        functools.partial(_rwkv_prompt_kernel, nb, tb, len(to_cast)),
        grid=(b // nb, t // tb),
        in_specs=[pl.BlockSpec((nb, tb, RWKV_COLS), lambda i, j: (i, j, 0))] + _rwkv_weight_specs(fixed)
        + cast_specs,
        out_specs=[pl.BlockSpec((nb, tb, RWKV_WIDTH), lambda i, j: (i, j, 0)),
                   pl.BlockSpec((nb, RWKV_HEADS, RWKV_N, RWKV_N), lambda i, j: (i, 0, 0, 0))] + cast_specs,
        out_shape=[jax.ShapeDtypeStruct((b, t, RWKV_WIDTH), BF16),
                   jax.ShapeDtypeStruct((b, RWKV_HEADS, RWKV_N, RWKV_N), F32)]
        + [jax.ShapeDtypeStruct(w.shape, BF16) for w in to_cast],
        scratch_shapes=[pltpu.VMEM((nb, RWKV_HEADS // 2, 2 * RWKV_N, 2 * RWKV_N), F32),
                        pltpu.VMEM((nb, SUBLANES, RWKV_COLS), F32),
                        pltpu.VMEM((nb * tb, RWKV_WIDTH), F32)],
        compiler_params=_params(("arbitrary", "arbitrary")),
        name="rwkv_prompt",
    )(pr, *weights, *to_cast)


def _rwkv_step_kernel(pr_ref, sh_ref, s_ref, mu_ref, w0_ref, a0_ref, wwa_ref, g2_ref, kk_ref, ka_ref,
                      rk_ref, lnw_ref, lnb_ref, ones_ref, o_ref, s_out_ref, vec_scr, row_scr, y_scr):
    h = pl.program_id(0)
    n = RWKV_N
    batch = pr_ref.shape[0]

    @pl.when(h == 0)
    def _():
        p = pr_ref[...]
        xs = p + (sh_ref[...] - p) * mu_ref[...]
        r, k, v, e, kk, kka, g = _rwkv_prep(xs, w0_ref[...], a0_ref[...], wwa_ref[...], g2_ref[...],
                                             kk_ref[...], ka_ref[...], ones_ref[...])
        for j, x in enumerate((-kk, kka, k, jnp.exp(-e), r, v)):
            vec_scr[j] = x.T.reshape(RWKV_HEADS, n, batch)
        for j, x in enumerate((r, k, v, g)):
            row_scr[j] = x

    nkk, kka, k, decay, r = (vec_scr[j, h] for j in range(5))
    for vi in range(n):
        s = s_ref[vi]
        sa = jnp.sum(s * nkk, axis=0, keepdims=True)
        s = s * decay + sa * kka + vec_scr[5, h, vi:vi + 1, :] * k
        s_out_ref[vi] = s
        y_scr[h, vi:vi + 1, :] = jnp.sum(s * r, axis=0, keepdims=True)

    @pl.when(h == RWKV_HEADS - 1)
    def _():
        y = y_scr[...].reshape(RWKV_WIDTH, batch).T
        o_ref[...] = _rwkv_out(y, row_scr[0], row_scr[1], row_scr[2], row_scr[3], rk_ref[...], lnw_ref[...],
                               lnb_ref[...], ones_ref[...])


def _rwkv_step(pr, shift, s0_t, weights):
    b = pr.shape[0]
    fixed = lambda i: (0, 0)
    state_spec = pl.BlockSpec((None, RWKV_N, RWKV_N, b), lambda i: (i, 0, 0, 0))
    return pl.pallas_call(
        _rwkv_step_kernel,
        grid=(RWKV_HEADS,),
        in_specs=[pl.BlockSpec((b, RWKV_COLS), fixed), pl.BlockSpec((b, RWKV_COLS), fixed), state_spec]
        + _rwkv_weight_specs(fixed),
        out_specs=[pl.BlockSpec((b, RWKV_WIDTH), fixed), state_spec],
        out_shape=[jax.ShapeDtypeStruct((b, RWKV_WIDTH), F32), jax.ShapeDtypeStruct(s0_t.shape, F32)],
        scratch_shapes=[pltpu.VMEM((6, RWKV_HEADS, RWKV_N, b), F32), pltpu.VMEM((4, b, RWKV_WIDTH), F32),
                        pltpu.VMEM((RWKV_HEADS, RWKV_N, b), F32)],
        compiler_params=_params(("arbitrary",)),
        name="rwkv_step",
    )(pr, shift, s0_t, *weights)


def _pad_lanes(x, width):
    return jnp.pad(x, ((0, 0), (0, width - x.shape[1])))


def kernel(x_prompt, x_sample, state_gdn_conv, state_gdn, state_rwkv_shift, state_rwkv, norm_ffn1, w_ffn1_in, w_ffn1_out, norm_mix, w_in, gdn_conv_w, gdn_a_log, gdn_dt_bias, gdn_norm_w, rwkv_mu, rwkv_w0, rwkv_w2, rwkv_a0, rwkv_a2, rwkv_g2, rwkv_k_k, rwkv_k_a, rwkv_r_k, rwkv_ln_w, rwkv_ln_b, w_out, norm_ffn2, w_ffn2_in, w_ffn2_out, norm_final):
    depth = norm_ffn1.shape[0]
    assert depth == 1, "the carried-state plumbing below is written for a single layer"
    b, t, _ = x_prompt.shape
    bs = x_sample.shape[0]
    assert x_sample.shape[1] == 1
    l = 0

    n_gdn_cols = GDN_MAIN + 2 * GDN_HEADS
    w_in_l = w_in[l]
    w_slabs = (w_in_l[:, :GDN_MAIN].astype(BF16), w_in_l[:, n_gdn_cols:].astype(BF16),
               _pad_lanes(w_in_l[:, GDN_MAIN:n_gdn_cols], AB_PAD).astype(BF16))
    w1a, w1b = w_ffn1_in[l].astype(BF16), w_ffn1_out[l].astype(BF16)
    woa, wob = w_out[l, :GDN_WIDTH].astype(BF16), w_out[l, GDN_WIDTH:].astype(BF16)
    half = LORA_WA // 2
    wwa = jnp.zeros((LORA_WA, 2 * RWKV_WIDTH), F32)
    wwa = wwa.at[:half, :RWKV_WIDTH].set(rwkv_w2[l]).at[half:, RWKV_WIDTH:].set(rwkv_a2[l]).astype(BF16)
    head_id = jnp.arange(RWKV_WIDTH) // RWKV_N
    head_ones = (head_id[:, None] == head_id[None, :]).astype(BF16)
    rwkv_w = (rwkv_mu[l][None], rwkv_w0[l][None], rwkv_a0[l][None], wwa, rwkv_g2[l].astype(BF16),
              rwkv_k_k[l][None], rwkv_k_a[l][None], rwkv_r_k[l].reshape(1, RWKV_WIDTH),
              rwkv_ln_w[l][None], rwkv_ln_b[l][None], head_ones)
    alog = _pad_lanes(gdn_a_log[l][None], AB_PAD)
    dtb = _pad_lanes(gdn_dt_bias[l][None], AB_PAD)
    gdn_w = (gdn_conv_w[l], alog, dtb, gdn_norm_w[l][None])
    nf1, nmix, nf2, nfin = norm_ffn1[l][None], norm_mix[l][None], norm_ffn2[l][None], norm_final[None]

    xp, xs = x_prompt.reshape(b * t, D_MODEL), x_sample.reshape(bs, D_MODEL)
    hp, hs = _ffn(xp, xs, nf1, w1a, w1b, tm=DENSE_TILE_ROWS, sub=DENSE_SUB_ROWS)
    pg, pr, pab, sg, sr, sab = _proj_in(hp, hs, nmix, w_slabs, tm=DENSE_TILE_ROWS)

    pg3, pr3, pab3 = pg.reshape(b, t, GDN_MAIN), pr.reshape(b, t, RWKV_COLS), pab.reshape(b, t, AB_PAD)
    oa, gdn_s = _gdn_prompt(pg3, pab3, *gdn_w, nb=MIXER_SEQS, tb=MIXER_BLOCK_ROWS)
    ob, wkv_s, w2a, w2b = _rwkv_prompt(pr3, rwkv_w, (w_ffn2_in[l], w_ffn2_out[l]),
                                       nb=MIXER_SEQS, tb=MIXER_BLOCK_ROWS)

    conv_in = state_gdn_conv[l].reshape(bs, (CONV_W - 1) * GDN_QKV)
    oa_s, conv_s, gdn_ss = _gdn_step(sg, sab, conv_in, state_gdn[l], *gdn_w, bb=GDN_STEP_ROWS)
    ob_s, wkv_t = _rwkv_step(sr, state_rwkv_shift[l], jnp.transpose(state_rwkv[l], (1, 2, 3, 0)), rwkv_w)
    wkv_ss = jnp.transpose(wkv_t, (3, 0, 1, 2))

    yp, ys = _ffn(hp, hs, nf2, w2a, w2b, final_w=nfin, tm=DENSE_TILE_ROWS, sub=DENSE_SUB_ROWS,
                  mix=(oa.reshape(b * t, GDN_WIDTH), ob.reshape(b * t, RWKV_WIDTH), oa_s, ob_s, woa, wob))

    return (yp.reshape(b, t, D_MODEL), ys.reshape(bs, 1, D_MODEL),
            pg3[:, t - (CONV_W - 1):, :GDN_QKV][None], gdn_s[None], pr3[:, t - 1, :][None], wkv_s[None],
            conv_s.reshape(1, bs, CONV_W - 1, GDN_QKV), gdn_ss[None], sr[None], wkv_ss[None])
```

```python
import functools
import math

import jax
import jax.numpy as jnp
from jax import lax
from jax.experimental import pallas as pl
from jax.experimental.pallas import tpu as pltpu

F32 = jnp.float32
BF16 = jnp.bfloat16

D_MODEL = 1024
D_FF = 2816
CONV_W = 4
GDN_HEADS = 4
GDN_DK = 128
GDN_QK = GDN_HEADS * GDN_DK
GDN_WIDTH = GDN_HEADS * GDN_DK
GDN_QKV = 3 * GDN_WIDTH
GDN_MAIN = GDN_QKV + GDN_WIDTH
RWKV_HEADS = 8
RWKV_N = 64
RWKV_WIDTH = RWKV_HEADS * RWKV_N
LORA_WA = 128
G_LORA = 128
RWKV_COLS = 3 * RWKV_WIDTH + LORA_WA + G_LORA
LANES = 128
SUBLANES = 8
AB_PAD = LANES
NORM_EPS = 1e-6
GN_EPS = 64e-5
L2_EPS = 1e-6

GDN_CHUNK = 128
RWKV_CHUNK = 64
VMEM_LIMIT = 56 * 1024 * 1024
DENSE_TILE_ROWS = 1024
DENSE_SUB_ROWS = 256
MIXER_SEQS = 2
MIXER_BLOCK_ROWS = 256
GDN_STEP_ROWS = 8


_NN = (((1,), (0,)), ((), ()))
_NT = (((1,), (1,)), ((), ()))


def _mm(a, b, dims=_NN):
    return lax.dot_general(a.astype(BF16), b.astype(BF16), dims, preferred_element_type=F32)


def _mm_split3(sel, x):
    sel = sel.astype(BF16)
    out = None
    for _ in range(3):
        piece = x.astype(BF16)
        part = jnp.dot(sel, piece, preferred_element_type=F32)
        out = part if out is None else out + part
        x = x - piece.astype(F32)
    return out


def _sigmoid(x):
    return 1.0 / (1.0 + jnp.exp(-x))


def _silu(x):
    return x * _sigmoid(x)


def _softplus(x):
    return jnp.maximum(x, 0.0) + jnp.log(1.0 + jnp.exp(-jnp.abs(x)))


def _rms(x, g):
    return x * lax.rsqrt(jnp.mean(x * x, axis=-1, keepdims=True) + NORM_EPS) * g


def _iota2(shape, dim):
    return lax.broadcasted_iota(jnp.int32, shape, dim)


def _shift_rows(prev_tail, x, j):
    rolled = pltpu.roll(jnp.concatenate([prev_tail, x], axis=0), j, axis=0)
    return rolled[SUBLANES:, :]


def _params(sem):
    return pltpu.CompilerParams(dimension_semantics=sem, vmem_limit_bytes=VMEM_LIMIT)


def _resident(shape):
    return pl.BlockSpec(shape, lambda i: (0,) * len(shape), pipeline_mode=pl.Buffered(1))


def _ffn_rows(x_ref, mix_refs, w_refs, out_ref, sub):
    woa_ref, wob_ref, nw_ref, w1_ref, w2_ref, fw_ref = w_refs
    for s in range(x_ref.shape[0] // sub):
        rows = slice(s * sub, (s + 1) * sub)
        x = x_ref[rows, :]
        if mix_refs is not None:
            oa_ref, ob_ref = mix_refs
            x = x + _mm(oa_ref[rows, :], woa_ref[...]) + _mm(ob_ref[rows, :], wob_ref[...])
        xn = _rms(x, nw_ref[...]).astype(BF16)
        gate = jnp.dot(xn, w1_ref[:, 0:D_FF], preferred_element_type=F32)
        up = jnp.dot(xn, w1_ref[:, D_FF:2 * D_FF], preferred_element_type=F32)
        act = (_silu(gate) * up).astype(BF16)
        h = x + 0.5 * jnp.dot(act, w2_ref[...], preferred_element_type=F32)
        if fw_ref is not None:
            h = _rms(h, fw_ref[...])
        out_ref[rows, :] = h


def _ffn_kernel(has_mix, final_norm, sub, *refs):
    it = iter(refs)
    x_ref, xt_ref = next(it), next(it)
    mix_refs = tail_mix_refs = None
    woa_ref = wob_ref = None
    if has_mix:
        mix_refs, tail_mix_refs = (next(it), next(it)), (next(it), next(it))
        woa_ref, wob_ref = next(it), next(it)
    nw_ref, w1_ref, w2_ref = next(it), next(it), next(it)
    fw_ref = next(it) if final_norm else None
    out_ref, out_tail_ref = next(it), next(it)
    w_refs = (woa_ref, wob_ref, nw_ref, w1_ref, w2_ref, fw_ref)
    i = pl.program_id(0)
    last = pl.num_programs(0) - 1

    @pl.when(i < last)
    def _():
        _ffn_rows(x_ref, mix_refs, w_refs, out_ref, sub)

    @pl.when(i == last)
    def _():
        _ffn_rows(xt_ref, tail_mix_refs, w_refs, out_tail_ref, xt_ref.shape[0])


def _ffn(x, x_tail, norm_w, w1, w2, *, mix=None, final_w=None, tm, sub):
    m, mt = x.shape[0], x_tail.shape[0]
    n_tiles = m // tm
    has_mix = mix is not None
    final_norm = final_w is not None
    row = lambda i: (jnp.minimum(i, n_tiles - 1), 0)
    tail = lambda i: (0, 0)
    in_specs = [pl.BlockSpec((tm, D_MODEL), row), pl.BlockSpec((mt, D_MODEL), tail)]
    args = [x, x_tail]
    if has_mix:
        oa, ob, oa_tail, ob_tail, woa, wob = mix
        in_specs += [pl.BlockSpec((tm, GDN_WIDTH), row), pl.BlockSpec((tm, RWKV_WIDTH), row),
                     pl.BlockSpec((mt, GDN_WIDTH), tail), pl.BlockSpec((mt, RWKV_WIDTH), tail),
                     _resident((GDN_WIDTH, D_MODEL)), _resident((RWKV_WIDTH, D_MODEL))]
        args += [oa, ob, oa_tail, ob_tail, woa, wob]
    in_specs += [_resident((1, D_MODEL)), _resident((D_MODEL, 2 * D_FF)), _resident((D_FF, D_MODEL))]
    args += [norm_w, w1, w2]
    if final_norm:
        in_specs.append(_resident((1, D_MODEL)))
        args.append(final_w)
    return pl.pallas_call(
        functools.partial(_ffn_kernel, has_mix, final_norm, sub),
        grid=(n_tiles + 1,),
        in_specs=in_specs,
        out_specs=[pl.BlockSpec((tm, D_MODEL), row), pl.BlockSpec((mt, D_MODEL), tail)],
        out_shape=[jax.ShapeDtypeStruct((m, D_MODEL), F32), jax.ShapeDtypeStruct((mt, D_MODEL), F32)],
        compiler_params=_params(("arbitrary",)),
        name="ffn_mix" if has_mix else "ffn",
    )(*args)


def _l2norm(x):
    return x * lax.rsqrt(jnp.sum(x * x, axis=-1, keepdims=True) + L2_EPS)


def _proj_in_rows(h_ref, nw_ref, w_refs, out_refs):
    n = _rms(h_ref[...], nw_ref[...]).astype(BF16)
    for w_ref, out_ref in zip(w_refs, out_refs):
        out_ref[...] = jnp.dot(n, w_ref[...], preferred_element_type=F32)


def _proj_in_kernel(h_ref, ht_ref, nw_ref, wg_ref, wr_ref, wab_ref, pg_ref, pr_ref, pab_ref,
                    pgt_ref, prt_ref, pabt_ref):
    i = pl.program_id(0)
    last = pl.num_programs(0) - 1
    w_refs = (wg_ref, wr_ref, wab_ref)

    @pl.when(i < last)
    def _():
        _proj_in_rows(h_ref, nw_ref, w_refs, (pg_ref, pr_ref, pab_ref))

    @pl.when(i == last)
    def _():
        _proj_in_rows(ht_ref, nw_ref, w_refs, (pgt_ref, prt_ref, pabt_ref))


def _proj_in(h, h_tail, norm_w, w_slabs, *, tm):
    m, mt = h.shape[0], h_tail.shape[0]
    n_tiles = m // tm
    row = lambda i: (jnp.minimum(i, n_tiles - 1), 0)
    tail = lambda i: (0, 0)
    widths = (GDN_MAIN, RWKV_COLS, AB_PAD)
    return pl.pallas_call(
        _proj_in_kernel,
        grid=(n_tiles + 1,),
        in_specs=[pl.BlockSpec((tm, D_MODEL), row), pl.BlockSpec((mt, D_MODEL), tail), _resident((1, D_MODEL))]
        + [_resident((D_MODEL, w)) for w in widths],
        out_specs=[pl.BlockSpec((tm, w), row) for w in widths] + [pl.BlockSpec((mt, w), tail) for w in widths],
        out_shape=[jax.ShapeDtypeStruct((m, w), F32) for w in widths]
        + [jax.ShapeDtypeStruct((mt, w), F32) for w in widths],
        compiler_params=_params(("arbitrary",)),
        name="proj_in",
    )(h, h_tail, norm_w, *w_slabs)


def _inv_unit_lower(lows, stack=None):
    shape = lows[0].shape
    n = shape[0]
    ri = _iota2(shape, 0)
    ci = _iota2(shape, 1) & (n - 1)
    eye = jnp.where(ri == ci, 1.0, 0.0)
    pair_blk = (ri >> 1) == (ci >> 1)
    ts = [eye - jnp.where(pair_blk, low, 0.0) for low in lows]
    s, lg = 2, 1
    while s < n:
        sel = ((ri >> (lg + 1)) == (ci >> (lg + 1))) & ((ri >> lg) != (ci >> lg))
        offs = [jnp.where(sel, low, 0.0) for low in lows]
        if stack is None:
            xs = [_mm(t, off) for t, off in zip(ts, offs)]
            ts = [t - _mm(x, t) for t, x in zip(ts, xs)]
        else:
            xs = [_mm(t, stack(off)) for t, off in zip(ts, offs)]
            ts = [t - _mm(x, stack(t)) for t, x in zip(ts, xs)]
        s, lg = 2 * s, lg + 1
    return ts


def _pair_stack(m):
    first = _iota2((m, 2 * m), 1) < m

    def stack(z):
        z = z.astype(BF16)
        zero = jnp.zeros_like(z)
        return jnp.concatenate([jnp.where(first, z, zero), jnp.where(first, zero, z)], axis=0)

    return stack


def _inv_unit_lower_halves(lows):
    n = lows[0].shape[0]
    m = n // 2
    first = _iota2((m, n), 1) < m
    stack = _pair_stack(m)
    t_diag = _inv_unit_lower([jnp.where(first, low[0:m, :], low[m:n, :]) for low in lows], stack)
    lower_left = (_iota2((n, n), 0) >= m) & (_iota2((n, n), 1) < m)
    xs = [_mm(t, jnp.where(lower_left, low, 0.0)) for t, low in zip(t_diag, lows)]
    t_ll = [_mm(x, stack(t)) for x, t in zip(xs, t_diag)]
    return [jnp.concatenate([jnp.where(first, t, 0.0), jnp.where(first, 0.0, t) - ll], axis=0)
            for t, ll in zip(t_diag, t_ll)]


def _gdn_gates(ab, alog, dtb):
    log_alpha = -jnp.exp(alog) * _softplus(ab + dtb)
    return log_alpha, _sigmoid(ab)


def _gdn_out(o, z, norm_w):
    o = o * lax.rsqrt(jnp.mean(o * o, axis=-1, keepdims=True) + NORM_EPS) * norm_w
    return o * _silu(z)


def _gdn_prompt_kernel(nb, tb, pg_ref, pab_ref, cw_ref, alog_ref, dtb_ref, nw_ref, o_ref, s_out_ref,
                       s_scr, ext_scr):
    t = pl.program_id(1)
    c = GDN_CHUNK
    chunks_per_seq = tb // c

    @pl.when(t == 0)
    def _():
        s_scr[...] = jnp.zeros_like(s_scr)
        ext_scr[...] = jnp.zeros_like(ext_scr)

    accs = []
    for q in range(nb):
        u = pg_ref[q, :, 0:GDN_QKV]
        acc = u * cw_ref[CONV_W - 1:CONV_W, :]
        for j in range(1, CONV_W):
            acc = acc + _shift_rows(ext_scr[q], u, j) * cw_ref[CONV_W - 1 - j:CONV_W - j, :]
        ext_scr[q] = u[tb - SUBLANES:tb, :]
        accs.append(acc)
    qkv = _silu(jnp.concatenate(accs, axis=0))

    pab = jnp.concatenate([pab_ref[q] for q in range(nb)], axis=0)
    log_alpha, beta_all = _gdn_gates(pab, alog_ref[...], dtb_ref[...])
    ri = _iota2((c, c), 0)
    ci = _iota2((c, c), 1)
    causal = ri >= ci
    strict = ri > ci
    tri = jnp.where(causal, 1.0, 0.0)

    n_chunks = nb * chunks_per_seq
    items = [(ch, h) for ch in range(n_chunks) for h in range(GDN_HEADS)]
    g_blk, gt_blk = [], []
    for ch in range(n_chunks):
        g = _mm_split3(tri, log_alpha[ch * c:(ch + 1) * c, :])
        g_blk.append(g)
        gt_blk.append(g.T)
    qs, ks, vbs, kbs, decays, g_cols = [], [], [], [], [], []
    for ch, h in items:
        rows = slice(ch * c, (ch + 1) * c)
        q = _l2norm(qkv[rows, h * GDN_DK:(h + 1) * GDN_DK]) * (GDN_DK ** -0.5)
        k = _l2norm(qkv[rows, GDN_QK + h * GDN_DK:GDN_QK + (h + 1) * GDN_DK])
        v = qkv[rows, 2 * GDN_QK + h * GDN_DK:2 * GDN_QK + (h + 1) * GDN_DK]
        beta = beta_all[rows, GDN_HEADS + h:GDN_HEADS + h + 1]
        g_col = g_blk[ch][:, h:h + 1]
        g_row = gt_blk[ch][h:h + 1, :]
        decays.append(jnp.exp(jnp.where(causal, g_col - g_row, -jnp.inf)))
        qs.append(q)
        ks.append(k)
        vbs.append(v * beta)
        kbs.append(k * beta)
        g_cols.append(g_col)
    kk = [_mm(kb, k, _NT) for kb, k in zip(kbs, ks)]
    qk = [_mm(q, k, _NT) for q, k in zip(qs, ks)]
    tinvs = _inv_unit_lower_halves([jnp.where(strict, x * d, 0.0) for x, d in zip(kk, decays)])
    attns = [x * d for x, d in zip(qk, decays)]
    egs = [jnp.exp(g_col) for g_col in g_cols]
    us = [_mm(tinv, vb) for tinv, vb in zip(tinvs, vbs)]
    ws = [_mm(tinv, kb * eg) for tinv, kb, eg in zip(tinvs, kbs, egs)]
    g_lasts = [g_col[c - 1:c, :] for g_col in g_cols]
    kd_ts = [(k * jnp.exp(g_last - g_col)).T for k, g_last, g_col in zip(ks, g_lasts, g_cols)]
    q_hats = [q * eg - _mm(attn, w) for q, eg, attn, w in zip(qs, egs, attns, ws)]
    o0s = [_mm(attn, u) for attn, u in zip(attns, us)]
    s_mix = [_mm(kd_t, w) for kd_t, w in zip(kd_ts, ws)]
    s_add = [_mm(kd_t, u) for kd_t, u in zip(kd_ts, us)]

    chains = [(q, h) for q in range(nb) for h in range(GDN_HEADS)]
    states = [s_scr[q, h] for q, h in chains]
    for blk in range(chunks_per_seq):
        idx = [(q * chunks_per_seq + blk) * GDN_HEADS + h for q, h in chains]
        for (q, h), i, s in zip(chains, idx, states):
            o = o0s[i] + _mm(q_hats[i], s)
            rows = slice(blk * c, (blk + 1) * c)
            z = pg_ref[q, rows, GDN_QKV + h * GDN_DK:GDN_QKV + (h + 1) * GDN_DK]
            o_ref[q, rows, h * GDN_DK:(h + 1) * GDN_DK] = _gdn_out(o, z, nw_ref[...]).astype(o_ref.dtype)
        states = [s * jnp.exp(g_lasts[i]) - _mm(s_mix[i], s) + s_add[i] for i, s in zip(idx, states)]
    for (q, h), s in zip(chains, states):
        s_scr[q, h] = s

    @pl.when(t == pl.num_programs(1) - 1)
    def _():
        s_out_ref[...] = s_scr[...]


def _gdn_prompt(pg, pab, conv_w, alog, dtb, norm_w, *, nb, tb):
    b, t, _ = pg.shape
    fixed = lambda i, j: (0, 0)
    return pl.pallas_call(
        functools.partial(_gdn_prompt_kernel, nb, tb),
        grid=(b // nb, t // tb),
        in_specs=[pl.BlockSpec((nb, tb, GDN_MAIN), lambda i, j: (i, j, 0)),
                  pl.BlockSpec((nb, tb, AB_PAD), lambda i, j: (i, j, 0)),
                  pl.BlockSpec((CONV_W, GDN_QKV), fixed), pl.BlockSpec((1, AB_PAD), fixed),
                  pl.BlockSpec((1, AB_PAD), fixed), pl.BlockSpec((1, GDN_DK), fixed)],
        out_specs=[pl.BlockSpec((nb, tb, GDN_WIDTH), lambda i, j: (i, j, 0)),
                   pl.BlockSpec((nb, GDN_HEADS, GDN_DK, GDN_DK), lambda i, j: (i, 0, 0, 0))],
        out_shape=[jax.ShapeDtypeStruct((b, t, GDN_WIDTH), BF16),
                   jax.ShapeDtypeStruct((b, GDN_HEADS, GDN_DK, GDN_DK), F32)],
        scratch_shapes=[pltpu.VMEM((nb, GDN_HEADS, GDN_DK, GDN_DK), F32),
                        pltpu.VMEM((nb, SUBLANES, GDN_QKV), F32)],
        compiler_params=_params(("parallel", "arbitrary")),
        name="gdn_prompt",
    )(pg, pab, conv_w, alog, dtb, norm_w)


def _gdn_step_kernel(bb, pg_ref, pab_ref, cs_ref, s_ref, cw_ref, alog_ref, dtb_ref, nw_ref,
                     o_ref, cs_out_ref, s_out_ref):
    u = pg_ref[:, 0:GDN_QKV]
    acc = u * cw_ref[CONV_W - 1:CONV_W, :]
    for j in range(CONV_W - 1):
        acc = acc + cs_ref[:, j * GDN_QKV:(j + 1) * GDN_QKV] * cw_ref[j:j + 1, :]
    cs_out_ref[:, 0:(CONV_W - 2) * GDN_QKV] = cs_ref[:, GDN_QKV:(CONV_W - 1) * GDN_QKV]
    cs_out_ref[:, (CONV_W - 2) * GDN_QKV:(CONV_W - 1) * GDN_QKV] = u
    qkv = _silu(acc)
    log_alpha, beta_all = _gdn_gates(pab_ref[...], alog_ref[...], dtb_ref[...])
    alpha_all = jnp.exp(log_alpha)
    eye = jnp.where(_iota2((GDN_DK, GDN_DK), 0) == _iota2((GDN_DK, GDN_DK), 1), 1.0, 0.0)

    def to_col(row):
        return jnp.sum(eye * row, axis=-1, keepdims=True)

    vs, k_cols, q_cols = [], [], []
    for h in range(GDN_HEADS):
        q = qkv[:, h * GDN_DK:(h + 1) * GDN_DK]
        k = qkv[:, GDN_QK + h * GDN_DK:GDN_QK + (h + 1) * GDN_DK]
        vs.append(qkv[:, 2 * GDN_QK + h * GDN_DK:2 * GDN_QK + (h + 1) * GDN_DK])
        q = _l2norm(q) * (GDN_DK ** -0.5)
        k = _l2norm(k)
        k_cols.append([to_col(k[i:i + 1, :]) for i in range(bb)])
        q_cols.append([to_col(q[i:i + 1, :]) for i in range(bb)])
    for h in range(GDN_HEADS):
        o_rows = []
        for i in range(bb):
            s = s_ref[i, h] * alpha_all[i:i + 1, h:h + 1]
            mem = jnp.sum(k_cols[h][i] * s, axis=0, keepdims=True)
            delta = (vs[h][i:i + 1, :] - mem) * beta_all[i:i + 1, GDN_HEADS + h:GDN_HEADS + h + 1]
            s = s + k_cols[h][i] * delta
            s_out_ref[i, h] = s
            o_rows.append(jnp.sum(q_cols[h][i] * s, axis=0, keepdims=True))
        o = jnp.concatenate(o_rows, axis=0)
        z = pg_ref[:, GDN_QKV + h * GDN_DK:GDN_QKV + (h + 1) * GDN_DK]
        o_ref[:, h * GDN_DK:(h + 1) * GDN_DK] = _gdn_out(o, z, nw_ref[...])


def _gdn_step(pg, pab, conv_state, s0, conv_w, alog, dtb, norm_w, *, bb):
    b = pg.shape[0]
    row = lambda i: (i, 0)
    fixed = lambda i: (0, 0)
    cs_cols = (CONV_W - 1) * GDN_QKV
    state_spec = pl.BlockSpec((bb, GDN_HEADS, GDN_DK, GDN_DK), lambda i: (i, 0, 0, 0))
    return pl.pallas_call(
        functools.partial(_gdn_step_kernel, bb),
        grid=(b // bb,),
        in_specs=[pl.BlockSpec((bb, GDN_MAIN), row), pl.BlockSpec((bb, AB_PAD), row),
                  pl.BlockSpec((bb, cs_cols), row), state_spec,
                  pl.BlockSpec((CONV_W, GDN_QKV), fixed), pl.BlockSpec((1, AB_PAD), fixed),
                  pl.BlockSpec((1, AB_PAD), fixed), pl.BlockSpec((1, GDN_DK), fixed)],
        out_specs=[pl.BlockSpec((bb, GDN_WIDTH), row), pl.BlockSpec((bb, cs_cols), row), state_spec],
        out_shape=[jax.ShapeDtypeStruct((b, GDN_WIDTH), F32), jax.ShapeDtypeStruct((b, cs_cols), F32),
                   jax.ShapeDtypeStruct(s0.shape, F32)],
        compiler_params=_params(("parallel",)),
        name="gdn_step",
    )(pg, pab, conv_state, s0, conv_w, alog, dtb, norm_w)


def _rwkv_prep(xs, w0, a0, wwa, g2, k_k, k_a, head_ones):
    w = RWKV_WIDTH
    r, k, v = xs[:, 0:w], xs[:, w:2 * w], xs[:, 2 * w:3 * w]
    wa_in = xs[:, 3 * w:3 * w + LORA_WA]
    lane = _iota2(wa_in.shape, 1)
    wa_in = jnp.where(lane < LORA_WA // 2, jnp.tanh(wa_in), wa_in)
    wa = _mm(wa_in, wwa)
    e = math.exp(-0.5) * _sigmoid(w0 + wa[:, 0:w])
    a = _sigmoid(a0 + wa[:, w:2 * w])
    g = _mm(_sigmoid(xs[:, 3 * w + LORA_WA:RWKV_COLS]), g2)
    kx = k * k_k
    kk = kx * lax.rsqrt(_mm(kx * kx, head_ones) + L2_EPS)
    k = k * (1.0 + (a - 1.0) * k_a)
    return r, k, v, e, kk, kk * a, g


def _rwkv_out(y, r, k, v, g, r_k, ln_w, ln_b, head_ones):
    inv_n = 1.0 / RWKV_N
    mean = _mm(y, head_ones) * inv_n
    yc = y - mean
    var = _mm(yc * yc, head_ones) * inv_n
    y = yc * lax.rsqrt(var + GN_EPS) * ln_w + ln_b
    bonus = _mm(r * k * r_k, head_ones) * v
    return (y + bonus) * g


def _rwkv_prompt_kernel(nb, tb, n_cast, pr_ref, mu_ref, w0_ref, a0_ref, wwa_ref, g2_ref, kk_ref, ka_ref, rk_ref,
                        lnw_ref, lnb_ref, ones_ref, *refs):
    cast_in, (o_ref, s_out_ref), cast_out = refs[:n_cast], refs[n_cast:n_cast + 2], refs[n_cast + 2:2 * n_cast + 2]
    s_scr, ext_scr, y_scr = refs[2 * n_cast + 2:]
    for src, dst in zip(cast_in, cast_out):
        dst[...] = src[...].astype(BF16)
    t = pl.program_id(1)
    c = RWKV_CHUNK
    n = RWKV_N
    pair = 2 * n
    lg_c = c.bit_length() - 1
    lg_n = n.bit_length() - 1
    n_pairs = RWKV_HEADS // 2
    chunks_per_seq = tb // c

    @pl.when(t == 0)
    def _():
        s_scr[...] = jnp.zeros_like(s_scr)
        ext_scr[...] = jnp.zeros_like(ext_scr)

    prevs = []
    for q in range(nb):
        prevs.append(_shift_rows(ext_scr[q], pr_ref[q], 1))
        ext_scr[q] = pr_ref[q, tb - SUBLANES:tb, :]
    p = jnp.concatenate([pr_ref[q] for q in range(nb)], axis=0)
    xs = p + (jnp.concatenate(prevs, axis=0) - p) * mu_ref[...]
    head_ones = ones_ref[...]
    r, k, v, e, kk, kka, g = _rwkv_prep(xs, w0_ref[...], a0_ref[...], wwa_ref[...], g2_ref[...],
                                         kk_ref[...], ka_ref[...], head_ones)

    ri = _iota2((tb, tb), 0)
    ci = _iota2((tb, tb), 1)
    tri = jnp.where((ri >= ci) & ((ri >> lg_c) == (ci >> lg_c)), 1.0, 0.0)
    lg_inc = -jnp.concatenate([_mm_split3(tri, e[q * tb:(q + 1) * tb, :]) for q in range(nb)], axis=0)
    lg_exc = lg_inc + e
    a_t = -kk * jnp.exp(lg_exc)
    r_t = r * jnp.exp(lg_inc)
    inv_g = jnp.exp(-lg_inc)
    b_t = kka * inv_g
    k_t = k * inv_g

    assert c == n, "two heads side by side fill one (chunk, 2 * chunk) score tile"
    stack = _pair_stack(n)

    tt = _iota2((c, pair), 0)
    ss = _iota2((c, pair), 1) & (n - 1)
    strict = tt > ss
    incl = tt >= ss
    bd = jnp.where((_iota2((pair, pair), 0) >> lg_n) == (_iota2((pair, pair), 1) >> lg_n), 1.0, 0.0)

    items = [(ch, hp) for ch in range(nb * chunks_per_seq) for hp in range(n_pairs)]

    def tile(x, ch, hp):
        return x[ch * c:(ch + 1) * c, hp * pair:(hp + 1) * pair]

    ats = [tile(a_t, *it) for it in items]
    rts = [tile(r_t, *it) for it in items]
    vts = [tile(v, *it) for it in items]
    svs = [stack(vt) for vt in vts]
    scs = [_mm(jnp.concatenate([at, rt], axis=0),
               jnp.concatenate([stack(tile(b_t, *it)), stack(tile(k_t, *it))], axis=0), _NT)
           for at, rt, it in zip(ats, rts, items)]
    tinvs = _inv_unit_lower([-jnp.where(strict, sc[0:c, 0:pair], 0.0) for sc in scs], stack)
    v_mix = [_mm(jnp.concatenate([jnp.where(strict, sc[0:c, pair:2 * pair], 0.0),
                                  jnp.where(incl, sc[c:2 * c, pair:2 * pair], 0.0)], axis=0), sv)
             for sc, sv in zip(scs, svs)]
    akv = [x[0:c, :] for x in v_mix]
    y_v = [x[c:2 * c, :] for x in v_mix]
    p_rbs = [jnp.where(incl, sc[c:2 * c, 0:pair], 0.0) for sc in scs]
    wu = [_mm(tinv, jnp.concatenate([stack(at), stack(x)], axis=1)) for tinv, at, x in zip(tinvs, ats, akv)]
    w_ts = [x[:, 0:pair] for x in wu]
    u0s = [x[:, pair:2 * pair] for x in wu]
    lg_lasts = [lg_inc[ch * c + c - 1:ch * c + c, hp * pair:(hp + 1) * pair] for ch, hp in items]
    to_ends = [jnp.exp(lg_last - tile(lg_inc, *it)) for lg_last, it in zip(lg_lasts, items)]
    bhs = [tile(kka, *it) * to_end for it, to_end in zip(items, to_ends)]
    khs = [tile(k, *it) * to_end for it, to_end in zip(items, to_ends)]
    ry = [_mm(p_rb, jnp.concatenate([stack(w_t), stack(u0)], axis=1)) for p_rb, w_t, u0 in zip(p_rbs, w_ts, u0s)]
    r_hats = [rt + x[:, 0:pair] for rt, x in zip(rts, ry)]
    y0s = [x[:, pair:2 * pair] + yv for x, yv in zip(ry, y_v)]
    s_mix = [bd * _mm(w_t.T, bh) for w_t, bh in zip(w_ts, bhs)]
    s_add = [bd * _mm(jnp.concatenate([u0, vt], axis=0).T, jnp.concatenate([bh, kh], axis=0))
             for u0, vt, bh, kh in zip(u0s, vts, bhs, khs)]

    chains = [(q, hp) for q in range(nb) for hp in range(n_pairs)]
    states = [s_scr[q, hp] for q, hp in chains]
    for blk in range(chunks_per_seq):
        idx = [(q * chunks_per_seq + blk) * n_pairs + hp for q, hp in chains]
        for (q, hp), i, s in zip(chains, idx, states):
            row0 = (q * chunks_per_seq + blk) * c
            y_scr[row0:row0 + c, hp * pair:(hp + 1) * pair] = y0s[i] + _mm(r_hats[i], s, _NT)
        states = [s * jnp.exp(lg_lasts[i]) + _mm(s, s_mix[i]) + s_add[i] for i, s in zip(idx, states)]
    for (q, hp), s in zip(chains, states):
        s_scr[q, hp] = s

    out = _rwkv_out(y_scr[...], r, k, v, g, rk_ref[...], lnw_ref[...], lnb_ref[...], head_ones)
    for q in range(nb):
        o_ref[q] = out[q * tb:(q + 1) * tb, :].astype(o_ref.dtype)

    @pl.when(t == pl.num_programs(1) - 1)
    def _():
        for q, hp in chains:
            s = s_scr[q, hp]
            s_out_ref[q, 2 * hp] = s[0:n, 0:n]
            s_out_ref[q, 2 * hp + 1] = s[n:pair, n:pair]


def _rwkv_weight_specs(fixed):
    vec = pl.BlockSpec((1, RWKV_WIDTH), fixed)
    return [pl.BlockSpec((1, RWKV_COLS), fixed), vec, vec,
            pl.BlockSpec((LORA_WA, 2 * RWKV_WIDTH), fixed), pl.BlockSpec((G_LORA, RWKV_WIDTH), fixed),
            vec, vec, vec, vec, vec, pl.BlockSpec((RWKV_WIDTH, RWKV_WIDTH), fixed)]


def _cast_chunk_spec(rows, cols, steps, t_blocks):
    chunks = max(c for c in range(1, steps + 1) if rows % c == 0 and (rows // c) % (2 * SUBLANES) == 0)
    return pl.BlockSpec((rows // chunks, cols), lambda i, j: (jnp.minimum(i * t_blocks + j, chunks - 1), 0))


def _rwkv_prompt(pr, weights, to_cast=(), *, nb, tb):
    b, t, _ = pr.shape
    fixed = lambda i, j: (0, 0)
    steps = (b // nb) * (t // tb)
    cast_specs = [_cast_chunk_spec(*w.shape, steps, t // tb) for w in to_cast]
    return pl.pallas_call(
        functools.partial(_rwkv_prompt_kernel, nb, tb, len(to_cast)),
        grid=(b // nb, t // tb),
        in_specs=[pl.BlockSpec((nb, tb, RWKV_COLS), lambda i, j: (i, j, 0))] + _rwkv_weight_specs(fixed)
        + cast_specs,
        out_specs=[pl.BlockSpec((nb, tb, RWKV_WIDTH), lambda i, j: (i, j, 0)),
                   pl.BlockSpec((nb, RWKV_HEADS, RWKV_N, RWKV_N), lambda i, j: (i, 0, 0, 0))] + cast_specs,
        out_shape=[jax.ShapeDtypeStruct((b, t, RWKV_WIDTH), BF16),
                   jax.ShapeDtypeStruct((b, RWKV_HEADS, RWKV_N, RWKV_N), F32)]
        + [jax.ShapeDtypeStruct(w.shape, BF16) for w in to_cast],
        scratch_shapes=[pltpu.VMEM((nb, RWKV_HEADS // 2, 2 * RWKV_N, 2 * RWKV_N), F32),
                        pltpu.VMEM((nb, SUBLANES, RWKV_COLS), F32),
                        pltpu.VMEM((nb * tb, RWKV_WIDTH), F32)],
        compiler_params=_params(("arbitrary", "arbitrary")),
        name="rwkv_prompt",
    )(pr, *weights, *to_cast)


def _rwkv_step_kernel(pr_ref, sh_ref, s_ref, mu_ref, w0_ref, a0_ref, wwa_ref, g2_ref, kk_ref, ka_ref,
                      rk_ref, lnw_ref, lnb_ref, ones_ref, o_ref, s_out_ref, vec_scr, row_scr, y_scr):
    h = pl.program_id(0)
    n = RWKV_N
    batch = pr_ref.shape[0]

    @pl.when(h == 0)
    def _():
        p = pr_ref[...]
        xs = p + (sh_ref[...] - p) * mu_ref[...]
        r, k, v, e, kk, kka, g = _rwkv_prep(xs, w0_ref[...], a0_ref[...], wwa_ref[...], g2_ref[...],
                                             kk_ref[...], ka_ref[...], ones_ref[...])
        for j, x in enumerate((-kk, kka, k, jnp.exp(-e), r, v)):
            vec_scr[j] = x.T.reshape(RWKV_HEADS, n, batch)
        for j, x in enumerate((r, k, v, g)):
            row_scr[j] = x

    nkk, kka, k, decay, r = (vec_scr[j, h] for j in range(5))
    for vi in range(n):
        s = s_ref[vi]
        sa = jnp.sum(s * nkk, axis=0, keepdims=True)
        s = s * decay + sa * kka + vec_scr[5, h, vi:vi + 1, :] * k
        s_out_ref[vi] = s
        y_scr[h, vi:vi + 1, :] = jnp.sum(s * r, axis=0, keepdims=True)

    @pl.when(h == RWKV_HEADS - 1)
    def _():
        y = y_scr[...].reshape(RWKV_WIDTH, batch).T
        o_ref[...] = _rwkv_out(y, row_scr[0], row_scr[1], row_scr[2], row_scr[3], rk_ref[...], lnw_ref[...],
                               lnb_ref[...], ones_ref[...])


def _rwkv_step(pr, shift, s0_t, weights):
    b = pr.shape[0]
    fixed = lambda i: (0, 0)
    state_spec = pl.BlockSpec((None, RWKV_N, RWKV_N, b), lambda i: (i, 0, 0, 0))
    return pl.pallas_call(
        _rwkv_step_kernel,
        grid=(RWKV_HEADS,),
        in_specs=[pl.BlockSpec((b, RWKV_COLS), fixed), pl.BlockSpec((b, RWKV_COLS), fixed), state_spec]
        + _rwkv_weight_specs(fixed),
        out_specs=[pl.BlockSpec((b, RWKV_WIDTH), fixed), state_spec],
        out_shape=[jax.ShapeDtypeStruct((b, RWKV_WIDTH), F32), jax.ShapeDtypeStruct(s0_t.shape, F32)],
        scratch_shapes=[pltpu.VMEM((6, RWKV_HEADS, RWKV_N, b), F32), pltpu.VMEM((4, b, RWKV_WIDTH), F32),
                        pltpu.VMEM((RWKV_HEADS, RWKV_N, b), F32)],
        compiler_params=_params(("arbitrary",)),
        name="rwkv_step",
    )(pr, shift, s0_t, *weights)


def _pad_lanes(x, width):
    return jnp.pad(x, ((0, 0), (0, width - x.shape[1])))


def kernel(x_prompt, x_sample, state_gdn_conv, state_gdn, state_rwkv_shift, state_rwkv, norm_ffn1, w_ffn1_in, w_ffn1_out, norm_mix, w_in, gdn_conv_w, gdn_a_log, gdn_dt_bias, gdn_norm_w, rwkv_mu, rwkv_w0, rwkv_w2, rwkv_a0, rwkv_a2, rwkv_g2, rwkv_k_k, rwkv_k_a, rwkv_r_k, rwkv_ln_w, rwkv_ln_b, w_out, norm_ffn2, w_ffn2_in, w_ffn2_out, norm_final):
    depth = norm_ffn1.shape[0]
    assert depth == 1, "the carried-state plumbing below is written for a single layer"
    b, t, _ = x_prompt.shape
    bs = x_sample.shape[0]
    assert x_sample.shape[1] == 1
    l = 0

    n_gdn_cols = GDN_MAIN + 2 * GDN_HEADS
    w_in_l = w_in[l]
    w_slabs = (w_in_l[:, :GDN_MAIN].astype(BF16), w_in_l[:, n_gdn_cols:].astype(BF16),
               _pad_lanes(w_in_l[:, GDN_MAIN:n_gdn_cols], AB_PAD).astype(BF16))
    w1a, w1b = w_ffn1_in[l].astype(BF16), w_ffn1_out[l].astype(BF16)
    woa, wob = w_out[l, :GDN_WIDTH].astype(BF16), w_out[l, GDN_WIDTH:].astype(BF16)
    half = LORA_WA // 2
    wwa = jnp.zeros((LORA_WA, 2 * RWKV_WIDTH), F32)
    wwa = wwa.at[:half, :RWKV_WIDTH].set(rwkv_w2[l]).at[half:, RWKV_WIDTH:].set(rwkv_a2[l]).astype(BF16)
    head_id = jnp.arange(RWKV_WIDTH) // RWKV_N
    head_ones = (head_id[:, None] == head_id[None, :]).astype(BF16)
    rwkv_w = (rwkv_mu[l][None], rwkv_w0[l][None], rwkv_a0[l][None], wwa, rwkv_g2[l].astype(BF16),
              rwkv_k_k[l][None], rwkv_k_a[l][None], rwkv_r_k[l].reshape(1, RWKV_WIDTH),
              rwkv_ln_w[l][None], rwkv_ln_b[l][None], head_ones)
    alog = _pad_lanes(gdn_a_log[l][None], AB_PAD)
    dtb = _pad_lanes(gdn_dt_bias[l][None], AB_PAD)
    gdn_w = (gdn_conv_w[l], alog, dtb, gdn_norm_w[l][None])
    nf1, nmix, nf2, nfin = norm_ffn1[l][None], norm_mix[l][None], norm_ffn2[l][None], norm_final[None]

    xp, xs = x_prompt.reshape(b * t, D_MODEL), x_sample.reshape(bs, D_MODEL)
    hp, hs = _ffn(xp, xs, nf1, w1a, w1b, tm=DENSE_TILE_ROWS, sub=DENSE_SUB_ROWS)
    pg, pr, pab, sg, sr, sab = _proj_in(hp, hs, nmix, w_slabs, tm=DENSE_TILE_ROWS)

    pg3, pr3, pab3 = pg.reshape(b, t, GDN_MAIN), pr.reshape(b, t, RWKV_COLS), pab.reshape(b, t, AB_PAD)
    oa, gdn_s = _gdn_prompt(pg3, pab3, *gdn_w, nb=MIXER_SEQS, tb=MIXER_BLOCK_ROWS)
    ob, wkv_s, w2a, w2b = _rwkv_prompt(pr3, rwkv_w, (w_ffn2_in[l], w_ffn2_out[l]),
                                       nb=MIXER_SEQS, tb=MIXER_BLOCK_ROWS)

    conv_in = state_gdn_conv[l].reshape(bs, (CONV_W - 1) * GDN_QKV)
    oa_s, conv_s, gdn_ss = _gdn_step(sg, sab, conv_in, state_gdn[l], *gdn_w, bb=GDN_STEP_ROWS)
    ob_s, wkv_t = _rwkv_step(sr, state_rwkv_shift[l], jnp.transpose(state_rwkv[l], (1, 2, 3, 0)), rwkv_w)
    wkv_ss = jnp.transpose(wkv_t, (3, 0, 1, 2))

    yp, ys = _ffn(hp, hs, nf2, w2a, w2b, final_w=nfin, tm=DENSE_TILE_ROWS, sub=DENSE_SUB_ROWS,
                  mix=(oa.reshape(b * t, GDN_WIDTH), ob.reshape(b * t, RWKV_WIDTH), oa_s, ob_s, woa, wob))

    return (yp.reshape(b, t, D_MODEL), ys.reshape(bs, 1, D_MODEL),
            pg3[:, t - (CONV_W - 1):, :GDN_QKV][None], gdn_s[None], pr3[:, t - 1, :][None], wkv_s[None],
            conv_s.reshape(1, bs, CONV_W - 1, GDN_QKV), gdn_ss[None], sr[None], wkv_ss[None])
```

```python
import functools
import math

import jax
import jax.numpy as jnp
from jax import lax
from jax.experimental import pallas as pl
from jax.experimental.pallas import tpu as pltpu

F32 = jnp.float32
BF16 = jnp.bfloat16

D_MODEL = 1024
D_FF = 2816
CONV_W = 4
GDN_HEADS = 4
GDN_DK = 128
GDN_QK = GDN_HEADS * GDN_DK
GDN_WIDTH = GDN_HEADS * GDN_DK
GDN_QKV = 3 * GDN_WIDTH
GDN_MAIN = GDN_QKV + GDN_WIDTH
RWKV_HEADS = 8
RWKV_N = 64
RWKV_WIDTH = RWKV_HEADS * RWKV_N
LORA_WA = 128
G_LORA = 128
RWKV_COLS = 3 * RWKV_WIDTH + LORA_WA + G_LORA
LANES = 128
SUBLANES = 8
AB_PAD = LANES
NORM_EPS = 1e-6
GN_EPS = 64e-5
L2_EPS = 1e-6

GDN_CHUNK = 128
RWKV_CHUNK = 64
VMEM_LIMIT = 56 * 1024 * 1024
DENSE_TILE_ROWS = 1024
DENSE_SUB_ROWS = 256
MIXER_SEQS = 2
MIXER_BLOCK_ROWS = 256
GDN_STEP_ROWS = 16


_NN = (((1,), (0,)), ((), ()))
_NT = (((1,), (1,)), ((), ()))


def _mm(a, b, dims=_NN):
    return lax.dot_general(a.astype(BF16), b.astype(BF16), dims, preferred_element_type=F32)


def _mm_split3(sel, x):
    sel = sel.astype(BF16)
    out = None
    for _ in range(3):
        piece = x.astype(BF16)
        part = jnp.dot(sel, piece, preferred_element_type=F32)
        out = part if out is None else out + part
        x = x - piece.astype(F32)
    return out


def _sigmoid(x):
    return 1.0 / (1.0 + jnp.exp(-x))


def _silu(x):
    return x * _sigmoid(x)


def _softplus(x):
    return jnp.maximum(x, 0.0) + jnp.log(1.0 + jnp.exp(-jnp.abs(x)))


def _rms(x, g):
    return x * lax.rsqrt(jnp.mean(x * x, axis=-1, keepdims=True) + NORM_EPS) * g


def _iota2(shape, dim):
    return lax.broadcasted_iota(jnp.int32, shape, dim)


def _shift_rows(prev_tail, x, j):
    rolled = pltpu.roll(jnp.concatenate([prev_tail, x], axis=0), j, axis=0)
    return rolled[SUBLANES:, :]


def _params(sem):
    return pltpu.CompilerParams(dimension_semantics=sem, vmem_limit_bytes=VMEM_LIMIT)


def _resident(shape):
    return pl.BlockSpec(shape, lambda i: (0,) * len(shape), pipeline_mode=pl.Buffered(1))


def _ffn_rows(x_ref, mix_refs, w_refs, out_ref, sub):
    woa_ref, wob_ref, nw_ref, w1_ref, w2_ref, fw_ref = w_refs
    for s in range(x_ref.shape[0] // sub):
        rows = slice(s * sub, (s + 1) * sub)
        x = x_ref[rows, :]
        if mix_refs is not None:
            oa_ref, ob_ref = mix_refs
            x = x + _mm(oa_ref[rows, :], woa_ref[...]) + _mm(ob_ref[rows, :], wob_ref[...])
        xn = _rms(x, nw_ref[...]).astype(BF16)
        gate = jnp.dot(xn, w1_ref[:, 0:D_FF], preferred_element_type=F32)
        up = jnp.dot(xn, w1_ref[:, D_FF:2 * D_FF], preferred_element_type=F32)
        act = (_silu(gate) * up).astype(BF16)
        h = x + 0.5 * jnp.dot(act, w2_ref[...], preferred_element_type=F32)
        if fw_ref is not None:
            h = _rms(h, fw_ref[...])
        out_ref[rows, :] = h


def _ffn_kernel(has_mix, final_norm, sub, *refs):
    it = iter(refs)
    x_ref, xt_ref = next(it), next(it)
    mix_refs = tail_mix_refs = None
    woa_ref = wob_ref = None
    if has_mix:
        mix_refs, tail_mix_refs = (next(it), next(it)), (next(it), next(it))
        woa_ref, wob_ref = next(it), next(it)
    nw_ref, w1_ref, w2_ref = next(it), next(it), next(it)
    fw_ref = next(it) if final_norm else None
    out_ref, out_tail_ref = next(it), next(it)
    w_refs = (woa_ref, wob_ref, nw_ref, w1_ref, w2_ref, fw_ref)
    i = pl.program_id(0)
    last = pl.num_programs(0) - 1

    @pl.when(i < last)
    def _():
        _ffn_rows(x_ref, mix_refs, w_refs, out_ref, sub)

    @pl.when(i == last)
    def _():
        _ffn_rows(xt_ref, tail_mix_refs, w_refs, out_tail_ref, xt_ref.shape[0])


def _ffn(x, x_tail, norm_w, w1, w2, *, mix=None, final_w=None, tm, sub):
    m, mt = x.shape[0], x_tail.shape[0]
    n_tiles = m // tm
    has_mix = mix is not None
    final_norm = final_w is not None
    row = lambda i: (jnp.minimum(i, n_tiles - 1), 0)
    tail = lambda i: (0, 0)
    in_specs = [pl.BlockSpec((tm, D_MODEL), row), pl.BlockSpec((mt, D_MODEL), tail)]
    args = [x, x_tail]
    if has_mix:
        oa, ob, oa_tail, ob_tail, woa, wob = mix
        in_specs += [pl.BlockSpec((tm, GDN_WIDTH), row), pl.BlockSpec((tm, RWKV_WIDTH), row),
                     pl.BlockSpec((mt, GDN_WIDTH), tail), pl.BlockSpec((mt, RWKV_WIDTH), tail),
                     _resident((GDN_WIDTH, D_MODEL)), _resident((RWKV_WIDTH, D_MODEL))]
        args += [oa, ob, oa_tail, ob_tail, woa, wob]
    in_specs += [_resident((1, D_MODEL)), _resident((D_MODEL, 2 * D_FF)), _resident((D_FF, D_MODEL))]
    args += [norm_w, w1, w2]
    if final_norm:
        in_specs.append(_resident((1, D_MODEL)))
        args.append(final_w)
    return pl.pallas_call(
        functools.partial(_ffn_kernel, has_mix, final_norm, sub),
        grid=(n_tiles + 1,),
        in_specs=in_specs,
        out_specs=[pl.BlockSpec((tm, D_MODEL), row), pl.BlockSpec((mt, D_MODEL), tail)],
        out_shape=[jax.ShapeDtypeStruct((m, D_MODEL), F32), jax.ShapeDtypeStruct((mt, D_MODEL), F32)],
        compiler_params=_params(("arbitrary",)),
        name="ffn_mix" if has_mix else "ffn",
    )(*args)


def _l2norm(x):
    return x * lax.rsqrt(jnp.sum(x * x, axis=-1, keepdims=True) + L2_EPS)


def _proj_in_rows(h_ref, nw_ref, w_refs, out_refs):
    n = _rms(h_ref[...], nw_ref[...]).astype(BF16)
    for w_ref, out_ref in zip(w_refs, out_refs):
        out_ref[...] = jnp.dot(n, w_ref[...], preferred_element_type=F32)


def _proj_in_kernel(h_ref, ht_ref, nw_ref, wg_ref, wr_ref, wab_ref, pg_ref, pr_ref, pab_ref,
                    pgt_ref, prt_ref, pabt_ref):
    i = pl.program_id(0)
    last = pl.num_programs(0) - 1
    w_refs = (wg_ref, wr_ref, wab_ref)

    @pl.when(i < last)
    def _():
        _proj_in_rows(h_ref, nw_ref, w_refs, (pg_ref, pr_ref, pab_ref))

    @pl.when(i == last)
    def _():
        _proj_in_rows(ht_ref, nw_ref, w_refs, (pgt_ref, prt_ref, pabt_ref))


def _proj_in(h, h_tail, norm_w, w_slabs, *, tm):
    m, mt = h.shape[0], h_tail.shape[0]
    n_tiles = m // tm
    row = lambda i: (jnp.minimum(i, n_tiles - 1), 0)
    tail = lambda i: (0, 0)
    widths = (GDN_MAIN, RWKV_COLS, AB_PAD)
    return pl.pallas_call(
        _proj_in_kernel,
        grid=(n_tiles + 1,),
        in_specs=[pl.BlockSpec((tm, D_MODEL), row), pl.BlockSpec((mt, D_MODEL), tail), _resident((1, D_MODEL))]
        + [_resident((D_MODEL, w)) for w in widths],
        out_specs=[pl.BlockSpec((tm, w), row) for w in widths] + [pl.BlockSpec((mt, w), tail) for w in widths],
        out_shape=[jax.ShapeDtypeStruct((m, w), F32) for w in widths]
        + [jax.ShapeDtypeStruct((mt, w), F32) for w in widths],
        compiler_params=_params(("arbitrary",)),
        name="proj_in",
    )(h, h_tail, norm_w, *w_slabs)


def _inv_unit_lower(lows, stack=None):
    shape = lows[0].shape
    n = shape[0]
    ri = _iota2(shape, 0)
    ci = _iota2(shape, 1) & (n - 1)
    eye = jnp.where(ri == ci, 1.0, 0.0)
    pair_blk = (ri >> 1) == (ci >> 1)
    ts = [eye - jnp.where(pair_blk, low, 0.0) for low in lows]
    s, lg = 2, 1
    while s < n:
        sel = ((ri >> (lg + 1)) == (ci >> (lg + 1))) & ((ri >> lg) != (ci >> lg))
        offs = [jnp.where(sel, low, 0.0) for low in lows]
        if stack is None:
            xs = [_mm(t, off) for t, off in zip(ts, offs)]
            ts = [t - _mm(x, t) for t, x in zip(ts, xs)]
        else:
            xs = [_mm(t, stack(off)) for t, off in zip(ts, offs)]
            ts = [t - _mm(x, stack(t)) for t, x in zip(ts, xs)]
        s, lg = 2 * s, lg + 1
    return ts


def _pair_stack(m):
    first = _iota2((m, 2 * m), 1) < m

    def stack(z):
        z = z.astype(BF16)
        zero = jnp.zeros_like(z)
        return jnp.concatenate([jnp.where(first, z, zero), jnp.where(first, zero, z)], axis=0)

    return stack


def _inv_unit_lower_halves(lows):
    n = lows[0].shape[0]
    m = n // 2
    first = _iota2((m, n), 1) < m
    stack = _pair_stack(m)
    t_diag = _inv_unit_lower([jnp.where(first, low[0:m, :], low[m:n, :]) for low in lows], stack)
    lower_left = (_iota2((n, n), 0) >= m) & (_iota2((n, n), 1) < m)
    xs = [_mm(t, jnp.where(lower_left, low, 0.0)) for t, low in zip(t_diag, lows)]
    t_ll = [_mm(x, stack(t)) for x, t in zip(xs, t_diag)]
    return [jnp.concatenate([jnp.where(first, t, 0.0), jnp.where(first, 0.0, t) - ll], axis=0)
            for t, ll in zip(t_diag, t_ll)]


def _gdn_gates(ab, alog, dtb):
    log_alpha = -jnp.exp(alog) * _softplus(ab + dtb)
    return log_alpha, _sigmoid(ab)


def _gdn_out(o, z, norm_w):
    o = o * lax.rsqrt(jnp.mean(o * o, axis=-1, keepdims=True) + NORM_EPS) * norm_w
    return o * _silu(z)


def _gdn_prompt_kernel(nb, tb, pg_ref, pab_ref, cw_ref, alog_ref, dtb_ref, nw_ref, o_ref, s_out_ref,
                       s_scr, ext_scr):
    t = pl.program_id(1)
    c = GDN_CHUNK
    chunks_per_seq = tb // c

    @pl.when(t == 0)
    def _():
        s_scr[...] = jnp.zeros_like(s_scr)
        ext_scr[...] = jnp.zeros_like(ext_scr)

    accs = []
    for q in range(nb):
        u = pg_ref[q, :, 0:GDN_QKV]
        acc = u * cw_ref[CONV_W - 1:CONV_W, :]
        for j in range(1, CONV_W):
            acc = acc + _shift_rows(ext_scr[q], u, j) * cw_ref[CONV_W - 1 - j:CONV_W - j, :]
        ext_scr[q] = u[tb - SUBLANES:tb, :]
        accs.append(acc)
    qkv = _silu(jnp.concatenate(accs, axis=0))

    pab = jnp.concatenate([pab_ref[q] for q in range(nb)], axis=0)
    log_alpha, beta_all = _gdn_gates(pab, alog_ref[...], dtb_ref[...])
    ri = _iota2((c, c), 0)
    ci = _iota2((c, c), 1)
    causal = ri >= ci
    strict = ri > ci
    tri = jnp.where(causal, 1.0, 0.0)

    n_chunks = nb * chunks_per_seq
    items = [(ch, h) for ch in range(n_chunks) for h in range(GDN_HEADS)]
    g_blk, gt_blk = [], []
    for ch in range(n_chunks):
        g = _mm_split3(tri, log_alpha[ch * c:(ch + 1) * c, :])
        g_blk.append(g)
        gt_blk.append(g.T)
    qs, ks, vbs, kbs, decays, g_cols = [], [], [], [], [], []
    for ch, h in items:
        rows = slice(ch * c, (ch + 1) * c)
        q = _l2norm(qkv[rows, h * GDN_DK:(h + 1) * GDN_DK]) * (GDN_DK ** -0.5)
        k = _l2norm(qkv[rows, GDN_QK + h * GDN_DK:GDN_QK + (h + 1) * GDN_DK])
        v = qkv[rows, 2 * GDN_QK + h * GDN_DK:2 * GDN_QK + (h + 1) * GDN_DK]
        beta = beta_all[rows, GDN_HEADS + h:GDN_HEADS + h + 1]
        g_col = g_blk[ch][:, h:h + 1]
        g_row = gt_blk[ch][h:h + 1, :]
        decays.append(jnp.exp(jnp.where(causal, g_col - g_row, -jnp.inf)))
        qs.append(q)
        ks.append(k)
        vbs.append(v * beta)
        kbs.append(k * beta)
        g_cols.append(g_col)
    kk = [_mm(kb, k, _NT) for kb, k in zip(kbs, ks)]
    qk = [_mm(q, k, _NT) for q, k in zip(qs, ks)]
    tinvs = _inv_unit_lower_halves([jnp.where(strict, x * d, 0.0) for x, d in zip(kk, decays)])
    attns = [x * d for x, d in zip(qk, decays)]
    egs = [jnp.exp(g_col) for g_col in g_cols]
    us = [_mm(tinv, vb) for tinv, vb in zip(tinvs, vbs)]
    ws = [_mm(tinv, kb * eg) for tinv, kb, eg in zip(tinvs, kbs, egs)]
    g_lasts = [g_col[c - 1:c, :] for g_col in g_cols]
    kd_ts = [(k * jnp.exp(g_last - g_col)).T for k, g_last, g_col in zip(ks, g_lasts, g_cols)]
    q_hats = [q * eg - _mm(attn, w) for q, eg, attn, w in zip(qs, egs, attns, ws)]
    o0s = [_mm(attn, u) for attn, u in zip(attns, us)]
    s_mix = [_mm(kd_t, w) for kd_t, w in zip(kd_ts, ws)]
    s_add = [_mm(kd_t, u) for kd_t, u in zip(kd_ts, us)]

    chains = [(q, h) for q in range(nb) for h in range(GDN_HEADS)]
    states = [s_scr[q, h] for q, h in chains]
    for blk in range(chunks_per_seq):
        idx = [(q * chunks_per_seq + blk) * GDN_HEADS + h for q, h in chains]
        for (q, h), i, s in zip(chains, idx, states):
            o = o0s[i] + _mm(q_hats[i], s)
            rows = slice(blk * c, (blk + 1) * c)
            z = pg_ref[q, rows, GDN_QKV + h * GDN_DK:GDN_QKV + (h + 1) * GDN_DK]
            o_ref[q, rows, h * GDN_DK:(h + 1) * GDN_DK] = _gdn_out(o, z, nw_ref[...]).astype(o_ref.dtype)
        states = [s * jnp.exp(g_lasts[i]) - _mm(s_mix[i], s) + s_add[i] for i, s in zip(idx, states)]
    for (q, h), s in zip(chains, states):
        s_scr[q, h] = s

    @pl.when(t == pl.num_programs(1) - 1)
    def _():
        s_out_ref[...] = s_scr[...]


def _gdn_prompt(pg, pab, conv_w, alog, dtb, norm_w, *, nb, tb):
    b, t, _ = pg.shape
    fixed = lambda i, j: (0, 0)
    return pl.pallas_call(
        functools.partial(_gdn_prompt_kernel, nb, tb),
        grid=(b // nb, t // tb),
        in_specs=[pl.BlockSpec((nb, tb, GDN_MAIN), lambda i, j: (i, j, 0)),
                  pl.BlockSpec((nb, tb, AB_PAD), lambda i, j: (i, j, 0)),
                  pl.BlockSpec((CONV_W, GDN_QKV), fixed), pl.BlockSpec((1, AB_PAD), fixed),
                  pl.BlockSpec((1, AB_PAD), fixed), pl.BlockSpec((1, GDN_DK), fixed)],
        out_specs=[pl.BlockSpec((nb, tb, GDN_WIDTH), lambda i, j: (i, j, 0)),
                   pl.BlockSpec((nb, GDN_HEADS, GDN_DK, GDN_DK), lambda i, j: (i, 0, 0, 0))],
        out_shape=[jax.ShapeDtypeStruct((b, t, GDN_WIDTH), BF16),
                   jax.ShapeDtypeStruct((b, GDN_HEADS, GDN_DK, GDN_DK), F32)],
        scratch_shapes=[pltpu.VMEM((nb, GDN_HEADS, GDN_DK, GDN_DK), F32),
                        pltpu.VMEM((nb, SUBLANES, GDN_QKV), F32)],
        compiler_params=_params(("parallel", "arbitrary")),
        name="gdn_prompt",
    )(pg, pab, conv_w, alog, dtb, norm_w)


def _gdn_step_kernel(bb, pg_ref, pab_ref, cs_ref, s_ref, cw_ref, alog_ref, dtb_ref, nw_ref,
                     o_ref, cs_out_ref, s_out_ref):
    u = pg_ref[:, 0:GDN_QKV]
    acc = u * cw_ref[CONV_W - 1:CONV_W, :]
    for j in range(CONV_W - 1):
        acc = acc + cs_ref[:, j * GDN_QKV:(j + 1) * GDN_QKV] * cw_ref[j:j + 1, :]
    cs_out_ref[:, 0:(CONV_W - 2) * GDN_QKV] = cs_ref[:, GDN_QKV:(CONV_W - 1) * GDN_QKV]
    cs_out_ref[:, (CONV_W - 2) * GDN_QKV:(CONV_W - 1) * GDN_QKV] = u
    qkv = _silu(acc)
    log_alpha, beta_all = _gdn_gates(pab_ref[...], alog_ref[...], dtb_ref[...])
    alpha_all = jnp.exp(log_alpha)
    eye = jnp.where(_iota2((GDN_DK, GDN_DK), 0) == _iota2((GDN_DK, GDN_DK), 1), 1.0, 0.0)

    def to_col(row):
        return jnp.sum(eye * row, axis=-1, keepdims=True)

    vs, k_cols, q_cols = [], [], []
    for h in range(GDN_HEADS):
        q = qkv[:, h * GDN_DK:(h + 1) * GDN_DK]
        k = qkv[:, GDN_QK + h * GDN_DK:GDN_QK + (h + 1) * GDN_DK]
        vs.append(qkv[:, 2 * GDN_QK + h * GDN_DK:2 * GDN_QK + (h + 1) * GDN_DK])
        q = _l2norm(q) * (GDN_DK ** -0.5)
        k = _l2norm(k)
        k_cols.append([to_col(k[i:i + 1, :]) for i in range(bb)])
        q_cols.append([to_col(q[i:i + 1, :]) for i in range(bb)])
    for h in range(GDN_HEADS):
        o_rows = []
        for i in range(bb):
            s = s_ref[i, h] * alpha_all[i:i + 1, h:h + 1]
            mem = jnp.sum(k_cols[h][i] * s, axis=0, keepdims=True)
            delta = (vs[h][i:i + 1, :] - mem) * beta_all[i:i + 1, GDN_HEADS + h:GDN_HEADS + h + 1]
            s = s + k_cols[h][i] * delta
            s_out_ref[i, h] = s
            o_rows.append(jnp.sum(q_cols[h][i] * s, axis=0, keepdims=True))
        o = jnp.concatenate(o_rows, axis=0)
        z = pg_ref[:, GDN_QKV + h * GDN_DK:GDN_QKV + (h + 1) * GDN_DK]
        o_ref[:, h * GDN_DK:(h + 1) * GDN_DK] = _gdn_out(o, z, nw_ref[...])


def _gdn_step(pg, pab, conv_state, s0, conv_w, alog, dtb, norm_w, *, bb):
    b = pg.shape[0]
    row = lambda i: (i, 0)
    fixed = lambda i: (0, 0)
    cs_cols = (CONV_W - 1) * GDN_QKV
    state_spec = pl.BlockSpec((bb, GDN_HEADS, GDN_DK, GDN_DK), lambda i: (i, 0, 0, 0))
    return pl.pallas_call(
        functools.partial(_gdn_step_kernel, bb),
        grid=(b // bb,),
        in_specs=[pl.BlockSpec((bb, GDN_MAIN), row), pl.BlockSpec((bb, AB_PAD), row),
                  pl.BlockSpec((bb, cs_cols), row), state_spec,
                  pl.BlockSpec((CONV_W, GDN_QKV), fixed), pl.BlockSpec((1, AB_PAD), fixed),
                  pl.BlockSpec((1, AB_PAD), fixed), pl.BlockSpec((1, GDN_DK), fixed)],
        out_specs=[pl.BlockSpec((bb, GDN_WIDTH), row), pl.BlockSpec((bb, cs_cols), row), state_spec],
        out_shape=[jax.ShapeDtypeStruct((b, GDN_WIDTH), F32), jax.ShapeDtypeStruct((b, cs_cols), F32),
                   jax.ShapeDtypeStruct(s0.shape, F32)],
        compiler_params=_params(("parallel",)),
        name="gdn_step",
    )(pg, pab, conv_state, s0, conv_w, alog, dtb, norm_w)


def _rwkv_prep(xs, w0, a0, wwa, g2, k_k, k_a, head_ones):
    w = RWKV_WIDTH
    r, k, v = xs[:, 0:w], xs[:, w:2 * w], xs[:, 2 * w:3 * w]
    wa_in = xs[:, 3 * w:3 * w + LORA_WA]
    lane = _iota2(wa_in.shape, 1)
    wa_in = jnp.where(lane < LORA_WA // 2, jnp.tanh(wa_in), wa_in)
    wa = _mm(wa_in, wwa)
    e = math.exp(-0.5) * _sigmoid(w0 + wa[:, 0:w])
    a = _sigmoid(a0 + wa[:, w:2 * w])
    g = _mm(_sigmoid(xs[:, 3 * w + LORA_WA:RWKV_COLS]), g2)
    kx = k * k_k
    kk = kx * lax.rsqrt(_mm(kx * kx, head_ones) + L2_EPS)
    k = k * (1.0 + (a - 1.0) * k_a)
    return r, k, v, e, kk, kk * a, g


def _rwkv_out(y, r, k, v, g, r_k, ln_w, ln_b, head_ones):
    inv_n = 1.0 / RWKV_N
    mean = _mm(y, head_ones) * inv_n
    yc = y - mean
    var = _mm(yc * yc, head_ones) * inv_n
    y = yc * lax.rsqrt(var + GN_EPS) * ln_w + ln_b
    bonus = _mm(r * k * r_k, head_ones) * v
    return (y + bonus) * g


def _rwkv_prompt_kernel(nb, tb, n_cast, pr_ref, mu_ref, w0_ref, a0_ref, wwa_ref, g2_ref, kk_ref, ka_ref, rk_ref,
                        lnw_ref, lnb_ref, ones_ref, *refs):
    cast_in, (o_ref, s_out_ref), cast_out = refs[:n_cast], refs[n_cast:n_cast + 2], refs[n_cast + 2:2 * n_cast + 2]
    s_scr, ext_scr, y_scr = refs[2 * n_cast + 2:]
    for src, dst in zip(cast_in, cast_out):
        dst[...] = src[...].astype(BF16)
    t = pl.program_id(1)
    c = RWKV_CHUNK
    n = RWKV_N
    pair = 2 * n
    lg_c = c.bit_length() - 1
    lg_n = n.bit_length() - 1
    n_pairs = RWKV_HEADS // 2
    chunks_per_seq = tb // c

    @pl.when(t == 0)
    def _():
        s_scr[...] = jnp.zeros_like(s_scr)
        ext_scr[...] = jnp.zeros_like(ext_scr)

    prevs = []
    for q in range(nb):
        prevs.append(_shift_rows(ext_scr[q], pr_ref[q], 1))
        ext_scr[q] = pr_ref[q, tb - SUBLANES:tb, :]
    p = jnp.concatenate([pr_ref[q] for q in range(nb)], axis=0)
    xs = p + (jnp.concatenate(prevs, axis=0) - p) * mu_ref[...]
    head_ones = ones_ref[...]
    r, k, v, e, kk, kka, g = _rwkv_prep(xs, w0_ref[...], a0_ref[...], wwa_ref[...], g2_ref[...],
                                         kk_ref[...], ka_ref[...], head_ones)

    ri = _iota2((tb, tb), 0)
    ci = _iota2((tb, tb), 1)
    tri = jnp.where((ri >= ci) & ((ri >> lg_c) == (ci >> lg_c)), 1.0, 0.0)
    lg_inc = -jnp.concatenate([_mm_split3(tri, e[q * tb:(q + 1) * tb, :]) for q in range(nb)], axis=0)
    lg_exc = lg_inc + e
    a_t = -kk * jnp.exp(lg_exc)
    r_t = r * jnp.exp(lg_inc)
    inv_g = jnp.exp(-lg_inc)
    b_t = kka * inv_g
    k_t = k * inv_g

    assert c == n, "two heads side by side fill one (chunk, 2 * chunk) score tile"
    stack = _pair_stack(n)

    tt = _iota2((c, pair), 0)
    ss = _iota2((c, pair), 1) & (n - 1)
    strict = tt > ss
    incl = tt >= ss
    bd = jnp.where((_iota2((pair, pair), 0) >> lg_n) == (_iota2((pair, pair), 1) >> lg_n), 1.0, 0.0)

    items = [(ch, hp) for ch in range(nb * chunks_per_seq) for hp in range(n_pairs)]

    def tile(x, ch, hp):
        return x[ch * c:(ch + 1) * c, hp * pair:(hp + 1) * pair]

    ats = [tile(a_t, *it) for it in items]
    rts = [tile(r_t, *it) for it in items]
    vts = [tile(v, *it) for it in items]
    svs = [stack(vt) for vt in vts]
    scs = [_mm(jnp.concatenate([at, rt], axis=0),
               jnp.concatenate([stack(tile(b_t, *it)), stack(tile(k_t, *it))], axis=0), _NT)
           for at, rt, it in zip(ats, rts, items)]
    tinvs = _inv_unit_lower([-jnp.where(strict, sc[0:c, 0:pair], 0.0) for sc in scs], stack)
    v_mix = [_mm(jnp.concatenate([jnp.where(strict, sc[0:c, pair:2 * pair], 0.0),
                                  jnp.where(incl, sc[c:2 * c, pair:2 * pair], 0.0)], axis=0), sv)
             for sc, sv in zip(scs, svs)]
    akv = [x[0:c, :] for x in v_mix]
    y_v = [x[c:2 * c, :] for x in v_mix]
    p_rbs = [jnp.where(incl, sc[c:2 * c, 0:pair], 0.0) for sc in scs]
    wu = [_mm(tinv, jnp.concatenate([stack(at), stack(x)], axis=1)) for tinv, at, x in zip(tinvs, ats, akv)]
    w_ts = [x[:, 0:pair] for x in wu]
    u0s = [x[:, pair:2 * pair] for x in wu]
    lg_lasts = [lg_inc[ch * c + c - 1:ch * c + c, hp * pair:(hp + 1) * pair] for ch, hp in items]
    to_ends = [jnp.exp(lg_last - tile(lg_inc, *it)) for lg_last, it in zip(lg_lasts, items)]
    bhs = [tile(kka, *it) * to_end for it, to_end in zip(items, to_ends)]
    khs = [tile(k, *it) * to_end for it, to_end in zip(items, to_ends)]
    ry = [_mm(p_rb, jnp.concatenate([stack(w_t), stack(u0)], axis=1)) for p_rb, w_t, u0 in zip(p_rbs, w_ts, u0s)]
    r_hats = [rt + x[:, 0:pair] for rt, x in zip(rts, ry)]
    y0s = [x[:, pair:2 * pair] + yv for x, yv in zip(ry, y_v)]
    s_mix = [bd * _mm(w_t.T, bh) for w_t, bh in zip(w_ts, bhs)]
    s_add = [bd * _mm(jnp.concatenate([u0, vt], axis=0).T, jnp.concatenate([bh, kh], axis=0))
             for u0, vt, bh, kh in zip(u0s, vts, bhs, khs)]

    chains = [(q, hp) for q in range(nb) for hp in range(n_pairs)]
    states = [s_scr[q, hp] for q, hp in chains]
    for blk in range(chunks_per_seq):
        idx = [(q * chunks_per_seq + blk) * n_pairs + hp for q, hp in chains]
        for (q, hp), i, s in zip(chains, idx, states):
            row0 = (q * chunks_per_seq + blk) * c
            y_scr[row0:row0 + c, hp * pair:(hp + 1) * pair] = y0s[i] + _mm(r_hats[i], s, _NT)
        states = [s * jnp.exp(lg_lasts[i]) + _mm(s, s_mix[i]) + s_add[i] for i, s in zip(idx, states)]
    for (q, hp), s in zip(chains, states):
        s_scr[q, hp] = s

    out = _rwkv_out(y_scr[...], r, k, v, g, rk_ref[...], lnw_ref[...], lnb_ref[...], head_ones)
    for q in range(nb):
        o_ref[q] = out[q * tb:(q + 1) * tb, :].astype(o_ref.dtype)

    @pl.when(t == pl.num_programs(1) - 1)
    def _():
        for q, hp in chains:
            s = s_scr[q, hp]
            s_out_ref[q, 2 * hp] = s[0:n, 0:n]
            s_out_ref[q, 2 * hp + 1] = s[n:pair, n:pair]


def _rwkv_weight_specs(fixed):
    vec = pl.BlockSpec((1, RWKV_WIDTH), fixed)
    return [pl.BlockSpec((1, RWKV_COLS), fixed), vec, vec,
            pl.BlockSpec((LORA_WA, 2 * RWKV_WIDTH), fixed), pl.BlockSpec((G_LORA, RWKV_WIDTH), fixed),
            vec, vec, vec, vec, vec, pl.BlockSpec((RWKV_WIDTH, RWKV_WIDTH), fixed)]


def _cast_chunk_spec(rows, cols, steps, t_blocks):
    chunks = max(c for c in range(1, steps + 1) if rows % c == 0 and (rows // c) % (2 * SUBLANES) == 0)
    return pl.BlockSpec((rows // chunks, cols), lambda i, j: (jnp.minimum(i * t_blocks + j, chunks - 1), 0))


def _rwkv_prompt(pr, weights, to_cast=(), *, nb, tb):
    b, t, _ = pr.shape
    fixed = lambda i, j: (0, 0)
    steps = (b // nb) * (t // tb)
    cast_specs = [_cast_chunk_spec(*w.shape, steps, t // tb) for w in to_cast]
    return pl.pallas_call(
        functools.partial(_rwkv_prompt_kernel, nb, tb, len(to_cast)),
        grid=(b // nb, t // tb),
        in_specs=[pl.BlockSpec((nb, tb, RWKV_COLS), lambda i, j: (i, j, 0))] + _rwkv_weight_specs(fixed)
        + cast_specs,
        out_specs=[pl.BlockSpec((nb, tb, RWKV_WIDTH), lambda i, j: (i, j, 0)),
                   pl.BlockSpec((nb, RWKV_HEADS, RWKV_N, RWKV_N), lambda i, j: (i, 0, 0, 0))] + cast_specs,
        out_shape=[jax.ShapeDtypeStruct((b, t, RWKV_WIDTH), BF16),
                   jax.ShapeDtypeStruct((b, RWKV_HEADS, RWKV_N, RWKV_N), F32)]
        + [jax.ShapeDtypeStruct(w.shape, BF16) for w in to_cast],
        scratch_shapes=[pltpu.VMEM((nb, RWKV_HEADS // 2, 2 * RWKV_N, 2 * RWKV_N), F32),
                        pltpu.VMEM((nb, SUBLANES, RWKV_COLS), F32),
                        pltpu.VMEM((nb * tb, RWKV_WIDTH), F32)],
        compiler_params=_params(("arbitrary", "arbitrary")),
        name="rwkv_prompt",
    )(pr, *weights, *to_cast)


def _rwkv_step_kernel(pr_ref, sh_ref, s_ref, mu_ref, w0_ref, a0_ref, wwa_ref, g2_ref, kk_ref, ka_ref,
                      rk_ref, lnw_ref, lnb_ref, ones_ref, o_ref, s_out_ref, vec_scr, row_scr, y_scr):
    h = pl.program_id(0)
    n = RWKV_N
    batch = pr_ref.shape[0]

    @pl.when(h == 0)
    def _():
        p = pr_ref[...]
        xs = p + (sh_ref[...] - p) * mu_ref[...]
        r, k, v, e, kk, kka, g = _rwkv_prep(xs, w0_ref[...], a0_ref[...], wwa_ref[...], g2_ref[...],
                                             kk_ref[...], ka_ref[...], ones_ref[...])
        for j, x in enumerate((-kk, kka, k, jnp.exp(-e), r, v)):
            vec_scr[j] = x.T.reshape(RWKV_HEADS, n, batch)
        for j, x in enumerate((r, k, v, g)):
            row_scr[j] = x

    nkk, kka, k, decay, r = (vec_scr[j, h] for j in range(5))
    for vi in range(n):
        s = s_ref[vi]
        sa = jnp.sum(s * nkk, axis=0, keepdims=True)
        s = s * decay + sa * kka + vec_scr[5, h, vi:vi + 1, :] * k
        s_out_ref[vi] = s
        y_scr[h, vi:vi + 1, :] = jnp.sum(s * r, axis=0, keepdims=True)

    @pl.when(h == RWKV_HEADS - 1)
    def _():
        y = y_scr[...].reshape(RWKV_WIDTH, batch).T
        o_ref[...] = _rwkv_out(y, row_scr[0], row_scr[1], row_scr[2], row_scr[3], rk_ref[...], lnw_ref[...],
                               lnb_ref[...], ones_ref[...])


def _rwkv_step(pr, shift, s0_t, weights):
    b = pr.shape[0]
    fixed = lambda i: (0, 0)
    state_spec = pl.BlockSpec((None, RWKV_N, RWKV_N, b), lambda i: (i, 0, 0, 0))
    return pl.pallas_call(
        _rwkv_step_kernel,
        grid=(RWKV_HEADS,),
        in_specs=[pl.BlockSpec((b, RWKV_COLS), fixed), pl.BlockSpec((b, RWKV_COLS), fixed), state_spec]
        + _rwkv_weight_specs(fixed),
        out_specs=[pl.BlockSpec((b, RWKV_WIDTH), fixed), state_spec],
        out_shape=[jax.ShapeDtypeStruct((b, RWKV_WIDTH), F32), jax.ShapeDtypeStruct(s0_t.shape, F32)],
        scratch_shapes=[pltpu.VMEM((6, RWKV_HEADS, RWKV_N, b), F32), pltpu.VMEM((4, b, RWKV_WIDTH), F32),
                        pltpu.VMEM((RWKV_HEADS, RWKV_N, b), F32)],
        compiler_params=_params(("arbitrary",)),
        name="rwkv_step",
    )(pr, shift, s0_t, *weights)


def _pad_lanes(x, width):
    return jnp.pad(x, ((0, 0), (0, width - x.shape[1])))


def kernel(x_prompt, x_sample, state_gdn_conv, state_gdn, state_rwkv_shift, state_rwkv, norm_ffn1, w_ffn1_in, w_ffn1_out, norm_mix, w_in, gdn_conv_w, gdn_a_log, gdn_dt_bias, gdn_norm_w, rwkv_mu, rwkv_w0, rwkv_w2, rwkv_a0, rwkv_a2, rwkv_g2, rwkv_k_k, rwkv_k_a, rwkv_r_k, rwkv_ln_w, rwkv_ln_b, w_out, norm_ffn2, w_ffn2_in, w_ffn2_out, norm_final):
    depth = norm_ffn1.shape[0]
    assert depth == 1, "the carried-state plumbing below is written for a single layer"
    b, t, _ = x_prompt.shape
    bs = x_sample.shape[0]
    assert x_sample.shape[1] == 1
    l = 0

    n_gdn_cols = GDN_MAIN + 2 * GDN_HEADS
    w_in_l = w_in[l]
    w_slabs = (w_in_l[:, :GDN_MAIN].astype(BF16), w_in_l[:, n_gdn_cols:].astype(BF16),
               _pad_lanes(w_in_l[:, GDN_MAIN:n_gdn_cols], AB_PAD).astype(BF16))
    w1a, w1b = w_ffn1_in[l].astype(BF16), w_ffn1_out[l].astype(BF16)
    woa, wob = w_out[l, :GDN_WIDTH].astype(BF16), w_out[l, GDN_WIDTH:].astype(BF16)
    half = LORA_WA // 2
    wwa = jnp.zeros((LORA_WA, 2 * RWKV_WIDTH), F32)
    wwa = wwa.at[:half, :RWKV_WIDTH].set(rwkv_w2[l]).at[half:, RWKV_WIDTH:].set(rwkv_a2[l]).astype(BF16)
    head_id = jnp.arange(RWKV_WIDTH) // RWKV_N
    head_ones = (head_id[:, None] == head_id[None, :]).astype(BF16)
    rwkv_w = (rwkv_mu[l][None], rwkv_w0[l][None], rwkv_a0[l][None], wwa, rwkv_g2[l].astype(BF16),
              rwkv_k_k[l][None], rwkv_k_a[l][None], rwkv_r_k[l].reshape(1, RWKV_WIDTH),
              rwkv_ln_w[l][None], rwkv_ln_b[l][None], head_ones)
    alog = _pad_lanes(gdn_a_log[l][None], AB_PAD)
    dtb = _pad_lanes(gdn_dt_bias[l][None], AB_PAD)
    gdn_w = (gdn_conv_w[l], alog, dtb, gdn_norm_w[l][None])
    nf1, nmix, nf2, nfin = norm_ffn1[l][None], norm_mix[l][None], norm_ffn2[l][None], norm_final[None]

    xp, xs = x_prompt.reshape(b * t, D_MODEL), x_sample.reshape(bs, D_MODEL)
    hp, hs = _ffn(xp, xs, nf1, w1a, w1b, tm=DENSE_TILE_ROWS, sub=DENSE_SUB_ROWS)
    pg, pr, pab, sg, sr, sab = _proj_in(hp, hs, nmix, w_slabs, tm=DENSE_TILE_ROWS)

    pg3, pr3, pab3 = pg.reshape(b, t, GDN_MAIN), pr.reshape(b, t, RWKV_COLS), pab.reshape(b, t, AB_PAD)
    oa, gdn_s = _gdn_prompt(pg3, pab3, *gdn_w, nb=MIXER_SEQS, tb=MIXER_BLOCK_ROWS)
    ob, wkv_s, w2a, w2b = _rwkv_prompt(pr3, rwkv_w, (w_ffn2_in[l], w_ffn2_out[l]),
                                       nb=MIXER_SEQS, tb=MIXER_BLOCK_ROWS)

    conv_in = state_gdn_conv[l].reshape(bs, (CONV_W - 1) * GDN_QKV)
    oa_s, conv_s, gdn_ss = _gdn_step(sg, sab, conv_in, state_gdn[l], *gdn_w, bb=GDN_STEP_ROWS)
    ob_s, wkv_t = _rwkv_step(sr, state_rwkv_shift[l], jnp.transpose(state_rwkv[l], (1, 2, 3, 0)), rwkv_w)
    wkv_ss = jnp.transpose(wkv_t, (3, 0, 1, 2))

    yp, ys = _ffn(hp, hs, nf2, w2a, w2b, final_w=nfin, tm=DENSE_TILE_ROWS, sub=DENSE_SUB_ROWS,
                  mix=(oa.reshape(b * t, GDN_WIDTH), ob.reshape(b * t, RWKV_WIDTH), oa_s, ob_s, woa, wob))

    return (yp.reshape(b, t, D_MODEL), ys.reshape(bs, 1, D_MODEL),
            pg3[:, t - (CONV_W - 1):, :GDN_QKV][None], gdn_s[None], pr3[:, t - 1, :][None], wkv_s[None],
            conv_s.reshape(1, bs, CONV_W - 1, GDN_QKV), gdn_ss[None], sr[None], wkv_ss[None])
```

```python
import functools
import math

import jax
import jax.numpy as jnp
from jax import lax
from jax.experimental import pallas as pl
from jax.experimental.pallas import tpu as pltpu

F32 = jnp.float32
BF16 = jnp.bfloat16

D_MODEL = 1024
D_FF = 2816
CONV_W = 4
GDN_HEADS = 4
GDN_DK = 128
GDN_QK = GDN_HEADS * GDN_DK
GDN_WIDTH = GDN_HEADS * GDN_DK
GDN_QKV = 3 * GDN_WIDTH
GDN_MAIN = GDN_QKV + GDN_WIDTH
RWKV_HEADS = 8
RWKV_N = 64
RWKV_WIDTH = RWKV_HEADS * RWKV_N
LORA_WA = 128
G_LORA = 128
RWKV_COLS = 3 * RWKV_WIDTH + LORA_WA + G_LORA
LANES = 128
SUBLANES = 8
AB_PAD = LANES
NORM_EPS = 1e-6
GN_EPS = 64e-5
L2_EPS = 1e-6

GDN_CHUNK = 128
RWKV_CHUNK = 64
VMEM_LIMIT = 56 * 1024 * 1024
DENSE_TILE_ROWS = 1024
DENSE_SUB_ROWS = 256
MIXER_SEQS = 2
MIXER_BLOCK_ROWS = 256
GDN_STEP_ROWS = 16


_NN = (((1,), (0,)), ((), ()))
_NT = (((1,), (1,)), ((), ()))


def _mm(a, b, dims=_NN):
    return lax.dot_general(a.astype(BF16), b.astype(BF16), dims, preferred_element_type=F32)


def _mm_split3(sel, x):
    sel = sel.astype(BF16)
    out = None
    for _ in range(3):
        piece = x.astype(BF16)
        part = jnp.dot(sel, piece, preferred_element_type=F32)
        out = part if out is None else out + part
        x = x - piece.astype(F32)
    return out


def _sigmoid(x):
    return 1.0 / (1.0 + jnp.exp(-x))


def _silu(x):
    return x * _sigmoid(x)


def _softplus(x):
    return jnp.maximum(x, 0.0) + jnp.log(1.0 + jnp.exp(-jnp.abs(x)))


def _rms(x, g):
    return x * lax.rsqrt(jnp.mean(x * x, axis=-1, keepdims=True) + NORM_EPS) * g


def _iota2(shape, dim):
    return lax.broadcasted_iota(jnp.int32, shape, dim)


def _shift_rows(prev_tail, x, j):
    rolled = pltpu.roll(jnp.concatenate([prev_tail, x], axis=0), j, axis=0)
    return rolled[SUBLANES:, :]


def _params(sem):
    return pltpu.CompilerParams(dimension_semantics=sem, vmem_limit_bytes=VMEM_LIMIT)


def _resident(shape):
    return pl.BlockSpec(shape, lambda i: (0,) * len(shape), pipeline_mode=pl.Buffered(1))


def _ffn_rows(x_ref, mix_refs, w_refs, out_ref, sub):
    woa_ref, wob_ref, nw_ref, w1_ref, w2_ref, fw_ref = w_refs
    for s in range(x_ref.shape[0] // sub):
        rows = slice(s * sub, (s + 1) * sub)
        x = x_ref[rows, :]
        if mix_refs is not None:
            oa_ref, ob_ref = mix_refs
            x = x + _mm(oa_ref[rows, :], woa_ref[...]) + _mm(ob_ref[rows, :], wob_ref[...])
        xn = _rms(x, nw_ref[...]).astype(BF16)
        gate = jnp.dot(xn, w1_ref[:, 0:D_FF], preferred_element_type=F32)
        up = jnp.dot(xn, w1_ref[:, D_FF:2 * D_FF], preferred_element_type=F32)
        act = (_silu(gate) * up).astype(BF16)
        h = x + 0.5 * jnp.dot(act, w2_ref[...], preferred_element_type=F32)
        if fw_ref is not None:
            h = _rms(h, fw_ref[...])
        out_ref[rows, :] = h


def _ffn_kernel(has_mix, final_norm, sub, *refs):
    it = iter(refs)
    x_ref, xt_ref = next(it), next(it)
    mix_refs = tail_mix_refs = None
    woa_ref = wob_ref = None
    if has_mix:
        mix_refs, tail_mix_refs = (next(it), next(it)), (next(it), next(it))
        woa_ref, wob_ref = next(it), next(it)
    nw_ref, w1_ref, w2_ref = next(it), next(it), next(it)
    fw_ref = next(it) if final_norm else None
    out_ref, out_tail_ref = next(it), next(it)
    w_refs = (woa_ref, wob_ref, nw_ref, w1_ref, w2_ref, fw_ref)
    i = pl.program_id(0)
    last = pl.num_programs(0) - 1

    @pl.when(i < last)
    def _():
        _ffn_rows(x_ref, mix_refs, w_refs, out_ref, sub)

    @pl.when(i == last)
    def _():
        _ffn_rows(xt_ref, tail_mix_refs, w_refs, out_tail_ref, xt_ref.shape[0])


def _ffn(x, x_tail, norm_w, w1, w2, *, mix=None, final_w=None, tm, sub):
    m, mt = x.shape[0], x_tail.shape[0]
    n_tiles = m // tm
    has_mix = mix is not None
    final_norm = final_w is not None
    row = lambda i: (jnp.minimum(i, n_tiles - 1), 0)
    tail = lambda i: (0, 0)
    in_specs = [pl.BlockSpec((tm, D_MODEL), row), pl.BlockSpec((mt, D_MODEL), tail)]
    args = [x, x_tail]
    if has_mix:
        oa, ob, oa_tail, ob_tail, woa, wob = mix
        in_specs += [pl.BlockSpec((tm, GDN_WIDTH), row), pl.BlockSpec((tm, RWKV_WIDTH), row),
                     pl.BlockSpec((mt, GDN_WIDTH), tail), pl.BlockSpec((mt, RWKV_WIDTH), tail),
                     _resident((GDN_WIDTH, D_MODEL)), _resident((RWKV_WIDTH, D_MODEL))]
        args += [oa, ob, oa_tail, ob_tail, woa, wob]
    in_specs += [_resident((1, D_MODEL)), _resident((D_MODEL, 2 * D_FF)), _resident((D_FF, D_MODEL))]
    args += [norm_w, w1, w2]
    if final_norm:
        in_specs.append(_resident((1, D_MODEL)))
        args.append(final_w)
    return pl.pallas_call(
        functools.partial(_ffn_kernel, has_mix, final_norm, sub),
        grid=(n_tiles + 1,),
        in_specs=in_specs,
        out_specs=[pl.BlockSpec((tm, D_MODEL), row), pl.BlockSpec((mt, D_MODEL), tail)],
        out_shape=[jax.ShapeDtypeStruct((m, D_MODEL), F32), jax.ShapeDtypeStruct((mt, D_MODEL), F32)],
        compiler_params=_params(("arbitrary",)),
        name="ffn_mix" if has_mix else "ffn",
    )(*args)


def _l2norm(x):
    return x * lax.rsqrt(jnp.sum(x * x, axis=-1, keepdims=True) + L2_EPS)


def _proj_in_rows(h_ref, nw_ref, w_refs, out_refs):
    n = _rms(h_ref[...], nw_ref[...]).astype(BF16)
    for w_ref, out_ref in zip(w_refs, out_refs):
        out_ref[...] = jnp.dot(n, w_ref[...], preferred_element_type=F32)


def _proj_in_kernel(h_ref, ht_ref, nw_ref, wg_ref, wr_ref, wab_ref, pg_ref, pr_ref, pab_ref,
                    pgt_ref, prt_ref, pabt_ref):
    i = pl.program_id(0)
    last = pl.num_programs(0) - 1
    w_refs = (wg_ref, wr_ref, wab_ref)

    @pl.when(i < last)
    def _():
        _proj_in_rows(h_ref, nw_ref, w_refs, (pg_ref, pr_ref, pab_ref))

    @pl.when(i == last)
    def _():
        _proj_in_rows(ht_ref, nw_ref, w_refs, (pgt_ref, prt_ref, pabt_ref))


def _proj_in(h, h_tail, norm_w, w_slabs, *, tm):
    m, mt = h.shape[0], h_tail.shape[0]
    n_tiles = m // tm
    row = lambda i: (jnp.minimum(i, n_tiles - 1), 0)
    tail = lambda i: (0, 0)
    widths = (GDN_MAIN, RWKV_COLS, AB_PAD)
    return pl.pallas_call(
        _proj_in_kernel,
        grid=(n_tiles + 1,),
        in_specs=[pl.BlockSpec((tm, D_MODEL), row), pl.BlockSpec((mt, D_MODEL), tail), _resident((1, D_MODEL))]
        + [_resident((D_MODEL, w)) for w in widths],
        out_specs=[pl.BlockSpec((tm, w), row) for w in widths] + [pl.BlockSpec((mt, w), tail) for w in widths],
        out_shape=[jax.ShapeDtypeStruct((m, w), F32) for w in widths]
        + [jax.ShapeDtypeStruct((mt, w), F32) for w in widths],
        compiler_params=_params(("arbitrary",)),
        name="proj_in",
    )(h, h_tail, norm_w, *w_slabs)


def _inv_unit_lower(lows, stack=None):
    shape = lows[0].shape
    n = shape[0]
    ri = _iota2(shape, 0)
    ci = _iota2(shape, 1) & (n - 1)
    eye = jnp.where(ri == ci, 1.0, 0.0)
    pair_blk = (ri >> 1) == (ci >> 1)
    ts = [eye - jnp.where(pair_blk, low, 0.0) for low in lows]
    s, lg = 2, 1
    while s < n:
        sel = ((ri >> (lg + 1)) == (ci >> (lg + 1))) & ((ri >> lg) != (ci >> lg))
        offs = [jnp.where(sel, low, 0.0) for low in lows]
        if stack is None:
            xs = [_mm(t, off) for t, off in zip(ts, offs)]
            ts = [t - _mm(x, t) for t, x in zip(ts, xs)]
        else:
            xs = [_mm(t, stack(off)) for t, off in zip(ts, offs)]
            ts = [t - _mm(x, stack(t)) for t, x in zip(ts, xs)]
        s, lg = 2 * s, lg + 1
    return ts


def _pair_stack(m):
    first = _iota2((m, 2 * m), 1) < m

    def stack(z):
        z = z.astype(BF16)
        zero = jnp.zeros_like(z)
        return jnp.concatenate([jnp.where(first, z, zero), jnp.where(first, zero, z)], axis=0)

    return stack


def _inv_unit_lower_halves(lows):
    n = lows[0].shape[0]
    m = n // 2
    first = _iota2((m, n), 1) < m
    stack = _pair_stack(m)
    t_diag = _inv_unit_lower([jnp.where(first, low[0:m, :], low[m:n, :]) for low in lows], stack)
    lower_left = (_iota2((n, n), 0) >= m) & (_iota2((n, n), 1) < m)
    xs = [_mm(t, jnp.where(lower_left, low, 0.0)) for t, low in zip(t_diag, lows)]
    t_ll = [_mm(x, stack(t)) for x, t in zip(xs, t_diag)]
    return [jnp.concatenate([jnp.where(first, t, 0.0), jnp.where(first, 0.0, t) - ll], axis=0)
            for t, ll in zip(t_diag, t_ll)]


def _gdn_gates(ab, alog, dtb):
    log_alpha = -jnp.exp(alog) * _softplus(ab + dtb)
    return log_alpha, _sigmoid(ab)


def _gdn_out(o, z, norm_w):
    o = o * lax.rsqrt(jnp.mean(o * o, axis=-1, keepdims=True) + NORM_EPS) * norm_w
    return o * _silu(z)


def _gdn_prompt_kernel(nb, tb, pg_ref, pab_ref, cw_ref, alog_ref, dtb_ref, nw_ref, o_ref, s_out_ref,
                       s_scr, ext_scr):
    t = pl.program_id(1)
    c = GDN_CHUNK
    chunks_per_seq = tb // c

    @pl.when(t == 0)
    def _():
        s_scr[...] = jnp.zeros_like(s_scr)
        ext_scr[...] = jnp.zeros_like(ext_scr)

    accs = []
    for q in range(nb):
        u = pg_ref[q, :, 0:GDN_QKV]
        acc = u * cw_ref[CONV_W - 1:CONV_W, :]
        for j in range(1, CONV_W):
            acc = acc + _shift_rows(ext_scr[q], u, j) * cw_ref[CONV_W - 1 - j:CONV_W - j, :]
        ext_scr[q] = u[tb - SUBLANES:tb, :]
        accs.append(acc)
    qkv = _silu(jnp.concatenate(accs, axis=0))

    pab = jnp.concatenate([pab_ref[q] for q in range(nb)], axis=0)
    log_alpha, beta_all = _gdn_gates(pab, alog_ref[...], dtb_ref[...])
    ri = _iota2((c, c), 0)
    ci = _iota2((c, c), 1)
    causal = ri >= ci
    strict = ri > ci
    tri = jnp.where(causal, 1.0, 0.0)

    n_chunks = nb * chunks_per_seq
    items = [(ch, h) for ch in range(n_chunks) for h in range(GDN_HEADS)]
    g_blk, gt_blk = [], []
    for ch in range(n_chunks):
        g = _mm_split3(tri, log_alpha[ch * c:(ch + 1) * c, :])
        g_blk.append(g)
        gt_blk.append(g.T)
    qs, ks, vbs, kbs, decays, g_cols = [], [], [], [], [], []
    for ch, h in items:
        rows = slice(ch * c, (ch + 1) * c)
        q = _l2norm(qkv[rows, h * GDN_DK:(h + 1) * GDN_DK]) * (GDN_DK ** -0.5)
        k = _l2norm(qkv[rows, GDN_QK + h * GDN_DK:GDN_QK + (h + 1) * GDN_DK])
        v = qkv[rows, 2 * GDN_QK + h * GDN_DK:2 * GDN_QK + (h + 1) * GDN_DK]
        beta = beta_all[rows, GDN_HEADS + h:GDN_HEADS + h + 1]
        g_col = g_blk[ch][:, h:h + 1]
        g_row = gt_blk[ch][h:h + 1, :]
        decays.append(jnp.exp(jnp.where(causal, g_col - g_row, -jnp.inf)))
        qs.append(q)
        ks.append(k)
        vbs.append(v * beta)
        kbs.append(k * beta)
        g_cols.append(g_col)
    kk = [_mm(kb, k, _NT) for kb, k in zip(kbs, ks)]
    qk = [_mm(q, k, _NT) for q, k in zip(qs, ks)]
    tinvs = _inv_unit_lower_halves([jnp.where(strict, x * d, 0.0) for x, d in zip(kk, decays)])
    attns = [x * d for x, d in zip(qk, decays)]
    egs = [jnp.exp(g_col) for g_col in g_cols]
    us = [_mm(tinv, vb) for tinv, vb in zip(tinvs, vbs)]
    ws = [_mm(tinv, kb * eg) for tinv, kb, eg in zip(tinvs, kbs, egs)]
    g_lasts = [g_col[c - 1:c, :] for g_col in g_cols]
    kd_ts = [(k * jnp.exp(g_last - g_col)).T for k, g_last, g_col in zip(ks, g_lasts, g_cols)]
    q_hats = [q * eg - _mm(attn, w) for q, eg, attn, w in zip(qs, egs, attns, ws)]
    o0s = [_mm(attn, u) for attn, u in zip(attns, us)]
    s_mix = [_mm(kd_t, w) for kd_t, w in zip(kd_ts, ws)]
    s_add = [_mm(kd_t, u) for kd_t, u in zip(kd_ts, us)]

    chains = [(q, h) for q in range(nb) for h in range(GDN_HEADS)]
    states = [s_scr[q, h] for q, h in chains]
    for blk in range(chunks_per_seq):
        idx = [(q * chunks_per_seq + blk) * GDN_HEADS + h for q, h in chains]
        for (q, h), i, s in zip(chains, idx, states):
            o = o0s[i] + _mm(q_hats[i], s)
            rows = slice(blk * c, (blk + 1) * c)
            z = pg_ref[q, rows, GDN_QKV + h * GDN_DK:GDN_QKV + (h + 1) * GDN_DK]
            o_ref[q, rows, h * GDN_DK:(h + 1) * GDN_DK] = _gdn_out(o, z, nw_ref[...]).astype(o_ref.dtype)
        states = [s * jnp.exp(g_lasts[i]) - _mm(s_mix[i], s) + s_add[i] for i, s in zip(idx, states)]
    for (q, h), s in zip(chains, states):
        s_scr[q, h] = s

    @pl.when(t == pl.num_programs(1) - 1)
    def _():
        s_out_ref[...] = s_scr[...]


def _gdn_prompt(pg, pab, conv_w, alog, dtb, norm_w, *, nb, tb):
    b, t, _ = pg.shape
    fixed = lambda i, j: (0, 0)
    return pl.pallas_call(
        functools.partial(_gdn_prompt_kernel, nb, tb),
        grid=(b // nb, t // tb),
        in_specs=[pl.BlockSpec((nb, tb, GDN_MAIN), lambda i, j: (i, j, 0)),
                  pl.BlockSpec((nb, tb, AB_PAD), lambda i, j: (i, j, 0)),
                  pl.BlockSpec((CONV_W, GDN_QKV), fixed), pl.BlockSpec((1, AB_PAD), fixed),
                  pl.BlockSpec((1, AB_PAD), fixed), pl.BlockSpec((1, GDN_DK), fixed)],
        out_specs=[pl.BlockSpec((nb, tb, GDN_WIDTH), lambda i, j: (i, j, 0)),
                   pl.BlockSpec((nb, GDN_HEADS, GDN_DK, GDN_DK), lambda i, j: (i, 0, 0, 0))],
        out_shape=[jax.ShapeDtypeStruct((b, t, GDN_WIDTH), BF16),
                   jax.ShapeDtypeStruct((b, GDN_HEADS, GDN_DK, GDN_DK), F32)],
        scratch_shapes=[pltpu.VMEM((nb, GDN_HEADS, GDN_DK, GDN_DK), F32),
                        pltpu.VMEM((nb, SUBLANES, GDN_QKV), F32)],
        compiler_params=_params(("parallel", "arbitrary")),
        name="gdn_prompt",
    )(pg, pab, conv_w, alog, dtb, norm_w)


def _gdn_step_kernel(bb, pg_ref, pab_ref, cs_ref, s_ref, cw_ref, alog_ref, dtb_ref, nw_ref,
                     o_ref, cs_out_ref, s_out_ref):
    u = pg_ref[:, 0:GDN_QKV]
    acc = u * cw_ref[CONV_W - 1:CONV_W, :]
    for j in range(CONV_W - 1):
        acc = acc + cs_ref[:, j * GDN_QKV:(j + 1) * GDN_QKV] * cw_ref[j:j + 1, :]
    cs_out_ref[:, 0:(CONV_W - 2) * GDN_QKV] = cs_ref[:, GDN_QKV:(CONV_W - 1) * GDN_QKV]
    cs_out_ref[:, (CONV_W - 2) * GDN_QKV:(CONV_W - 1) * GDN_QKV] = u
    qkv = _silu(acc)
    log_alpha, beta_all = _gdn_gates(pab_ref[...], alog_ref[...], dtb_ref[...])
    alpha_all = jnp.exp(log_alpha)
    eye = jnp.where(_iota2((GDN_DK, GDN_DK), 0) == _iota2((GDN_DK, GDN_DK), 1), 1.0, 0.0)

    def to_col(row):
        return jnp.sum(eye * row, axis=-1, keepdims=True)

    vs, k_cols, q_cols = [], [], []
    for h in range(GDN_HEADS):
        q = qkv[:, h * GDN_DK:(h + 1) * GDN_DK]
        k = qkv[:, GDN_QK + h * GDN_DK:GDN_QK + (h + 1) * GDN_DK]
        vs.append(qkv[:, 2 * GDN_QK + h * GDN_DK:2 * GDN_QK + (h + 1) * GDN_DK])
        q = _l2norm(q) * (GDN_DK ** -0.5)
        k = _l2norm(k)
        k_cols.append([to_col(k[i:i + 1, :]) for i in range(bb)])
        q_cols.append([to_col(q[i:i + 1, :]) for i in range(bb)])
    for h in range(GDN_HEADS):
        o_rows = []
        for i in range(bb):
            s = s_ref[i, h] * alpha_all[i:i + 1, h:h + 1]
            mem = jnp.sum(k_cols[h][i] * s, axis=0, keepdims=True)
            delta = (vs[h][i:i + 1, :] - mem) * beta_all[i:i + 1, GDN_HEADS + h:GDN_HEADS + h + 1]
            s = s + k_cols[h][i] * delta
            s_out_ref[i, h] = s
            o_rows.append(jnp.sum(q_cols[h][i] * s, axis=0, keepdims=True))
        o = jnp.concatenate(o_rows, axis=0)
        z = pg_ref[:, GDN_QKV + h * GDN_DK:GDN_QKV + (h + 1) * GDN_DK]
        o_ref[:, h * GDN_DK:(h + 1) * GDN_DK] = _gdn_out(o, z, nw_ref[...])


def _gdn_step(pg, pab, conv_state, s0, conv_w, alog, dtb, norm_w, *, bb):
    b = pg.shape[0]
    row = lambda i: (i, 0)
    fixed = lambda i: (0, 0)
    cs_cols = (CONV_W - 1) * GDN_QKV
    state_spec = pl.BlockSpec((bb, GDN_HEADS, GDN_DK, GDN_DK), lambda i: (i, 0, 0, 0))
    return pl.pallas_call(
        functools.partial(_gdn_step_kernel, bb),
        grid=(b // bb,),
        in_specs=[pl.BlockSpec((bb, GDN_MAIN), row), pl.BlockSpec((bb, AB_PAD), row),
                  pl.BlockSpec((bb, cs_cols), row), state_spec,
                  pl.BlockSpec((CONV_W, GDN_QKV), fixed), pl.BlockSpec((1, AB_PAD), fixed),
                  pl.BlockSpec((1, AB_PAD), fixed), pl.BlockSpec((1, GDN_DK), fixed)],
        out_specs=[pl.BlockSpec((bb, GDN_WIDTH), row), pl.BlockSpec((bb, cs_cols), row), state_spec],
        out_shape=[jax.ShapeDtypeStruct((b, GDN_WIDTH), F32), jax.ShapeDtypeStruct((b, cs_cols), F32),
                   jax.ShapeDtypeStruct(s0.shape, F32)],
        compiler_params=_params(("parallel",)),
        name="gdn_step",
    )(pg, pab, conv_state, s0, conv_w, alog, dtb, norm_w)


def _head_sums(x, head_ones):
    width = 2 * LANES
    return jnp.concatenate([_mm(x[:, lo:lo + width], head_ones[lo:lo + width, lo:lo + width])
                            for lo in range(0, RWKV_WIDTH, width)], axis=1)


def _rwkv_prep(xs, w0, a0, wwa, g2, k_k, k_a, head_ones):
    w = RWKV_WIDTH
    r, k, v = xs[:, 0:w], xs[:, w:2 * w], xs[:, 2 * w:3 * w]
    wa_in = xs[:, 3 * w:3 * w + LORA_WA]
    lane = _iota2(wa_in.shape, 1)
    wa_in = jnp.where(lane < LORA_WA // 2, jnp.tanh(wa_in), wa_in)
    wa = _mm(wa_in, wwa)
    e = math.exp(-0.5) * _sigmoid(w0 + wa[:, 0:w])
    a = _sigmoid(a0 + wa[:, w:2 * w])
    g = _mm(_sigmoid(xs[:, 3 * w + LORA_WA:RWKV_COLS]), g2)
    kx = k * k_k
    kk = kx * lax.rsqrt(_head_sums(kx * kx, head_ones) + L2_EPS)
    k = k * (1.0 + (a - 1.0) * k_a)
    return r, k, v, e, kk, kk * a, g


def _rwkv_out(y, r, k, v, g, r_k, ln_w, ln_b, head_ones):
    inv_n = 1.0 / RWKV_N
    mean = _head_sums(y, head_ones) * inv_n
    yc = y - mean
    var = _head_sums(yc * yc, head_ones) * inv_n
    y = yc * lax.rsqrt(var + GN_EPS) * ln_w + ln_b
    bonus = _head_sums(r * k * r_k, head_ones) * v
    return (y + bonus) * g


def _rwkv_prompt_kernel(nb, tb, n_cast, pr_ref, mu_ref, w0_ref, a0_ref, wwa_ref, g2_ref, kk_ref, ka_ref, rk_ref,
                        lnw_ref, lnb_ref, ones_ref, *refs):
    cast_in, (o_ref, s_out_ref), cast_out = refs[:n_cast], refs[n_cast:n_cast + 2], refs[n_cast + 2:2 * n_cast + 2]
    s_scr, ext_scr, y_scr = refs[2 * n_cast + 2:]
    for src, dst in zip(cast_in, cast_out):
        dst[...] = src[...].astype(BF16)
    t = pl.program_id(1)
    c = RWKV_CHUNK
    n = RWKV_N
    pair = 2 * n
    lg_c = c.bit_length() - 1
    lg_n = n.bit_length() - 1
    n_pairs = RWKV_HEADS // 2
    chunks_per_seq = tb // c

    @pl.when(t == 0)
    def _():
        s_scr[...] = jnp.zeros_like(s_scr)
        ext_scr[...] = jnp.zeros_like(ext_scr)

    prevs = []
    for q in range(nb):
        prevs.append(_shift_rows(ext_scr[q], pr_ref[q], 1))
        ext_scr[q] = pr_ref[q, tb - SUBLANES:tb, :]
    p = jnp.concatenate([pr_ref[q] for q in range(nb)], axis=0)
    xs = p + (jnp.concatenate(prevs, axis=0) - p) * mu_ref[...]
    head_ones = ones_ref[...]
    r, k, v, e, kk, kka, g = _rwkv_prep(xs, w0_ref[...], a0_ref[...], wwa_ref[...], g2_ref[...],
                                         kk_ref[...], ka_ref[...], head_ones)

    ri = _iota2((tb, tb), 0)
    ci = _iota2((tb, tb), 1)
    tri = jnp.where((ri >= ci) & ((ri >> lg_c) == (ci >> lg_c)), 1.0, 0.0)
    lg_inc = -jnp.concatenate([_mm_split3(tri, e[q * tb:(q + 1) * tb, :]) for q in range(nb)], axis=0)
    lg_exc = lg_inc + e
    a_t = -kk * jnp.exp(lg_exc)
    r_t = r * jnp.exp(lg_inc)
    inv_g = jnp.exp(-lg_inc)
    b_t = kka * inv_g
    k_t = k * inv_g

    assert c == n, "two heads side by side fill one (chunk, 2 * chunk) score tile"
    stack = _pair_stack(n)

    tt = _iota2((c, pair), 0)
    ss = _iota2((c, pair), 1) & (n - 1)
    strict = tt > ss
    incl = tt >= ss
    bd = jnp.where((_iota2((pair, pair), 0) >> lg_n) == (_iota2((pair, pair), 1) >> lg_n), 1.0, 0.0)

    items = [(ch, hp) for ch in range(nb * chunks_per_seq) for hp in range(n_pairs)]

    def tile(x, ch, hp):
        return x[ch * c:(ch + 1) * c, hp * pair:(hp + 1) * pair]

    ats = [tile(a_t, *it) for it in items]
    rts = [tile(r_t, *it) for it in items]
    vts = [tile(v, *it) for it in items]
    svs = [stack(vt) for vt in vts]
    scs = [_mm(jnp.concatenate([at, rt], axis=0),
               jnp.concatenate([stack(tile(b_t, *it)), stack(tile(k_t, *it))], axis=0), _NT)
           for at, rt, it in zip(ats, rts, items)]
    tinvs = _inv_unit_lower([-jnp.where(strict, sc[0:c, 0:pair], 0.0) for sc in scs], stack)
    v_mix = [_mm(jnp.concatenate([jnp.where(strict, sc[0:c, pair:2 * pair], 0.0),
                                  jnp.where(incl, sc[c:2 * c, pair:2 * pair], 0.0)], axis=0), sv)
             for sc, sv in zip(scs, svs)]
    akv = [x[0:c, :] for x in v_mix]
    y_v = [x[c:2 * c, :] for x in v_mix]
    p_rbs = [jnp.where(incl, sc[c:2 * c, 0:pair], 0.0) for sc in scs]
    wu = [_mm(tinv, jnp.concatenate([stack(at), stack(x)], axis=1)) for tinv, at, x in zip(tinvs, ats, akv)]
    w_ts = [x[:, 0:pair] for x in wu]
    u0s = [x[:, pair:2 * pair] for x in wu]
    lg_lasts = [lg_inc[ch * c + c - 1:ch * c + c, hp * pair:(hp + 1) * pair] for ch, hp in items]
    to_ends = [jnp.exp(lg_last - tile(lg_inc, *it)) for lg_last, it in zip(lg_lasts, items)]
    bhs = [tile(kka, *it) * to_end for it, to_end in zip(items, to_ends)]
    khs = [tile(k, *it) * to_end for it, to_end in zip(items, to_ends)]
    ry = [_mm(p_rb, jnp.concatenate([stack(w_t), stack(u0)], axis=1)) for p_rb, w_t, u0 in zip(p_rbs, w_ts, u0s)]
    r_hats = [rt + x[:, 0:pair] for rt, x in zip(rts, ry)]
    y0s = [x[:, pair:2 * pair] + yv for x, yv in zip(ry, y_v)]
    s_mix = [bd * _mm(w_t.T, bh) for w_t, bh in zip(w_ts, bhs)]
    s_add = [bd * _mm(jnp.concatenate([u0, vt], axis=0).T, jnp.concatenate([bh, kh], axis=0))
             for u0, vt, bh, kh in zip(u0s, vts, bhs, khs)]

    chains = [(q, hp) for q in range(nb) for hp in range(n_pairs)]
    states = [s_scr[q, hp] for q, hp in chains]
    for blk in range(chunks_per_seq):
        idx = [(q * chunks_per_seq + blk) * n_pairs + hp for q, hp in chains]
        for (q, hp), i, s in zip(chains, idx, states):
            row0 = (q * chunks_per_seq + blk) * c
            y_scr[row0:row0 + c, hp * pair:(hp + 1) * pair] = y0s[i] + _mm(r_hats[i], s, _NT)
        states = [s * jnp.exp(lg_lasts[i]) + _mm(s, s_mix[i]) + s_add[i] for i, s in zip(idx, states)]
    for (q, hp), s in zip(chains, states):
        s_scr[q, hp] = s

    out = _rwkv_out(y_scr[...], r, k, v, g, rk_ref[...], lnw_ref[...], lnb_ref[...], head_ones)
    for q in range(nb):
        o_ref[q] = out[q * tb:(q + 1) * tb, :].astype(o_ref.dtype)

    @pl.when(t == pl.num_programs(1) - 1)
    def _():
        for q, hp in chains:
            s = s_scr[q, hp]
            s_out_ref[q, 2 * hp] = s[0:n, 0:n]
            s_out_ref[q, 2 * hp + 1] = s[n:pair, n:pair]


def _rwkv_weight_specs(fixed):
    vec = pl.BlockSpec((1, RWKV_WIDTH), fixed)
    return [pl.BlockSpec((1, RWKV_COLS), fixed), vec, vec,
            pl.BlockSpec((LORA_WA, 2 * RWKV_WIDTH), fixed), pl.BlockSpec((G_LORA, RWKV_WIDTH), fixed),
            vec, vec, vec, vec, vec, pl.BlockSpec((RWKV_WIDTH, RWKV_WIDTH), fixed)]


def _cast_chunk_spec(rows, cols, steps, t_blocks):
    chunks = max(c for c in range(1, steps + 1) if rows % c == 0 and (rows // c) % (2 * SUBLANES) == 0)
    return pl.BlockSpec((rows // chunks, cols), lambda i, j: (jnp.minimum(i * t_blocks + j, chunks - 1), 0))


def _rwkv_prompt(pr, weights, to_cast=(), *, nb, tb):
    b, t, _ = pr.shape
    fixed = lambda i, j: (0, 0)
    steps = (b // nb) * (t // tb)
    cast_specs = [_cast_chunk_spec(*w.shape, steps, t // tb) for w in to_cast]
    return pl.pallas_call(
        functools.partial(_rwkv_prompt_kernel, nb, tb, len(to_cast)),
        grid=(b // nb, t // tb),
        in_specs=[pl.BlockSpec((nb, tb, RWKV_COLS), lambda i, j: (i, j, 0))] + _rwkv_weight_specs(fixed)
        + cast_specs,
        out_specs=[pl.BlockSpec((nb, tb, RWKV_WIDTH), lambda i, j: (i, j, 0)),
                   pl.BlockSpec((nb, RWKV_HEADS, RWKV_N, RWKV_N), lambda i, j: (i, 0, 0, 0))] + cast_specs,
        out_shape=[jax.ShapeDtypeStruct((b, t, RWKV_WIDTH), BF16),
                   jax.ShapeDtypeStruct((b, RWKV_HEADS, RWKV_N, RWKV_N), F32)]
        + [jax.ShapeDtypeStruct(w.shape, BF16) for w in to_cast],
        scratch_shapes=[pltpu.VMEM((nb, RWKV_HEADS // 2, 2 * RWKV_N, 2 * RWKV_N), F32),
                        pltpu.VMEM((nb, SUBLANES, RWKV_COLS), F32),
                        pltpu.VMEM((nb * tb, RWKV_WIDTH), F32)],
        compiler_params=_params(("arbitrary", "arbitrary")),
        name="rwkv_prompt",
    )(pr, *weights, *to_cast)


def _rwkv_step_kernel(pr_ref, sh_ref, s_ref, mu_ref, w0_ref, a0_ref, wwa_ref, g2_ref, kk_ref, ka_ref,
                      rk_ref, lnw_ref, lnb_ref, ones_ref, o_ref, s_out_ref, vec_scr, row_scr, y_scr):
    h = pl.program_id(0)
    n = RWKV_N
    batch = pr_ref.shape[0]

    @pl.when(h == 0)
    def _():
        p = pr_ref[...]
        xs = p + (sh_ref[...] - p) * mu_ref[...]
        r, k, v, e, kk, kka, g = _rwkv_prep(xs, w0_ref[...], a0_ref[...], wwa_ref[...], g2_ref[...],
                                             kk_ref[...], ka_ref[...], ones_ref[...])
        for j, x in enumerate((-kk, kka, k, jnp.exp(-e), r, v)):
            vec_scr[j] = x.T.reshape(RWKV_HEADS, n, batch)
        for j, x in enumerate((r, k, v, g)):
            row_scr[j] = x

    nkk, kka, k, decay, r = (vec_scr[j, h] for j in range(5))
    for vi in range(n):
        s = s_ref[vi]
        sa = jnp.sum(s * nkk, axis=0, keepdims=True)
        s = s * decay + sa * kka + vec_scr[5, h, vi:vi + 1, :] * k
        s_out_ref[vi] = s
        y_scr[h, vi:vi + 1, :] = jnp.sum(s * r, axis=0, keepdims=True)

    @pl.when(h == RWKV_HEADS - 1)
    def _():
        y = y_scr[...].reshape(RWKV_WIDTH, batch).T
        o_ref[...] = _rwkv_out(y, row_scr[0], row_scr[1], row_scr[2], row_scr[3], rk_ref[...], lnw_ref[...],
                               lnb_ref[...], ones_ref[...])


def _rwkv_step(pr, shift, s0_t, weights):
    b = pr.shape[0]
    fixed = lambda i: (0, 0)
    state_spec = pl.BlockSpec((None, RWKV_N, RWKV_N, b), lambda i: (i, 0, 0, 0))
    return pl.pallas_call(
        _rwkv_step_kernel,
        grid=(RWKV_HEADS,),
        in_specs=[pl.BlockSpec((b, RWKV_COLS), fixed), pl.BlockSpec((b, RWKV_COLS), fixed), state_spec]
        + _rwkv_weight_specs(fixed),
        out_specs=[pl.BlockSpec((b, RWKV_WIDTH), fixed), state_spec],
        out_shape=[jax.ShapeDtypeStruct((b, RWKV_WIDTH), F32), jax.ShapeDtypeStruct(s0_t.shape, F32)],
        scratch_shapes=[pltpu.VMEM((6, RWKV_HEADS, RWKV_N, b), F32), pltpu.VMEM((4, b, RWKV_WIDTH), F32),
                        pltpu.VMEM((RWKV_HEADS, RWKV_N, b), F32)],
        compiler_params=_params(("arbitrary",)),
        name="rwkv_step",
    )(pr, shift, s0_t, *weights)


def _pad_lanes(x, width):
    return jnp.pad(x, ((0, 0), (0, width - x.shape[1])))


def kernel(x_prompt, x_sample, state_gdn_conv, state_gdn, state_rwkv_shift, state_rwkv, norm_ffn1, w_ffn1_in, w_ffn1_out, norm_mix, w_in, gdn_conv_w, gdn_a_log, gdn_dt_bias, gdn_norm_w, rwkv_mu, rwkv_w0, rwkv_w2, rwkv_a0, rwkv_a2, rwkv_g2, rwkv_k_k, rwkv_k_a, rwkv_r_k, rwkv_ln_w, rwkv_ln_b, w_out, norm_ffn2, w_ffn2_in, w_ffn2_out, norm_final):
    depth = norm_ffn1.shape[0]
    assert depth == 1, "the carried-state plumbing below is written for a single layer"
    b, t, _ = x_prompt.shape
    bs = x_sample.shape[0]
    assert x_sample.shape[1] == 1
    l = 0

    n_gdn_cols = GDN_MAIN + 2 * GDN_HEADS
    w_in_l = w_in[l]
    w_slabs = (w_in_l[:, :GDN_MAIN].astype(BF16), w_in_l[:, n_gdn_cols:].astype(BF16),
               _pad_lanes(w_in_l[:, GDN_MAIN:n_gdn_cols], AB_PAD).astype(BF16))
    w1a, w1b = w_ffn1_in[l].astype(BF16), w_ffn1_out[l].astype(BF16)
    woa, wob = w_out[l, :GDN_WIDTH].astype(BF16), w_out[l, GDN_WIDTH:].astype(BF16)
    half = LORA_WA // 2
    wwa = jnp.zeros((LORA_WA, 2 * RWKV_WIDTH), F32)
    wwa = wwa.at[:half, :RWKV_WIDTH].set(rwkv_w2[l]).at[half:, RWKV_WIDTH:].set(rwkv_a2[l]).astype(BF16)
    head_id = jnp.arange(RWKV_WIDTH) // RWKV_N
    head_ones = (head_id[:, None] == head_id[None, :]).astype(BF16)
    rwkv_w = (rwkv_mu[l][None], rwkv_w0[l][None], rwkv_a0[l][None], wwa, rwkv_g2[l].astype(BF16),
              rwkv_k_k[l][None], rwkv_k_a[l][None], rwkv_r_k[l].reshape(1, RWKV_WIDTH),
              rwkv_ln_w[l][None], rwkv_ln_b[l][None], head_ones)
    alog = _pad_lanes(gdn_a_log[l][None], AB_PAD)
    dtb = _pad_lanes(gdn_dt_bias[l][None], AB_PAD)
    gdn_w = (gdn_conv_w[l], alog, dtb, gdn_norm_w[l][None])
    nf1, nmix, nf2, nfin = norm_ffn1[l][None], norm_mix[l][None], norm_ffn2[l][None], norm_final[None]

    xp, xs = x_prompt.reshape(b * t, D_MODEL), x_sample.reshape(bs, D_MODEL)
    hp, hs = _ffn(xp, xs, nf1, w1a, w1b, tm=DENSE_TILE_ROWS, sub=DENSE_SUB_ROWS)
    pg, pr, pab, sg, sr, sab = _proj_in(hp, hs, nmix, w_slabs, tm=DENSE_TILE_ROWS)

    pg3, pr3, pab3 = pg.reshape(b, t, GDN_MAIN), pr.reshape(b, t, RWKV_COLS), pab.reshape(b, t, AB_PAD)
    oa, gdn_s = _gdn_prompt(pg3, pab3, *gdn_w, nb=MIXER_SEQS, tb=MIXER_BLOCK_ROWS)
    ob, wkv_s, w2a, w2b = _rwkv_prompt(pr3, rwkv_w, (w_ffn2_in[l], w_ffn2_out[l]),
                                       nb=MIXER_SEQS, tb=MIXER_BLOCK_ROWS)

    conv_in = state_gdn_conv[l].reshape(bs, (CONV_W - 1) * GDN_QKV)
    oa_s, conv_s, gdn_ss = _gdn_step(sg, sab, conv_in, state_gdn[l], *gdn_w, bb=GDN_STEP_ROWS)
    ob_s, wkv_t = _rwkv_step(sr, state_rwkv_shift[l], jnp.transpose(state_rwkv[l], (1, 2, 3, 0)), rwkv_w)
    wkv_ss = jnp.transpose(wkv_t, (3, 0, 1, 2))

    yp, ys = _ffn(hp, hs, nf2, w2a, w2b, final_w=nfin, tm=DENSE_TILE_ROWS, sub=DENSE_SUB_ROWS,
                  mix=(oa.reshape(b * t, GDN_WIDTH), ob.reshape(b * t, RWKV_WIDTH), oa_s, ob_s, woa, wob))

    return (yp.reshape(b, t, D_MODEL), ys.reshape(bs, 1, D_MODEL),
            pg3[:, t - (CONV_W - 1):, :GDN_QKV][None], gdn_s[None], pr3[:, t - 1, :][None], wkv_s[None],
            conv_s.reshape(1, bs, CONV_W - 1, GDN_QKV), gdn_ss[None], sr[None], wkv_ss[None])
```

```python
import functools
import math

import jax
import jax.numpy as jnp
from jax import lax
from jax.experimental import pallas as pl
from jax.experimental.pallas import tpu as pltpu

F32 = jnp.float32
BF16 = jnp.bfloat16

D_MODEL = 1024
D_FF = 2816
CONV_W = 4
GDN_HEADS = 4
GDN_DK = 128
GDN_QK = GDN_HEADS * GDN_DK
GDN_WIDTH = GDN_HEADS * GDN_DK
GDN_QKV = 3 * GDN_WIDTH
GDN_MAIN = GDN_QKV + GDN_WIDTH
RWKV_HEADS = 8
RWKV_N = 64
RWKV_WIDTH = RWKV_HEADS * RWKV_N
LORA_WA = 128
G_LORA = 128
RWKV_COLS = 3 * RWKV_WIDTH + LORA_WA + G_LORA
LANES = 128
SUBLANES = 8
AB_PAD = LANES
NORM_EPS = 1e-6
GN_EPS = 64e-5
L2_EPS = 1e-6

GDN_CHUNK = 128
RWKV_CHUNK = 64
VMEM_LIMIT = 56 * 1024 * 1024
DENSE_TILE_ROWS = 1024
DENSE_SUB_ROWS = 256
MIXER_SEQS = 2
MIXER_BLOCK_ROWS = 256
GDN_STEP_ROWS = 16


_NN = (((1,), (0,)), ((), ()))
_NT = (((1,), (1,)), ((), ()))


def _mm(a, b, dims=_NN):
    return lax.dot_general(a.astype(BF16), b.astype(BF16), dims, preferred_element_type=F32)


def _mm_split3(a, b, sel_first=True, dims=_NN):
    sel, x = (a, b) if sel_first else (b, a)
    sel = sel.astype(BF16)
    out = None
    for _ in range(3):
        piece = x.astype(BF16)
        part = lax.dot_general(*((sel, piece) if sel_first else (piece, sel)), dims, preferred_element_type=F32)
        out = part if out is None else out + part
        x = x - piece.astype(F32)
    return out


def _sigmoid(x):
    return 1.0 / (1.0 + jnp.exp(-x))


def _silu(x):
    return x * _sigmoid(x)


def _softplus(x):
    return jnp.maximum(x, 0.0) + jnp.log(1.0 + jnp.exp(-jnp.abs(x)))


def _rms(x, g):
    return x * lax.rsqrt(jnp.mean(x * x, axis=-1, keepdims=True) + NORM_EPS) * g


def _iota2(shape, dim):
    return lax.broadcasted_iota(jnp.int32, shape, dim)


def _shift_rows(prev_tail, x, j):
    rolled = pltpu.roll(jnp.concatenate([prev_tail, x], axis=0), j, axis=0)
    return rolled[SUBLANES:, :]


def _params(sem):
    return pltpu.CompilerParams(dimension_semantics=sem, vmem_limit_bytes=VMEM_LIMIT)


def _resident(shape):
    return pl.BlockSpec(shape, lambda i: (0,) * len(shape), pipeline_mode=pl.Buffered(1))


def _ffn_rows(x_ref, mix_refs, w_refs, out_ref, sub):
    woa_ref, wob_ref, nw_ref, w1_ref, w2_ref, fw_ref = w_refs
    for s in range(x_ref.shape[0] // sub):
        rows = slice(s * sub, (s + 1) * sub)
        x = x_ref[rows, :]
        if mix_refs is not None:
            oa_ref, ob_ref = mix_refs
            x = x + _mm(oa_ref[rows, :], woa_ref[...]) + _mm(ob_ref[rows, :], wob_ref[...])
        xn = _rms(x, nw_ref[...]).astype(BF16)
        gate = jnp.dot(xn, w1_ref[:, 0:D_FF], preferred_element_type=F32)
        up = jnp.dot(xn, w1_ref[:, D_FF:2 * D_FF], preferred_element_type=F32)
        act = (_silu(gate) * up).astype(BF16)
        h = x + 0.5 * jnp.dot(act, w2_ref[...], preferred_element_type=F32)
        if fw_ref is not None:
            h = _rms(h, fw_ref[...])
        out_ref[rows, :] = h


def _ffn_kernel(has_mix, final_norm, sub, *refs):
    it = iter(refs)
    x_ref, xt_ref = next(it), next(it)
    mix_refs = tail_mix_refs = None
    woa_ref = wob_ref = None
    if has_mix:
        mix_refs, tail_mix_refs = (next(it), next(it)), (next(it), next(it))
        woa_ref, wob_ref = next(it), next(it)
    nw_ref, w1_ref, w2_ref = next(it), next(it), next(it)
    fw_ref = next(it) if final_norm else None
    out_ref, out_tail_ref = next(it), next(it)
    w_refs = (woa_ref, wob_ref, nw_ref, w1_ref, w2_ref, fw_ref)
    i = pl.program_id(0)
    last = pl.num_programs(0) - 1

    @pl.when(i < last)
    def _():
        _ffn_rows(x_ref, mix_refs, w_refs, out_ref, sub)

    @pl.when(i == last)
    def _():
        _ffn_rows(xt_ref, tail_mix_refs, w_refs, out_tail_ref, xt_ref.shape[0])


def _ffn(x, x_tail, norm_w, w1, w2, *, mix=None, final_w=None, tm, sub):
    m, mt = x.shape[0], x_tail.shape[0]
    n_tiles = m // tm
    has_mix = mix is not None
    final_norm = final_w is not None
    row = lambda i: (jnp.minimum(i, n_tiles - 1), 0)
    tail = lambda i: (0, 0)
    in_specs = [pl.BlockSpec((tm, D_MODEL), row), pl.BlockSpec((mt, D_MODEL), tail)]
    args = [x, x_tail]
    if has_mix:
        oa, ob, oa_tail, ob_tail, woa, wob = mix
        in_specs += [pl.BlockSpec((tm, GDN_WIDTH), row), pl.BlockSpec((tm, RWKV_WIDTH), row),
                     pl.BlockSpec((mt, GDN_WIDTH), tail), pl.BlockSpec((mt, RWKV_WIDTH), tail),
                     _resident((GDN_WIDTH, D_MODEL)), _resident((RWKV_WIDTH, D_MODEL))]
        args += [oa, ob, oa_tail, ob_tail, woa, wob]
    in_specs += [_resident((1, D_MODEL)), _resident((D_MODEL, 2 * D_FF)), _resident((D_FF, D_MODEL))]
    args += [norm_w, w1, w2]
    if final_norm:
        in_specs.append(_resident((1, D_MODEL)))
        args.append(final_w)
    return pl.pallas_call(
        functools.partial(_ffn_kernel, has_mix, final_norm, sub),
        grid=(n_tiles + 1,),
        in_specs=in_specs,
        out_specs=[pl.BlockSpec((tm, D_MODEL), row), pl.BlockSpec((mt, D_MODEL), tail)],
        out_shape=[jax.ShapeDtypeStruct((m, D_MODEL), F32), jax.ShapeDtypeStruct((mt, D_MODEL), F32)],
        compiler_params=_params(("arbitrary",)),
        name="ffn_mix" if has_mix else "ffn",
    )(*args)


def _l2norm(x):
    return x * lax.rsqrt(jnp.sum(x * x, axis=-1, keepdims=True) + L2_EPS)


def _proj_in_rows(h_ref, nw_ref, w_refs, out_refs):
    n = _rms(h_ref[...], nw_ref[...]).astype(BF16)
    for w_ref, out_ref in zip(w_refs, out_refs):
        out_ref[...] = jnp.dot(n, w_ref[...], preferred_element_type=F32)


def _proj_in_kernel(h_ref, ht_ref, nw_ref, wg_ref, wr_ref, wab_ref, pg_ref, pr_ref, pab_ref,
                    pgt_ref, prt_ref, pabt_ref):
    i = pl.program_id(0)
    last = pl.num_programs(0) - 1
    w_refs = (wg_ref, wr_ref, wab_ref)

    @pl.when(i < last)
    def _():
        _proj_in_rows(h_ref, nw_ref, w_refs, (pg_ref, pr_ref, pab_ref))

    @pl.when(i == last)
    def _():
        _proj_in_rows(ht_ref, nw_ref, w_refs, (pgt_ref, prt_ref, pabt_ref))


def _proj_in(h, h_tail, norm_w, w_slabs, *, tm):
    m, mt = h.shape[0], h_tail.shape[0]
    n_tiles = m // tm
    row = lambda i: (jnp.minimum(i, n_tiles - 1), 0)
    tail = lambda i: (0, 0)
    widths = (GDN_MAIN, RWKV_COLS, AB_PAD)
    return pl.pallas_call(
        _proj_in_kernel,
        grid=(n_tiles + 1,),
        in_specs=[pl.BlockSpec((tm, D_MODEL), row), pl.BlockSpec((mt, D_MODEL), tail), _resident((1, D_MODEL))]
        + [_resident((D_MODEL, w)) for w in widths],
        out_specs=[pl.BlockSpec((tm, w), row) for w in widths] + [pl.BlockSpec((mt, w), tail) for w in widths],
        out_shape=[jax.ShapeDtypeStruct((m, w), F32) for w in widths]
        + [jax.ShapeDtypeStruct((mt, w), F32) for w in widths],
        compiler_params=_params(("arbitrary",)),
        name="proj_in",
    )(h, h_tail, norm_w, *w_slabs)


def _inv_unit_lower(lows, stack=None):
    shape = lows[0].shape
    n = shape[0]
    ri = _iota2(shape, 0)
    ci = _iota2(shape, 1) & (n - 1)
    eye = jnp.where(ri == ci, 1.0, 0.0)
    pair_blk = (ri >> 1) == (ci >> 1)
    ts = [eye - jnp.where(pair_blk, low, 0.0) for low in lows]
    s, lg = 2, 1
    while s < n:
        sel = ((ri >> (lg + 1)) == (ci >> (lg + 1))) & ((ri >> lg) != (ci >> lg))
        offs = [jnp.where(sel, low, 0.0) for low in lows]
        if stack is None:
            xs = [_mm(t, off) for t, off in zip(ts, offs)]
            ts = [t - _mm(x, t) for t, x in zip(ts, xs)]
        else:
            xs = [_mm(t, stack(off)) for t, off in zip(ts, offs)]
            ts = [t - _mm(x, stack(t)) for t, x in zip(ts, xs)]
        s, lg = 2 * s, lg + 1
    return ts


def _pair_stack(m):
    first = _iota2((m, 2 * m), 1) < m

    def stack(z):
        z = z.astype(BF16)
        zero = jnp.zeros_like(z)
        return jnp.concatenate([jnp.where(first, z, zero), jnp.where(first, zero, z)], axis=0)

    return stack


def _inv_unit_lower_halves(lows):
    n = lows[0].shape[0]
    m = n // 2
    first = _iota2((m, n), 1) < m
    stack = _pair_stack(m)
    t_diag = _inv_unit_lower([jnp.where(first, low[0:m, :], low[m:n, :]) for low in lows], stack)
    lower_left = (_iota2((n, n), 0) >= m) & (_iota2((n, n), 1) < m)
    xs = [_mm(t, jnp.where(lower_left, low, 0.0)) for t, low in zip(t_diag, lows)]
    t_ll = [_mm(x, stack(t)) for x, t in zip(xs, t_diag)]
    return [jnp.concatenate([jnp.where(first, t, 0.0), jnp.where(first, 0.0, t) - ll], axis=0)
            for t, ll in zip(t_diag, t_ll)]


def _gdn_gates(ab, alog, dtb):
    log_alpha = -jnp.exp(alog) * _softplus(ab + dtb)
    return log_alpha, _sigmoid(ab)


def _gdn_out(o, z, norm_w):
    o = o * lax.rsqrt(jnp.mean(o * o, axis=-1, keepdims=True) + NORM_EPS) * norm_w
    return o * _silu(z)


def _gdn_prompt_kernel(nb, tb, pg_ref, pab_ref, cw_ref, alog_ref, dtb_ref, nw_ref, o_ref, s_out_ref,
                       s_scr, ext_scr):
    t = pl.program_id(1)
    c = GDN_CHUNK
    chunks_per_seq = tb // c

    @pl.when(t == 0)
    def _():
        s_scr[...] = jnp.zeros_like(s_scr)
        ext_scr[...] = jnp.zeros_like(ext_scr)

    accs = []
    for q in range(nb):
        u = pg_ref[q, :, 0:GDN_QKV]
        acc = u * cw_ref[CONV_W - 1:CONV_W, :]
        for j in range(1, CONV_W):
            acc = acc + _shift_rows(ext_scr[q], u, j) * cw_ref[CONV_W - 1 - j:CONV_W - j, :]
        ext_scr[q] = u[tb - SUBLANES:tb, :]
        accs.append(acc)
    qkv = _silu(jnp.concatenate(accs, axis=0))

    pab = jnp.concatenate([pab_ref[q] for q in range(nb)], axis=0)
    log_alpha, beta_all = _gdn_gates(pab, alog_ref[...], dtb_ref[...])
    ri = _iota2((c, c), 0)
    ci = _iota2((c, c), 1)
    causal = ri >= ci
    strict = ri > ci
    tri = jnp.where(causal, 1.0, 0.0)

    n_chunks = nb * chunks_per_seq
    items = [(ch, h) for ch in range(n_chunks) for h in range(GDN_HEADS)]
    g_blk, gt_blk = [], []
    for ch in range(n_chunks):
        g = _mm_split3(tri, log_alpha[ch * c:(ch + 1) * c, :])
        g_blk.append(g)
        gt_blk.append(g.T)
    qs, ks, vbs, kbs, decays, g_cols = [], [], [], [], [], []
    for ch, h in items:
        rows = slice(ch * c, (ch + 1) * c)
        q = _l2norm(qkv[rows, h * GDN_DK:(h + 1) * GDN_DK]) * (GDN_DK ** -0.5)
        k = _l2norm(qkv[rows, GDN_QK + h * GDN_DK:GDN_QK + (h + 1) * GDN_DK])
        v = qkv[rows, 2 * GDN_QK + h * GDN_DK:2 * GDN_QK + (h + 1) * GDN_DK]
        beta = beta_all[rows, GDN_HEADS + h:GDN_HEADS + h + 1]
        g_col = g_blk[ch][:, h:h + 1]
        g_row = gt_blk[ch][h:h + 1, :]
        decays.append(jnp.exp(jnp.where(causal, g_col - g_row, -jnp.inf)))
        qs.append(q)
        ks.append(k)
        vbs.append(v * beta)
        kbs.append(k * beta)
        g_cols.append(g_col)
    kk = [_mm(kb, k, _NT) for kb, k in zip(kbs, ks)]
    qk = [_mm(q, k, _NT) for q, k in zip(qs, ks)]
    tinvs = _inv_unit_lower_halves([jnp.where(strict, x * d, 0.0) for x, d in zip(kk, decays)])
    attns = [x * d for x, d in zip(qk, decays)]
    egs = [jnp.exp(g_col) for g_col in g_cols]
    us = [_mm(tinv, vb) for tinv, vb in zip(tinvs, vbs)]
    ws = [_mm(tinv, kb * eg) for tinv, kb, eg in zip(tinvs, kbs, egs)]
    g_lasts = [g_col[c - 1:c, :] for g_col in g_cols]
    kd_ts = [(k * jnp.exp(g_last - g_col)).T for k, g_last, g_col in zip(ks, g_lasts, g_cols)]
    q_hats = [q * eg - _mm(attn, w) for q, eg, attn, w in zip(qs, egs, attns, ws)]
    o0s = [_mm(attn, u) for attn, u in zip(attns, us)]
    s_mix = [_mm(kd_t, w) for kd_t, w in zip(kd_ts, ws)]
    s_add = [_mm(kd_t, u) for kd_t, u in zip(kd_ts, us)]

    chains = [(q, h) for q in range(nb) for h in range(GDN_HEADS)]
    states = [s_scr[q, h] for q, h in chains]
    for blk in range(chunks_per_seq):
        idx = [(q * chunks_per_seq + blk) * GDN_HEADS + h for q, h in chains]
        for (q, h), i, s in zip(chains, idx, states):
            o = o0s[i] + _mm(q_hats[i], s)
            rows = slice(blk * c, (blk + 1) * c)
            z = pg_ref[q, rows, GDN_QKV + h * GDN_DK:GDN_QKV + (h + 1) * GDN_DK]
            o_ref[q, rows, h * GDN_DK:(h + 1) * GDN_DK] = _gdn_out(o, z, nw_ref[...]).astype(o_ref.dtype)
        states = [s * jnp.exp(g_lasts[i]) - _mm(s_mix[i], s) + s_add[i] for i, s in zip(idx, states)]
    for (q, h), s in zip(chains, states):
        s_scr[q, h] = s

    @pl.when(t == pl.num_programs(1) - 1)
    def _():
        s_out_ref[...] = s_scr[...]


def _gdn_prompt(pg, pab, conv_w, alog, dtb, norm_w, *, nb, tb):
    b, t, _ = pg.shape
    fixed = lambda i, j: (0, 0)
    return pl.pallas_call(
        functools.partial(_gdn_prompt_kernel, nb, tb),
        grid=(b // nb, t // tb),
        in_specs=[pl.BlockSpec((nb, tb, GDN_MAIN), lambda i, j: (i, j, 0)),
                  pl.BlockSpec((nb, tb, AB_PAD), lambda i, j: (i, j, 0)),
                  pl.BlockSpec((CONV_W, GDN_QKV), fixed), pl.BlockSpec((1, AB_PAD), fixed),
                  pl.BlockSpec((1, AB_PAD), fixed), pl.BlockSpec((1, GDN_DK), fixed)],
        out_specs=[pl.BlockSpec((nb, tb, GDN_WIDTH), lambda i, j: (i, j, 0)),
                   pl.BlockSpec((nb, GDN_HEADS, GDN_DK, GDN_DK), lambda i, j: (i, 0, 0, 0))],
        out_shape=[jax.ShapeDtypeStruct((b, t, GDN_WIDTH), BF16),
                   jax.ShapeDtypeStruct((b, GDN_HEADS, GDN_DK, GDN_DK), F32)],
        scratch_shapes=[pltpu.VMEM((nb, GDN_HEADS, GDN_DK, GDN_DK), F32),
                        pltpu.VMEM((nb, SUBLANES, GDN_QKV), F32)],
        compiler_params=_params(("parallel", "arbitrary")),
        name="gdn_prompt",
    )(pg, pab, conv_w, alog, dtb, norm_w)


def _gdn_step_kernel(bb, pg_ref, pab_ref, cs_ref, s_ref, cw_ref, alog_ref, dtb_ref, nw_ref,
                     o_ref, cs_out_ref, s_out_ref):
    u = pg_ref[:, 0:GDN_QKV]
    acc = u * cw_ref[CONV_W - 1:CONV_W, :]
    for j in range(CONV_W - 1):
        acc = acc + cs_ref[:, j * GDN_QKV:(j + 1) * GDN_QKV] * cw_ref[j:j + 1, :]
    cs_out_ref[:, 0:(CONV_W - 2) * GDN_QKV] = cs_ref[:, GDN_QKV:(CONV_W - 1) * GDN_QKV]
    cs_out_ref[:, (CONV_W - 2) * GDN_QKV:(CONV_W - 1) * GDN_QKV] = u
    qkv = _silu(acc)
    log_alpha, beta_all = _gdn_gates(pab_ref[...], alog_ref[...], dtb_ref[...])
    alpha_all = jnp.exp(log_alpha)
    eye = jnp.where(_iota2((GDN_DK, GDN_DK), 0) == _iota2((GDN_DK, GDN_DK), 1), 1.0, 0.0)

    def to_col(row):
        return jnp.sum(eye * row, axis=-1, keepdims=True)

    lg_lanes = LANES.bit_length() - 1
    spread = jnp.where(_iota2((bb, bb * LANES), 0) == (_iota2((bb, bb * LANES), 1) >> lg_lanes), 1.0, 0.0)

    def to_cols(x):
        return _mm_split3(x, spread, sel_first=False, dims=(((0,), (0,)), ((), ())))

    vs, k_cols, q_cols = [], [], []
    for h in range(GDN_HEADS):
        q = qkv[:, h * GDN_DK:(h + 1) * GDN_DK]
        k = qkv[:, GDN_QK + h * GDN_DK:GDN_QK + (h + 1) * GDN_DK]
        vs.append(qkv[:, 2 * GDN_QK + h * GDN_DK:2 * GDN_QK + (h + 1) * GDN_DK])
        q = _l2norm(q) * (GDN_DK ** -0.5)
        k = _l2norm(k)
        k_cols.append([to_col(k[i:i + 1, :]) for i in range(bb)])
        q_all = to_cols(q)
        q_cols.append([q_all[:, i * LANES:(i + 1) * LANES] for i in range(bb)])
    for h in range(GDN_HEADS):
        o_rows = []
        for i in range(bb):
            s = s_ref[i, h] * alpha_all[i:i + 1, h:h + 1]
            mem = jnp.sum(k_cols[h][i] * s, axis=0, keepdims=True)
            delta = (vs[h][i:i + 1, :] - mem) * beta_all[i:i + 1, GDN_HEADS + h:GDN_HEADS + h + 1]
            s = s + k_cols[h][i] * delta
            s_out_ref[i, h] = s
            o_rows.append(jnp.sum(q_cols[h][i] * s, axis=0, keepdims=True))
        o = jnp.concatenate(o_rows, axis=0)
        z = pg_ref[:, GDN_QKV + h * GDN_DK:GDN_QKV + (h + 1) * GDN_DK]
        o_ref[:, h * GDN_DK:(h + 1) * GDN_DK] = _gdn_out(o, z, nw_ref[...])


def _gdn_step(pg, pab, conv_state, s0, conv_w, alog, dtb, norm_w, *, bb):
    b = pg.shape[0]
    row = lambda i: (i, 0)
    fixed = lambda i: (0, 0)
    cs_cols = (CONV_W - 1) * GDN_QKV
    state_spec = pl.BlockSpec((bb, GDN_HEADS, GDN_DK, GDN_DK), lambda i: (i, 0, 0, 0))
    return pl.pallas_call(
        functools.partial(_gdn_step_kernel, bb),
        grid=(b // bb,),
        in_specs=[pl.BlockSpec((bb, GDN_MAIN), row), pl.BlockSpec((bb, AB_PAD), row),
                  pl.BlockSpec((bb, cs_cols), row), state_spec,
                  pl.BlockSpec((CONV_W, GDN_QKV), fixed), pl.BlockSpec((1, AB_PAD), fixed),
                  pl.BlockSpec((1, AB_PAD), fixed), pl.BlockSpec((1, GDN_DK), fixed)],
        out_specs=[pl.BlockSpec((bb, GDN_WIDTH), row), pl.BlockSpec((bb, cs_cols), row), state_spec],
        out_shape=[jax.ShapeDtypeStruct((b, GDN_WIDTH), F32), jax.ShapeDtypeStruct((b, cs_cols), F32),
                   jax.ShapeDtypeStruct(s0.shape, F32)],
        compiler_params=_params(("parallel",)),
        name="gdn_step",
    )(pg, pab, conv_state, s0, conv_w, alog, dtb, norm_w)


def _head_sums(x, head_ones):
    width = 2 * LANES
    return jnp.concatenate([_mm(x[:, lo:lo + width], head_ones[lo:lo + width, lo:lo + width])
                            for lo in range(0, RWKV_WIDTH, width)], axis=1)


def _rwkv_prep(xs, w0, a0, wwa, g2, k_k, k_a, head_ones):
    w = RWKV_WIDTH
    r, k, v = xs[:, 0:w], xs[:, w:2 * w], xs[:, 2 * w:3 * w]
    wa_in = xs[:, 3 * w:3 * w + LORA_WA]
    lane = _iota2(wa_in.shape, 1)
    wa_in = jnp.where(lane < LORA_WA // 2, jnp.tanh(wa_in), wa_in)
    wa = _mm(wa_in, wwa)
    e = math.exp(-0.5) * _sigmoid(w0 + wa[:, 0:w])
    a = _sigmoid(a0 + wa[:, w:2 * w])
    g = _mm(_sigmoid(xs[:, 3 * w + LORA_WA:RWKV_COLS]), g2)
    kx = k * k_k
    kk = kx * lax.rsqrt(_head_sums(kx * kx, head_ones) + L2_EPS)
    k = k * (1.0 + (a - 1.0) * k_a)
    return r, k, v, e, kk, kk * a, g


def _rwkv_out(y, r, k, v, g, r_k, ln_w, ln_b, head_ones):
    inv_n = 1.0 / RWKV_N
    mean = _head_sums(y, head_ones) * inv_n
    yc = y - mean
    var = _head_sums(yc * yc, head_ones) * inv_n
    y = yc * lax.rsqrt(var + GN_EPS) * ln_w + ln_b
    bonus = _head_sums(r * k * r_k, head_ones) * v
    return (y + bonus) * g


def _rwkv_prompt_kernel(nb, tb, n_cast, pr_ref, mu_ref, w0_ref, a0_ref, wwa_ref, g2_ref, kk_ref, ka_ref, rk_ref,
                        lnw_ref, lnb_ref, ones_ref, *refs):
    cast_in, (o_ref, s_out_ref), cast_out = refs[:n_cast], refs[n_cast:n_cast + 2], refs[n_cast + 2:2 * n_cast + 2]
    s_scr, ext_scr, y_scr = refs[2 * n_cast + 2:]
    for src, dst in zip(cast_in, cast_out):
        dst[...] = src[...].astype(BF16)
    t = pl.program_id(1)
    c = RWKV_CHUNK
    n = RWKV_N
    pair = 2 * n
    lg_c = c.bit_length() - 1
    lg_n = n.bit_length() - 1
    n_pairs = RWKV_HEADS // 2
    chunks_per_seq = tb // c

    @pl.when(t == 0)
    def _():
        s_scr[...] = jnp.zeros_like(s_scr)
        ext_scr[...] = jnp.zeros_like(ext_scr)

    prevs = []
    for q in range(nb):
        prevs.append(_shift_rows(ext_scr[q], pr_ref[q], 1))
        ext_scr[q] = pr_ref[q, tb - SUBLANES:tb, :]
    p = jnp.concatenate([pr_ref[q] for q in range(nb)], axis=0)
    xs = p + (jnp.concatenate(prevs, axis=0) - p) * mu_ref[...]
    head_ones = ones_ref[...]
    r, k, v, e, kk, kka, g = _rwkv_prep(xs, w0_ref[...], a0_ref[...], wwa_ref[...], g2_ref[...],
                                         kk_ref[...], ka_ref[...], head_ones)

    ri = _iota2((tb, tb), 0)
    ci = _iota2((tb, tb), 1)
    tri = jnp.where((ri >= ci) & ((ri >> lg_c) == (ci >> lg_c)), 1.0, 0.0)
    e_t = jnp.concatenate([e[q * tb:(q + 1) * tb, :].T for q in range(nb)], axis=0)
    lg_t = _mm_split3(e_t, tri.T, sel_first=False)
    lg_inc = -jnp.concatenate([lg_t[q * RWKV_WIDTH:(q + 1) * RWKV_WIDTH, :].T for q in range(nb)], axis=0)
    lg_exc = lg_inc + e
    a_t = -kk * jnp.exp(lg_exc)
    r_t = r * jnp.exp(lg_inc)
    inv_g = jnp.exp(-lg_inc)
    b_t = kka * inv_g
    k_t = k * inv_g

    assert c == n, "two heads side by side fill one (chunk, 2 * chunk) score tile"
    stack = _pair_stack(n)

    tt = _iota2((c, pair), 0)
    ss = _iota2((c, pair), 1) & (n - 1)
    strict = tt > ss
    incl = tt >= ss
    bd = jnp.where((_iota2((pair, pair), 0) >> lg_n) == (_iota2((pair, pair), 1) >> lg_n), 1.0, 0.0)

    items = [(ch, hp) for ch in range(nb * chunks_per_seq) for hp in range(n_pairs)]

    def tile(x, ch, hp):
        return x[ch * c:(ch + 1) * c, hp * pair:(hp + 1) * pair]

    ats = [tile(a_t, *it) for it in items]
    rts = [tile(r_t, *it) for it in items]
    vts = [tile(v, *it) for it in items]
    svs = [stack(vt) for vt in vts]
    scs = [_mm(jnp.concatenate([at, rt], axis=0),
               jnp.concatenate([stack(tile(b_t, *it)), stack(tile(k_t, *it))], axis=0), _NT)
           for at, rt, it in zip(ats, rts, items)]
    tinvs = _inv_unit_lower([-jnp.where(strict, sc[0:c, 0:pair], 0.0) for sc in scs], stack)
    v_mix = [_mm(jnp.concatenate([jnp.where(strict, sc[0:c, pair:2 * pair], 0.0),
                                  jnp.where(incl, sc[c:2 * c, pair:2 * pair], 0.0)], axis=0), sv)
             for sc, sv in zip(scs, svs)]
    akv = [x[0:c, :] for x in v_mix]
    y_v = [x[c:2 * c, :] for x in v_mix]
    p_rbs = [jnp.where(incl, sc[c:2 * c, 0:pair], 0.0) for sc in scs]
    wu = [_mm(tinv, jnp.concatenate([stack(at), stack(x)], axis=1)) for tinv, at, x in zip(tinvs, ats, akv)]
    w_ts = [x[:, 0:pair] for x in wu]
    u0s = [x[:, pair:2 * pair] for x in wu]
    lg_lasts = [lg_inc[ch * c + c - 1:ch * c + c, hp * pair:(hp + 1) * pair] for ch, hp in items]
    to_ends = [jnp.exp(lg_last - tile(lg_inc, *it)) for lg_last, it in zip(lg_lasts, items)]
    bhs = [tile(kka, *it) * to_end for it, to_end in zip(items, to_ends)]
    khs = [tile(k, *it) * to_end for it, to_end in zip(items, to_ends)]
    ry = [_mm(p_rb, jnp.concatenate([stack(w_t), stack(u0)], axis=1)) for p_rb, w_t, u0 in zip(p_rbs, w_ts, u0s)]
    r_hats = [rt + x[:, 0:pair] for rt, x in zip(rts, ry)]
    y0s = [x[:, pair:2 * pair] + yv for x, yv in zip(ry, y_v)]
    s_mix = [bd * _mm(w_t.T, bh) for w_t, bh in zip(w_ts, bhs)]
    s_add = [bd * _mm(jnp.concatenate([u0, vt], axis=0).T, jnp.concatenate([bh, kh], axis=0))
             for u0, vt, bh, kh in zip(u0s, vts, bhs, khs)]

    chains = [(q, hp) for q in range(nb) for hp in range(n_pairs)]
    states = [s_scr[q, hp] for q, hp in chains]
    for blk in range(chunks_per_seq):
        idx = [(q * chunks_per_seq + blk) * n_pairs + hp for q, hp in chains]
        for (q, hp), i, s in zip(chains, idx, states):
            row0 = (q * chunks_per_seq + blk) * c
            y_scr[row0:row0 + c, hp * pair:(hp + 1) * pair] = y0s[i] + _mm(r_hats[i], s, _NT)
        states = [s * jnp.exp(lg_lasts[i]) + _mm(s, s_mix[i]) + s_add[i] for i, s in zip(idx, states)]
    for (q, hp), s in zip(chains, states):
        s_scr[q, hp] = s

    out = _rwkv_out(y_scr[...], r, k, v, g, rk_ref[...], lnw_ref[...], lnb_ref[...], head_ones)
    for q in range(nb):
        o_ref[q] = out[q * tb:(q + 1) * tb, :].astype(o_ref.dtype)

    @pl.when(t == pl.num_programs(1) - 1)
    def _():
        for q, hp in chains:
            s = s_scr[q, hp]
            s_out_ref[q, 2 * hp] = s[0:n, 0:n]
            s_out_ref[q, 2 * hp + 1] = s[n:pair, n:pair]


def _rwkv_weight_specs(fixed):
    vec = pl.BlockSpec((1, RWKV_WIDTH), fixed)
    return [pl.BlockSpec((1, RWKV_COLS), fixed), vec, vec,
            pl.BlockSpec((LORA_WA, 2 * RWKV_WIDTH), fixed), pl.BlockSpec((G_LORA, RWKV_WIDTH), fixed),
            vec, vec, vec, vec, vec, pl.BlockSpec((RWKV_WIDTH, RWKV_WIDTH), fixed)]


def _cast_chunk_spec(rows, cols, steps, t_blocks):
    chunks = max(c for c in range(1, steps + 1) if rows % c == 0 and (rows // c) % (2 * SUBLANES) == 0)
    return pl.BlockSpec((rows // chunks, cols), lambda i, j: (jnp.minimum(i * t_blocks + j, chunks - 1), 0))


def _rwkv_prompt(pr, weights, to_cast=(), *, nb, tb):
    b, t, _ = pr.shape
    fixed = lambda i, j: (0, 0)
    steps = (b // nb) * (t // tb)
    cast_specs = [_cast_chunk_spec(*w.shape, steps, t // tb) for w in to_cast]
    return pl.pallas_call(
        functools.partial(_rwkv_prompt_kernel, nb, tb, len(to_cast)),
        grid=(b // nb, t // tb),
        in_specs=[pl.BlockSpec((nb, tb, RWKV_COLS), lambda i, j: (i, j, 0))] + _rwkv_weight_specs(fixed)
        + cast_specs,
        out_specs=[pl.BlockSpec((nb, tb, RWKV_WIDTH), lambda i, j: (i, j, 0)),
                   pl.BlockSpec((nb, RWKV_HEADS, RWKV_N, RWKV_N), lambda i, j: (i, 0, 0, 0))] + cast_specs,
        out_shape=[jax.ShapeDtypeStruct((b, t, RWKV_WIDTH), BF16),
                   jax.ShapeDtypeStruct((b, RWKV_HEADS, RWKV_N, RWKV_N), F32)]
        + [jax.ShapeDtypeStruct(w.shape, BF16) for w in to_cast],
        scratch_shapes=[pltpu.VMEM((nb, RWKV_HEADS // 2, 2 * RWKV_N, 2 * RWKV_N), F32),
                        pltpu.VMEM((nb, SUBLANES, RWKV_COLS), F32),
                        pltpu.VMEM((nb * tb, RWKV_WIDTH), F32)],
        compiler_params=_params(("arbitrary", "arbitrary")),
        name="rwkv_prompt",
    )(pr, *weights, *to_cast)


def _rwkv_step_kernel(pr_ref, sh_ref, s_ref, mu_ref, w0_ref, a0_ref, wwa_ref, g2_ref, kk_ref, ka_ref,
                      rk_ref, lnw_ref, lnb_ref, ones_ref, o_ref, s_out_ref, vec_scr, row_scr, y_scr):
    h = pl.program_id(0)
    n = RWKV_N
    batch = pr_ref.shape[0]

    @pl.when(h == 0)
    def _():
        p = pr_ref[...]
        xs = p + (sh_ref[...] - p) * mu_ref[...]
        r, k, v, e, kk, kka, g = _rwkv_prep(xs, w0_ref[...], a0_ref[...], wwa_ref[...], g2_ref[...],
                                             kk_ref[...], ka_ref[...], ones_ref[...])
        for j, x in enumerate((-kk, kka, k, jnp.exp(-e), r, v)):
            vec_scr[j] = x.T.reshape(RWKV_HEADS, n, batch)
        for j, x in enumerate((r, k, v, g)):
            row_scr[j] = x

    nkk, kka, k, decay, r = (vec_scr[j, h] for j in range(5))
    for vi in range(n):
        s = s_ref[vi]
        sa = jnp.sum(s * nkk, axis=0, keepdims=True)
        s = s * decay + sa * kka + vec_scr[5, h, vi:vi + 1, :] * k
        s_out_ref[vi] = s
        y_scr[h, vi:vi + 1, :] = jnp.sum(s * r, axis=0, keepdims=True)

    @pl.when(h == RWKV_HEADS - 1)
    def _():
        y = y_scr[...].reshape(RWKV_WIDTH, batch).T
        o_ref[...] = _rwkv_out(y, row_scr[0], row_scr[1], row_scr[2], row_scr[3], rk_ref[...], lnw_ref[...],
                               lnb_ref[...], ones_ref[...])


def _rwkv_step(pr, shift, s0_t, weights):
    b = pr.shape[0]
    fixed = lambda i: (0, 0)
    state_spec = pl.BlockSpec((None, RWKV_N, RWKV_N, b), lambda i: (i, 0, 0, 0))
    return pl.pallas_call(
        _rwkv_step_kernel,
        grid=(RWKV_HEADS,),
        in_specs=[pl.BlockSpec((b, RWKV_COLS), fixed), pl.BlockSpec((b, RWKV_COLS), fixed), state_spec]
        + _rwkv_weight_specs(fixed),
        out_specs=[pl.BlockSpec((b, RWKV_WIDTH), fixed), state_spec],
        out_shape=[jax.ShapeDtypeStruct((b, RWKV_WIDTH), F32), jax.ShapeDtypeStruct(s0_t.shape, F32)],
        scratch_shapes=[pltpu.VMEM((6, RWKV_HEADS, RWKV_N, b), F32), pltpu.VMEM((4, b, RWKV_WIDTH), F32),
                        pltpu.VMEM((RWKV_HEADS, RWKV_N, b), F32)],
        compiler_params=_params(("arbitrary",)),
        name="rwkv_step",
    )(pr, shift, s0_t, *weights)


def _pad_lanes(x, width):
    return jnp.pad(x, ((0, 0), (0, width - x.shape[1])))


def kernel(x_prompt, x_sample, state_gdn_conv, state_gdn, state_rwkv_shift, state_rwkv, norm_ffn1, w_ffn1_in, w_ffn1_out, norm_mix, w_in, gdn_conv_w, gdn_a_log, gdn_dt_bias, gdn_norm_w, rwkv_mu, rwkv_w0, rwkv_w2, rwkv_a0, rwkv_a2, rwkv_g2, rwkv_k_k, rwkv_k_a, rwkv_r_k, rwkv_ln_w, rwkv_ln_b, w_out, norm_ffn2, w_ffn2_in, w_ffn2_out, norm_final):
    depth = norm_ffn1.shape[0]
    assert depth == 1, "the carried-state plumbing below is written for a single layer"
    b, t, _ = x_prompt.shape
    bs = x_sample.shape[0]
    assert x_sample.shape[1] == 1
    l = 0

    n_gdn_cols = GDN_MAIN + 2 * GDN_HEADS
    w_in_l = w_in[l]
    w_slabs = (w_in_l[:, :GDN_MAIN].astype(BF16), w_in_l[:, n_gdn_cols:].astype(BF16),
               _pad_lanes(w_in_l[:, GDN_MAIN:n_gdn_cols], AB_PAD).astype(BF16))
    w1a, w1b = w_ffn1_in[l].astype(BF16), w_ffn1_out[l].astype(BF16)
    woa, wob = w_out[l, :GDN_WIDTH].astype(BF16), w_out[l, GDN_WIDTH:].astype(BF16)
    half = LORA_WA // 2
    wwa = jnp.zeros((LORA_WA, 2 * RWKV_WIDTH), F32)
    wwa = wwa.at[:half, :RWKV_WIDTH].set(rwkv_w2[l]).at[half:, RWKV_WIDTH:].set(rwkv_a2[l]).astype(BF16)
    head_id = jnp.arange(RWKV_WIDTH) // RWKV_N
    head_ones = (head_id[:, None] == head_id[None, :]).astype(BF16)
    rwkv_w = (rwkv_mu[l][None], rwkv_w0[l][None], rwkv_a0[l][None], wwa, rwkv_g2[l].astype(BF16),
              rwkv_k_k[l][None], rwkv_k_a[l][None], rwkv_r_k[l].reshape(1, RWKV_WIDTH),
              rwkv_ln_w[l][None], rwkv_ln_b[l][None], head_ones)
    alog = _pad_lanes(gdn_a_log[l][None], AB_PAD)
    dtb = _pad_lanes(gdn_dt_bias[l][None], AB_PAD)
    gdn_w = (gdn_conv_w[l], alog, dtb, gdn_norm_w[l][None])
    nf1, nmix, nf2, nfin = norm_ffn1[l][None], norm_mix[l][None], norm_ffn2[l][None], norm_final[None]

    xp, xs = x_prompt.reshape(b * t, D_MODEL), x_sample.reshape(bs, D_MODEL)
    hp, hs = _ffn(xp, xs, nf1, w1a, w1b, tm=DENSE_TILE_ROWS, sub=DENSE_SUB_ROWS)
    pg, pr, pab, sg, sr, sab = _proj_in(hp, hs, nmix, w_slabs, tm=DENSE_TILE_ROWS)

    pg3, pr3, pab3 = pg.reshape(b, t, GDN_MAIN), pr.reshape(b, t, RWKV_COLS), pab.reshape(b, t, AB_PAD)
    oa, gdn_s = _gdn_prompt(pg3, pab3, *gdn_w, nb=MIXER_SEQS, tb=MIXER_BLOCK_ROWS)
    ob, wkv_s, w2a, w2b = _rwkv_prompt(pr3, rwkv_w, (w_ffn2_in[l], w_ffn2_out[l]),
                                       nb=MIXER_SEQS, tb=MIXER_BLOCK_ROWS)

    conv_in = state_gdn_conv[l].reshape(bs, (CONV_W - 1) * GDN_QKV)
    oa_s, conv_s, gdn_ss = _gdn_step(sg, sab, conv_in, state_gdn[l], *gdn_w, bb=GDN_STEP_ROWS)
    ob_s, wkv_t = _rwkv_step(sr, state_rwkv_shift[l], jnp.transpose(state_rwkv[l], (1, 2, 3, 0)), rwkv_w)
    wkv_ss = jnp.transpose(wkv_t, (3, 0, 1, 2))

    yp, ys = _ffn(hp, hs, nf2, w2a, w2b, final_w=nfin, tm=DENSE_TILE_ROWS, sub=DENSE_SUB_ROWS,
                  mix=(oa.reshape(b * t, GDN_WIDTH), ob.reshape(b * t, RWKV_WIDTH), oa_s, ob_s, woa, wob))

    return (yp.reshape(b, t, D_MODEL), ys.reshape(bs, 1, D_MODEL),
            pg3[:, t - (CONV_W - 1):, :GDN_QKV][None], gdn_s[None], pr3[:, t - 1, :][None], wkv_s[None],
            conv_s.reshape(1, bs, CONV_W - 1, GDN_QKV), gdn_ss[None], sr[None], wkv_ss[None])
```

```python
import functools
import math

import jax
import jax.numpy as jnp
from jax import lax
from jax.experimental import pallas as pl
from jax.experimental.pallas import tpu as pltpu

F32 = jnp.float32
BF16 = jnp.bfloat16

D_MODEL = 1024
D_FF = 2816
CONV_W = 4
GDN_HEADS = 4
GDN_DK = 128
GDN_QK = GDN_HEADS * GDN_DK
GDN_WIDTH = GDN_HEADS * GDN_DK
GDN_QKV = 3 * GDN_WIDTH
GDN_MAIN = GDN_QKV + GDN_WIDTH
RWKV_HEADS = 8
RWKV_N = 64
RWKV_WIDTH = RWKV_HEADS * RWKV_N
LORA_WA = 128
G_LORA = 128
RWKV_COLS = 3 * RWKV_WIDTH + LORA_WA + G_LORA
LANES = 128
SUBLANES = 8
AB_PAD = LANES
NORM_EPS = 1e-6
GN_EPS = 64e-5
L2_EPS = 1e-6

GDN_CHUNK = 128
RWKV_CHUNK = 64
VMEM_LIMIT = 56 * 1024 * 1024
DENSE_TILE_ROWS = 1024
DENSE_SUB_ROWS = 256
MIXER_SEQS = 2
MIXER_BLOCK_ROWS = 256
GDN_STEP_ROWS = 16


_NN = (((1,), (0,)), ((), ()))
_NT = (((1,), (1,)), ((), ()))


def _mm(a, b, dims=_NN):
    return lax.dot_general(a.astype(BF16), b.astype(BF16), dims, preferred_element_type=F32)


def _mm_split3(a, b, sel_first=True, dims=_NN):
    sel, x = (a, b) if sel_first else (b, a)
    sel = sel.astype(BF16)
    out = None
    for _ in range(3):
        piece = x.astype(BF16)
        part = lax.dot_general(*((sel, piece) if sel_first else (piece, sel)), dims, preferred_element_type=F32)
        out = part if out is None else out + part
        x = x - piece.astype(F32)
    return out


def _sigmoid(x):
    return 1.0 / (1.0 + jnp.exp(-x))


def _silu(x):
    return x * _sigmoid(x)


def _softplus(x):
    return jnp.maximum(x, 0.0) + jnp.log(1.0 + jnp.exp(-jnp.abs(x)))


def _rms(x, g):
    return x * lax.rsqrt(jnp.mean(x * x, axis=-1, keepdims=True) + NORM_EPS) * g


def _iota2(shape, dim):
    return lax.broadcasted_iota(jnp.int32, shape, dim)


def _shift_rows(prev_tail, x, j):
    rolled = pltpu.roll(jnp.concatenate([prev_tail, x], axis=0), j, axis=0)
    return rolled[SUBLANES:, :]


def _params(sem):
    return pltpu.CompilerParams(dimension_semantics=sem, vmem_limit_bytes=VMEM_LIMIT)


def _resident(shape):
    return pl.BlockSpec(shape, lambda i: (0,) * len(shape), pipeline_mode=pl.Buffered(1))


def _ffn_rows(x_ref, mix_refs, w_refs, out_ref, sub):
    woa_ref, wob_ref, nw_ref, w1_ref, w2_ref, fw_ref = w_refs
    for s in range(x_ref.shape[0] // sub):
        rows = slice(s * sub, (s + 1) * sub)
        x = x_ref[rows, :]
        if mix_refs is not None:
            oa_ref, ob_ref = mix_refs
            x = x + _mm(oa_ref[rows, :], woa_ref[...]) + _mm(ob_ref[rows, :], wob_ref[...])
        xn = _rms(x, nw_ref[...]).astype(BF16)
        gate = jnp.dot(xn, w1_ref[:, 0:D_FF], preferred_element_type=F32)
        up = jnp.dot(xn, w1_ref[:, D_FF:2 * D_FF], preferred_element_type=F32)
        act = (_silu(gate) * up).astype(BF16)
        h = x + 0.5 * jnp.dot(act, w2_ref[...], preferred_element_type=F32)
        if fw_ref is not None:
            h = _rms(h, fw_ref[...])
        out_ref[rows, :] = h


def _ffn_kernel(has_mix, final_norm, sub, *refs):
    it = iter(refs)
    x_ref, xt_ref = next(it), next(it)
    mix_refs = tail_mix_refs = None
    woa_ref = wob_ref = None
    if has_mix:
        mix_refs, tail_mix_refs = (next(it), next(it)), (next(it), next(it))
        woa_ref, wob_ref = next(it), next(it)
    nw_ref, w1_ref, w2_ref = next(it), next(it), next(it)
    fw_ref = next(it) if final_norm else None
    out_ref, out_tail_ref = next(it), next(it)
    w_refs = (woa_ref, wob_ref, nw_ref, w1_ref, w2_ref, fw_ref)
    i = pl.program_id(0)
    last = pl.num_programs(0) - 1

    @pl.when(i < last)
    def _():
        _ffn_rows(x_ref, mix_refs, w_refs, out_ref, sub)

    @pl.when(i == last)
    def _():
        _ffn_rows(xt_ref, tail_mix_refs, w_refs, out_tail_ref, xt_ref.shape[0])


def _ffn(x, x_tail, norm_w, w1, w2, *, mix=None, final_w=None, tm, sub):
    m, mt = x.shape[0], x_tail.shape[0]
    n_tiles = m // tm
    has_mix = mix is not None
    final_norm = final_w is not None
    row = lambda i: (jnp.minimum(i, n_tiles - 1), 0)
    tail = lambda i: (0, 0)
    in_specs = [pl.BlockSpec((tm, D_MODEL), row), pl.BlockSpec((mt, D_MODEL), tail)]
    args = [x, x_tail]
    if has_mix:
        oa, ob, oa_tail, ob_tail, woa, wob = mix
        in_specs += [pl.BlockSpec((tm, GDN_WIDTH), row), pl.BlockSpec((tm, RWKV_WIDTH), row),
                     pl.BlockSpec((mt, GDN_WIDTH), tail), pl.BlockSpec((mt, RWKV_WIDTH), tail),
                     _resident((GDN_WIDTH, D_MODEL)), _resident((RWKV_WIDTH, D_MODEL))]
        args += [oa, ob, oa_tail, ob_tail, woa, wob]
    in_specs += [_resident((1, D_MODEL)), _resident((D_MODEL, 2 * D_FF)), _resident((D_FF, D_MODEL))]
    args += [norm_w, w1, w2]
    if final_norm:
        in_specs.append(_resident((1, D_MODEL)))
        args.append(final_w)
    return pl.pallas_call(
        functools.partial(_ffn_kernel, has_mix, final_norm, sub),
        grid=(n_tiles + 1,),
        in_specs=in_specs,
        out_specs=[pl.BlockSpec((tm, D_MODEL), row), pl.BlockSpec((mt, D_MODEL), tail)],
        out_shape=[jax.ShapeDtypeStruct((m, D_MODEL), F32), jax.ShapeDtypeStruct((mt, D_MODEL), F32)],
        compiler_params=_params(("arbitrary",)),
        name="ffn_mix" if has_mix else "ffn",
    )(*args)


def _l2norm(x):
    return x * lax.rsqrt(jnp.sum(x * x, axis=-1, keepdims=True) + L2_EPS)


def _proj_in_rows(h_ref, nw_ref, w_refs, out_refs):
    n = _rms(h_ref[...], nw_ref[...]).astype(BF16)
    for w_ref, out_ref in zip(w_refs, out_refs):
        out_ref[...] = jnp.dot(n, w_ref[...], preferred_element_type=F32)


def _proj_in_kernel(h_ref, ht_ref, nw_ref, wg_ref, wr_ref, wab_ref, pg_ref, pr_ref, pab_ref,
                    pgt_ref, prt_ref, pabt_ref):
    i = pl.program_id(0)
    last = pl.num_programs(0) - 1
    w_refs = (wg_ref, wr_ref, wab_ref)

    @pl.when(i < last)
    def _():
        _proj_in_rows(h_ref, nw_ref, w_refs, (pg_ref, pr_ref, pab_ref))

    @pl.when(i == last)
    def _():
        _proj_in_rows(ht_ref, nw_ref, w_refs, (pgt_ref, prt_ref, pabt_ref))


def _proj_in(h, h_tail, norm_w, w_slabs, *, tm):
    m, mt = h.shape[0], h_tail.shape[0]
    n_tiles = m // tm
    row = lambda i: (jnp.minimum(i, n_tiles - 1), 0)
    tail = lambda i: (0, 0)
    widths = (GDN_MAIN, RWKV_COLS, AB_PAD)
    return pl.pallas_call(
        _proj_in_kernel,
        grid=(n_tiles + 1,),
        in_specs=[pl.BlockSpec((tm, D_MODEL), row), pl.BlockSpec((mt, D_MODEL), tail), _resident((1, D_MODEL))]
        + [_resident((D_MODEL, w)) for w in widths],
        out_specs=[pl.BlockSpec((tm, w), row) for w in widths] + [pl.BlockSpec((mt, w), tail) for w in widths],
        out_shape=[jax.ShapeDtypeStruct((m, w), F32) for w in widths]
        + [jax.ShapeDtypeStruct((mt, w), F32) for w in widths],
        compiler_params=_params(("arbitrary",)),
        name="proj_in",
    )(h, h_tail, norm_w, *w_slabs)


def _inv_unit_lower(lows, stack=None):
    shape = lows[0].shape
    n = shape[0]
    ri = _iota2(shape, 0)
    ci = _iota2(shape, 1) & (n - 1)
    eye = jnp.where(ri == ci, 1.0, 0.0)
    pair_blk = (ri >> 1) == (ci >> 1)
    ts = [eye - jnp.where(pair_blk, low, 0.0) for low in lows]
    s, lg = 2, 1
    while s < n:
        sel = ((ri >> (lg + 1)) == (ci >> (lg + 1))) & ((ri >> lg) != (ci >> lg))
        offs = [jnp.where(sel, low, 0.0) for low in lows]
        if stack is None:
            xs = [_mm(t, off) for t, off in zip(ts, offs)]
            ts = [t - _mm(x, t) for t, x in zip(ts, xs)]
        else:
            xs = [_mm(t, stack(off)) for t, off in zip(ts, offs)]
            ts = [t - _mm(x, stack(t)) for t, x in zip(ts, xs)]
        s, lg = 2 * s, lg + 1
    return ts


def _pair_stack(m):
    first = _iota2((m, 2 * m), 1) < m

    def stack(z):
        z = z.astype(BF16)
        zero = jnp.zeros_like(z)
        return jnp.concatenate([jnp.where(first, z, zero), jnp.where(first, zero, z)], axis=0)

    return stack


def _inv_unit_lower_halves(lows):
    n = lows[0].shape[0]
    m = n // 2
    first = _iota2((m, n), 1) < m
    stack = _pair_stack(m)
    t_diag = _inv_unit_lower([jnp.where(first, low[0:m, :], low[m:n, :]) for low in lows], stack)
    lower_left = (_iota2((n, n), 0) >= m) & (_iota2((n, n), 1) < m)
    xs = [_mm(t, jnp.where(lower_left, low, 0.0)) for t, low in zip(t_diag, lows)]
    t_ll = [_mm(x, stack(t)) for x, t in zip(xs, t_diag)]
    return [jnp.concatenate([jnp.where(first, t, 0.0), jnp.where(first, 0.0, t) - ll], axis=0)
            for t, ll in zip(t_diag, t_ll)]


def _gdn_gates(ab, alog, dtb):
    log_alpha = -jnp.exp(alog) * _softplus(ab + dtb)
    return log_alpha, _sigmoid(ab)


def _gdn_out(o, z, norm_w):
    o = o * lax.rsqrt(jnp.mean(o * o, axis=-1, keepdims=True) + NORM_EPS) * norm_w
    return o * _silu(z)


def _gdn_prompt_kernel(nb, tb, pg_ref, pab_ref, cw_ref, alog_ref, dtb_ref, nw_ref, o_ref, s_out_ref,
                       s_scr, ext_scr):
    t = pl.program_id(1)
    c = GDN_CHUNK
    chunks_per_seq = tb // c

    @pl.when(t == 0)
    def _():
        s_scr[...] = jnp.zeros_like(s_scr)
        ext_scr[...] = jnp.zeros_like(ext_scr)

    def conv_tile(ch, col0):
        seq, r0 = ch // chunks_per_seq, (ch % chunks_per_seq) * c
        cols = slice(col0, col0 + GDN_DK)
        x = pg_ref[seq, r0:r0 + c, cols]
        above = ext_scr[seq, :, cols] if r0 == 0 else pg_ref[seq, r0 - SUBLANES:r0, cols]
        acc = x * cw_ref[CONV_W - 1:CONV_W, cols]
        for j in range(1, CONV_W):
            acc = acc + _shift_rows(above, x, j) * cw_ref[CONV_W - 1 - j:CONV_W - j, cols]
        return _silu(acc)

    pab = jnp.concatenate([pab_ref[q] for q in range(nb)], axis=0)
    log_alpha, beta_all = _gdn_gates(pab, alog_ref[...], dtb_ref[...])
    ri = _iota2((c, c), 0)
    ci = _iota2((c, c), 1)
    causal = ri >= ci
    strict = ri > ci
    tri = jnp.where(causal, 1.0, 0.0)

    n_chunks = nb * chunks_per_seq
    items = [(ch, h) for ch in range(n_chunks) for h in range(GDN_HEADS)]
    g_blk, gt_blk = [], []
    for ch in range(n_chunks):
        g = _mm_split3(tri, log_alpha[ch * c:(ch + 1) * c, :])
        g_blk.append(g)
        gt_blk.append(g.T)
    qs, ks, vbs, kbs, decays, g_cols = [], [], [], [], [], []
    for ch, h in items:
        rows = slice(ch * c, (ch + 1) * c)
        q = _l2norm(conv_tile(ch, h * GDN_DK)) * (GDN_DK ** -0.5)
        k = _l2norm(conv_tile(ch, GDN_QK + h * GDN_DK))
        v = conv_tile(ch, 2 * GDN_QK + h * GDN_DK)
        beta = beta_all[rows, GDN_HEADS + h:GDN_HEADS + h + 1]
        g_col = g_blk[ch][:, h:h + 1]
        g_row = gt_blk[ch][h:h + 1, :]
        decays.append(jnp.exp(jnp.where(causal, g_col - g_row, -jnp.inf)))
        qs.append(q)
        ks.append(k)
        vbs.append(v * beta)
        kbs.append(k * beta)
        g_cols.append(g_col)
    kk = [_mm(kb, k, _NT) for kb, k in zip(kbs, ks)]
    qk = [_mm(q, k, _NT) for q, k in zip(qs, ks)]
    tinvs = _inv_unit_lower_halves([jnp.where(strict, x * d, 0.0) for x, d in zip(kk, decays)])
    attns = [x * d for x, d in zip(qk, decays)]
    egs = [jnp.exp(g_col) for g_col in g_cols]
    us = [_mm(tinv, vb) for tinv, vb in zip(tinvs, vbs)]
    ws = [_mm(tinv, kb * eg) for tinv, kb, eg in zip(tinvs, kbs, egs)]
    g_lasts = [g_col[c - 1:c, :] for g_col in g_cols]
    kd_ts = [(k * jnp.exp(g_last - g_col)).T for k, g_last, g_col in zip(ks, g_lasts, g_cols)]
    q_hats = [q * eg - _mm(attn, w) for q, eg, attn, w in zip(qs, egs, attns, ws)]
    o0s = [_mm(attn, u) for attn, u in zip(attns, us)]
    s_mix = [_mm(kd_t, w) for kd_t, w in zip(kd_ts, ws)]
    s_add = [_mm(kd_t, u) for kd_t, u in zip(kd_ts, us)]

    chains = [(q, h) for q in range(nb) for h in range(GDN_HEADS)]
    states = [s_scr[q, h] for q, h in chains]
    for blk in range(chunks_per_seq):
        idx = [(q * chunks_per_seq + blk) * GDN_HEADS + h for q, h in chains]
        for (q, h), i, s in zip(chains, idx, states):
            o = o0s[i] + _mm(q_hats[i], s)
            rows = slice(blk * c, (blk + 1) * c)
            z = pg_ref[q, rows, GDN_QKV + h * GDN_DK:GDN_QKV + (h + 1) * GDN_DK]
            o_ref[q, rows, h * GDN_DK:(h + 1) * GDN_DK] = _gdn_out(o, z, nw_ref[...]).astype(o_ref.dtype)
        states = [s * jnp.exp(g_lasts[i]) - _mm(s_mix[i], s) + s_add[i] for i, s in zip(idx, states)]
    for (q, h), s in zip(chains, states):
        s_scr[q, h] = s
    for q in range(nb):
        ext_scr[q] = pg_ref[q, tb - SUBLANES:tb, 0:GDN_QKV]

    @pl.when(t == pl.num_programs(1) - 1)
    def _():
        s_out_ref[...] = s_scr[...]


def _gdn_prompt(pg, pab, conv_w, alog, dtb, norm_w, *, nb, tb):
    b, t, _ = pg.shape
    fixed = lambda i, j: (0, 0)
    return pl.pallas_call(
        functools.partial(_gdn_prompt_kernel, nb, tb),
        grid=(b // nb, t // tb),
        in_specs=[pl.BlockSpec((nb, tb, GDN_MAIN), lambda i, j: (i, j, 0)),
                  pl.BlockSpec((nb, tb, AB_PAD), lambda i, j: (i, j, 0)),
                  pl.BlockSpec((CONV_W, GDN_QKV), fixed), pl.BlockSpec((1, AB_PAD), fixed),
                  pl.BlockSpec((1, AB_PAD), fixed), pl.BlockSpec((1, GDN_DK), fixed)],
        out_specs=[pl.BlockSpec((nb, tb, GDN_WIDTH), lambda i, j: (i, j, 0)),
                   pl.BlockSpec((nb, GDN_HEADS, GDN_DK, GDN_DK), lambda i, j: (i, 0, 0, 0))],
        out_shape=[jax.ShapeDtypeStruct((b, t, GDN_WIDTH), BF16),
                   jax.ShapeDtypeStruct((b, GDN_HEADS, GDN_DK, GDN_DK), F32)],
        scratch_shapes=[pltpu.VMEM((nb, GDN_HEADS, GDN_DK, GDN_DK), F32),
                        pltpu.VMEM((nb, SUBLANES, GDN_QKV), F32)],
        compiler_params=_params(("parallel", "arbitrary")),
        name="gdn_prompt",
    )(pg, pab, conv_w, alog, dtb, norm_w)


def _gdn_step_kernel(bb, pg_ref, pab_ref, cs_ref, s_ref, cw_ref, alog_ref, dtb_ref, nw_ref,
                     o_ref, cs_out_ref, s_out_ref):
    u = pg_ref[:, 0:GDN_QKV]
    acc = u * cw_ref[CONV_W - 1:CONV_W, :]
    for j in range(CONV_W - 1):
        acc = acc + cs_ref[:, j * GDN_QKV:(j + 1) * GDN_QKV] * cw_ref[j:j + 1, :]
    cs_out_ref[:, 0:(CONV_W - 2) * GDN_QKV] = cs_ref[:, GDN_QKV:(CONV_W - 1) * GDN_QKV]
    cs_out_ref[:, (CONV_W - 2) * GDN_QKV:(CONV_W - 1) * GDN_QKV] = u
    qkv = _silu(acc)
    log_alpha, beta_all = _gdn_gates(pab_ref[...], alog_ref[...], dtb_ref[...])
    alpha_all = jnp.exp(log_alpha)
    eye = jnp.where(_iota2((GDN_DK, GDN_DK), 0) == _iota2((GDN_DK, GDN_DK), 1), 1.0, 0.0)

    def to_col(row):
        return jnp.sum(eye * row, axis=-1, keepdims=True)

    lg_lanes = LANES.bit_length() - 1
    spread = jnp.where(_iota2((bb, bb * LANES), 0) == (_iota2((bb, bb * LANES), 1) >> lg_lanes), 1.0, 0.0)

    def to_cols(x):
        return _mm_split3(x, spread, sel_first=False, dims=(((0,), (0,)), ((), ())))

    vs, k_cols, q_cols = [], [], []
    for h in range(GDN_HEADS):
        q = qkv[:, h * GDN_DK:(h + 1) * GDN_DK]
        k = qkv[:, GDN_QK + h * GDN_DK:GDN_QK + (h + 1) * GDN_DK]
        vs.append(qkv[:, 2 * GDN_QK + h * GDN_DK:2 * GDN_QK + (h + 1) * GDN_DK])
        q = _l2norm(q) * (GDN_DK ** -0.5)
        k = _l2norm(k)
        k_cols.append([to_col(k[i:i + 1, :]) for i in range(bb)])
        q_all = to_cols(q)
        q_cols.append([q_all[:, i * LANES:(i + 1) * LANES] for i in range(bb)])
    for h in range(GDN_HEADS):
        o_rows = []
        for i in range(bb):
            s = s_ref[i, h] * alpha_all[i:i + 1, h:h + 1]
            mem = jnp.sum(k_cols[h][i] * s, axis=0, keepdims=True)
            delta = (vs[h][i:i + 1, :] - mem) * beta_all[i:i + 1, GDN_HEADS + h:GDN_HEADS + h + 1]
            s = s + k_cols[h][i] * delta
            s_out_ref[i, h] = s
            o_rows.append(jnp.sum(q_cols[h][i] * s, axis=0, keepdims=True))
        o = jnp.concatenate(o_rows, axis=0)
        z = pg_ref[:, GDN_QKV + h * GDN_DK:GDN_QKV + (h + 1) * GDN_DK]
        o_ref[:, h * GDN_DK:(h + 1) * GDN_DK] = _gdn_out(o, z, nw_ref[...])


def _gdn_step(pg, pab, conv_state, s0, conv_w, alog, dtb, norm_w, *, bb):
    b = pg.shape[0]
    row = lambda i: (i, 0)
    fixed = lambda i: (0, 0)
    cs_cols = (CONV_W - 1) * GDN_QKV
    state_spec = pl.BlockSpec((bb, GDN_HEADS, GDN_DK, GDN_DK), lambda i: (i, 0, 0, 0))
    return pl.pallas_call(
        functools.partial(_gdn_step_kernel, bb),
        grid=(b // bb,),
        in_specs=[pl.BlockSpec((bb, GDN_MAIN), row), pl.BlockSpec((bb, AB_PAD), row),
                  pl.BlockSpec((bb, cs_cols), row), state_spec,
                  pl.BlockSpec((CONV_W, GDN_QKV), fixed), pl.BlockSpec((1, AB_PAD), fixed),
                  pl.BlockSpec((1, AB_PAD), fixed), pl.BlockSpec((1, GDN_DK), fixed)],
        out_specs=[pl.BlockSpec((bb, GDN_WIDTH), row), pl.BlockSpec((bb, cs_cols), row), state_spec],
        out_shape=[jax.ShapeDtypeStruct((b, GDN_WIDTH), F32), jax.ShapeDtypeStruct((b, cs_cols), F32),
                   jax.ShapeDtypeStruct(s0.shape, F32)],
        compiler_params=_params(("parallel",)),
        name="gdn_step",
    )(pg, pab, conv_state, s0, conv_w, alog, dtb, norm_w)


def _head_sums(x, head_ones):
    width = 2 * LANES
    return jnp.concatenate([_mm(x[:, lo:lo + width], head_ones[lo:lo + width, lo:lo + width])
                            for lo in range(0, RWKV_WIDTH, width)], axis=1)


def _rwkv_prep(xs, w0, a0, wwa, g2, k_k, k_a, head_ones):
    w = RWKV_WIDTH
    r, k, v = xs[:, 0:w], xs[:, w:2 * w], xs[:, 2 * w:3 * w]
    wa_in = xs[:, 3 * w:3 * w + LORA_WA]
    lane = _iota2(wa_in.shape, 1)
    wa_in = jnp.where(lane < LORA_WA // 2, jnp.tanh(wa_in), wa_in)
    wa = _mm(wa_in, wwa)
    e = math.exp(-0.5) * _sigmoid(w0 + wa[:, 0:w])
    a = _sigmoid(a0 + wa[:, w:2 * w])
    g = _mm(_sigmoid(xs[:, 3 * w + LORA_WA:RWKV_COLS]), g2)
    kx = k * k_k
    kk = kx * lax.rsqrt(_head_sums(kx * kx, head_ones) + L2_EPS)
    k = k * (1.0 + (a - 1.0) * k_a)
    return r, k, v, e, kk, kk * a, g


def _rwkv_out(y, r, k, v, g, r_k, ln_w, ln_b, head_ones):
    inv_n = 1.0 / RWKV_N
    mean = _head_sums(y, head_ones) * inv_n
    yc = y - mean
    var = _head_sums(yc * yc, head_ones) * inv_n
    y = yc * lax.rsqrt(var + GN_EPS) * ln_w + ln_b
    bonus = _head_sums(r * k * r_k, head_ones) * v
    return (y + bonus) * g


def _rwkv_prompt_kernel(nb, tb, n_cast, pr_ref, mu_ref, w0_ref, a0_ref, wwa_ref, g2_ref, kk_ref, ka_ref, rk_ref,
                        lnw_ref, lnb_ref, ones_ref, *refs):
    cast_in, (o_ref, s_out_ref), cast_out = refs[:n_cast], refs[n_cast:n_cast + 2], refs[n_cast + 2:2 * n_cast + 2]
    s_scr, ext_scr, y_scr = refs[2 * n_cast + 2:]
    for src, dst in zip(cast_in, cast_out):
        dst[...] = src[...].astype(BF16)
    t = pl.program_id(1)
    c = RWKV_CHUNK
    n = RWKV_N
    pair = 2 * n
    lg_c = c.bit_length() - 1
    lg_n = n.bit_length() - 1
    n_pairs = RWKV_HEADS // 2
    chunks_per_seq = tb // c

    @pl.when(t == 0)
    def _():
        s_scr[...] = jnp.zeros_like(s_scr)
        ext_scr[...] = jnp.zeros_like(ext_scr)

    prevs = []
    for q in range(nb):
        prevs.append(_shift_rows(ext_scr[q], pr_ref[q], 1))
        ext_scr[q] = pr_ref[q, tb - SUBLANES:tb, :]
    p = jnp.concatenate([pr_ref[q] for q in range(nb)], axis=0)
    xs = p + (jnp.concatenate(prevs, axis=0) - p) * mu_ref[...]
    head_ones = ones_ref[...]
    r, k, v, e, kk, kka, g = _rwkv_prep(xs, w0_ref[...], a0_ref[...], wwa_ref[...], g2_ref[...],
                                         kk_ref[...], ka_ref[...], head_ones)

    ri = _iota2((tb, tb), 0)
    ci = _iota2((tb, tb), 1)
    tri = jnp.where((ri >= ci) & ((ri >> lg_c) == (ci >> lg_c)), 1.0, 0.0)
    e_t = jnp.concatenate([e[q * tb:(q + 1) * tb, :].T for q in range(nb)], axis=0)
    lg_t = _mm_split3(e_t, tri.T, sel_first=False)
    lg_inc = -jnp.concatenate([lg_t[q * RWKV_WIDTH:(q + 1) * RWKV_WIDTH, :].T for q in range(nb)], axis=0)
    lg_exc = lg_inc + e
    a_t = -kk * jnp.exp(lg_exc)
    r_t = r * jnp.exp(lg_inc)
    inv_g = jnp.exp(-lg_inc)
    b_t = kka * inv_g
    k_t = k * inv_g

    assert c == n, "two heads side by side fill one (chunk, 2 * chunk) score tile"
    stack = _pair_stack(n)

    tt = _iota2((c, pair), 0)
    ss = _iota2((c, pair), 1) & (n - 1)
    strict = tt > ss
    incl = tt >= ss
    bd = jnp.where((_iota2((pair, pair), 0) >> lg_n) == (_iota2((pair, pair), 1) >> lg_n), 1.0, 0.0)

    items = [(ch, hp) for ch in range(nb * chunks_per_seq) for hp in range(n_pairs)]

    def tile(x, ch, hp):
        return x[ch * c:(ch + 1) * c, hp * pair:(hp + 1) * pair]

    ats = [tile(a_t, *it) for it in items]
    rts = [tile(r_t, *it) for it in items]
    vts = [tile(v, *it) for it in items]
    svs = [stack(vt) for vt in vts]
    scs = [_mm(jnp.concatenate([at, rt], axis=0),
               jnp.concatenate([stack(tile(b_t, *it)), stack(tile(k_t, *it))], axis=0), _NT)
           for at, rt, it in zip(ats, rts, items)]
    tinvs = _inv_unit_lower([-jnp.where(strict, sc[0:c, 0:pair], 0.0) for sc in scs], stack)
    v_mix = [_mm(jnp.concatenate([jnp.where(strict, sc[0:c, pair:2 * pair], 0.0),
                                  jnp.where(incl, sc[c:2 * c, pair:2 * pair], 0.0)], axis=0), sv)
             for sc, sv in zip(scs, svs)]
    akv = [x[0:c, :] for x in v_mix]
    y_v = [x[c:2 * c, :] for x in v_mix]
    p_rbs = [jnp.where(incl, sc[c:2 * c, 0:pair], 0.0) for sc in scs]
    wu = [_mm(tinv, jnp.concatenate([stack(at), stack(x)], axis=1)) for tinv, at, x in zip(tinvs, ats, akv)]
    w_ts = [x[:, 0:pair] for x in wu]
    u0s = [x[:, pair:2 * pair] for x in wu]
    lg_lasts = [lg_inc[ch * c + c - 1:ch * c + c, hp * pair:(hp + 1) * pair] for ch, hp in items]
    to_ends = [jnp.exp(lg_last - tile(lg_inc, *it)) for lg_last, it in zip(lg_lasts, items)]
    bhs = [tile(kka, *it) * to_end for it, to_end in zip(items, to_ends)]
    khs = [tile(k, *it) * to_end for it, to_end in zip(items, to_ends)]
    ry = [_mm(p_rb, jnp.concatenate([stack(w_t), stack(u0)], axis=1)) for p_rb, w_t, u0 in zip(p_rbs, w_ts, u0s)]
    r_hats = [rt + x[:, 0:pair] for rt, x in zip(rts, ry)]
    y0s = [x[:, pair:2 * pair] + yv for x, yv in zip(ry, y_v)]
    s_mix = [bd * _mm(w_t.T, bh) for w_t, bh in zip(w_ts, bhs)]
    s_add = [bd * _mm(jnp.concatenate([u0, vt], axis=0).T, jnp.concatenate([bh, kh], axis=0))
             for u0, vt, bh, kh in zip(u0s, vts, bhs, khs)]

    chains = [(q, hp) for q in range(nb) for hp in range(n_pairs)]
    states = [s_scr[q, hp] for q, hp in chains]
    for blk in range(chunks_per_seq):
        idx = [(q * chunks_per_seq + blk) * n_pairs + hp for q, hp in chains]
        for (q, hp), i, s in zip(chains, idx, states):
            row0 = (q * chunks_per_seq + blk) * c
            y_scr[row0:row0 + c, hp * pair:(hp + 1) * pair] = y0s[i] + _mm(r_hats[i], s, _NT)
        states = [s * jnp.exp(lg_lasts[i]) + _mm(s, s_mix[i]) + s_add[i] for i, s in zip(idx, states)]
    for (q, hp), s in zip(chains, states):
        s_scr[q, hp] = s

    out = _rwkv_out(y_scr[...], r, k, v, g, rk_ref[...], lnw_ref[...], lnb_ref[...], head_ones)
    for q in range(nb):
        o_ref[q] = out[q * tb:(q + 1) * tb, :].astype(o_ref.dtype)

    @pl.when(t == pl.num_programs(1) - 1)
    def _():
        for q, hp in chains:
            s = s_scr[q, hp]
            s_out_ref[q, 2 * hp] = s[0:n, 0:n]
            s_out_ref[q, 2 * hp + 1] = s[n:pair, n:pair]


def _rwkv_weight_specs(fixed):
    vec = pl.BlockSpec((1, RWKV_WIDTH), fixed)
    return [pl.BlockSpec((1, RWKV_COLS), fixed), vec, vec,
            pl.BlockSpec((LORA_WA, 2 * RWKV_WIDTH), fixed), pl.BlockSpec((G_LORA, RWKV_WIDTH), fixed),
            vec, vec, vec, vec, vec, pl.BlockSpec((RWKV_WIDTH, RWKV_WIDTH), fixed)]


def _cast_chunk_spec(rows, cols, steps, t_blocks):
    chunks = max(c for c in range(1, steps + 1) if rows % c == 0 and (rows // c) % (2 * SUBLANES) == 0)
    return pl.BlockSpec((rows // chunks, cols), lambda i, j: (jnp.minimum(i * t_blocks + j, chunks - 1), 0))


def _rwkv_prompt(pr, weights, to_cast=(), *, nb, tb):
    b, t, _ = pr.shape
    fixed = lambda i, j: (0, 0)
    steps = (b // nb) * (t // tb)
    cast_specs = [_cast_chunk_spec(*w.shape, steps, t // tb) for w in to_cast]
    return pl.pallas_call(
        functools.partial(_rwkv_prompt_kernel, nb, tb, len(to_cast)),
        grid=(b // nb, t // tb),
        in_specs=[pl.BlockSpec((nb, tb, RWKV_COLS), lambda i, j: (i, j, 0))] + _rwkv_weight_specs(fixed)
        + cast_specs,
        out_specs=[pl.BlockSpec((nb, tb, RWKV_WIDTH), lambda i, j: (i, j, 0)),
                   pl.BlockSpec((nb, RWKV_HEADS, RWKV_N, RWKV_N), lambda i, j: (i, 0, 0, 0))] + cast_specs,
        out_shape=[jax.ShapeDtypeStruct((b, t, RWKV_WIDTH), BF16),
                   jax.ShapeDtypeStruct((b, RWKV_HEADS, RWKV_N, RWKV_N), F32)]
        + [jax.ShapeDtypeStruct(w.shape, BF16) for w in to_cast],
        scratch_shapes=[pltpu.VMEM((nb, RWKV_HEADS // 2, 2 * RWKV_N, 2 * RWKV_N), F32),
                        pltpu.VMEM((nb, SUBLANES, RWKV_COLS), F32),
                        pltpu.VMEM((nb * tb, RWKV_WIDTH), F32)],
        compiler_params=_params(("arbitrary", "arbitrary")),
        name="rwkv_prompt",
    )(pr, *weights, *to_cast)


def _rwkv_step_kernel(pr_ref, sh_ref, s_ref, mu_ref, w0_ref, a0_ref, wwa_ref, g2_ref, kk_ref, ka_ref,
                      rk_ref, lnw_ref, lnb_ref, ones_ref, o_ref, s_out_ref, vec_scr, row_scr, y_scr):
    h = pl.program_id(0)
    n = RWKV_N
    batch = pr_ref.shape[0]

    @pl.when(h == 0)
    def _():
        p = pr_ref[...]
        xs = p + (sh_ref[...] - p) * mu_ref[...]
        r, k, v, e, kk, kka, g = _rwkv_prep(xs, w0_ref[...], a0_ref[...], wwa_ref[...], g2_ref[...],
                                             kk_ref[...], ka_ref[...], ones_ref[...])
        for j, x in enumerate((-kk, kka, k, jnp.exp(-e), r, v)):
            vec_scr[j] = x.T.reshape(RWKV_HEADS, n, batch)
        for j, x in enumerate((r, k, v, g)):
            row_scr[j] = x

    nkk, kka, k, decay, r = (vec_scr[j, h] for j in range(5))
    for vi in range(n):
        s = s_ref[vi]
        sa = jnp.sum(s * nkk, axis=0, keepdims=True)
        s = s * decay + sa * kka + vec_scr[5, h, vi:vi + 1, :] * k
        s_out_ref[vi] = s
        y_scr[h, vi:vi + 1, :] = jnp.sum(s * r, axis=0, keepdims=True)

    @pl.when(h == RWKV_HEADS - 1)
    def _():
        y = y_scr[...].reshape(RWKV_WIDTH, batch).T
        o_ref[...] = _rwkv_out(y, row_scr[0], row_scr[1], row_scr[2], row_scr[3], rk_ref[...], lnw_ref[...],
                               lnb_ref[...], ones_ref[...])


def _rwkv_step(pr, shift, s0_t, weights):
    b = pr.shape[0]
    fixed = lambda i: (0, 0)
    state_spec = pl.BlockSpec((None, RWKV_N, RWKV_N, b), lambda i: (i, 0, 0, 0))
    return pl.pallas_call(
        _rwkv_step_kernel,
        grid=(RWKV_HEADS,),
        in_specs=[pl.BlockSpec((b, RWKV_COLS), fixed), pl.BlockSpec((b, RWKV_COLS), fixed), state_spec]
        + _rwkv_weight_specs(fixed),
        out_specs=[pl.BlockSpec((b, RWKV_WIDTH), fixed), state_spec],
        out_shape=[jax.ShapeDtypeStruct((b, RWKV_WIDTH), F32), jax.ShapeDtypeStruct(s0_t.shape, F32)],
        scratch_shapes=[pltpu.VMEM((6, RWKV_HEADS, RWKV_N, b), F32), pltpu.VMEM((4, b, RWKV_WIDTH), F32),
                        pltpu.VMEM((RWKV_HEADS, RWKV_N, b), F32)],
        compiler_params=_params(("arbitrary",)),
        name="rwkv_step",
    )(pr, shift, s0_t, *weights)


def _pad_lanes(x, width):
    return jnp.pad(x, ((0, 0), (0, width - x.shape[1])))


def kernel(x_prompt, x_sample, state_gdn_conv, state_gdn, state_rwkv_shift, state_rwkv, norm_ffn1, w_ffn1_in, w_ffn1_out, norm_mix, w_in, gdn_conv_w, gdn_a_log, gdn_dt_bias, gdn_norm_w, rwkv_mu, rwkv_w0, rwkv_w2, rwkv_a0, rwkv_a2, rwkv_g2, rwkv_k_k, rwkv_k_a, rwkv_r_k, rwkv_ln_w, rwkv_ln_b, w_out, norm_ffn2, w_ffn2_in, w_ffn2_out, norm_final):
    depth = norm_ffn1.shape[0]
    assert depth == 1, "the carried-state plumbing below is written for a single layer"
    b, t, _ = x_prompt.shape
    bs = x_sample.shape[0]
    assert x_sample.shape[1] == 1
    l = 0

    n_gdn_cols = GDN_MAIN + 2 * GDN_HEADS
    w_in_l = w_in[l]
    w_slabs = (w_in_l[:, :GDN_MAIN].astype(BF16), w_in_l[:, n_gdn_cols:].astype(BF16),
               _pad_lanes(w_in_l[:, GDN_MAIN:n_gdn_cols], AB_PAD).astype(BF16))
    w1a, w1b = w_ffn1_in[l].astype(BF16), w_ffn1_out[l].astype(BF16)
    woa, wob = w_out[l, :GDN_WIDTH].astype(BF16), w_out[l, GDN_WIDTH:].astype(BF16)
    half = LORA_WA // 2
    wwa = jnp.zeros((LORA_WA, 2 * RWKV_WIDTH), F32)
    wwa = wwa.at[:half, :RWKV_WIDTH].set(rwkv_w2[l]).at[half:, RWKV_WIDTH:].set(rwkv_a2[l]).astype(BF16)
    head_id = jnp.arange(RWKV_WIDTH) // RWKV_N
    head_ones = (head_id[:, None] == head_id[None, :]).astype(BF16)
    rwkv_w = (rwkv_mu[l][None], rwkv_w0[l][None], rwkv_a0[l][None], wwa, rwkv_g2[l].astype(BF16),
              rwkv_k_k[l][None], rwkv_k_a[l][None], rwkv_r_k[l].reshape(1, RWKV_WIDTH),
              rwkv_ln_w[l][None], rwkv_ln_b[l][None], head_ones)
    alog = _pad_lanes(gdn_a_log[l][None], AB_PAD)
    dtb = _pad_lanes(gdn_dt_bias[l][None], AB_PAD)
    gdn_w = (gdn_conv_w[l], alog, dtb, gdn_norm_w[l][None])
    nf1, nmix, nf2, nfin = norm_ffn1[l][None], norm_mix[l][None], norm_ffn2[l][None], norm_final[None]

    xp, xs = x_prompt.reshape(b * t, D_MODEL), x_sample.reshape(bs, D_MODEL)
    hp, hs = _ffn(xp, xs, nf1, w1a, w1b, tm=DENSE_TILE_ROWS, sub=DENSE_SUB_ROWS)
    pg, pr, pab, sg, sr, sab = _proj_in(hp, hs, nmix, w_slabs, tm=DENSE_TILE_ROWS)

    pg3, pr3, pab3 = pg.reshape(b, t, GDN_MAIN), pr.reshape(b, t, RWKV_COLS), pab.reshape(b, t, AB_PAD)
    oa, gdn_s = _gdn_prompt(pg3, pab3, *gdn_w, nb=MIXER_SEQS, tb=MIXER_BLOCK_ROWS)
    ob, wkv_s, w2a, w2b = _rwkv_prompt(pr3, rwkv_w, (w_ffn2_in[l], w_ffn2_out[l]),
                                       nb=MIXER_SEQS, tb=MIXER_BLOCK_ROWS)

    conv_in = state_gdn_conv[l].reshape(bs, (CONV_W - 1) * GDN_QKV)
    oa_s, conv_s, gdn_ss = _gdn_step(sg, sab, conv_in, state_gdn[l], *gdn_w, bb=GDN_STEP_ROWS)
    ob_s, wkv_t = _rwkv_step(sr, state_rwkv_shift[l], jnp.transpose(state_rwkv[l], (1, 2, 3, 0)), rwkv_w)
    wkv_ss = jnp.transpose(wkv_t, (3, 0, 1, 2))

    yp, ys = _ffn(hp, hs, nf2, w2a, w2b, final_w=nfin, tm=DENSE_TILE_ROWS, sub=DENSE_SUB_ROWS,
                  mix=(oa.reshape(b * t, GDN_WIDTH), ob.reshape(b * t, RWKV_WIDTH), oa_s, ob_s, woa, wob))

    return (yp.reshape(b, t, D_MODEL), ys.reshape(bs, 1, D_MODEL),
            pg3[:, t - (CONV_W - 1):, :GDN_QKV][None], gdn_s[None], pr3[:, t - 1, :][None], wkv_s[None],
            conv_s.reshape(1, bs, CONV_W - 1, GDN_QKV), gdn_ss[None], sr[None], wkv_ss[None])
```
